```python
import math
import jax, jax.numpy as jnp
from jax import lax
import numpy as np

D_MODEL = 1024
BATCH = 16
SEQ = 4096
DEPTH = 1

ATTN_WIDTH = D_MODEL // 2
LRU_WIDTH = D_MODEL - ATTN_WIDTH
ATTN_DK = 64
ATTN_DV = 2 * ATTN_DK
ATTN_HEADS = ATTN_WIDTH // ATTN_DV
QK_WIDTH = ATTN_HEADS * 2 * ATTN_DK
LRU_BLOCKS = 8
LRU_BW = LRU_WIDTH // LRU_BLOCKS
CONV_W = 4
LRU_C = 8.0
D_IN_PROJ = 2 * QK_WIDTH + ATTN_WIDTH + 2 * LRU_WIDTH
Q_BLOCK = 128
N_EXPERTS = 32
TOP_K = 4
D_FF = D_MODEL
SWIGLU_LIMIT = 7.0
SWIGLU_ALPHA = 1.702
MOE_BLOCK = 256
EPS = 1e-6

kernel_name = "hymba_diffattn_rglru_moe_adaln"


def rms_norm(x, g):
    xf = x.astype(jnp.float32)
    y = xf * lax.rsqrt(jnp.mean(xf * xf, axis=-1, keepdims=True) + EPS)
    return (y * g.astype(jnp.float32)).astype(x.dtype)


def modulate(h, shift, scale):
    return h * (1.0 + scale[:, None, :]) + shift[:, None, :]


def causal_diff_attention(q, k, v, lam):
    B, S, H, _, dk = q.shape
    n_blocks = S // Q_BLOCK
    scale = dk ** -0.5
    kpos = jnp.arange(S)

    def block(i):
        start = i * Q_BLOCK
        qb = lax.dynamic_slice_in_dim(q, start, Q_BLOCK, axis=1)
        s = jnp.einsum('bqhcd,bkhcd->bhcqk', qb, k).astype(jnp.float32) * scale
        qpos = start + jnp.arange(Q_BLOCK)
        mask = kpos[None, :] <= qpos[:, None]
        s = jnp.where(mask, s, -jnp.inf)
        p = jax.nn.softmax(s, axis=-1)
        a = p[:, :, 0] - lam * p[:, :, 1]
        return jnp.einsum('bhqk,bkhd->bqhd', a.astype(v.dtype), v)

    o = lax.map(block, jnp.arange(n_blocks))
    return o.transpose(1, 0, 2, 3, 4).reshape(B, S, H, v.shape[-1])


def _linear_scan_combine(left, right):
    a1, b1 = left
    a2, b2 = right
    return a1 * a2, a2 * b1 + b2


def rg_lru_branch(xr, gr, conv_w, conv_b, wa, ba, wx, bx, lru_lambda):
    B, S, C = xr.shape
    xc = lax.conv_general_dilated(
        xr, conv_w[:, None, :].astype(xr.dtype), window_strides=(1,),
        padding=[(CONV_W - 1, 0)], dimension_numbers=('NWC', 'WIO', 'NWC'),
        feature_group_count=C) + conv_b
    xb = xc.reshape(B, S, LRU_BLOCKS, LRU_BW)
    r = jax.nn.sigmoid(jnp.einsum('bsni,nij->bsnj', xb, wa).reshape(B, S, C) + ba)
    i = jax.nn.sigmoid(jnp.einsum('bsni,nij->bsnj', xb, wx).reshape(B, S, C) + bx)
    log_a = -LRU_C * r.astype(jnp.float32) * jax.nn.softplus(-lru_lambda.astype(jnp.float32))
    a = jnp.exp(log_a)
    b = jnp.sqrt(-jnp.expm1(2.0 * log_a)) * (i * xc).astype(jnp.float32)
    _, h = lax.associative_scan(_linear_scan_combine, (a, b), axis=1)
    return h.astype(xr.dtype) * jax.nn.gelu(gr, approximate=True)


def moe_ffn(h, router_w, router_b, w_gu, b_gu, w_dn, b_dn):
    B, S, D = h.shape
    T = B * S
    xt = h.reshape(T, D)
    logits = (xt @ router_w + router_b).astype(jnp.float32)
    top_v, top_i = lax.top_k(logits, TOP_K)
    gates = jax.nn.softmax(top_v, axis=-1)
    N = T * TOP_K
    e_flat = top_i.reshape(N)
    tok_flat = jnp.arange(N, dtype=jnp.int32) // TOP_K
    g_flat = gates.reshape(N)
    order = jnp.argsort(e_flat)
    e_s, tok_s, g_s = e_flat[order], tok_flat[order], g_flat[order]
    counts = jnp.bincount(e_flat, length=N_EXPERTS)
    padded = ((counts + MOE_BLOCK - 1) // MOE_BLOCK) * MOE_BLOCK
    starts = jnp.cumsum(counts) - counts
    pad_ends = jnp.cumsum(padded)
    pad_starts = pad_ends - padded
    dest = pad_starts[e_s] + (jnp.arange(N) - starts[e_s])
    n_blocks = -(-N // MOE_BLOCK) + N_EXPERTS
    P = n_blocks * MOE_BLOCK
    slot_tok = jnp.full((P,), T, dtype=jnp.int32).at[dest].set(tok_s)
    slot_gate = jnp.zeros((P,), jnp.float32).at[dest].set(g_s)
    block_expert = jnp.minimum(
        jnp.searchsorted(pad_ends, jnp.arange(n_blocks) * MOE_BLOCK, side='right'),
        N_EXPERTS - 1).astype(jnp.int32)
    x_pad = jnp.concatenate([xt, jnp.zeros((1, D), xt.dtype)], axis=0)

    def run_block(args):
        tok_b, e = args
        xb = x_pad[tok_b]
        gu = xb @ w_gu[e] + b_gu[e]
        gate, up = jnp.split(gu, 2, axis=-1)
        gate = jnp.minimum(gate, SWIGLU_LIMIT)
        up = jnp.clip(up, -SWIGLU_LIMIT, SWIGLU_LIMIT)
        act = (up + 1.0) * (gate * jax.nn.sigmoid(SWIGLU_ALPHA * gate))
        return act @ w_dn[e] + b_dn[e]

    y_slots = lax.map(run_block, (slot_tok.reshape(n_blocks, MOE_BLOCK), block_expert))
    y_slots = y_slots.reshape(P, D) * slot_gate[:, None].astype(xt.dtype)
    y = jnp.zeros((T + 1, D), xt.dtype).at[slot_tok].add(y_slots)
    return y[:T].reshape(B, S, D)


def setup_inputs(seed: int = 0) -> dict:
    key = jax.random.key(seed)
    ks = jax.random.split(key, 32)

    def nrm(k, shape, scale):
        return scale * jax.random.normal(k, shape, jnp.float32)

    L = DEPTH
    u = jax.random.uniform(ks[19], (L, LRU_WIDTH), jnp.float32, minval=0.9, maxval=0.999)
    a0 = u ** (1.0 / LRU_C)
    lru_lambda = jnp.log(a0) - jnp.log1p(-a0)
    return {
        "x": nrm(ks[0], (BATCH, SEQ, D_MODEL), 1.0),
        "c": nrm(ks[1], (BATCH, D_MODEL), 1.0),
        "ada_w": nrm(ks[2], (L, D_MODEL, 6 * D_MODEL), D_MODEL ** -0.5),
        "ada_b": nrm(ks[3], (L, 6 * D_MODEL), 0.01),
        "norm1_g": 1.0 + nrm(ks[4], (L, D_MODEL), 0.02),
        "w_in": nrm(ks[5], (L, D_MODEL, D_IN_PROJ), D_MODEL ** -0.5),
        "q_norm_g": 1.0 + nrm(ks[6], (L, ATTN_DK), 0.02),
        "k_norm_g": 1.0 + nrm(ks[7], (L, ATTN_DK), 0.02),
        "lambda_q1": nrm(ks[8], (L, ATTN_DK), 0.1),
        "lambda_k1": nrm(ks[9], (L, ATTN_DK), 0.1),
        "lambda_q2": nrm(ks[10], (L, ATTN_DK), 0.1),
        "lambda_k2": nrm(ks[11], (L, ATTN_DK), 0.1),
        "attn_subln_g": 1.0 + nrm(ks[12], (L, ATTN_DV), 0.02),
        "conv_w": nrm(ks[13], (L, CONV_W, LRU_WIDTH), CONV_W ** -0.5),
        "conv_b": nrm(ks[14], (L, LRU_WIDTH), 0.01),
        "lru_wa": nrm(ks[15], (L, LRU_BLOCKS, LRU_BW, LRU_BW), LRU_BW ** -0.5),
        "lru_ba": nrm(ks[16], (L, LRU_WIDTH), 0.01),
        "lru_wx": nrm(ks[17], (L, LRU_BLOCKS, LRU_BW, LRU_BW), LRU_BW ** -0.5),
        "lru_bx": nrm(ks[18], (L, LRU_WIDTH), 0.01),
        "lru_lambda": lru_lambda,
        "lru_out_g": 1.0 + nrm(ks[20], (L, LRU_WIDTH), 0.02),
        "w_out": nrm(ks[21], (L, D_MODEL, D_MODEL), D_MODEL ** -0.5),
        "norm2_g": 1.0 + nrm(ks[22], (L, D_MODEL), 0.02),
        "router_w": nrm(ks[23], (L, D_MODEL, N_EXPERTS), D_MODEL ** -0.5),
        "router_b": nrm(ks[24], (L, N_EXPERTS), 0.01),
        "w_gate_up": nrm(ks[25], (L, N_EXPERTS, D_MODEL, 2 * D_FF), D_MODEL ** -0.5),
        "b_gate_up": nrm(ks[26], (L, N_EXPERTS, 2 * D_FF), 0.01),
        "w_down": nrm(ks[27], (L, N_EXPERTS, D_FF, D_MODEL), D_FF ** -0.5),
        "b_down": nrm(ks[28], (L, N_EXPERTS, D_MODEL), 0.01),
    }


def reference(x, c, ada_w, ada_b, norm1_g, w_in, q_norm_g, k_norm_g,
              lambda_q1, lambda_k1, lambda_q2, lambda_k2, attn_subln_g,
              conv_w, conv_b, lru_wa, lru_ba, lru_wx, lru_bx, lru_lambda, lru_out_g,
              w_out, norm2_g, router_w, router_b, w_gate_up, b_gate_up, w_down, b_down):
    B, S, D = x.shape
    for l in range(DEPTH):
        lam_init = 0.8 - 0.6 * math.exp(-0.3 * l)
        mod = jax.nn.silu(c) @ ada_w[l] + ada_b[l]
        sh1, sc1, g1, sh2, sc2, g2 = jnp.split(mod, 6, axis=-1)

        h = modulate(rms_norm(x, norm1_g[l]), sh1, sc1)
        proj = h @ w_in[l]
        q, k, v, xr, gr = jnp.split(
            proj, [QK_WIDTH, 2 * QK_WIDTH, 2 * QK_WIDTH + ATTN_WIDTH,
                   2 * QK_WIDTH + ATTN_WIDTH + LRU_WIDTH], axis=-1)
        q = rms_norm(q.reshape(B, S, ATTN_HEADS, 2, ATTN_DK), q_norm_g[l])
        k = rms_norm(k.reshape(B, S, ATTN_HEADS, 2, ATTN_DK), k_norm_g[l])
        v = v.reshape(B, S, ATTN_HEADS, ATTN_DV)
        lam = (jnp.exp(jnp.sum(lambda_q1[l].astype(jnp.float32) * lambda_k1[l].astype(jnp.float32)))
               - jnp.exp(jnp.sum(lambda_q2[l].astype(jnp.float32) * lambda_k2[l].astype(jnp.float32)))
               + lam_init)
        attn = causal_diff_attention(q, k, v, lam)
        attn = (rms_norm(attn, attn_subln_g[l]) * (1.0 - lam_init)).reshape(B, S, ATTN_WIDTH)
        lru = rg_lru_branch(xr, gr, conv_w[l], conv_b[l], lru_wa[l], lru_ba[l],
                            lru_wx[l], lru_bx[l], lru_lambda[l])
        lru = rms_norm(lru, lru_out_g[l])
        mix = jnp.concatenate([attn, lru], axis=-1) @ w_out[l]
        x = x + g1[:, None, :] * mix

        h = modulate(rms_norm(x, norm2_g[l]), sh2, sc2)
        y = moe_ffn(h, router_w[l], router_b[l], w_gate_up[l], b_gate_up[l], w_down[l], b_down[l])
        x = x + g2[:, None, :] * y
    return x
```

```python
import functools
import math

import jax
import jax.numpy as jnp
from jax import lax
from jax.experimental import pallas as pl
from jax.experimental.pallas import tpu as pltpu

F32 = jnp.float32
BF16 = jnp.bfloat16
I32 = jnp.int32

ATTN_DK = 64
ATTN_DV = 2 * ATTN_DK
CONV_W = 4
LRU_C = 8.0
N_EXPERTS = 32
TOP_K = 4
SWIGLU_LIMIT = 7.0
SWIGLU_ALPHA = 1.702
MOE_BLOCK = 256
EPS = 1e-6
NEG_BIG = -1e30

LANES = 128
SUBLANES = 8
VMEM_LIMIT = 56 * 1024 * 1024


def _cparams(sem):
    return pltpu.CompilerParams(dimension_semantics=sem, vmem_limit_bytes=VMEM_LIMIT)


def _split_hi_lo(x):
    hi = x.astype(BF16)
    lo = (x - hi.astype(F32)).astype(BF16)
    return hi, lo


def _dot(a, b):
    return jnp.dot(a, b, preferred_element_type=F32)


def _dot_tb(a, b):
    return lax.dot_general(a, b, (((1,), (1,)), ((), ())), preferred_element_type=F32)


def _adaln_kernel(lam_init, c_ref, w_ref, b_ref, lq1, lk1, lq2, lk2, mod_ref, lam_ref):
    c = c_ref[...]
    s = c * jax.nn.sigmoid(c)
    s_hi, s_lo = _split_hi_lo(s)
    w = w_ref[...]
    w_hi, w_lo = _split_hi_lo(w)
    mod_ref[...] = _dot(s_hi, w_hi) + _dot(s_hi, w_lo) + _dot(s_lo, w_hi) + b_ref[...]
    d1 = jnp.sum(lq1[...] * lk1[...], axis=-1, keepdims=True)
    d2 = jnp.sum(lq2[...] * lk2[...], axis=-1, keepdims=True)
    lam = jnp.exp(d1) - jnp.exp(d2) + lam_init
    lam_ref[...] = jnp.broadcast_to(lam, lam_ref.shape)


def _adaln(c, ada_w, ada_b, lq1, lk1, lq2, lk2, lam_init):
    B, D = c.shape
    n = ada_w.shape[1] // D
    vec = lambda: pl.BlockSpec((1, ATTN_DK), lambda j: (0, 0))
    return pl.pallas_call(
        functools.partial(_adaln_kernel, lam_init),
        grid=(n,),
        in_specs=[
            pl.BlockSpec((B, D), lambda j: (0, 0)),
            pl.BlockSpec((D, D), lambda j: (0, j)),
            pl.BlockSpec((1, D), lambda j: (0, j)),
            vec(), vec(), vec(), vec(),
        ],
        out_specs=[
            pl.BlockSpec((B, D), lambda j: (0, j)),
            pl.BlockSpec((1, LANES), lambda j: (0, 0)),
        ],
        out_shape=[
            jax.ShapeDtypeStruct((B, n * D), F32),
            jax.ShapeDtypeStruct((1, LANES), F32),
        ],
        compiler_params=_cparams(("arbitrary",)),
        name="adaln",
    )(c, ada_w, ada_b.reshape(1, -1), lq1.reshape(1, -1), lk1.reshape(1, -1),
      lq2.reshape(1, -1), lk2.reshape(1, -1))


def _rms_modulate(x, g, shift, scale):
    ms = jnp.mean(x * x, axis=-1, keepdims=True)
    y = x * lax.rsqrt(ms + EPS) * g
    return y * (1.0 + scale) + shift


def _group_rms_scale(q, gsum, gexp):
    sq_hi, sq_lo = _split_hi_lo(q * q)
    ss = _dot(sq_hi, gsum) + _dot(sq_lo, gsum)
    r = lax.rsqrt(ss * (1.0 / ATTN_DK) + EPS)
    r_hi, r_lo = _split_hi_lo(r)
    return _dot(r_hi, gexp) + _dot(r_lo, gexp)


def _in_proj_kernel(qkw, aw, lw, x_ref, mod_ref, g_ref, w_ref, gq_ref, gk_ref, gsum_ref, gexp_ref,
                    q_ref, k_ref, v_ref, xr_ref, gr_ref):
    h = _rms_modulate(x_ref[...], g_ref[...], mod_ref[0:1, :], mod_ref[1:2, :])
    hb = h.astype(BF16)
    gsum = gsum_ref[...]
    gexp = gexp_ref[...]
    o = 0
    q = _dot(hb, w_ref[:, o:o + qkw]); o += qkw
    q_ref[...] = (q * _group_rms_scale(q, gsum, gexp) * gq_ref[...]).astype(BF16)
    k = _dot(hb, w_ref[:, o:o + qkw]); o += qkw
    k_ref[...] = (k * _group_rms_scale(k, gsum, gexp) * gk_ref[...]).astype(BF16)
    v_ref[...] = _dot(hb, w_ref[:, o:o + aw]).astype(BF16); o += aw
    xr_ref[...] = _dot(hb, w_ref[:, o:o + lw]); o += lw
    gr_ref[...] = _dot(hb, w_ref[:, o:o + lw])


def _in_proj(x2, mod3, norm_g, w_in_b, gq, gk, S, tm, qkw, aw, lw):
    T, D = x2.shape
    tps = S // tm
    n_groups = qkw // ATTN_DK
    grp = jnp.arange(qkw, dtype=I32) // ATTN_DK
    gsum = (grp[:, None] == jnp.arange(LANES, dtype=I32)[None, :]).astype(BF16)
    gexp = gsum.T
    const = lambda shape: pl.BlockSpec(shape, lambda i: (0,) * len(shape))
    row = lambda w: pl.BlockSpec((tm, w), lambda i: (i, 0))
    return pl.pallas_call(
        functools.partial(_in_proj_kernel, qkw, aw, lw),
        grid=(T // tm,),
        in_specs=[
            row(D),
            pl.BlockSpec((None, 6, D), lambda i: (i // tps, 0, 0)),
            const((1, D)),
            const(w_in_b.shape),
            const((1, qkw)), const((1, qkw)),
            const((qkw, LANES)), const((LANES, qkw)),
        ],
        out_specs=[row(qkw), row(qkw), row(aw), row(lw), row(lw)],
        out_shape=[
            jax.ShapeDtypeStruct((T, qkw), BF16),
            jax.ShapeDtypeStruct((T, qkw), BF16),
            jax.ShapeDtypeStruct((T, aw), BF16),
            jax.ShapeDtypeStruct((T, lw), F32),
            jax.ShapeDtypeStruct((T, lw), F32),
        ],
        compiler_params=_cparams(("arbitrary",)),
        name="in_proj",
    )(x2, mod3, norm_g, w_in_b, gq, gk, gsum, gexp)


def _attn_kernel(bq, out_scale, q_ref, k_ref, v_ref, lam_ref, g_ref, o_ref):
    i = pl.program_id(2)
    q = q_ref[...]
    lane = lax.broadcasted_iota(I32, q.shape, 1)
    zero = jnp.zeros_like(q)
    qq = jnp.concatenate([jnp.where(lane < ATTN_DK, q, zero),
                          jnp.where(lane >= ATTN_DK, q, zero)], axis=0)

    def step(j, carry, masked):
        m, l, acc = carry
        start = pl.multiple_of(j * bq, bq)
        kj = k_ref[pl.ds(start, bq), :]
        vj = v_ref[pl.ds(start, bq), :]
        s = _dot_tb(qq, kj)
        if masked:
            r = lax.broadcasted_iota(I32, s.shape, 0)
            r = jnp.where(r >= bq, r - bq, r)
            c = lax.broadcasted_iota(I32, s.shape, 1)
            s = jnp.where(c <= r, s, NEG_BIG)
        m_new = jnp.maximum(m, jnp.max(s, axis=-1, keepdims=True))
        alpha = jnp.exp(m - m_new)
        p = jnp.exp(s - m_new)
        l = alpha * l + jnp.sum(p, axis=-1, keepdims=True)
        acc = alpha * acc + _dot(p.astype(BF16), vj)
        return m_new, l, acc

    init = (jnp.full((2 * bq, 1), NEG_BIG, F32), jnp.zeros((2 * bq, 1), F32),
            jnp.zeros((2 * bq, ATTN_DV), F32))
    carry = lax.fori_loop(0, i, lambda j, c: step(j, c, False), init)
    _, l, acc = step(i, carry, True)
    o = acc / l
    a = o[:bq] - lam_ref[0:1, 0:1] * o[bq:]
    ms = jnp.mean(a * a, axis=-1, keepdims=True)
    o_ref[...] = (a * lax.rsqrt(ms + EPS) * g_ref[...] * out_scale).astype(BF16)


def _attention(q, k, v, lam, subln_g, B, S, bq, out_scale):
    T, qkw = q.shape
    H = qkw // ATTN_DV
    nq = S // bq
    return pl.pallas_call(
        functools.partial(_attn_kernel, bq, out_scale),
        grid=(B, H, nq),
        in_specs=[
            pl.BlockSpec((bq, ATTN_DV), lambda b, h, i: (b * nq + i, h)),
            pl.BlockSpec((S, ATTN_DV), lambda b, h, i: (b, h)),
            pl.BlockSpec((S, ATTN_DV), lambda b, h, i: (b, h)),
            pl.BlockSpec((1, LANES), lambda b, h, i: (0, 0)),
            pl.BlockSpec((1, ATTN_DV), lambda b, h, i: (0, 0)),
        ],
        out_specs=pl.BlockSpec((bq, ATTN_DV), lambda b, h, i: (b * nq + i, h)),
        out_shape=jax.ShapeDtypeStruct((T, H * ATTN_DV), BF16),
        compiler_params=_cparams(("arbitrary", "arbitrary", "arbitrary")),
        name="attention",
    )(q, k, v, lam, subln_g)


def _shift_rows(x, d, fill):
    row = lax.broadcasted_iota(I32, x.shape, 0)
    return jnp.where(row >= d, pltpu.roll(x, d, 0), fill)


def _lru_kernel(tc, cw, x_ref, gate_ref, cw_ref, cb_ref, wg_ref, bg_ref, lam_ref, og_ref,
                o_ref, ext_ref, a_ref, b_ref, hc_ref):
    c = pl.program_id(1)

    @pl.when(c == 0)
    def _():
        ext_ref[0:SUBLANES, :] = jnp.zeros((SUBLANES, cw), F32)
        hc_ref[...] = jnp.zeros_like(hc_ref)

    ext_ref[SUBLANES:SUBLANES + tc, :] = x_ref[...]
    xc = cb_ref[...] + cw_ref[CONV_W - 1:CONV_W, :] * x_ref[...]
    for w in range(CONV_W - 1):
        sh = CONV_W - 1 - w
        xc = xc + cw_ref[w:w + 1, :] * ext_ref[SUBLANES - sh:SUBLANES - sh + tc, :]
    tail = ext_ref[tc:tc + SUBLANES, :]
    ext_ref[0:SUBLANES, :] = tail

    g = _dot(xc.astype(BF16), wg_ref[...]) + bg_ref[...]
    r = jax.nn.sigmoid(g[:, :cw])
    ig = jax.nn.sigmoid(g[:, cw:])
    nl = -lam_ref[...]
    softplus = jnp.maximum(nl, 0.0) + jnp.log1p(jnp.exp(-jnp.abs(nl)))
    log_a = (-LRU_C) * r * softplus
    th = jnp.tanh(log_a)
    a_ref[...] = jnp.exp(log_a)
    b_ref[...] = jnp.sqrt(-2.0 * th / (1.0 - th)) * (ig * xc)

    def tile_scan(n, hc):
        rows = pl.ds(pl.multiple_of(n * SUBLANES, SUBLANES), SUBLANES)
        a = a_ref[rows, :]
        b = b_ref[rows, :]
        for d in (1, 2, 4):
            b = a * _shift_rows(b, d, 0.0) + b
            a = a * _shift_rows(a, d, 1.0)
        h = b + a * hc
        b_ref[rows, :] = h
        return jnp.broadcast_to(h[SUBLANES - 1:SUBLANES, :], h.shape)

    hc_ref[...] = lax.fori_loop(0, tc // SUBLANES, tile_scan, hc_ref[...])

    gt = gate_ref[...]
    gelu = 0.5 * gt * (1.0 + jnp.tanh(math.sqrt(2.0 / math.pi) * (gt + 0.044715 * gt * gt * gt)))
    y = b_ref[...] * gelu
    ms = jnp.mean(y * y, axis=-1, keepdims=True)
    o_ref[...] = (y * lax.rsqrt(ms + EPS) * og_ref[...]).astype(BF16)


def _lru(xr, gr, conv_w, conv_b, w_gates, b_gates, lru_lambda, out_g, B, S, tc):
    T, cw = xr.shape
    nc = S // tc
    const = lambda shape: pl.BlockSpec(shape, lambda b, c: (0,) * len(shape))
    row = pl.BlockSpec((tc, cw), lambda b, c: (b * nc + c, 0))
    return pl.pallas_call(
        functools.partial(_lru_kernel, tc, cw),
        grid=(B, nc),
        in_specs=[row, row, const((CONV_W, cw)), const((1, cw)), const((cw, 2 * cw)),
                  const((1, 2 * cw)), const((1, cw)), const((1, cw))],
        out_specs=row,
        out_shape=jax.ShapeDtypeStruct((T, cw), BF16),
        scratch_shapes=[
            pltpu.VMEM((tc + SUBLANES, cw), F32),
            pltpu.VMEM((tc, cw), F32),
            pltpu.VMEM((tc, cw), F32),
            pltpu.VMEM((SUBLANES, cw), F32),
        ],
        compiler_params=_cparams(("arbitrary", "arbitrary")),
        name="lru",
    )(xr, gr, conv_w, conv_b, w_gates, b_gates, lru_lambda, out_g)


def _out_proj_kernel(tm, aw, attn_ref, lru_ref, x_ref, mod_ref, w_ref, g_ref, rw_ref, rb_ref, tri_ref,
                     x1_ref, h2_ref, ti_ref, gt_ref, rk_ref, cnt_ref, run_ref):
    i = pl.program_id(0)

    @pl.when(i == 0)
    def _():
        run_ref[...] = jnp.zeros_like(run_ref)

    mix = _dot(attn_ref[...], w_ref[0:aw, :]) + _dot(lru_ref[...], w_ref[aw:, :])
    x1 = x_ref[...] + mod_ref[2:3, :] * mix
    x1_ref[...] = x1
    h2 = _rms_modulate(x1, g_ref[...], mod_ref[3:4, :], mod_ref[4:5, :])
    for j in range(h2.shape[1] // LANES):
        h2_ref[pl.ds(j, tm, stride=SUBLANES), :] = h2[:, j * LANES:(j + 1) * LANES]

    h_hi, h_lo = _split_hi_lo(h2)
    w_hi, w_lo = _split_hi_lo(rw_ref[...])
    lg = _dot_tb(w_hi, h_hi) + _dot_tb(w_hi, h_lo) + _dot_tb(w_lo, h_hi) + rb_ref[...]

    eidx = lax.broadcasted_iota(I32, lg.shape, 0)
    picked = jnp.zeros(lg.shape, F32)
    vals, idxs = [], []
    for _ in range(TOP_K):
        m = jnp.max(lg, axis=0, keepdims=True)
        idx = jnp.min(jnp.where(lg == m, eidx, N_EXPERTS), axis=0, keepdims=True)
        sel = eidx == idx
        lg = jnp.where(sel, -jnp.inf, lg)
        picked = picked + sel.astype(F32)
        vals.append(m)
        idxs.append(idx)
    es = [jnp.exp(v - vals[0]) for v in vals]
    den = es[0] + es[1] + es[2] + es[3]
    gt_ref[...] = jnp.concatenate([e / den for e in es], axis=0)
    ti_ref[...] = jnp.concatenate(idxs, axis=0)

    before = _dot(picked.astype(BF16), tri_ref[...]) + run_ref[:, 0:1]
    ranks = [jnp.sum(jnp.where(eidx == idx, before, 0.0), axis=0, keepdims=True) for idx in idxs]
    rk_ref[...] = jnp.concatenate(ranks, axis=0).astype(I32)
    run = run_ref[...] + jnp.sum(picked, axis=1, keepdims=True)
    run_ref[...] = run
    cnt_ref[...] = run.astype(I32)


def _out_proj(attn, lru, x2, mod3, w_out_b, norm_g, router_wt, router_b, S, tm):
    T, D = x2.shape
    aw = attn.shape[1]
    lw = lru.shape[1]
    tps = S // tm
    tri = (jnp.arange(tm, dtype=I32)[:, None] < jnp.arange(tm, dtype=I32)[None, :]).astype(BF16)
    const = lambda shape: pl.BlockSpec(shape, lambda i: (0,) * len(shape))
    row = lambda w: pl.BlockSpec((tm, w), lambda i: (i, 0))
    col = pl.BlockSpec((TOP_K, tm), lambda i: (0, i))
    return pl.pallas_call(
        functools.partial(_out_proj_kernel, tm, aw),
        grid=(T // tm,),
        in_specs=[
            row(aw), row(lw), row(D),
            pl.BlockSpec((None, 6, D), lambda i: (i // tps, 0, 0)),
            const((D, D)), const((1, D)), const((N_EXPERTS, D)), const((N_EXPERTS, 1)),
            const((tm, tm)),
        ],
        out_specs=[
            row(D),
            pl.BlockSpec((tm * SUBLANES, LANES), lambda i: (i, 0)),
            col, col, col,
            const((N_EXPERTS, LANES)),
        ],
        out_shape=[
            jax.ShapeDtypeStruct((T, D), F32),
            jax.ShapeDtypeStruct((T * SUBLANES, LANES), F32),
            jax.ShapeDtypeStruct((TOP_K, T), I32),
            jax.ShapeDtypeStruct((TOP_K, T), F32),
            jax.ShapeDtypeStruct((TOP_K, T), I32),
            jax.ShapeDtypeStruct((N_EXPERTS, LANES), I32),
        ],
        scratch_shapes=[pltpu.VMEM((N_EXPERTS, LANES), F32)],
        compiler_params=_cparams(("arbitrary",)),
        name="out_proj",
    )(attn, lru, x2, mod3, w_out_b, norm_g, router_wt, router_b, tri)


def _row_tile(ref, r):
    return ref.at[pl.ds(pl.multiple_of(r * SUBLANES, SUBLANES), SUBLANES), :]


def _index_copy(idx_hbm, idx_smem, sem, i):
    n = idx_smem.shape[0]
    return pltpu.make_async_copy(idx_hbm.at[pl.ds(pl.multiple_of(i * n, n), n)], idx_smem, sem)


def _dispatch_kernel(tmd, dest_hbm, h_hbm, xs_in, xs_hbm, idx_smem, isem, sem):
    del xs_in
    i = pl.program_id(0)
    cp = _index_copy(dest_hbm, idx_smem, isem, i)
    cp.start()
    cp.wait()

    def copy(r, k):
        return pltpu.make_async_copy(_row_tile(h_hbm, i * tmd + r), _row_tile(xs_hbm, idx_smem[k * tmd + r]), sem)

    def issue(r, _):
        for k in range(TOP_K):
            copy(r, k).start()
        return 0

    def drain(r, _):
        for k in range(TOP_K):
            copy(r, k).wait()
        return 0

    lax.fori_loop(0, tmd, issue, 0)
    lax.fori_loop(0, tmd, drain, 0)


def _dispatch(dest_tiles, h2t, n_slots, tmd):
    n_tiles = dest_tiles.shape[0] // (TOP_K * tmd)
    xs0 = jnp.zeros((n_slots * SUBLANES, LANES), F32)
    return pl.pallas_call(
        functools.partial(_dispatch_kernel, tmd),
        grid=(n_tiles,),
        in_specs=[pl.BlockSpec(memory_space=pl.ANY)] * 3,
        out_specs=pl.BlockSpec(memory_space=pl.ANY),
        out_shape=jax.ShapeDtypeStruct(xs0.shape, F32),
        scratch_shapes=[pltpu.SMEM((TOP_K * tmd,), I32), pltpu.SemaphoreType.DMA, pltpu.SemaphoreType.DMA],
        input_output_aliases={2: 0},
        compiler_params=_cparams(("arbitrary",)),
        name="dispatch",
    )(dest_tiles, h2t, xs0)


def _experts_kernel(ff, be_ref, nv_ref, xs_ref, wgu_ref, bgu_ref, wdn_ref, bdn_ref, ys_ref):
    i = pl.program_id(0)
    nv = nv_ref[i]
    n_slabs = xs_ref.shape[0] // MOE_BLOCK

    @pl.when(nv > 0)
    def _():
        x = jnp.concatenate(
            [xs_ref[pl.ds(j, MOE_BLOCK, stride=SUBLANES), :] for j in range(n_slabs)], axis=1)
        gu = _dot(x.astype(BF16), wgu_ref[...]) + bgu_ref[...]
        gate = jnp.minimum(gu[:, :ff], SWIGLU_LIMIT)
        up = jnp.clip(gu[:, ff:], -SWIGLU_LIMIT, SWIGLU_LIMIT)
        act = (up + 1.0) * (gate * jax.nn.sigmoid(SWIGLU_ALPHA * gate))
        y = _dot(act.astype(BF16), wdn_ref[...]) + bdn_ref[...]
        for j in range(n_slabs):
            ys_ref[pl.ds(j, MOE_BLOCK, stride=SUBLANES), :] = y[:, j * LANES:(j + 1) * LANES]

    @pl.when(nv == 0)
    def _():
        ys_ref[...] = jnp.zeros_like(ys_ref)


def _experts(block_expert, block_valid, xs, w_gu_b, b_gu, w_dn_b, b_dn):
    n_blocks = block_expert.shape[0]
    E, D, ff2 = w_gu_b.shape
    ff = ff2 // 2
    rows = pl.BlockSpec((MOE_BLOCK * SUBLANES, LANES), lambda i, be, nv: (i, 0))
    grid_spec = pltpu.PrefetchScalarGridSpec(
        num_scalar_prefetch=2,
        grid=(n_blocks,),
        in_specs=[
            rows,
            pl.BlockSpec((None, D, ff2), lambda i, be, nv: (be[i], 0, 0)),
            pl.BlockSpec((None, 1, ff2), lambda i, be, nv: (be[i], 0, 0)),
            pl.BlockSpec((None, ff, D), lambda i, be, nv: (be[i], 0, 0)),
            pl.BlockSpec((None, 1, D), lambda i, be, nv: (be[i], 0, 0)),
        ],
        out_specs=rows,
    )
    return pl.pallas_call(
        functools.partial(_experts_kernel, ff),
        grid_spec=grid_spec,
        out_shape=jax.ShapeDtypeStruct(xs.shape, F32),
        compiler_params=_cparams(("arbitrary",)),
        name="experts",
    )(block_expert, block_valid, xs, w_gu_b, b_gu.reshape(E, 1, ff2), w_dn_b, b_dn.reshape(E, 1, D))


def _combine_kernel(tmc, dest_hbm, ys_hbm, gates_ref, x1_ref, mod_ref, o_ref, idx_smem, buf_ref, isem, sem):
    i = pl.program_id(0)
    cp = _index_copy(dest_hbm, idx_smem, isem, i)
    cp.start()
    cp.wait()

    def copy(r, k):
        return pltpu.make_async_copy(_row_tile(ys_hbm, idx_smem[k * tmc + r]), _row_tile(buf_ref, k * tmc + r), sem)

    def issue(r, _):
        for k in range(TOP_K):
            copy(r, k).start()
        return 0

    def drain(r, _):
        for k in range(TOP_K):
            copy(r, k).wait()
        return 0

    lax.fori_loop(0, tmc, issue, 0)
    lax.fori_loop(0, tmc, drain, 0)

    n_slabs = o_ref.shape[1] // LANES
    y = jnp.zeros(o_ref.shape, F32)
    for k in range(TOP_K):
        rows = jnp.concatenate(
            [buf_ref[pl.ds(k * tmc * SUBLANES + j, tmc, stride=SUBLANES), :] for j in range(n_slabs)], axis=1)
        y = y + gates_ref[:, k:k + 1] * rows
    o_ref[...] = x1_ref[...] + mod_ref[5:6, :] * y


def _combine(dest_tiles, ys, gates_t, x1, mod3, S, tmc):
    T, D = x1.shape
    tps = S // tmc
    return pl.pallas_call(
        functools.partial(_combine_kernel, tmc),
        grid=(T // tmc,),
        in_specs=[
            pl.BlockSpec(memory_space=pl.ANY),
            pl.BlockSpec(memory_space=pl.ANY),
            pl.BlockSpec((tmc, TOP_K), lambda i: (i, 0)),
            pl.BlockSpec((tmc, D), lambda i: (i, 0)),
            pl.BlockSpec((None, 6, D), lambda i: (i // tps, 0, 0)),
        ],
        out_specs=pl.BlockSpec((tmc, D), lambda i: (i, 0)),
        out_shape=jax.ShapeDtypeStruct((T, D), F32),
        scratch_shapes=[
            pltpu.SMEM((TOP_K * tmc,), I32),
            pltpu.VMEM((TOP_K * tmc * SUBLANES, LANES), F32),
            pltpu.SemaphoreType.DMA,
            pltpu.SemaphoreType.DMA,
        ],
        compiler_params=_cparams(("arbitrary",)),
        name="combine",
    )(dest_tiles, ys, gates_t, x1, mod3)


def _tile(n, target):
    t = min(n, target)
    while n % t:
        t //= 2
    return t


def _block_diag(w):
    n, bw, _ = w.shape
    eye = jnp.eye(n, dtype=w.dtype)
    return (eye[:, None, :, None] * w[:, :, None, :]).reshape(n * bw, n * bw)


def _layer(l, x2, B, S, c, ada_w, ada_b, norm1_g, w_in, q_norm_g, k_norm_g, lambda_q1, lambda_k1, lambda_q2,
           lambda_k2, attn_subln_g, conv_w, conv_b, lru_wa, lru_ba, lru_wx, lru_bx, lru_lambda, lru_out_g,
           w_out, norm2_g, router_w, router_b, w_gate_up, b_gate_up, w_down, b_down):
    T, D = x2.shape
    lam_init = 0.8 - 0.6 * math.exp(-0.3 * l)
    aw = D // 2
    lw = D - aw
    heads = aw // ATTN_DV
    qkw = heads * 2 * ATTN_DK

    mod, lam = _adaln(c, ada_w, ada_b, lambda_q1, lambda_k1, lambda_q2, lambda_k2, lam_init)
    mod3 = mod.reshape(c.shape[0], 6, D)

    reps = qkw // ATTN_DK
    gq = (jnp.tile(q_norm_g, reps) * (ATTN_DK ** -0.5)).reshape(1, qkw)
    gk = jnp.tile(k_norm_g, reps).reshape(1, qkw)
    tm = _tile(S, 512)
    q, k, v, xr, gr = _in_proj(x2, mod3, norm1_g.reshape(1, D), w_in.astype(BF16), gq, gk, S, tm, qkw, aw, lw)

    attn = _attention(q, k, v, lam, attn_subln_g.reshape(1, ATTN_DV), B, S, _tile(S, 512), 1.0 - lam_init)

    w_gates = jnp.concatenate([_block_diag(lru_wa), _block_diag(lru_wx)], axis=1).astype(BF16)
    b_gates = jnp.concatenate([lru_ba, lru_bx]).reshape(1, 2 * lw)
    lru = _lru(xr, gr, conv_w, conv_b.reshape(1, lw), w_gates, b_gates, lru_lambda.reshape(1, lw),
               lru_out_g.reshape(1, lw), B, S, _tile(S, 512))

    x1, h2t, top_i, gates, rank, counts = _out_proj(
        attn, lru, x2, mod3, w_out.astype(BF16), norm2_g.reshape(1, D), router_w.T,
        router_b.reshape(N_EXPERTS, 1), S, tm)

    counts = counts[:, 0]
    padded = ((counts + MOE_BLOCK - 1) // MOE_BLOCK) * MOE_BLOCK
    pad_ends = jnp.cumsum(padded)
    pad_starts = pad_ends - padded
    n_blocks = (T * TOP_K) // MOE_BLOCK + N_EXPERTS
    blk_start = jnp.arange(n_blocks, dtype=I32) * MOE_BLOCK
    block_expert = jnp.minimum(jnp.searchsorted(pad_ends, blk_start, side='right'), N_EXPERTS - 1).astype(I32)
    block_valid = jnp.clip(pad_starts[block_expert] + counts[block_expert] - blk_start, 0, MOE_BLOCK).astype(I32)
    dest = (pad_starts[top_i] + rank).astype(I32)

    def tiles(t):
        return dest.reshape(TOP_K, T // t, t).transpose(1, 0, 2).reshape(T * TOP_K)

    tmd = _tile(T, 2048)
    xs = _dispatch(tiles(tmd), h2t, n_blocks * MOE_BLOCK, tmd)
    ys = _experts(block_expert, block_valid, xs, w_gate_up.astype(BF16), b_gate_up,
                  w_down.astype(BF16), b_down)
    tmc = _tile(S, 256)
    return _combine(tiles(tmc), ys, gates.T, x1, mod3, S, tmc)


def kernel(x, c, ada_w, ada_b, norm1_g, w_in, q_norm_g, k_norm_g, lambda_q1, lambda_k1, lambda_q2, lambda_k2,
           attn_subln_g, conv_w, conv_b, lru_wa, lru_ba, lru_wx, lru_bx, lru_lambda, lru_out_g, w_out, norm2_g,
           router_w, router_b, w_gate_up, b_gate_up, w_down, b_down):
    B, S, D = x.shape
    params = (ada_w, ada_b, norm1_g, w_in, q_norm_g, k_norm_g, lambda_q1, lambda_k1, lambda_q2, lambda_k2,
              attn_subln_g, conv_w, conv_b, lru_wa, lru_ba, lru_wx, lru_bx, lru_lambda, lru_out_g, w_out,
              norm2_g, router_w, router_b, w_gate_up, b_gate_up, w_down, b_down)
    x2 = x.reshape(B * S, D)
    for l in range(ada_w.shape[0]):
        x2 = _layer(l, x2, B, S, c, *[p[l] for p in params])
    return x2.reshape(B, S, D)
```

```python
import functools
import math

import jax
import jax.numpy as jnp
from jax import lax
from jax.experimental import pallas as pl
from jax.experimental.pallas import tpu as pltpu

F32 = jnp.float32
BF16 = jnp.bfloat16
I32 = jnp.int32

ATTN_DK = 64
ATTN_DV = 2 * ATTN_DK
CONV_W = 4
LRU_C = 8.0
N_EXPERTS = 32
TOP_K = 4
SWIGLU_LIMIT = 7.0
SWIGLU_ALPHA = 1.702
MOE_BLOCK = 256
EPS = 1e-6
NEG_BIG = -1e30

LANES = 128
SUBLANES = 8
VMEM_LIMIT = 56 * 1024 * 1024


def _cparams(sem):
    return pltpu.CompilerParams(dimension_semantics=sem, vmem_limit_bytes=VMEM_LIMIT)


def _split_hi_lo(x):
    hi = x.astype(BF16)
    lo = (x - hi.astype(F32)).astype(BF16)
    return hi, lo


def _dot(a, b):
    return jnp.dot(a, b, preferred_element_type=F32)


def _dot_tb(a, b):
    return lax.dot_general(a, b, (((1,), (1,)), ((), ())), preferred_element_type=F32)


def _adaln_kernel(lam_init, c_ref, w_ref, b_ref, lq1, lk1, lq2, lk2, mod_ref, lam_ref):
    c = c_ref[...]
    s = c * jax.nn.sigmoid(c)
    s_hi, s_lo = _split_hi_lo(s)
    w = w_ref[...]
    w_hi, w_lo = _split_hi_lo(w)
    mod_ref[...] = _dot(s_hi, w_hi) + _dot(s_hi, w_lo) + _dot(s_lo, w_hi) + b_ref[...]
    d1 = jnp.sum(lq1[...] * lk1[...], axis=-1, keepdims=True)
    d2 = jnp.sum(lq2[...] * lk2[...], axis=-1, keepdims=True)
    lam = jnp.exp(d1) - jnp.exp(d2) + lam_init
    lam_ref[...] = jnp.broadcast_to(lam, lam_ref.shape)


def _adaln(c, ada_w, ada_b, lq1, lk1, lq2, lk2, lam_init):
    B, D = c.shape
    n = ada_w.shape[1] // D
    vec = lambda: pl.BlockSpec((1, ATTN_DK), lambda j: (0, 0))
    return pl.pallas_call(
        functools.partial(_adaln_kernel, lam_init),
        grid=(n,),
        in_specs=[
            pl.BlockSpec((B, D), lambda j: (0, 0)),
            pl.BlockSpec((D, D), lambda j: (0, j)),
            pl.BlockSpec((1, D), lambda j: (0, j)),
            vec(), vec(), vec(), vec(),
        ],
        out_specs=[
            pl.BlockSpec((B, D), lambda j: (0, j)),
            pl.BlockSpec((1, LANES), lambda j: (0, 0)),
        ],
        out_shape=[
            jax.ShapeDtypeStruct((B, n * D), F32),
            jax.ShapeDtypeStruct((1, LANES), F32),
        ],
        compiler_params=_cparams(("arbitrary",)),
        name="adaln",
    )(c, ada_w, ada_b.reshape(1, -1), lq1.reshape(1, -1), lk1.reshape(1, -1),
      lq2.reshape(1, -1), lk2.reshape(1, -1))


def _rms_modulate(x, g, shift, scale):
    ms = jnp.mean(x * x, axis=-1, keepdims=True)
    y = x * lax.rsqrt(ms + EPS) * g
    return y * (1.0 + scale) + shift


def _group_rms_scale(q, gsum, gexp):
    sq_hi, sq_lo = _split_hi_lo(q * q)
    ss = _dot(sq_hi, gsum) + _dot(sq_lo, gsum)
    r = lax.rsqrt(ss * (1.0 / ATTN_DK) + EPS)
    r_hi, r_lo = _split_hi_lo(r)
    return _dot(r_hi, gexp) + _dot(r_lo, gexp)


def _in_proj_kernel(qkw, aw, lw, x_ref, mod_ref, g_ref, w_ref, gq_ref, gk_ref, gsum_ref, gexp_ref,
                    q_ref, k_ref, v_ref, xr_ref, gr_ref):
    h = _rms_modulate(x_ref[...], g_ref[...], mod_ref[0:1, :], mod_ref[1:2, :])
    hb = h.astype(BF16)
    gsum = gsum_ref[...]
    gexp = gexp_ref[...]
    o = 0
    q = _dot(hb, w_ref[:, o:o + qkw]); o += qkw
    q_ref[...] = (q * _group_rms_scale(q, gsum, gexp) * gq_ref[...]).astype(BF16)
    k = _dot(hb, w_ref[:, o:o + qkw]); o += qkw
    k_ref[...] = (k * _group_rms_scale(k, gsum, gexp) * gk_ref[...]).astype(BF16)
    v_ref[...] = _dot(hb, w_ref[:, o:o + aw]).astype(BF16); o += aw
    xr_ref[...] = _dot(hb, w_ref[:, o:o + lw]); o += lw
    gr_ref[...] = _dot(hb, w_ref[:, o:o + lw])


def _in_proj(x2, mod3, norm_g, w_in_b, gq, gk, S, tm, qkw, aw, lw):
    T, D = x2.shape
    tps = S // tm
    n_groups = qkw // ATTN_DK
    grp = jnp.arange(qkw, dtype=I32) // ATTN_DK
    gsum = (grp[:, None] == jnp.arange(LANES, dtype=I32)[None, :]).astype(BF16)
    gexp = gsum.T
    const = lambda shape: pl.BlockSpec(shape, lambda i: (0,) * len(shape))
    row = lambda w: pl.BlockSpec((tm, w), lambda i: (i, 0))
    return pl.pallas_call(
        functools.partial(_in_proj_kernel, qkw, aw, lw),
        grid=(T // tm,),
        in_specs=[
            row(D),
            pl.BlockSpec((None, 6, D), lambda i: (i // tps, 0, 0)),
            const((1, D)),
            const(w_in_b.shape),
            const((1, qkw)), const((1, qkw)),
            const((qkw, LANES)), const((LANES, qkw)),
        ],
        out_specs=[row(qkw), row(qkw), row(aw), row(lw), row(lw)],
        out_shape=[
            jax.ShapeDtypeStruct((T, qkw), BF16),
            jax.ShapeDtypeStruct((T, qkw), BF16),
            jax.ShapeDtypeStruct((T, aw), BF16),
            jax.ShapeDtypeStruct((T, lw), F32),
            jax.ShapeDtypeStruct((T, lw), F32),
        ],
        compiler_params=_cparams(("arbitrary",)),
        name="in_proj",
    )(x2, mod3, norm_g, w_in_b, gq, gk, gsum, gexp)


def _attn_kernel(bq, out_scale, q_ref, k_ref, v_ref, lam_ref, g_ref, o_ref):
    i = pl.program_id(2)
    q = q_ref[...]
    lane = lax.broadcasted_iota(I32, q.shape, 1)
    zero = jnp.zeros_like(q)
    qq = jnp.concatenate([jnp.where(lane < ATTN_DK, q, zero),
                          jnp.where(lane >= ATTN_DK, q, zero)], axis=0)

    def step(j, carry, masked):
        m, l, acc = carry
        start = pl.multiple_of(j * bq, bq)
        kj = k_ref[pl.ds(start, bq), :]
        vj = v_ref[pl.ds(start, bq), :]
        s = _dot_tb(qq, kj)
        if masked:
            r = lax.broadcasted_iota(I32, s.shape, 0)
            r = jnp.where(r >= bq, r - bq, r)
            c = lax.broadcasted_iota(I32, s.shape, 1)
            s = jnp.where(c <= r, s, NEG_BIG)
        m_new = jnp.maximum(m, jnp.max(s, axis=-1, keepdims=True))
        alpha = jnp.exp(m - m_new)
        p = jnp.exp(s - m_new)
        l = alpha * l + jnp.sum(p, axis=-1, keepdims=True)
        acc = alpha * acc + _dot(p.astype(BF16), vj)
        return m_new, l, acc

    init = (jnp.full((2 * bq, 1), NEG_BIG, F32), jnp.zeros((2 * bq, 1), F32),
            jnp.zeros((2 * bq, ATTN_DV), F32))
    carry = lax.fori_loop(0, i, lambda j, c: step(j, c, False), init)
    _, l, acc = step(i, carry, True)
    o = acc / l
    a = o[:bq] - lam_ref[0:1, 0:1] * o[bq:]
    ms = jnp.mean(a * a, axis=-1, keepdims=True)
    o_ref[...] = (a * lax.rsqrt(ms + EPS) * g_ref[...] * out_scale).astype(BF16)


def _attention(q, k, v, lam, subln_g, B, S, bq, out_scale):
    T, qkw = q.shape
    H = qkw // ATTN_DV
    nq = S // bq
    return pl.pallas_call(
        functools.partial(_attn_kernel, bq, out_scale),
        grid=(B, H, nq),
        in_specs=[
            pl.BlockSpec((bq, ATTN_DV), lambda b, h, i: (b * nq + i, h)),
            pl.BlockSpec((S, ATTN_DV), lambda b, h, i: (b, h)),
            pl.BlockSpec((S, ATTN_DV), lambda b, h, i: (b, h)),
            pl.BlockSpec((1, LANES), lambda b, h, i: (0, 0)),
            pl.BlockSpec((1, ATTN_DV), lambda b, h, i: (0, 0)),
        ],
        out_specs=pl.BlockSpec((bq, ATTN_DV), lambda b, h, i: (b * nq + i, h)),
        out_shape=jax.ShapeDtypeStruct((T, H * ATTN_DV), BF16),
        compiler_params=_cparams(("arbitrary", "arbitrary", "arbitrary")),
        name="attention",
    )(q, k, v, lam, subln_g)


def _shift_rows(x, d, fill):
    row = lax.broadcasted_iota(I32, x.shape, 0)
    return jnp.where(row >= d, pltpu.roll(x, d, 0), fill)


def _lru_kernel(tc, cw, x_ref, gate_ref, cw_ref, cb_ref, wg_ref, bg_ref, lam_ref, og_ref,
                o_ref, ext_ref, a_ref, b_ref, hc_ref):
    c = pl.program_id(1)

    @pl.when(c == 0)
    def _():
        ext_ref[0:SUBLANES, :] = jnp.zeros((SUBLANES, cw), F32)
        hc_ref[...] = jnp.zeros_like(hc_ref)

    ext_ref[SUBLANES:SUBLANES + tc, :] = x_ref[...]
    xc = cb_ref[...] + cw_ref[CONV_W - 1:CONV_W, :] * x_ref[...]
    for w in range(CONV_W - 1):
        sh = CONV_W - 1 - w
        xc = xc + cw_ref[w:w + 1, :] * ext_ref[SUBLANES - sh:SUBLANES - sh + tc, :]
    tail = ext_ref[tc:tc + SUBLANES, :]
    ext_ref[0:SUBLANES, :] = tail

    g = _dot(xc.astype(BF16), wg_ref[...]) + bg_ref[...]
    r = jax.nn.sigmoid(g[:, :cw])
    ig = jax.nn.sigmoid(g[:, cw:])
    nl = -lam_ref[...]
    softplus = jnp.maximum(nl, 0.0) + jnp.log1p(jnp.exp(-jnp.abs(nl)))
    log_a = (-LRU_C) * r * softplus
    th = jnp.tanh(log_a)
    a_ref[...] = jnp.exp(log_a)
    b_ref[...] = jnp.sqrt(-2.0 * th / (1.0 - th)) * (ig * xc)

    def tile_scan(n, hc):
        rows = pl.ds(pl.multiple_of(n * SUBLANES, SUBLANES), SUBLANES)
        a = a_ref[rows, :]
        b = b_ref[rows, :]
        for d in (1, 2, 4):
            b = a * _shift_rows(b, d, 0.0) + b
            a = a * _shift_rows(a, d, 1.0)
        h = b + a * hc
        b_ref[rows, :] = h
        return jnp.broadcast_to(h[SUBLANES - 1:SUBLANES, :], h.shape)

    hc_ref[...] = lax.fori_loop(0, tc // SUBLANES, tile_scan, hc_ref[...])

    gt = gate_ref[...]
    gelu = 0.5 * gt * (1.0 + jnp.tanh(math.sqrt(2.0 / math.pi) * (gt + 0.044715 * gt * gt * gt)))
    y = b_ref[...] * gelu
    ms = jnp.mean(y * y, axis=-1, keepdims=True)
    o_ref[...] = (y * lax.rsqrt(ms + EPS) * og_ref[...]).astype(BF16)


def _lru(xr, gr, conv_w, conv_b, w_gates, b_gates, lru_lambda, out_g, B, S, tc):
    T, cw = xr.shape
    nc = S // tc
    const = lambda shape: pl.BlockSpec(shape, lambda b, c: (0,) * len(shape))
    row = pl.BlockSpec((tc, cw), lambda b, c: (b * nc + c, 0))
    return pl.pallas_call(
        functools.partial(_lru_kernel, tc, cw),
        grid=(B, nc),
        in_specs=[row, row, const((CONV_W, cw)), const((1, cw)), const((cw, 2 * cw)),
                  const((1, 2 * cw)), const((1, cw)), const((1, cw))],
        out_specs=row,
        out_shape=jax.ShapeDtypeStruct((T, cw), BF16),
        scratch_shapes=[
            pltpu.VMEM((tc + SUBLANES, cw), F32),
            pltpu.VMEM((tc, cw), F32),
            pltpu.VMEM((tc, cw), F32),
            pltpu.VMEM((SUBLANES, cw), F32),
        ],
        compiler_params=_cparams(("arbitrary", "arbitrary")),
        name="lru",
    )(xr, gr, conv_w, conv_b, w_gates, b_gates, lru_lambda, out_g)


def _out_proj_kernel(tm, aw, attn_ref, lru_ref, x_ref, mod_ref, w_ref, g_ref, rw_ref, rb_ref, tri_ref,
                     x1_ref, h2_ref, ti_ref, gt_ref, rk_ref, cnt_ref, run_ref):
    i = pl.program_id(0)

    @pl.when(i == 0)
    def _():
        run_ref[...] = jnp.zeros_like(run_ref)

    mix = _dot(attn_ref[...], w_ref[0:aw, :]) + _dot(lru_ref[...], w_ref[aw:, :])
    x1 = x_ref[...] + mod_ref[2:3, :] * mix
    x1_ref[...] = x1
    h2 = _rms_modulate(x1, g_ref[...], mod_ref[3:4, :], mod_ref[4:5, :])
    for j in range(h2.shape[1] // LANES):
        h2_ref[pl.ds(j, tm, stride=SUBLANES), :] = h2[:, j * LANES:(j + 1) * LANES]

    h_hi, h_lo = _split_hi_lo(h2)
    w_hi, w_lo = _split_hi_lo(rw_ref[...])
    lg = _dot_tb(w_hi, h_hi) + _dot_tb(w_hi, h_lo) + _dot_tb(w_lo, h_hi) + rb_ref[...]

    eidx = lax.broadcasted_iota(I32, lg.shape, 0)
    picked = jnp.zeros(lg.shape, F32)
    vals, idxs = [], []
    for _ in range(TOP_K):
        m = jnp.max(lg, axis=0, keepdims=True)
        idx = jnp.min(jnp.where(lg == m, eidx, N_EXPERTS), axis=0, keepdims=True)
        sel = eidx == idx
        lg = jnp.where(sel, -jnp.inf, lg)
        picked = picked + sel.astype(F32)
        vals.append(m)
        idxs.append(idx)
    es = [jnp.exp(v - vals[0]) for v in vals]
    den = es[0] + es[1] + es[2] + es[3]
    gt_ref[...] = jnp.concatenate([e / den for e in es], axis=0)
    ti_ref[...] = jnp.concatenate(idxs, axis=0)

    before = _dot(picked.astype(BF16), tri_ref[...]) + run_ref[:, 0:1]
    ranks = [jnp.sum(jnp.where(eidx == idx, before, 0.0), axis=0, keepdims=True) for idx in idxs]
    rk_ref[...] = jnp.concatenate(ranks, axis=0).astype(I32)
    run = run_ref[...] + jnp.sum(picked, axis=1, keepdims=True)
    run_ref[...] = run
    cnt_ref[...] = run.astype(I32)


def _out_proj(attn, lru, x2, mod3, w_out_b, norm_g, router_wt, router_b, S, tm):
    T, D = x2.shape
    aw = attn.shape[1]
    lw = lru.shape[1]
    tps = S // tm
    tri = (jnp.arange(tm, dtype=I32)[:, None] < jnp.arange(tm, dtype=I32)[None, :]).astype(BF16)
    const = lambda shape: pl.BlockSpec(shape, lambda i: (0,) * len(shape))
    row = lambda w: pl.BlockSpec((tm, w), lambda i: (i, 0))
    col = pl.BlockSpec((TOP_K, tm), lambda i: (0, i))
    return pl.pallas_call(
        functools.partial(_out_proj_kernel, tm, aw),
        grid=(T // tm,),
        in_specs=[
            row(aw), row(lw), row(D),
            pl.BlockSpec((None, 6, D), lambda i: (i // tps, 0, 0)),
            const((D, D)), const((1, D)), const((N_EXPERTS, D)), const((N_EXPERTS, 1)),
            const((tm, tm)),
        ],
        out_specs=[
            row(D),
            pl.BlockSpec((tm * SUBLANES, LANES), lambda i: (i, 0)),
            col, col, col,
            const((N_EXPERTS, LANES)),
        ],
        out_shape=[
            jax.ShapeDtypeStruct((T, D), F32),
            jax.ShapeDtypeStruct((T * SUBLANES, LANES), F32),
            jax.ShapeDtypeStruct((TOP_K, T), I32),
            jax.ShapeDtypeStruct((TOP_K, T), F32),
            jax.ShapeDtypeStruct((TOP_K, T), I32),
            jax.ShapeDtypeStruct((N_EXPERTS, LANES), I32),
        ],
        scratch_shapes=[pltpu.VMEM((N_EXPERTS, LANES), F32)],
        compiler_params=_cparams(("arbitrary",)),
        name="out_proj",
    )(attn, lru, x2, mod3, w_out_b, norm_g, router_wt, router_b, tri)


def _slots_kernel(ps_ref, ti_ref, rk_ref, o_ref):
    ti = ti_ref[...]
    base = jnp.zeros(ti.shape, I32)
    for e in range(N_EXPERTS):
        base = jnp.where(ti == e, ps_ref[e], base)
    o_ref[...] = base + rk_ref[...]


def _slots(pad_starts, top_i, rank, t):
    T = top_i.shape[1]
    col = pl.BlockSpec((TOP_K, t), lambda i, ps: (0, i))
    grid_spec = pltpu.PrefetchScalarGridSpec(
        num_scalar_prefetch=1,
        grid=(T // t,),
        in_specs=[col, col],
        out_specs=pl.BlockSpec((None, TOP_K, t), lambda i, ps: (i, 0, 0)),
    )
    return pl.pallas_call(
        _slots_kernel,
        grid_spec=grid_spec,
        out_shape=jax.ShapeDtypeStruct((T // t, TOP_K, t), I32),
        compiler_params=_cparams(("arbitrary",)),
        name="slots",
    )(pad_starts, top_i, rank)


def _row_tile(ref, r):
    return ref.at[pl.ds(pl.multiple_of(r * SUBLANES, SUBLANES), SUBLANES), :]


def _fetch_indices(idx_hbm, idx_smem, isem):
    i = pl.program_id(0)
    n = pl.num_programs(0)
    slot = i % 2

    def copy(tile, s):
        return pltpu.make_async_copy(idx_hbm.at[tile], idx_smem.at[s], isem.at[s])

    @pl.when(i == 0)
    def _():
        copy(0, 0).start()

    copy(i, slot).wait()

    @pl.when(i + 1 < n)
    def _():
        copy(i + 1, 1 - slot).start()

    return idx_smem.at[slot]


def _dispatch_kernel(t, dest_hbm, h_ref, xs_hbm, idx_smem, isem, sem):
    idx = _fetch_indices(dest_hbm, idx_smem, isem)

    def copy(r, k):
        return pltpu.make_async_copy(_row_tile(h_ref, r), _row_tile(xs_hbm, idx[k, r]), sem)

    def issue(r, _):
        for k in range(TOP_K):
            copy(r, k).start()
        return 0

    def drain(r, _):
        for k in range(TOP_K):
            copy(r, k).wait()
        return 0

    lax.fori_loop(0, t, issue, 0)
    lax.fori_loop(0, t, drain, 0)


def _dispatch(dest_tiles, h2t, n_slots):
    n_tiles, _, t = dest_tiles.shape
    return pl.pallas_call(
        functools.partial(_dispatch_kernel, t),
        grid=(n_tiles,),
        in_specs=[pl.BlockSpec(memory_space=pl.ANY),
                  pl.BlockSpec((t * SUBLANES, LANES), lambda i: (i, 0))],
        out_specs=pl.BlockSpec(memory_space=pl.ANY),
        out_shape=jax.ShapeDtypeStruct((n_slots * SUBLANES, LANES), F32),
        scratch_shapes=[pltpu.SMEM((2, TOP_K, t), I32), pltpu.SemaphoreType.DMA((2,)), pltpu.SemaphoreType.DMA],
        compiler_params=_cparams(("arbitrary",)),
        name="dispatch",
    )(dest_tiles, h2t)


def _experts_kernel(ff, be_ref, nv_ref, xs_ref, wgu_ref, bgu_ref, wdn_ref, bdn_ref, ys_ref):
    i = pl.program_id(0)
    nv = nv_ref[i]
    n_slabs = xs_ref.shape[0] // MOE_BLOCK

    @pl.when(nv > 0)
    def _():
        x = jnp.concatenate(
            [xs_ref[pl.ds(j, MOE_BLOCK, stride=SUBLANES), :] for j in range(n_slabs)], axis=1)
        x = jnp.where(lax.broadcasted_iota(I32, x.shape, 0) < nv, x, 0.0)
        gu = _dot(x.astype(BF16), wgu_ref[...]) + bgu_ref[...]
        gate = jnp.minimum(gu[:, :ff], SWIGLU_LIMIT)
        up = jnp.clip(gu[:, ff:], -SWIGLU_LIMIT, SWIGLU_LIMIT)
        act = (up + 1.0) * (gate * jax.nn.sigmoid(SWIGLU_ALPHA * gate))
        y = _dot(act.astype(BF16), wdn_ref[...]) + bdn_ref[...]
        for j in range(n_slabs):
            ys_ref[pl.ds(j, MOE_BLOCK, stride=SUBLANES), :] = y[:, j * LANES:(j + 1) * LANES]

    @pl.when(nv == 0)
    def _():
        ys_ref[...] = jnp.zeros_like(ys_ref)


def _experts(block_expert, block_valid, xs, w_gu_b, b_gu, w_dn_b, b_dn):
    n_blocks = block_expert.shape[0]
    E, D, ff2 = w_gu_b.shape
    ff = ff2 // 2
    rows = pl.BlockSpec((MOE_BLOCK * SUBLANES, LANES), lambda i, be, nv: (i, 0))
    grid_spec = pltpu.PrefetchScalarGridSpec(
        num_scalar_prefetch=2,
        grid=(n_blocks,),
        in_specs=[
            rows,
            pl.BlockSpec((None, D, ff2), lambda i, be, nv: (be[i], 0, 0)),
            pl.BlockSpec((None, 1, ff2), lambda i, be, nv: (be[i], 0, 0)),
            pl.BlockSpec((None, ff, D), lambda i, be, nv: (be[i], 0, 0)),
            pl.BlockSpec((None, 1, D), lambda i, be, nv: (be[i], 0, 0)),
        ],
        out_specs=rows,
    )
    return pl.pallas_call(
        functools.partial(_experts_kernel, ff),
        grid_spec=grid_spec,
        out_shape=jax.ShapeDtypeStruct(xs.shape, F32),
        compiler_params=_cparams(("arbitrary",)),
        name="experts",
    )(block_expert, block_valid, xs, w_gu_b, b_gu.reshape(E, 1, ff2), w_dn_b, b_dn.reshape(E, 1, D))


def _combine_kernel(tmc, dest_hbm, ys_hbm, gates_ref, x1_ref, mod_ref, o_ref, idx_smem, buf_ref, isem, sem):
    idx = _fetch_indices(dest_hbm, idx_smem, isem)

    def copy(r, k):
        return pltpu.make_async_copy(_row_tile(ys_hbm, idx[k, r]), _row_tile(buf_ref, k * tmc + r), sem)

    def issue(r, _):
        for k in range(TOP_K):
            copy(r, k).start()
        return 0

    def drain(r, _):
        for k in range(TOP_K):
            copy(r, k).wait()
        return 0

    lax.fori_loop(0, tmc, issue, 0)
    lax.fori_loop(0, tmc, drain, 0)

    n_slabs = o_ref.shape[1] // LANES
    y = jnp.zeros(o_ref.shape, F32)
    for k in range(TOP_K):
        rows = jnp.concatenate(
            [buf_ref[pl.ds(k * tmc * SUBLANES + j, tmc, stride=SUBLANES), :] for j in range(n_slabs)], axis=1)
        y = y + gates_ref[:, k:k + 1] * rows
    o_ref[...] = x1_ref[...] + mod_ref[5:6, :] * y


def _combine(dest_tiles, ys, gates_t, x1, mod3, S, tmc):
    T, D = x1.shape
    tps = S // tmc
    return pl.pallas_call(
        functools.partial(_combine_kernel, tmc),
        grid=(T // tmc,),
        in_specs=[
            pl.BlockSpec(memory_space=pl.ANY),
            pl.BlockSpec(memory_space=pl.ANY),
            pl.BlockSpec((tmc, TOP_K), lambda i: (i, 0)),
            pl.BlockSpec((tmc, D), lambda i: (i, 0)),
            pl.BlockSpec((None, 6, D), lambda i: (i // tps, 0, 0)),
        ],
        out_specs=pl.BlockSpec((tmc, D), lambda i: (i, 0)),
        out_shape=jax.ShapeDtypeStruct((T, D), F32),
        scratch_shapes=[
            pltpu.SMEM((2, TOP_K, tmc), I32),
            pltpu.VMEM((TOP_K * tmc * SUBLANES, LANES), F32),
            pltpu.SemaphoreType.DMA((2,)),
            pltpu.SemaphoreType.DMA,
        ],
        compiler_params=_cparams(("arbitrary",)),
        name="combine",
    )(dest_tiles, ys, gates_t, x1, mod3)


def _tile(n, target):
    t = min(n, target)
    while n % t:
        t //= 2
    return t


def _block_diag(w):
    n, bw, _ = w.shape
    eye = jnp.eye(n, dtype=w.dtype)
    return (eye[:, None, :, None] * w[:, :, None, :]).reshape(n * bw, n * bw)


def _layer(l, x2, B, S, c, ada_w, ada_b, norm1_g, w_in, q_norm_g, k_norm_g, lambda_q1, lambda_k1, lambda_q2,
           lambda_k2, attn_subln_g, conv_w, conv_b, lru_wa, lru_ba, lru_wx, lru_bx, lru_lambda, lru_out_g,
           w_out, norm2_g, router_w, router_b, w_gate_up, b_gate_up, w_down, b_down):
    T, D = x2.shape
    lam_init = 0.8 - 0.6 * math.exp(-0.3 * l)
    aw = D // 2
    lw = D - aw
    heads = aw // ATTN_DV
    qkw = heads * 2 * ATTN_DK

    mod, lam = _adaln(c, ada_w, ada_b, lambda_q1, lambda_k1, lambda_q2, lambda_k2, lam_init)
    mod3 = mod.reshape(c.shape[0], 6, D)

    reps = qkw // ATTN_DK
    gq = (jnp.tile(q_norm_g, reps) * (ATTN_DK ** -0.5)).reshape(1, qkw)
    gk = jnp.tile(k_norm_g, reps).reshape(1, qkw)
    tm = _tile(S, 512)
    q, k, v, xr, gr = _in_proj(x2, mod3, norm1_g.reshape(1, D), w_in.astype(BF16), gq, gk, S, tm, qkw, aw, lw)

    attn = _attention(q, k, v, lam, attn_subln_g.reshape(1, ATTN_DV), B, S, _tile(S, 512), 1.0 - lam_init)

    w_gates = jnp.concatenate([_block_diag(lru_wa), _block_diag(lru_wx)], axis=1).astype(BF16)
    b_gates = jnp.concatenate([lru_ba, lru_bx]).reshape(1, 2 * lw)
    lru = _lru(xr, gr, conv_w, conv_b.reshape(1, lw), w_gates, b_gates, lru_lambda.reshape(1, lw),
               lru_out_g.reshape(1, lw), B, S, _tile(S, 512))

    x1, h2t, top_i, gates, rank, counts = _out_proj(
        attn, lru, x2, mod3, w_out.astype(BF16), norm2_g.reshape(1, D), router_w.T,
        router_b.reshape(N_EXPERTS, 1), S, tm)

    counts = counts[:, 0]
    padded = ((counts + MOE_BLOCK - 1) // MOE_BLOCK) * MOE_BLOCK
    pad_ends = jnp.cumsum(padded)
    pad_starts = pad_ends - padded
    n_blocks = (T * TOP_K) // MOE_BLOCK + N_EXPERTS
    blk_start = jnp.arange(n_blocks, dtype=I32) * MOE_BLOCK
    owner = blk_start[:, None] >= pad_ends[None, :]
    block_expert = jnp.minimum(jnp.sum(owner, axis=1), N_EXPERTS - 1).astype(I32)
    onehot = block_expert[:, None] == jnp.arange(N_EXPERTS, dtype=I32)[None, :]
    row_end = jnp.sum(jnp.where(onehot, (pad_starts + counts)[None, :], 0), axis=1)
    block_valid = jnp.clip(row_end - blk_start, 0, MOE_BLOCK).astype(I32)

    tmc = _tile(S, 256)
    dest_tiles = _slots(pad_starts.astype(I32), top_i, rank, tmc)
    xs = _dispatch(dest_tiles, h2t, n_blocks * MOE_BLOCK)
    ys = _experts(block_expert, block_valid, xs, w_gate_up.astype(BF16), b_gate_up,
                  w_down.astype(BF16), b_down)
    return _combine(dest_tiles, ys, gates.T, x1, mod3, S, tmc)


def kernel(x, c, ada_w, ada_b, norm1_g, w_in, q_norm_g, k_norm_g, lambda_q1, lambda_k1, lambda_q2, lambda_k2,
           attn_subln_g, conv_w, conv_b, lru_wa, lru_ba, lru_wx, lru_bx, lru_lambda, lru_out_g, w_out, norm2_g,
           router_w, router_b, w_gate_up, b_gate_up, w_down, b_down):
    B, S, D = x.shape
    params = (ada_w, ada_b, norm1_g, w_in, q_norm_g, k_norm_g, lambda_q1, lambda_k1, lambda_q2, lambda_k2,
              attn_subln_g, conv_w, conv_b, lru_wa, lru_ba, lru_wx, lru_bx, lru_lambda, lru_out_g, w_out,
              norm2_g, router_w, router_b, w_gate_up, b_gate_up, w_down, b_down)
    x2 = x.reshape(B * S, D)
    for l in range(ada_w.shape[0]):
        x2 = _layer(l, x2, B, S, c, *[p[l] for p in params])
    return x2.reshape(B, S, D)
```

```python
import functools
import math

import jax
import jax.numpy as jnp
from jax import lax
from jax.experimental import pallas as pl
from jax.experimental.pallas import tpu as pltpu

F32 = jnp.float32
BF16 = jnp.bfloat16
I32 = jnp.int32

ATTN_DK = 64
ATTN_DV = 2 * ATTN_DK
CONV_W = 4
LRU_C = 8.0
N_EXPERTS = 32
TOP_K = 4
SWIGLU_LIMIT = 7.0
SWIGLU_ALPHA = 1.702
MOE_BLOCK = 512
EPS = 1e-6
NEG_BIG = -1e30

LANES = 128
SUBLANES = 8
VMEM_LIMIT = 56 * 1024 * 1024


def _cparams(sem):
    return pltpu.CompilerParams(dimension_semantics=sem, vmem_limit_bytes=VMEM_LIMIT)


def _split_hi_lo(x):
    hi = x.astype(BF16)
    lo = (x - hi.astype(F32)).astype(BF16)
    return hi, lo


def _dot(a, b):
    return jnp.dot(a, b, preferred_element_type=F32)


def _dot_tb(a, b):
    return lax.dot_general(a, b, (((1,), (1,)), ((), ())), preferred_element_type=F32)


def _adaln_kernel(lam_init, c_ref, w_ref, b_ref, lq1, lk1, lq2, lk2, mod_ref, lam_ref):
    c = c_ref[...]
    s = c * jax.nn.sigmoid(c)
    s_hi, s_lo = _split_hi_lo(s)
    w = w_ref[...]
    w_hi, w_lo = _split_hi_lo(w)
    mod_ref[...] = _dot(s_hi, w_hi) + _dot(s_hi, w_lo) + _dot(s_lo, w_hi) + b_ref[...]
    d1 = jnp.sum(lq1[...] * lk1[...], axis=-1, keepdims=True)
    d2 = jnp.sum(lq2[...] * lk2[...], axis=-1, keepdims=True)
    lam = jnp.exp(d1) - jnp.exp(d2) + lam_init
    lam_ref[...] = jnp.broadcast_to(lam, lam_ref.shape)


def _adaln(c, ada_w, ada_b, lq1, lk1, lq2, lk2, lam_init):
    B, D = c.shape
    n = ada_w.shape[1] // D
    vec = lambda: pl.BlockSpec((1, ATTN_DK), lambda j: (0, 0))
    return pl.pallas_call(
        functools.partial(_adaln_kernel, lam_init),
        grid=(n,),
        in_specs=[
            pl.BlockSpec((B, D), lambda j: (0, 0)),
            pl.BlockSpec((D, D), lambda j: (0, j)),
            pl.BlockSpec((1, D), lambda j: (0, j)),
            vec(), vec(), vec(), vec(),
        ],
        out_specs=[
            pl.BlockSpec((B, D), lambda j: (0, j)),
            pl.BlockSpec((1, LANES), lambda j: (0, 0)),
        ],
        out_shape=[
            jax.ShapeDtypeStruct((B, n * D), F32),
            jax.ShapeDtypeStruct((1, LANES), F32),
        ],
        compiler_params=_cparams(("arbitrary",)),
        name="adaln",
    )(c, ada_w, ada_b.reshape(1, -1), lq1.reshape(1, -1), lk1.reshape(1, -1),
      lq2.reshape(1, -1), lk2.reshape(1, -1))


def _rms_modulate(x, g, shift, scale):
    ms = jnp.mean(x * x, axis=-1, keepdims=True)
    y = x * lax.rsqrt(ms + EPS) * g
    return y * (1.0 + scale) + shift


def _group_rms_scale(q, group_ones):
    ss = _dot((q * q).astype(BF16), group_ones)
    return lax.rsqrt(ss * (1.0 / ATTN_DK) + EPS)


def _in_proj_kernel(qkw, aw, lw, x_ref, mod_ref, g_ref, w_ref, gq_ref, gk_ref, ones_ref,
                    q_ref, k_ref, v_ref, xr_ref, gr_ref):
    h = _rms_modulate(x_ref[...], g_ref[...], mod_ref[0:1, :], mod_ref[1:2, :])
    hb = h.astype(BF16)
    group_ones = ones_ref[...]
    o = 0
    q = _dot(hb, w_ref[:, o:o + qkw]); o += qkw
    q_ref[...] = (q * _group_rms_scale(q, group_ones) * gq_ref[...]).astype(BF16)
    k = _dot(hb, w_ref[:, o:o + qkw]); o += qkw
    k_ref[...] = (k * _group_rms_scale(k, group_ones) * gk_ref[...]).astype(BF16)
    v_ref[...] = _dot(hb, w_ref[:, o:o + aw]).astype(BF16); o += aw
    xr_ref[...] = _dot(hb, w_ref[:, o:o + lw]); o += lw
    gr_ref[...] = _dot(hb, w_ref[:, o:o + lw])


def _in_proj(x2, mod3, norm_g, w_in_b, gq, gk, S, tm, qkw, aw, lw):
    T, D = x2.shape
    tps = S // tm
    grp = jnp.arange(qkw, dtype=I32) // ATTN_DK
    group_ones = (grp[:, None] == grp[None, :]).astype(BF16)
    const = lambda shape: pl.BlockSpec(shape, lambda i: (0,) * len(shape))
    row = lambda w: pl.BlockSpec((tm, w), lambda i: (i, 0))
    return pl.pallas_call(
        functools.partial(_in_proj_kernel, qkw, aw, lw),
        grid=(T // tm,),
        in_specs=[
            row(D),
            pl.BlockSpec((None, 6, D), lambda i: (i // tps, 0, 0)),
            const((1, D)),
            const(w_in_b.shape),
            const((1, qkw)), const((1, qkw)),
            const((qkw, qkw)),
        ],
        out_specs=[row(qkw), row(qkw), row(aw), row(lw), row(lw)],
        out_shape=[
            jax.ShapeDtypeStruct((T, qkw), BF16),
            jax.ShapeDtypeStruct((T, qkw), BF16),
            jax.ShapeDtypeStruct((T, aw), BF16),
            jax.ShapeDtypeStruct((T, lw), F32),
            jax.ShapeDtypeStruct((T, lw), F32),
        ],
        compiler_params=_cparams(("arbitrary",)),
        name="in_proj",
    )(x2, mod3, norm_g, w_in_b, gq, gk, group_ones)


ATTN_ROW_CHUNK = 128


def _attn_kernel(bq, bk, out_scale, q_ref, k_ref, v_ref, lam_ref, g_ref, o_ref, qq_ref, vp_ref, s_ref, m_ref, acc_ref):
    i = pl.program_id(2)
    rc = ATTN_ROW_CHUNK
    n_chunks = 2 * bq // rc

    @pl.when(i == 0)
    def _():
        col = lax.broadcasted_iota(I32, (vp_ref.shape[0], ATTN_DV), 1)
        vp_ref[:, :ATTN_DV] = v_ref[...]
        vp_ref[:, ATTN_DV:] = jnp.where(col == 0, 1.0, 0.0).astype(BF16)

    q = q_ref[...]
    lane = lax.broadcasted_iota(I32, q.shape, 1)
    zero = jnp.zeros_like(q)
    qq_ref[0:bq, :] = jnp.where(lane < ATTN_DK, q, zero)
    qq_ref[bq:, :] = jnp.where(lane >= ATTN_DK, q, zero)
    m_ref[...] = jnp.full(m_ref.shape, NEG_BIG, F32)
    acc_ref[...] = jnp.zeros_like(acc_ref)

    def chunk(r):
        return pl.ds(r * rc, rc)

    def scores(r, start, kw):
        return _dot_tb(qq_ref[chunk(r), :], k_ref[pl.ds(start, kw), :])

    def softmax_pv(s, r, start, diag_offset):
        rows = chunk(r)
        kw = s.shape[1]
        if diag_offset is not None:
            rr = lax.broadcasted_iota(I32, s.shape, 0) + (r * rc) % bq
            cc = lax.broadcasted_iota(I32, s.shape, 1) + diag_offset
            s = jnp.where(cc <= rr, s, NEG_BIG)
        m_old = m_ref[rows, :]
        m_new = jnp.maximum(m_old, jnp.max(s, axis=-1, keepdims=True))
        alpha = jnp.exp2(m_old - m_new)
        p = jnp.exp2(s - jnp.tile(m_new, (1, kw // LANES)))
        pv = _dot(p.astype(BF16), vp_ref[pl.ds(start, kw), :])
        acc_ref[rows, :] = acc_ref[rows, :] * jnp.tile(alpha, (1, 2)) + pv
        m_ref[rows, :] = m_new

    for r in range(n_chunks):
        s_ref[chunk(r), :] = scores(r, 0, bk)

    def body(j, carry):
        start = pl.multiple_of(j * bk, bk)
        for r in range(n_chunks):
            s = s_ref[chunk(r), :]
            s_ref[chunk(r), :] = scores(r, start + bk, bk)
            softmax_pv(s, r, start, None)
        return carry

    n_full = i * (bq // bk)
    lax.fori_loop(0, n_full, body, 0)

    for d in range(bq // bk):
        start = pl.multiple_of((n_full + d) * bk, bk)
        for r in range(n_chunks):
            q0 = (r * rc) % bq
            kw = min(q0 + rc - d * bk, bk)
            if kw <= 0:
                continue
            s = s_ref[chunk(r), 0:kw] if d == 0 else scores(r, start, kw)
            softmax_pv(s, r, start, None if q0 >= (d + 1) * bk else d * bk)

    acc = acc_ref[...]
    o = acc[:, :ATTN_DV] / acc[:, ATTN_DV:ATTN_DV + 1]
    a = o[:bq] - lam_ref[0:1, 0:1] * o[bq:]
    ms = jnp.mean(a * a, axis=-1, keepdims=True)
    o_ref[...] = (a * lax.rsqrt(ms + EPS) * g_ref[...] * out_scale).astype(BF16)


def _attention(q, k, v, lam, subln_g, B, S, bq, bk, out_scale):
    T, qkw = q.shape
    H = qkw // ATTN_DV
    nq = S // bq
    return pl.pallas_call(
        functools.partial(_attn_kernel, bq, bk, out_scale),
        grid=(B, H, nq),
        in_specs=[
            pl.BlockSpec((bq, ATTN_DV), lambda b, h, i: (b * nq + i, h)),
            pl.BlockSpec((S, ATTN_DV), lambda b, h, i: (b, h)),
            pl.BlockSpec((S, ATTN_DV), lambda b, h, i: (b, h)),
            pl.BlockSpec((1, LANES), lambda b, h, i: (0, 0)),
            pl.BlockSpec((1, ATTN_DV), lambda b, h, i: (0, 0)),
        ],
        out_specs=pl.BlockSpec((bq, ATTN_DV), lambda b, h, i: (b * nq + i, h)),
        out_shape=jax.ShapeDtypeStruct((T, H * ATTN_DV), BF16),
        scratch_shapes=[
            pltpu.VMEM((2 * bq, ATTN_DV), BF16),
            pltpu.VMEM((S, 2 * ATTN_DV), BF16),
            pltpu.VMEM((2 * bq, bk), F32),
            pltpu.VMEM((2 * bq, LANES), F32),
            pltpu.VMEM((2 * bq, 2 * ATTN_DV), F32),
        ],
        compiler_params=_cparams(("arbitrary", "arbitrary", "arbitrary")),
        name="attention",
    )(q, k, v, lam, subln_g)


def _shift_rows(x, d, fill):
    row = lax.broadcasted_iota(I32, x.shape, 0)
    return jnp.where(row >= d, pltpu.roll(x, d, 0), fill)


def _lru_kernel(tc, cw, x_ref, gate_ref, cw_ref, cb_ref, wg_ref, bg_ref, lam_ref, og_ref,
                o_ref, ext_ref, a_ref, b_ref, hc_ref):
    c = pl.program_id(1)

    @pl.when(c == 0)
    def _():
        ext_ref[0:SUBLANES, :] = jnp.zeros((SUBLANES, cw), F32)
        hc_ref[...] = jnp.zeros_like(hc_ref)

    ext_ref[SUBLANES:SUBLANES + tc, :] = x_ref[...]
    xc = cb_ref[...] + cw_ref[CONV_W - 1:CONV_W, :] * x_ref[...]
    for w in range(CONV_W - 1):
        sh = CONV_W - 1 - w
        xc = xc + cw_ref[w:w + 1, :] * ext_ref[SUBLANES - sh:SUBLANES - sh + tc, :]
    tail = ext_ref[tc:tc + SUBLANES, :]
    ext_ref[0:SUBLANES, :] = tail

    g = _dot(xc.astype(BF16), wg_ref[...]) + bg_ref[...]
    r = jax.nn.sigmoid(g[:, :cw])
    ig = jax.nn.sigmoid(g[:, cw:])
    nl = -lam_ref[...]
    softplus = jnp.maximum(nl, 0.0) + jnp.log1p(jnp.exp(-jnp.abs(nl)))
    log_a = (-LRU_C) * r * softplus
    th = jnp.tanh(log_a)
    a_ref[...] = jnp.exp(log_a)
    b_ref[...] = jnp.sqrt(-2.0 * th / (1.0 - th)) * (ig * xc)

    def tile_scan(n, hc):
        rows = pl.ds(pl.multiple_of(n * SUBLANES, SUBLANES), SUBLANES)
        a = a_ref[rows, :]
        b = b_ref[rows, :]
        for d in (1, 2, 4):
            b = a * _shift_rows(b, d, 0.0) + b
            a = a * _shift_rows(a, d, 1.0)
        h = b + a * hc
        b_ref[rows, :] = h
        return jnp.broadcast_to(h[SUBLANES - 1:SUBLANES, :], h.shape)

    hc_ref[...] = lax.fori_loop(0, tc // SUBLANES, tile_scan, hc_ref[...])

    gt = gate_ref[...]
    gelu = 0.5 * gt * (1.0 + jnp.tanh(math.sqrt(2.0 / math.pi) * (gt + 0.044715 * gt * gt * gt)))
    y = b_ref[...] * gelu
    ms = jnp.mean(y * y, axis=-1, keepdims=True)
    o_ref[...] = (y * lax.rsqrt(ms + EPS) * og_ref[...]).astype(BF16)


def _lru(xr, gr, conv_w, conv_b, w_gates, b_gates, lru_lambda, out_g, B, S, tc):
    T, cw = xr.shape
    nc = S // tc
    const = lambda shape: pl.BlockSpec(shape, lambda b, c: (0,) * len(shape))
    row = pl.BlockSpec((tc, cw), lambda b, c: (b * nc + c, 0))
    return pl.pallas_call(
        functools.partial(_lru_kernel, tc, cw),
        grid=(B, nc),
        in_specs=[row, row, const((CONV_W, cw)), const((1, cw)), const((cw, 2 * cw)),
                  const((1, 2 * cw)), const((1, cw)), const((1, cw))],
        out_specs=row,
        out_shape=jax.ShapeDtypeStruct((T, cw), BF16),
        scratch_shapes=[
            pltpu.VMEM((tc + SUBLANES, cw), F32),
            pltpu.VMEM((tc, cw), F32),
            pltpu.VMEM((tc, cw), F32),
            pltpu.VMEM((SUBLANES, cw), F32),
        ],
        compiler_params=_cparams(("arbitrary", "arbitrary")),
        name="lru",
    )(xr, gr, conv_w, conv_b, w_gates, b_gates, lru_lambda, out_g)


def _out_proj_kernel(tm, aw, attn_ref, lru_ref, x_ref, mod_ref, w_ref, g_ref, rw_ref, rb_ref, tri_ref,
                     x1_ref, h2_ref, ti_ref, gt_ref, rk_ref, cnt_ref, run_ref):
    i = pl.program_id(0)

    @pl.when(i == 0)
    def _():
        run_ref[...] = jnp.zeros_like(run_ref)

    mix = _dot(attn_ref[...], w_ref[0:aw, :]) + _dot(lru_ref[...], w_ref[aw:, :])
    x1 = x_ref[...] + mod_ref[2:3, :] * mix
    x1_ref[...] = x1
    h2 = _rms_modulate(x1, g_ref[...], mod_ref[3:4, :], mod_ref[4:5, :])
    for j in range(h2.shape[1] // LANES):
        h2_ref[pl.ds(j, tm, stride=SUBLANES), :] = h2[:, j * LANES:(j + 1) * LANES]

    h_hi, h_lo = _split_hi_lo(h2)
    w_hi, w_lo = _split_hi_lo(rw_ref[...])
    lg = _dot_tb(w_hi, h_hi) + _dot_tb(w_hi, h_lo) + _dot_tb(w_lo, h_hi) + rb_ref[...]

    eidx = lax.broadcasted_iota(I32, lg.shape, 0)
    picked = jnp.zeros(lg.shape, F32)
    vals, idxs = [], []
    for _ in range(TOP_K):
        m = jnp.max(lg, axis=0, keepdims=True)
        idx = jnp.min(jnp.where(lg == m, eidx, N_EXPERTS), axis=0, keepdims=True)
        sel = eidx == idx
        lg = jnp.where(sel, -jnp.inf, lg)
        picked = picked + sel.astype(F32)
        vals.append(m)
        idxs.append(idx)
    es = [jnp.exp(v - vals[0]) for v in vals]
    den = es[0] + es[1] + es[2] + es[3]
    gt_ref[...] = jnp.concatenate([e / den for e in es], axis=0)
    ti_ref[...] = jnp.concatenate(idxs, axis=0)

    before = _dot(picked.astype(BF16), tri_ref[...]) + run_ref[:, 0:1]
    ranks = [jnp.sum(jnp.where(eidx == idx, before, 0.0), axis=0, keepdims=True) for idx in idxs]
    rk_ref[...] = jnp.concatenate(ranks, axis=0).astype(I32)
    run = run_ref[...] + jnp.sum(picked, axis=1, keepdims=True)
    run_ref[...] = run
    cnt_ref[...] = run.astype(I32)


def _out_proj(attn, lru, x2, mod3, w_out_b, norm_g, router_wt, router_b, S, tm):
    T, D = x2.shape
    aw = attn.shape[1]
    lw = lru.shape[1]
    tps = S // tm
    tri = (jnp.arange(tm, dtype=I32)[:, None] < jnp.arange(tm, dtype=I32)[None, :]).astype(BF16)
    const = lambda shape: pl.BlockSpec(shape, lambda i: (0,) * len(shape))
    row = lambda w: pl.BlockSpec((tm, w), lambda i: (i, 0))
    col = pl.BlockSpec((TOP_K, tm), lambda i: (0, i))
    return pl.pallas_call(
        functools.partial(_out_proj_kernel, tm, aw),
        grid=(T // tm,),
        in_specs=[
            row(aw), row(lw), row(D),
            pl.BlockSpec((None, 6, D), lambda i: (i // tps, 0, 0)),
            const((D, D)), const((1, D)), const((N_EXPERTS, D)), const((N_EXPERTS, 1)),
            const((tm, tm)),
        ],
        out_specs=[
            row(D),
            pl.BlockSpec((tm * SUBLANES, LANES), lambda i: (i, 0)),
            col, col, col,
            const((N_EXPERTS, LANES)),
        ],
        out_shape=[
            jax.ShapeDtypeStruct((T, D), F32),
            jax.ShapeDtypeStruct((T * SUBLANES, LANES), F32),
            jax.ShapeDtypeStruct((TOP_K, T), I32),
            jax.ShapeDtypeStruct((TOP_K, T), F32),
            jax.ShapeDtypeStruct((TOP_K, T), I32),
            jax.ShapeDtypeStruct((N_EXPERTS, LANES), I32),
        ],
        scratch_shapes=[pltpu.VMEM((N_EXPERTS, LANES), F32)],
        compiler_params=_cparams(("arbitrary",)),
        name="out_proj",
    )(attn, lru, x2, mod3, w_out_b, norm_g, router_wt, router_b, tri)


def _slots_kernel(ps_ref, ti_ref, rk_ref, o_ref):
    ti = ti_ref[...]
    base = jnp.zeros(ti.shape, I32)
    for e in range(N_EXPERTS):
        base = jnp.where(ti == e, ps_ref[e], base)
    o_ref[...] = base + rk_ref[...]


def _slots(pad_starts, top_i, rank, t):
    T = top_i.shape[1]
    col = pl.BlockSpec((TOP_K, t), lambda i, ps: (0, i))
    grid_spec = pltpu.PrefetchScalarGridSpec(
        num_scalar_prefetch=1,
        grid=(T // t,),
        in_specs=[col, col],
        out_specs=pl.BlockSpec((None, TOP_K, t), lambda i, ps: (i, 0, 0)),
    )
    return pl.pallas_call(
        _slots_kernel,
        grid_spec=grid_spec,
        out_shape=jax.ShapeDtypeStruct((T // t, TOP_K, t), I32),
        compiler_params=_cparams(("arbitrary",)),
        name="slots",
    )(pad_starts, top_i, rank)


def _row_tile(ref, r):
    return ref.at[pl.ds(pl.multiple_of(r * SUBLANES, SUBLANES), SUBLANES), :]


def _fetch_indices(idx_hbm, idx_smem, isem):
    i = pl.program_id(0)
    n = pl.num_programs(0)
    slot = i % 2

    def copy(tile, s):
        return pltpu.make_async_copy(idx_hbm.at[tile], idx_smem.at[s], isem.at[s])

    @pl.when(i == 0)
    def _():
        copy(0, 0).start()

    copy(i, slot).wait()

    @pl.when(i + 1 < n)
    def _():
        copy(i + 1, 1 - slot).start()

    return idx_smem.at[slot]


def _dispatch_kernel(t, dest_hbm, h_ref, xs_hbm, idx_smem, isem, sem):
    idx = _fetch_indices(dest_hbm, idx_smem, isem)

    def copy(r, k):
        return pltpu.make_async_copy(_row_tile(h_ref, r), _row_tile(xs_hbm, idx[k, r]), sem)

    def issue(r, _):
        for k in range(TOP_K):
            copy(r, k).start()
        return 0

    def drain(r, _):
        for k in range(TOP_K):
            copy(r, k).wait()
        return 0

    lax.fori_loop(0, t, issue, 0)
    lax.fori_loop(0, t, drain, 0)


def _dispatch(dest_tiles, h2t, n_slots):
    n_tiles, _, t = dest_tiles.shape
    return pl.pallas_call(
        functools.partial(_dispatch_kernel, t),
        grid=(n_tiles,),
        in_specs=[pl.BlockSpec(memory_space=pl.ANY),
                  pl.BlockSpec((t * SUBLANES, LANES), lambda i: (i, 0))],
        out_specs=pl.BlockSpec(memory_space=pl.ANY),
        out_shape=jax.ShapeDtypeStruct((n_slots * SUBLANES, LANES), F32),
        scratch_shapes=[pltpu.SMEM((2, TOP_K, t), I32), pltpu.SemaphoreType.DMA((2,)), pltpu.SemaphoreType.DMA],
        compiler_params=_cparams(("arbitrary",)),
        name="dispatch",
    )(dest_tiles, h2t)


def _experts_kernel(ff, be_ref, nv_ref, xs_ref, wgu_ref, bgu_ref, wdn_ref, bdn_ref, ys_ref):
    i = pl.program_id(0)
    nv = nv_ref[i]
    n_slabs = xs_ref.shape[0] // MOE_BLOCK

    @pl.when(nv > 0)
    def _():
        x = jnp.concatenate(
            [xs_ref[pl.ds(j, MOE_BLOCK, stride=SUBLANES), :] for j in range(n_slabs)], axis=1)
        x = jnp.where(lax.broadcasted_iota(I32, x.shape, 0) < nv, x, 0.0)
        gu = _dot(x.astype(BF16), wgu_ref[...]) + bgu_ref[...]
        gate = jnp.minimum(gu[:, :ff], SWIGLU_LIMIT)
        up = jnp.clip(gu[:, ff:], -SWIGLU_LIMIT, SWIGLU_LIMIT)
        act = (up + 1.0) * (gate * jax.nn.sigmoid(SWIGLU_ALPHA * gate))
        y = _dot(act.astype(BF16), wdn_ref[...]) + bdn_ref[...]
        for j in range(n_slabs):
            ys_ref[pl.ds(j, MOE_BLOCK, stride=SUBLANES), :] = y[:, j * LANES:(j + 1) * LANES]

    @pl.when(nv == 0)
    def _():
        ys_ref[...] = jnp.zeros_like(ys_ref)


def _experts(block_expert, block_valid, xs, w_gu_b, b_gu, w_dn_b, b_dn):
    n_blocks = block_expert.shape[0]
    E, D, ff2 = w_gu_b.shape
    ff = ff2 // 2
    rows = pl.BlockSpec((MOE_BLOCK * SUBLANES, LANES), lambda i, be, nv: (i, 0))
    grid_spec = pltpu.PrefetchScalarGridSpec(
        num_scalar_prefetch=2,
        grid=(n_blocks,),
        in_specs=[
            rows,
            pl.BlockSpec((None, D, ff2), lambda i, be, nv: (be[i], 0, 0)),
            pl.BlockSpec((None, 1, ff2), lambda i, be, nv: (be[i], 0, 0)),
            pl.BlockSpec((None, ff, D), lambda i, be, nv: (be[i], 0, 0)),
            pl.BlockSpec((None, 1, D), lambda i, be, nv: (be[i], 0, 0)),
        ],
        out_specs=rows,
    )
    return pl.pallas_call(
        functools.partial(_experts_kernel, ff),
        grid_spec=grid_spec,
        out_shape=jax.ShapeDtypeStruct(xs.shape, F32),
        compiler_params=_cparams(("arbitrary",)),
        name="experts",
    )(block_expert, block_valid, xs, w_gu_b, b_gu.reshape(E, 1, ff2), w_dn_b, b_dn.reshape(E, 1, D))


def _combine_kernel(tmc, dest_hbm, ys_hbm, gates_ref, x1_ref, mod_ref, o_ref, idx_smem, buf_ref, isem, sem):
    idx = _fetch_indices(dest_hbm, idx_smem, isem)

    def copy(r, k):
        return pltpu.make_async_copy(_row_tile(ys_hbm, idx[k, r]), _row_tile(buf_ref, k * tmc + r), sem)

    def issue(r, _):
        for k in range(TOP_K):
            copy(r, k).start()
        return 0

    def drain(r, _):
        for k in range(TOP_K):
            copy(r, k).wait()
        return 0

    lax.fori_loop(0, tmc, issue, 0)
    lax.fori_loop(0, tmc, drain, 0)

    n_slabs = o_ref.shape[1] // LANES
    y = jnp.zeros(o_ref.shape, F32)
    for k in range(TOP_K):
        rows = jnp.concatenate(
            [buf_ref[pl.ds(k * tmc * SUBLANES + j, tmc, stride=SUBLANES), :] for j in range(n_slabs)], axis=1)
        y = y + gates_ref[:, k:k + 1] * rows
    o_ref[...] = x1_ref[...] + mod_ref[5:6, :] * y


def _combine(dest_tiles, ys, gates_t, x1, mod3, S, tmc):
    T, D = x1.shape
    tps = S // tmc
    return pl.pallas_call(
        functools.partial(_combine_kernel, tmc),
        grid=(T // tmc,),
        in_specs=[
            pl.BlockSpec(memory_space=pl.ANY),
            pl.BlockSpec(memory_space=pl.ANY),
            pl.BlockSpec((tmc, TOP_K), lambda i: (i, 0)),
            pl.BlockSpec((tmc, D), lambda i: (i, 0)),
            pl.BlockSpec((None, 6, D), lambda i: (i // tps, 0, 0)),
        ],
        out_specs=pl.BlockSpec((tmc, D), lambda i: (i, 0)),
        out_shape=jax.ShapeDtypeStruct((T, D), F32),
        scratch_shapes=[
            pltpu.SMEM((2, TOP_K, tmc), I32),
            pltpu.VMEM((TOP_K * tmc * SUBLANES, LANES), F32),
            pltpu.SemaphoreType.DMA((2,)),
            pltpu.SemaphoreType.DMA,
        ],
        compiler_params=_cparams(("arbitrary",)),
        name="combine",
    )(dest_tiles, ys, gates_t, x1, mod3)


def _tile(n, target):
    t = min(n, target)
    while n % t:
        t //= 2
    return t


def _block_diag(w):
    n, bw, _ = w.shape
    eye = jnp.eye(n, dtype=w.dtype)
    return (eye[:, None, :, None] * w[:, :, None, :]).reshape(n * bw, n * bw)


def _layer(l, x2, B, S, c, ada_w, ada_b, norm1_g, w_in, q_norm_g, k_norm_g, lambda_q1, lambda_k1, lambda_q2,
           lambda_k2, attn_subln_g, conv_w, conv_b, lru_wa, lru_ba, lru_wx, lru_bx, lru_lambda, lru_out_g,
           w_out, norm2_g, router_w, router_b, w_gate_up, b_gate_up, w_down, b_down):
    T, D = x2.shape
    lam_init = 0.8 - 0.6 * math.exp(-0.3 * l)
    aw = D // 2
    lw = D - aw
    heads = aw // ATTN_DV
    qkw = heads * 2 * ATTN_DK

    mod, lam = _adaln(c, ada_w, ada_b, lambda_q1, lambda_k1, lambda_q2, lambda_k2, lam_init)
    mod3 = mod.reshape(c.shape[0], 6, D)

    reps = qkw // ATTN_DK
    gq = (jnp.tile(q_norm_g, reps) * (ATTN_DK ** -0.5 * math.log2(math.e))).reshape(1, qkw)
    gk = jnp.tile(k_norm_g, reps).reshape(1, qkw)
    tm = _tile(S, 512)
    q, k, v, xr, gr = _in_proj(x2, mod3, norm1_g.reshape(1, D), w_in.astype(BF16), gq, gk, S, _tile(S, 1024),
                               qkw, aw, lw)

    attn = _attention(q, k, v, lam, attn_subln_g.reshape(1, ATTN_DV), B, S, _tile(S, 1024), _tile(S, 512),
                      1.0 - lam_init)

    w_gates = jnp.concatenate([_block_diag(lru_wa), _block_diag(lru_wx)], axis=1).astype(BF16)
    b_gates = jnp.concatenate([lru_ba, lru_bx]).reshape(1, 2 * lw)
    lru = _lru(xr, gr, conv_w, conv_b.reshape(1, lw), w_gates, b_gates, lru_lambda.reshape(1, lw),
               lru_out_g.reshape(1, lw), B, S, _tile(S, 512))

    x1, h2t, top_i, gates, rank, counts = _out_proj(
        attn, lru, x2, mod3, w_out.astype(BF16), norm2_g.reshape(1, D), router_w.T,
        router_b.reshape(N_EXPERTS, 1), S, tm)

    counts = counts[:, 0]
    padded = ((counts + MOE_BLOCK - 1) // MOE_BLOCK) * MOE_BLOCK
    pad_ends = jnp.cumsum(padded)
    pad_starts = pad_ends - padded
    n_blocks = (T * TOP_K) // MOE_BLOCK + N_EXPERTS
    blk_start = jnp.arange(n_blocks, dtype=I32) * MOE_BLOCK
    owner = blk_start[:, None] >= pad_ends[None, :]
    block_expert = jnp.minimum(jnp.sum(owner, axis=1), N_EXPERTS - 1).astype(I32)
    onehot = block_expert[:, None] == jnp.arange(N_EXPERTS, dtype=I32)[None, :]
    row_end = jnp.sum(jnp.where(onehot, (pad_starts + counts)[None, :], 0), axis=1)
    block_valid = jnp.clip(row_end - blk_start, 0, MOE_BLOCK).astype(I32)

    tmc = _tile(S, 256)
    dest_tiles = _slots(pad_starts.astype(I32), top_i, rank, tmc)
    xs = _dispatch(dest_tiles, h2t, n_blocks * MOE_BLOCK)
    ys = _experts(block_expert, block_valid, xs, w_gate_up.astype(BF16), b_gate_up,
                  w_down.astype(BF16), b_down)
    return _combine(dest_tiles, ys, gates.T, x1, mod3, S, tmc)


def kernel(x, c, ada_w, ada_b, norm1_g, w_in, q_norm_g, k_norm_g, lambda_q1, lambda_k1, lambda_q2, lambda_k2,
           attn_subln_g, conv_w, conv_b, lru_wa, lru_ba, lru_wx, lru_bx, lru_lambda, lru_out_g, w_out, norm2_g,
           router_w, router_b, w_gate_up, b_gate_up, w_down, b_down):
    B, S, D = x.shape
    params = (ada_w, ada_b, norm1_g, w_in, q_norm_g, k_norm_g, lambda_q1, lambda_k1, lambda_q2, lambda_k2,
              attn_subln_g, conv_w, conv_b, lru_wa, lru_ba, lru_wx, lru_bx, lru_lambda, lru_out_g, w_out,
              norm2_g, router_w, router_b, w_gate_up, b_gate_up, w_down, b_down)
    x2 = x.reshape(B * S, D)
    for l in range(ada_w.shape[0]):
        x2 = _layer(l, x2, B, S, c, *[p[l] for p in params])
    return x2.reshape(B, S, D)
```

```python
import functools
import math

import jax
import jax.numpy as jnp
from jax import lax
from jax.experimental import pallas as pl
from jax.experimental.pallas import tpu as pltpu

F32 = jnp.float32
BF16 = jnp.bfloat16
I32 = jnp.int32

ATTN_DK = 64
ATTN_DV = 2 * ATTN_DK
CONV_W = 4
LRU_C = 8.0
N_EXPERTS = 32
TOP_K = 4
SWIGLU_LIMIT = 7.0
SWIGLU_ALPHA = 1.702
MOE_BLOCK = 512
EPS = 1e-6
NEG_BIG = -1e30

LANES = 128
SUBLANES = 8
VMEM_LIMIT = 56 * 1024 * 1024


def _cparams(sem):
    return pltpu.CompilerParams(dimension_semantics=sem, vmem_limit_bytes=VMEM_LIMIT)


def _split_hi_lo(x):
    hi = x.astype(BF16)
    lo = (x - hi.astype(F32)).astype(BF16)
    return hi, lo


def _sigmoid(x):
    return 0.5 * jnp.tanh(0.5 * x) + 0.5


def _dot(a, b):
    return jnp.dot(a, b, preferred_element_type=F32)


def _dot_tb(a, b):
    return lax.dot_general(a, b, (((1,), (1,)), ((), ())), preferred_element_type=F32)


def _adaln_kernel(lam_init, c_ref, w_ref, b_ref, lq1, lk1, lq2, lk2, mod_ref, lam_ref):
    c = c_ref[...]
    s = c * jax.nn.sigmoid(c)
    s_hi, s_lo = _split_hi_lo(s)
    w = w_ref[...]
    w_hi, w_lo = _split_hi_lo(w)
    mod_ref[...] = _dot(s_hi, w_hi) + _dot(s_hi, w_lo) + _dot(s_lo, w_hi) + b_ref[...]
    d1 = jnp.sum(lq1[...] * lk1[...], axis=-1, keepdims=True)
    d2 = jnp.sum(lq2[...] * lk2[...], axis=-1, keepdims=True)
    lam = jnp.exp(d1) - jnp.exp(d2) + lam_init
    lam_ref[...] = jnp.broadcast_to(lam, lam_ref.shape)


def _adaln(c, ada_w, ada_b, lq1, lk1, lq2, lk2, lam_init):
    B, D = c.shape
    n = ada_w.shape[1] // D
    vec = lambda: pl.BlockSpec((1, ATTN_DK), lambda j: (0, 0))
    return pl.pallas_call(
        functools.partial(_adaln_kernel, lam_init),
        grid=(n,),
        in_specs=[
            pl.BlockSpec((B, D), lambda j: (0, 0)),
            pl.BlockSpec((D, D), lambda j: (0, j)),
            pl.BlockSpec((1, D), lambda j: (0, j)),
            vec(), vec(), vec(), vec(),
        ],
        out_specs=[
            pl.BlockSpec((B, D), lambda j: (0, j)),
            pl.BlockSpec((1, LANES), lambda j: (0, 0)),
        ],
        out_shape=[
            jax.ShapeDtypeStruct((B, n * D), F32),
            jax.ShapeDtypeStruct((1, LANES), F32),
        ],
        compiler_params=_cparams(("arbitrary",)),
        name="adaln",
    )(c, ada_w, ada_b.reshape(1, -1), lq1.reshape(1, -1), lk1.reshape(1, -1),
      lq2.reshape(1, -1), lk2.reshape(1, -1))


def _rms_modulate(x, g, shift, scale):
    ms = jnp.mean(x * x, axis=-1, keepdims=True)
    y = x * lax.rsqrt(ms + EPS) * g
    return y * (1.0 + scale) + shift


def _group_rms_scale(q, group_ones):
    ss = _dot((q * q).astype(BF16), group_ones)
    return lax.rsqrt(ss * (1.0 / ATTN_DK) + EPS)


def _in_proj_kernel(qkw, aw, lw, x_ref, mod_ref, g_ref, w_ref, gq_ref, gk_ref, ones_ref,
                    q_ref, k_ref, v_ref, xr_ref, gr_ref):
    h = _rms_modulate(x_ref[...], g_ref[...], mod_ref[0:1, :], mod_ref[1:2, :])
    hb = h.astype(BF16)
    group_ones = ones_ref[...]
    o = 0
    q = _dot(hb, w_ref[:, o:o + qkw]); o += qkw
    q_ref[...] = (q * _group_rms_scale(q, group_ones) * gq_ref[...]).astype(BF16)
    k = _dot(hb, w_ref[:, o:o + qkw]); o += qkw
    k_ref[...] = (k * _group_rms_scale(k, group_ones) * gk_ref[...]).astype(BF16)
    v_ref[...] = _dot(hb, w_ref[:, o:o + aw]).astype(BF16); o += aw
    xr_ref[...] = _dot(hb, w_ref[:, o:o + lw]); o += lw
    gr_ref[...] = _dot(hb, w_ref[:, o:o + lw])


def _in_proj(x2, mod3, norm_g, w_in_b, gq, gk, S, tm, qkw, aw, lw):
    T, D = x2.shape
    tps = S // tm
    grp = jnp.arange(qkw, dtype=I32) // ATTN_DK
    group_ones = (grp[:, None] == grp[None, :]).astype(BF16)
    const = lambda shape: pl.BlockSpec(shape, lambda i: (0,) * len(shape))
    row = lambda w: pl.BlockSpec((tm, w), lambda i: (i, 0))
    return pl.pallas_call(
        functools.partial(_in_proj_kernel, qkw, aw, lw),
        grid=(T // tm,),
        in_specs=[
            row(D),
            pl.BlockSpec((None, 6, D), lambda i: (i // tps, 0, 0)),
            const((1, D)),
            const(w_in_b.shape),
            const((1, qkw)), const((1, qkw)),
            const((qkw, qkw)),
        ],
        out_specs=[row(qkw), row(qkw), row(aw), row(lw), row(lw)],
        out_shape=[
            jax.ShapeDtypeStruct((T, qkw), BF16),
            jax.ShapeDtypeStruct((T, qkw), BF16),
            jax.ShapeDtypeStruct((T, aw), BF16),
            jax.ShapeDtypeStruct((T, lw), F32),
            jax.ShapeDtypeStruct((T, lw), F32),
        ],
        compiler_params=_cparams(("arbitrary",)),
        name="in_proj",
    )(x2, mod3, norm_g, w_in_b, gq, gk, group_ones)


ATTN_ROW_CHUNK = 128


def _attn_kernel(bq, bk, out_scale, q_ref, k_ref, v_ref, lam_ref, g_ref, o_ref, qq_ref, vp_ref, s_ref, m_ref, acc_ref):
    i = pl.program_id(2)
    rc = ATTN_ROW_CHUNK
    n_chunks = 2 * bq // rc

    @pl.when(i == 0)
    def _():
        col = lax.broadcasted_iota(I32, (vp_ref.shape[0], ATTN_DV), 1)
        vp_ref[:, :ATTN_DV] = v_ref[...]
        vp_ref[:, ATTN_DV:] = jnp.where(col == 0, 1.0, 0.0).astype(BF16)

    q = q_ref[...]
    lane = lax.broadcasted_iota(I32, q.shape, 1)
    zero = jnp.zeros_like(q)
    qq_ref[0:bq, :] = jnp.where(lane < ATTN_DK, q, zero)
    qq_ref[bq:, :] = jnp.where(lane >= ATTN_DK, q, zero)
    m_ref[...] = jnp.full(m_ref.shape, NEG_BIG, F32)
    acc_ref[...] = jnp.zeros_like(acc_ref)

    def chunk(r):
        return pl.ds(r * rc, rc)

    def scores(r, start, kw):
        return _dot_tb(qq_ref[chunk(r), :], k_ref[pl.ds(start, kw), :])

    def softmax_pv(s, r, start, diag_offset):
        rows = chunk(r)
        kw = s.shape[1]
        if diag_offset is not None:
            rr = lax.broadcasted_iota(I32, s.shape, 0) + (r * rc) % bq
            cc = lax.broadcasted_iota(I32, s.shape, 1) + diag_offset
            s = jnp.where(cc <= rr, s, NEG_BIG)
        m_old = m_ref[rows, :]
        m_new = jnp.maximum(m_old, jnp.max(s, axis=-1, keepdims=True))
        alpha = jnp.exp2(m_old - m_new)
        p = jnp.exp2(s - jnp.tile(m_new, (1, kw // LANES)))
        pv = _dot(p.astype(BF16), vp_ref[pl.ds(start, kw), :])
        acc_ref[rows, :] = acc_ref[rows, :] * jnp.tile(alpha, (1, 2)) + pv
        m_ref[rows, :] = m_new

    for r in range(n_chunks):
        s_ref[chunk(r), :] = scores(r, 0, bk)

    def body(j, carry):
        start = pl.multiple_of(j * bk, bk)
        for r in range(n_chunks):
            s = s_ref[chunk(r), :]
            s_ref[chunk(r), :] = scores(r, start + bk, bk)
            softmax_pv(s, r, start, None)
        return carry

    n_full = i * (bq // bk)
    lax.fori_loop(0, n_full, body, 0)

    for d in range(bq // bk):
        start = pl.multiple_of((n_full + d) * bk, bk)
        for r in range(n_chunks):
            q0 = (r * rc) % bq
            kw = min(q0 + rc - d * bk, bk)
            if kw <= 0:
                continue
            s = s_ref[chunk(r), 0:kw] if d == 0 else scores(r, start, kw)
            softmax_pv(s, r, start, None if q0 >= (d + 1) * bk else d * bk)

    acc = acc_ref[...]
    o = acc[:, :ATTN_DV] / acc[:, ATTN_DV:ATTN_DV + 1]
    a = o[:bq] - lam_ref[0:1, 0:1] * o[bq:]
    ms = jnp.mean(a * a, axis=-1, keepdims=True)
    o_ref[...] = (a * lax.rsqrt(ms + EPS) * g_ref[...] * out_scale).astype(BF16)


def _attention(q, k, v, lam, subln_g, B, S, bq, bk, out_scale):
    T, qkw = q.shape
    H = qkw // ATTN_DV
    nq = S // bq
    return pl.pallas_call(
        functools.partial(_attn_kernel, bq, bk, out_scale),
        grid=(B, H, nq),
        in_specs=[
            pl.BlockSpec((bq, ATTN_DV), lambda b, h, i: (b * nq + i, h)),
            pl.BlockSpec((S, ATTN_DV), lambda b, h, i: (b, h)),
            pl.BlockSpec((S, ATTN_DV), lambda b, h, i: (b, h)),
            pl.BlockSpec((1, LANES), lambda b, h, i: (0, 0)),
            pl.BlockSpec((1, ATTN_DV), lambda b, h, i: (0, 0)),
        ],
        out_specs=pl.BlockSpec((bq, ATTN_DV), lambda b, h, i: (b * nq + i, h)),
        out_shape=jax.ShapeDtypeStruct((T, H * ATTN_DV), BF16),
        scratch_shapes=[
            pltpu.VMEM((2 * bq, ATTN_DV), BF16),
            pltpu.VMEM((S, 2 * ATTN_DV), BF16),
            pltpu.VMEM((2 * bq, bk), F32),
            pltpu.VMEM((2 * bq, LANES), F32),
            pltpu.VMEM((2 * bq, 2 * ATTN_DV), F32),
        ],
        compiler_params=_cparams(("arbitrary", "arbitrary", "arbitrary")),
        name="attention",
    )(q, k, v, lam, subln_g)


def _lru_kernel(tc, cw, x_ref, gate_ref, cw_ref, cb_ref, wg_ref, bg_ref, lam_ref, og_ref,
                o_ref, ext_ref, a_ref, b_ref, hc_ref):
    c = pl.program_id(1)

    @pl.when(c == 0)
    def _():
        ext_ref[0:SUBLANES, :] = jnp.zeros((SUBLANES, cw), F32)
        hc_ref[...] = jnp.zeros_like(hc_ref)

    ext_ref[SUBLANES:SUBLANES + tc, :] = x_ref[...]
    xc = cb_ref[...] + cw_ref[CONV_W - 1:CONV_W, :] * x_ref[...]
    for w in range(CONV_W - 1):
        sh = CONV_W - 1 - w
        xc = xc + cw_ref[w:w + 1, :] * ext_ref[SUBLANES - sh:SUBLANES - sh + tc, :]
    tail = ext_ref[tc:tc + SUBLANES, :]
    ext_ref[0:SUBLANES, :] = tail

    g = _dot(xc.astype(BF16), wg_ref[...]) + bg_ref[...]
    r = _sigmoid(g[:, :cw])
    ig = _sigmoid(g[:, cw:])
    nl = -lam_ref[...]
    softplus = jnp.maximum(nl, 0.0) + jnp.log1p(jnp.exp(-jnp.abs(nl)))
    log_a = (-LRU_C) * r * softplus
    th = jnp.tanh(log_a)
    a_ref[...] = jnp.exp(log_a)
    b_ref[...] = jnp.sqrt(-2.0 * th / (1.0 - th)) * (ig * xc)

    row = lax.broadcasted_iota(I32, (SUBLANES, cw), 0)
    has_src = {d: row >= d for d in (1, 2, 4)}

    def tile_scan(n, hc):
        rows = pl.ds(pl.multiple_of(n * SUBLANES, SUBLANES), SUBLANES)
        a = a_ref[rows, :]
        b = b_ref[rows, :]
        for d in (1, 2, 4):
            b = jnp.where(has_src[d], a * pltpu.roll(b, d, 0), 0.0) + b
            a = jnp.where(has_src[d], a * pltpu.roll(a, d, 0), a)
        h = b + a * hc
        b_ref[rows, :] = h
        return jnp.broadcast_to(h[SUBLANES - 1:SUBLANES, :], h.shape)

    hc_ref[...] = lax.fori_loop(0, tc // SUBLANES, tile_scan, hc_ref[...], unroll=4)

    gt = gate_ref[...]
    gelu = 0.5 * gt * (1.0 + jnp.tanh(math.sqrt(2.0 / math.pi) * (gt + 0.044715 * gt * gt * gt)))
    y = b_ref[...] * gelu
    ms = jnp.mean(y * y, axis=-1, keepdims=True)
    o_ref[...] = (y * lax.rsqrt(ms + EPS) * og_ref[...]).astype(BF16)


def _lru(xr, gr, conv_w, conv_b, w_gates, b_gates, lru_lambda, out_g, B, S, tc):
    T, cw = xr.shape
    nc = S // tc
    const = lambda shape: pl.BlockSpec(shape, lambda b, c: (0,) * len(shape))
    row = pl.BlockSpec((tc, cw), lambda b, c: (b * nc + c, 0))
    return pl.pallas_call(
        functools.partial(_lru_kernel, tc, cw),
        grid=(B, nc),
        in_specs=[row, row, const((CONV_W, cw)), const((1, cw)), const((cw, 2 * cw)),
                  const((1, 2 * cw)), const((1, cw)), const((1, cw))],
        out_specs=row,
        out_shape=jax.ShapeDtypeStruct((T, cw), BF16),
        scratch_shapes=[
            pltpu.VMEM((tc + SUBLANES, cw), F32),
            pltpu.VMEM((tc, cw), F32),
            pltpu.VMEM((tc, cw), F32),
            pltpu.VMEM((SUBLANES, cw), F32),
        ],
        compiler_params=_cparams(("arbitrary", "arbitrary")),
        name="lru",
    )(xr, gr, conv_w, conv_b, w_gates, b_gates, lru_lambda, out_g)


def _out_proj_kernel(tm, aw, attn_ref, lru_ref, x_ref, mod_ref, w_ref, g_ref, rw_ref, rb_ref, tri_ref,
                     x1_ref, h2_ref, ti_ref, gt_ref, rk_ref, cnt_ref):
    mix = _dot(attn_ref[...], w_ref[0:aw, :]) + _dot(lru_ref[...], w_ref[aw:, :])
    x1 = x_ref[...] + mod_ref[2:3, :] * mix
    x1_ref[...] = x1
    h2 = _rms_modulate(x1, g_ref[...], mod_ref[3:4, :], mod_ref[4:5, :])
    for j in range(h2.shape[1] // LANES):
        h2_ref[pl.ds(j, tm, stride=SUBLANES), :] = h2[:, j * LANES:(j + 1) * LANES]

    h_hi, h_lo = _split_hi_lo(h2)
    w_hi, w_lo = _split_hi_lo(rw_ref[...])
    lg = _dot_tb(w_hi, h_hi) + _dot_tb(w_hi, h_lo) + _dot_tb(w_lo, h_hi) + rb_ref[...]

    eidx = lax.broadcasted_iota(I32, lg.shape, 0)
    picked = jnp.zeros(lg.shape, F32)
    vals, idxs = [], []
    for _ in range(TOP_K):
        m = jnp.max(lg, axis=0, keepdims=True)
        idx = jnp.min(jnp.where(lg == m, eidx, N_EXPERTS), axis=0, keepdims=True)
        sel = eidx == idx
        lg = jnp.where(sel, -jnp.inf, lg)
        picked = picked + sel.astype(F32)
        vals.append(m)
        idxs.append(idx)
    es = [jnp.exp(v - vals[0]) for v in vals]
    den = es[0] + es[1] + es[2] + es[3]
    gt_ref[...] = jnp.concatenate([e / den for e in es], axis=0)
    ti_ref[...] = jnp.concatenate(idxs, axis=0)

    before = _dot(picked.astype(BF16), tri_ref[...])
    ranks = [jnp.sum(jnp.where(eidx == idx, before, 0.0), axis=0, keepdims=True) for idx in idxs]
    rk_ref[...] = jnp.concatenate(ranks, axis=0).astype(I32)
    cnt = jnp.sum(picked, axis=1, keepdims=True)
    cnt_ref[...] = jnp.broadcast_to(cnt, cnt_ref.shape).astype(I32)


def _out_proj(attn, lru, x2, mod3, w_out_b, norm_g, router_wt, router_b, S, tm):
    T, D = x2.shape
    aw = attn.shape[1]
    lw = lru.shape[1]
    tps = S // tm
    tri = (jnp.arange(tm, dtype=I32)[:, None] < jnp.arange(tm, dtype=I32)[None, :]).astype(BF16)
    const = lambda shape: pl.BlockSpec(shape, lambda i: (0,) * len(shape))
    row = lambda w: pl.BlockSpec((tm, w), lambda i: (i, 0))
    col = pl.BlockSpec((TOP_K, tm), lambda i: (0, i))
    return pl.pallas_call(
        functools.partial(_out_proj_kernel, tm, aw),
        grid=(T // tm,),
        in_specs=[
            row(aw), row(lw), row(D),
            pl.BlockSpec((None, 6, D), lambda i: (i // tps, 0, 0)),
            const((D, D)), const((1, D)), const((N_EXPERTS, D)), const((N_EXPERTS, 1)),
            const((tm, tm)),
        ],
        out_specs=[
            row(D),
            pl.BlockSpec((tm * SUBLANES, LANES), lambda i: (i, 0)),
            col, col, col,
            pl.BlockSpec((None, N_EXPERTS, LANES), lambda i: (i, 0, 0)),
        ],
        out_shape=[
            jax.ShapeDtypeStruct((T, D), F32),
            jax.ShapeDtypeStruct((T * SUBLANES, LANES), F32),
            jax.ShapeDtypeStruct((TOP_K, T), I32),
            jax.ShapeDtypeStruct((TOP_K, T), F32),
            jax.ShapeDtypeStruct((TOP_K, T), I32),
            jax.ShapeDtypeStruct((T // tm, N_EXPERTS, LANES), I32),
        ],
        compiler_params=_cparams(("arbitrary",)),
        name="out_proj",
    )(attn, lru, x2, mod3, w_out_b, norm_g, router_wt, router_b, tri)


def _slots_kernel(off_ref, ti_ref, rk_ref, gt_ref, pos_ref, gate_ref):
    i = pl.program_id(0)
    ti = ti_ref[...]
    base = jnp.zeros(ti.shape, I32)
    for e in range(N_EXPERTS):
        base = jnp.where(ti == e, off_ref[i * N_EXPERTS + e], base)
    pos_ref[...] = (base + rk_ref[...]) * SUBLANES
    gate_ref[...] = gt_ref[...]


def _slots(run_off, top_i, rank, gates, t):
    T = top_i.shape[1]
    col = pl.BlockSpec((TOP_K, t), lambda i, off: (0, i))
    tile = pl.BlockSpec((None, TOP_K, t), lambda i, off: (i, 0, 0))
    grid_spec = pltpu.PrefetchScalarGridSpec(
        num_scalar_prefetch=1,
        grid=(T // t,),
        in_specs=[col, col, col],
        out_specs=[tile, tile],
    )
    return pl.pallas_call(
        _slots_kernel,
        grid_spec=grid_spec,
        out_shape=[jax.ShapeDtypeStruct((T // t, TOP_K, t), I32),
                   jax.ShapeDtypeStruct((T // t, TOP_K, t), F32)],
        compiler_params=_cparams(("arbitrary",)),
        name="slots",
    )(run_off, top_i, rank, gates)


def _rows(ref, first_row, n_rows):
    return ref.at[pl.ds(pl.multiple_of(first_row * SUBLANES, SUBLANES), n_rows * SUBLANES), :]


def _row_at(ref, sublane_offset):
    return ref.at[pl.ds(pl.multiple_of(sublane_offset, SUBLANES), SUBLANES), :]


def _fetch_tile(tiles_hbm, smem, sems):
    i = pl.program_id(0)
    n = pl.num_programs(0)
    slot = i % 2

    def copy(tile, s):
        return pltpu.make_async_copy(tiles_hbm.at[tile], smem.at[s], sems.at[s])

    @pl.when(i == 0)
    def _():
        copy(0, 0).start()

    copy(i, slot).wait()

    @pl.when(i + 1 < n)
    def _():
        copy(i + 1, 1 - slot).start()

    return smem.at[slot]


def _run_copies(t, tile, cnt_ref, off_ref, dst_ref, make_copy):
    def expert(e, carry):
        n = cnt_ref[tile * N_EXPERTS + e]
        off = off_ref[tile * N_EXPERTS + e]
        dst = dst_ref[tile * N_EXPERTS + e]
        for b in reversed(range(t.bit_length())):
            done = (n >> (b + 1)) << (b + 1)

            @pl.when(((n >> b) & 1) == 1)
            def _():
                make_copy(off + done, dst + done, 1 << b).start()
        return carry

    lax.fori_loop(0, N_EXPERTS, expert, 0)


def _dispatch_kernel(t, cnt_ref, off_ref, dst_ref, pos_hbm, h_ref, xs_hbm, pos_smem, stage_ref, psem, sem):
    i = pl.program_id(0)
    n = pl.num_programs(0)
    slot = i % 2
    pos = _fetch_tile(pos_hbm, pos_smem, psem)
    stage = stage_ref.at[slot]

    def place(r, carry):
        row = _rows(h_ref, r, 1)[...]
        for k in range(TOP_K):
            _row_at(stage, pos[k, r])[...] = row
        return carry

    lax.fori_loop(0, t, place, 0, unroll=8)

    def all_runs(s):
        return pltpu.make_async_copy(stage_ref.at[s], _rows(xs_hbm, 0, TOP_K * t), sem.at[s])

    _run_copies(t, i, cnt_ref, off_ref, dst_ref,
                lambda a, b, m: pltpu.make_async_copy(_rows(stage, a, m), _rows(xs_hbm, b, m), sem.at[slot]))

    @pl.when(i > 0)
    def _():
        all_runs(1 - slot).wait()

    @pl.when(i == n - 1)
    def _():
        all_runs(slot).wait()


def _dispatch(tile_cnt, tile_off, tile_dst, pos_tiles, h2t, n_slots):
    n_tiles, _, t = pos_tiles.shape
    grid_spec = pltpu.PrefetchScalarGridSpec(
        num_scalar_prefetch=3,
        grid=(n_tiles,),
        in_specs=[pl.BlockSpec(memory_space=pl.ANY),
                  pl.BlockSpec((t * SUBLANES, LANES), lambda i, *_: (i, 0))],
        out_specs=pl.BlockSpec(memory_space=pl.ANY),
        scratch_shapes=[
            pltpu.SMEM((2, TOP_K, t), I32),
            pltpu.VMEM((2, TOP_K * t * SUBLANES, LANES), F32),
            pltpu.SemaphoreType.DMA((2,)),
            pltpu.SemaphoreType.DMA((2,)),
        ],
    )
    return pl.pallas_call(
        functools.partial(_dispatch_kernel, t),
        grid_spec=grid_spec,
        out_shape=jax.ShapeDtypeStruct((n_slots * SUBLANES, LANES), F32),
        compiler_params=_cparams(("arbitrary",)),
        name="dispatch",
    )(tile_cnt, tile_off, tile_dst, pos_tiles, h2t)


def _experts_kernel(ff, be_ref, nv_ref, xs_ref, wgu_ref, bgu_ref, wdn_ref, bdn_ref, ys_ref):
    i = pl.program_id(0)
    nv = nv_ref[i]
    n_slabs = xs_ref.shape[0] // MOE_BLOCK

    @pl.when(nv > 0)
    def _():
        x = jnp.concatenate(
            [xs_ref[pl.ds(j, MOE_BLOCK, stride=SUBLANES), :] for j in range(n_slabs)], axis=1)
        x = jnp.where(lax.broadcasted_iota(I32, x.shape, 0) < nv, x, 0.0)
        gu = _dot(x.astype(BF16), wgu_ref[...]) + bgu_ref[...]
        gate = jnp.minimum(gu[:, :ff], SWIGLU_LIMIT)
        up = jnp.clip(gu[:, ff:], -SWIGLU_LIMIT, SWIGLU_LIMIT)
        act = (up + 1.0) * (gate * _sigmoid(SWIGLU_ALPHA * gate))
        y = _dot(act.astype(BF16), wdn_ref[...]) + bdn_ref[...]
        for j in range(n_slabs):
            ys_ref[pl.ds(j, MOE_BLOCK, stride=SUBLANES), :] = y[:, j * LANES:(j + 1) * LANES]

    @pl.when(nv == 0)
    def _():
        ys_ref[...] = jnp.zeros_like(ys_ref)


def _experts(block_expert, block_valid, xs, w_gu_b, b_gu, w_dn_b, b_dn):
    n_blocks = block_expert.shape[0]
    E, D, ff2 = w_gu_b.shape
    ff = ff2 // 2
    rows = pl.BlockSpec((MOE_BLOCK * SUBLANES, LANES), lambda i, be, nv: (i, 0))
    grid_spec = pltpu.PrefetchScalarGridSpec(
        num_scalar_prefetch=2,
        grid=(n_blocks,),
        in_specs=[
            rows,
            pl.BlockSpec((None, D, ff2), lambda i, be, nv: (be[i], 0, 0)),
            pl.BlockSpec((None, 1, ff2), lambda i, be, nv: (be[i], 0, 0)),
            pl.BlockSpec((None, ff, D), lambda i, be, nv: (be[i], 0, 0)),
            pl.BlockSpec((None, 1, D), lambda i, be, nv: (be[i], 0, 0)),
        ],
        out_specs=rows,
    )
    return pl.pallas_call(
        functools.partial(_experts_kernel, ff),
        grid_spec=grid_spec,
        out_shape=jax.ShapeDtypeStruct(xs.shape, F32),
        compiler_params=_cparams(("arbitrary",)),
        name="experts",
    )(block_expert, block_valid, xs, w_gu_b, b_gu.reshape(E, 1, ff2), w_dn_b, b_dn.reshape(E, 1, D))


def _combine_kernel(t, cnt_ref, off_ref, dst_ref, pos_hbm, gate_hbm, ys_hbm, x1_ref, mod_ref, o_ref,
                    pos_smem, gate_smem, stage_ref, y_ref, psem, gsem, sem):
    i = pl.program_id(0)
    n = pl.num_programs(0)
    slot = i % 2
    pos = _fetch_tile(pos_hbm, pos_smem, psem)
    gate = _fetch_tile(gate_hbm, gate_smem, gsem)

    def fetch_runs(tile, s):
        stage = stage_ref.at[s]
        _run_copies(t, tile, cnt_ref, off_ref, dst_ref,
                    lambda a, b, m: pltpu.make_async_copy(_rows(ys_hbm, b, m), _rows(stage, a, m), sem.at[s]))

    @pl.when(i == 0)
    def _():
        fetch_runs(0, 0)

    @pl.when(i + 1 < n)
    def _():
        fetch_runs(i + 1, 1 - slot)

    pltpu.make_async_copy(_rows(ys_hbm, 0, TOP_K * t), stage_ref.at[slot], sem.at[slot]).wait()
    stage = stage_ref.at[slot]

    def gather(r, carry):
        acc = gate[0, r] * _row_at(stage, pos[0, r])[...]
        for k in range(1, TOP_K):
            acc = acc + gate[k, r] * _row_at(stage, pos[k, r])[...]
        _rows(y_ref, r, 1)[...] = acc
        return carry

    lax.fori_loop(0, t, gather, 0, unroll=8)

    n_slabs = o_ref.shape[1] // LANES
    y = jnp.concatenate([y_ref[pl.ds(j, t, stride=SUBLANES), :] for j in range(n_slabs)], axis=1)
    o_ref[...] = x1_ref[...] + mod_ref[5:6, :] * y


def _combine(tile_cnt, tile_off, tile_dst, pos_tiles, gate_tiles, ys, x1, mod3, S):
    T, D = x1.shape
    n_tiles, _, t = pos_tiles.shape
    tps = S // t
    grid_spec = pltpu.PrefetchScalarGridSpec(
        num_scalar_prefetch=3,
        grid=(n_tiles,),
        in_specs=[
            pl.BlockSpec(memory_space=pl.ANY),
            pl.BlockSpec(memory_space=pl.ANY),
            pl.BlockSpec(memory_space=pl.ANY),
            pl.BlockSpec((t, D), lambda i, *_: (i, 0)),
            pl.BlockSpec((None, 6, D), lambda i, *_: (i // tps, 0, 0)),
        ],
        out_specs=pl.BlockSpec((t, D), lambda i, *_: (i, 0)),
        scratch_shapes=[
            pltpu.SMEM((2, TOP_K, t), I32),
            pltpu.SMEM((2, TOP_K, t), F32),
            pltpu.VMEM((2, TOP_K * t * SUBLANES, LANES), F32),
            pltpu.VMEM((t * SUBLANES, LANES), F32),
            pltpu.SemaphoreType.DMA((2,)),
            pltpu.SemaphoreType.DMA((2,)),
            pltpu.SemaphoreType.DMA((2,)),
        ],
    )
    return pl.pallas_call(
        functools.partial(_combine_kernel, t),
        grid_spec=grid_spec,
        out_shape=jax.ShapeDtypeStruct((T, D), F32),
        compiler_params=_cparams(("arbitrary",)),
        name="combine",
    )(tile_cnt, tile_off, tile_dst, pos_tiles, gate_tiles, ys, x1, mod3)


def _tile(n, target):
    t = min(n, target)
    while n % t:
        t //= 2
    return t


def _block_diag(w):
    n, bw, _ = w.shape
    eye = jnp.eye(n, dtype=w.dtype)
    return (eye[:, None, :, None] * w[:, :, None, :]).reshape(n * bw, n * bw)


def _layer(l, x2, B, S, c, ada_w, ada_b, norm1_g, w_in, q_norm_g, k_norm_g, lambda_q1, lambda_k1, lambda_q2,
           lambda_k2, attn_subln_g, conv_w, conv_b, lru_wa, lru_ba, lru_wx, lru_bx, lru_lambda, lru_out_g,
           w_out, norm2_g, router_w, router_b, w_gate_up, b_gate_up, w_down, b_down):
    T, D = x2.shape
    lam_init = 0.8 - 0.6 * math.exp(-0.3 * l)
    aw = D // 2
    lw = D - aw
    heads = aw // ATTN_DV
    qkw = heads * 2 * ATTN_DK

    mod, lam = _adaln(c, ada_w, ada_b, lambda_q1, lambda_k1, lambda_q2, lambda_k2, lam_init)
    mod3 = mod.reshape(c.shape[0], 6, D)

    reps = qkw // ATTN_DK
    gq = (jnp.tile(q_norm_g, reps) * (ATTN_DK ** -0.5 * math.log2(math.e))).reshape(1, qkw)
    gk = jnp.tile(k_norm_g, reps).reshape(1, qkw)
    tm = _tile(S, 512)
    q, k, v, xr, gr = _in_proj(x2, mod3, norm1_g.reshape(1, D), w_in.astype(BF16), gq, gk, S, _tile(S, 1024),
                               qkw, aw, lw)

    attn = _attention(q, k, v, lam, attn_subln_g.reshape(1, ATTN_DV), B, S, _tile(S, 1024), _tile(S, 512),
                      1.0 - lam_init)

    w_gates = jnp.concatenate([_block_diag(lru_wa), _block_diag(lru_wx)], axis=1).astype(BF16)
    b_gates = jnp.concatenate([lru_ba, lru_bx]).reshape(1, 2 * lw)
    lru = _lru(xr, gr, conv_w, conv_b.reshape(1, lw), w_gates, b_gates, lru_lambda.reshape(1, lw),
               lru_out_g.reshape(1, lw), B, S, _tile(S, 512))

    x1, h2t, top_i, gates, rank, tile_cnt = _out_proj(
        attn, lru, x2, mod3, w_out.astype(BF16), norm2_g.reshape(1, D), router_w.T,
        router_b.reshape(N_EXPERTS, 1), S, tm)

    tile_cnt = tile_cnt[:, :, 0]
    counts = jnp.sum(tile_cnt, axis=0)
    padded = ((counts + MOE_BLOCK - 1) // MOE_BLOCK) * MOE_BLOCK
    pad_ends = jnp.cumsum(padded)
    pad_starts = pad_ends - padded
    n_blocks = (T * TOP_K) // MOE_BLOCK + N_EXPERTS
    blk_start = jnp.arange(n_blocks, dtype=I32) * MOE_BLOCK
    owner = blk_start[:, None] >= pad_ends[None, :]
    block_expert = jnp.minimum(jnp.sum(owner, axis=1), N_EXPERTS - 1).astype(I32)
    onehot = block_expert[:, None] == jnp.arange(N_EXPERTS, dtype=I32)[None, :]
    row_end = jnp.sum(jnp.where(onehot, (pad_starts + counts)[None, :], 0), axis=1)
    block_valid = jnp.clip(row_end - blk_start, 0, MOE_BLOCK).astype(I32)
    tile_off = jnp.cumsum(tile_cnt, axis=1) - tile_cnt
    tile_dst = pad_starts[None, :] + jnp.cumsum(tile_cnt, axis=0) - tile_cnt
    flat = lambda a: a.reshape(-1).astype(I32)

    pos_tiles, gate_tiles = _slots(flat(tile_off), top_i, rank, gates, tm)
    xs = _dispatch(flat(tile_cnt), flat(tile_off), flat(tile_dst), pos_tiles, h2t, n_blocks * MOE_BLOCK)
    ys = _experts(block_expert, block_valid, xs, w_gate_up.astype(BF16), b_gate_up,
                  w_down.astype(BF16), b_down)
    return _combine(flat(tile_cnt), flat(tile_off), flat(tile_dst), pos_tiles, gate_tiles, ys, x1, mod3, S)


def kernel(x, c, ada_w, ada_b, norm1_g, w_in, q_norm_g, k_norm_g, lambda_q1, lambda_k1, lambda_q2, lambda_k2,
           attn_subln_g, conv_w, conv_b, lru_wa, lru_ba, lru_wx, lru_bx, lru_lambda, lru_out_g, w_out, norm2_g,
           router_w, router_b, w_gate_up, b_gate_up, w_down, b_down):
    B, S, D = x.shape
    params = (ada_w, ada_b, norm1_g, w_in, q_norm_g, k_norm_g, lambda_q1, lambda_k1, lambda_q2, lambda_k2,
              attn_subln_g, conv_w, conv_b, lru_wa, lru_ba, lru_wx, lru_bx, lru_lambda, lru_out_g, w_out,
              norm2_g, router_w, router_b, w_gate_up, b_gate_up, w_down, b_down)
    x2 = x.reshape(B * S, D)
    for l in range(ada_w.shape[0]):
        x2 = _layer(l, x2, B, S, c, *[p[l] for p in params])
    return x2.reshape(B, S, D)
```

```python
import functools
import math

import jax
import jax.numpy as jnp
from jax import lax
from jax.experimental import pallas as pl
from jax.experimental.pallas import tpu as pltpu

F32 = jnp.float32
BF16 = jnp.bfloat16
I32 = jnp.int32

ATTN_DK = 64
ATTN_DV = 2 * ATTN_DK
CONV_W = 4
LRU_C = 8.0
N_EXPERTS = 32
TOP_K = 4
SWIGLU_LIMIT = 7.0
SWIGLU_ALPHA = 1.702
MOE_BLOCK = 512
EPS = 1e-6
NEG_BIG = -1e30

LANES = 128
SUBLANES = 8
VMEM_LIMIT = 56 * 1024 * 1024


def _cparams(sem):
    return pltpu.CompilerParams(dimension_semantics=sem, vmem_limit_bytes=VMEM_LIMIT)


def _split_hi_lo(x):
    hi = x.astype(BF16)
    lo = (x - hi.astype(F32)).astype(BF16)
    return hi, lo


def _sigmoid(x):
    return 0.5 * jnp.tanh(0.5 * x) + 0.5


def _dot(a, b):
    return jnp.dot(a, b, preferred_element_type=F32)


def _dot_tb(a, b):
    return lax.dot_general(a, b, (((1,), (1,)), ((), ())), preferred_element_type=F32)


def _adaln_kernel(lam_init, c_ref, w_ref, b_ref, lq1, lk1, lq2, lk2, mod_ref, lam_ref):
    c = c_ref[...]
    s = c * jax.nn.sigmoid(c)
    s_hi, s_lo = _split_hi_lo(s)
    w = w_ref[...]
    w_hi, w_lo = _split_hi_lo(w)
    mod_ref[...] = _dot(s_hi, w_hi) + _dot(s_hi, w_lo) + _dot(s_lo, w_hi) + b_ref[...]
    d1 = jnp.sum(lq1[...] * lk1[...], axis=-1, keepdims=True)
    d2 = jnp.sum(lq2[...] * lk2[...], axis=-1, keepdims=True)
    lam = jnp.exp(d1) - jnp.exp(d2) + lam_init
    lam_ref[...] = jnp.broadcast_to(lam, lam_ref.shape)


def _adaln(c, ada_w, ada_b, lq1, lk1, lq2, lk2, lam_init):
    B, D = c.shape
    n = ada_w.shape[1] // D
    vec = lambda: pl.BlockSpec((1, ATTN_DK), lambda j: (0, 0))
    return pl.pallas_call(
        functools.partial(_adaln_kernel, lam_init),
        grid=(n,),
        in_specs=[
            pl.BlockSpec((B, D), lambda j: (0, 0)),
            pl.BlockSpec((D, D), lambda j: (0, j)),
            pl.BlockSpec((1, D), lambda j: (0, j)),
            vec(), vec(), vec(), vec(),
        ],
        out_specs=[
            pl.BlockSpec((B, D), lambda j: (0, j)),
            pl.BlockSpec((1, LANES), lambda j: (0, 0)),
        ],
        out_shape=[
            jax.ShapeDtypeStruct((B, n * D), F32),
            jax.ShapeDtypeStruct((1, LANES), F32),
        ],
        compiler_params=_cparams(("arbitrary",)),
        name="adaln",
    )(c, ada_w, ada_b.reshape(1, -1), lq1.reshape(1, -1), lk1.reshape(1, -1),
      lq2.reshape(1, -1), lk2.reshape(1, -1))


def _rms_modulate(x, g, shift, scale):
    ms = jnp.mean(x * x, axis=-1, keepdims=True)
    y = x * lax.rsqrt(ms + EPS) * g
    return y * (1.0 + scale) + shift


def _group_rms_scale(q, group_ones):
    ss = _dot((q * q).astype(BF16), group_ones)
    return lax.rsqrt(ss * (1.0 / ATTN_DK) + EPS)


def _in_proj_kernel(qkw, aw, lw, x_ref, mod_ref, g_ref, w_ref, gq_ref, gk_ref, ones_ref,
                    q_ref, k_ref, v_ref, xr_ref, gr_ref):
    h = _rms_modulate(x_ref[...], g_ref[...], mod_ref[0:1, :], mod_ref[1:2, :])
    hb = h.astype(BF16)
    group_ones = ones_ref[...]
    o = 0
    q = _dot(hb, w_ref[:, o:o + qkw]); o += qkw
    q_ref[...] = (q * _group_rms_scale(q, group_ones) * gq_ref[...]).astype(BF16)
    k = _dot(hb, w_ref[:, o:o + qkw]); o += qkw
    k_ref[...] = (k * _group_rms_scale(k, group_ones) * gk_ref[...]).astype(BF16)
    v_ref[...] = _dot(hb, w_ref[:, o:o + aw]).astype(BF16); o += aw
    xr_ref[...] = _dot(hb, w_ref[:, o:o + lw]); o += lw
    gr_ref[...] = _dot(hb, w_ref[:, o:o + lw])


def _in_proj(x2, mod3, norm_g, w_in_b, gq, gk, S, tm, qkw, aw, lw):
    T, D = x2.shape
    tps = S // tm
    grp = jnp.arange(qkw, dtype=I32) // ATTN_DK
    group_ones = (grp[:, None] == grp[None, :]).astype(BF16)
    const = lambda shape: pl.BlockSpec(shape, lambda i: (0,) * len(shape))
    row = lambda w: pl.BlockSpec((tm, w), lambda i: (i, 0))
    return pl.pallas_call(
        functools.partial(_in_proj_kernel, qkw, aw, lw),
        grid=(T // tm,),
        in_specs=[
            row(D),
            pl.BlockSpec((None, 6, D), lambda i: (i // tps, 0, 0)),
            const((1, D)),
            const(w_in_b.shape),
            const((1, qkw)), const((1, qkw)),
            const((qkw, qkw)),
        ],
        out_specs=[row(qkw), row(qkw), row(aw), row(lw), row(lw)],
        out_shape=[
            jax.ShapeDtypeStruct((T, qkw), BF16),
            jax.ShapeDtypeStruct((T, qkw), BF16),
            jax.ShapeDtypeStruct((T, aw), BF16),
            jax.ShapeDtypeStruct((T, lw), F32),
            jax.ShapeDtypeStruct((T, lw), F32),
        ],
        compiler_params=_cparams(("arbitrary",)),
        name="in_proj",
    )(x2, mod3, norm_g, w_in_b, gq, gk, group_ones)


ATTN_ROW_CHUNK = 128


def _attn_kernel(bq, bk, out_scale, q_ref, k_ref, v_ref, lam_ref, g_ref, o_ref,
                 qq_ref, vp_ref, s_ref, sd_ref, m_ref, acc_ref):
    i = pl.program_id(2)
    rc = ATTN_ROW_CHUNK
    n_chunks = 2 * bq // rc

    @pl.when(i == 0)
    def _():
        col = lax.broadcasted_iota(I32, (vp_ref.shape[0], ATTN_DV), 1)
        vp_ref[:, :ATTN_DV] = v_ref[...]
        vp_ref[:, ATTN_DV:] = jnp.where(col == 0, 1.0, 0.0).astype(BF16)

    q = q_ref[...]
    lane = lax.broadcasted_iota(I32, q.shape, 1)
    zero = jnp.zeros_like(q)
    qq_ref[0:bq, :] = jnp.where(lane < ATTN_DK, q, zero)
    qq_ref[bq:, :] = jnp.where(lane >= ATTN_DK, q, zero)
    m_ref[...] = jnp.full(m_ref.shape, NEG_BIG, F32)
    acc_ref[...] = jnp.zeros_like(acc_ref)

    def chunk(r):
        return pl.ds(r * rc, rc)

    def scores(r, start, kw):
        return _dot_tb(qq_ref[chunk(r), :], k_ref[pl.ds(start, kw), :])

    def softmax_pv(s, r, start):
        rows = chunk(r)
        kw = s.shape[1]
        m_old = m_ref[rows, :]
        m_new = jnp.maximum(m_old, jnp.max(s, axis=-1, keepdims=True))
        alpha = jnp.exp2(m_old - m_new)
        p = jnp.exp2(s - jnp.tile(m_new, (1, kw // LANES)))
        pv = _dot(p.astype(BF16), vp_ref[pl.ds(start, kw), :])
        acc_ref[rows, :] = acc_ref[rows, :] * jnp.tile(alpha, (1, 2)) + pv
        m_ref[rows, :] = m_new

    for r in range(n_chunks):
        s_ref[chunk(r), :] = scores(r, 0, bk)

    def body(j, carry):
        start = pl.multiple_of(j * bk, bk)
        for r in range(n_chunks):
            s = s_ref[chunk(r), :]
            s_ref[chunk(r), :] = scores(r, start + bk, bk)
            softmax_pv(s, r, start)
        return carry

    n_full = i * (bq // bk)
    lax.fori_loop(0, n_full, body, 0)

    start = pl.multiple_of(n_full * bk, bk)
    tri = (lax.broadcasted_iota(I32, (rc, rc), 1) <= lax.broadcasted_iota(I32, (rc, rc), 0))
    first_key = [(r * rc) % bq for r in range(n_chunks)]
    for r in range(n_chunks):
        if first_key[r] + rc > bk:
            sd_ref[chunk(r), 0:first_key[r] + rc - bk] = scores(r, start + bk, first_key[r] + rc - bk)
    for r in range(n_chunks):
        q0 = first_key[r]
        kw = q0 + rc
        parts = [s_ref[chunk(r), 0:min(q0, bk)]] if q0 else []
        if q0 > bk:
            parts.append(sd_ref[chunk(r), 0:q0 - bk])
        last = s_ref[chunk(r), q0:kw] if kw <= bk else sd_ref[chunk(r), q0 - bk:kw - bk]
        parts.append(jnp.where(tri, last, NEG_BIG))
        softmax_pv(parts[0] if len(parts) == 1 else jnp.concatenate(parts, axis=1), r, start)

    acc = acc_ref[...]
    o = acc[:, :ATTN_DV] / acc[:, ATTN_DV:ATTN_DV + 1]
    a = o[:bq] - lam_ref[0:1, 0:1] * o[bq:]
    ms = jnp.mean(a * a, axis=-1, keepdims=True)
    o_ref[...] = (a * lax.rsqrt(ms + EPS) * g_ref[...] * out_scale).astype(BF16)


def _attention(q, k, v, lam, subln_g, B, S, bq, bk, out_scale):
    T, qkw = q.shape
    H = qkw // ATTN_DV
    nq = S // bq
    return pl.pallas_call(
        functools.partial(_attn_kernel, bq, bk, out_scale),
        grid=(B, H, nq),
        in_specs=[
            pl.BlockSpec((bq, ATTN_DV), lambda b, h, i: (b * nq + i, h)),
            pl.BlockSpec((S, ATTN_DV), lambda b, h, i: (b, h)),
            pl.BlockSpec((S, ATTN_DV), lambda b, h, i: (b, h)),
            pl.BlockSpec((1, LANES), lambda b, h, i: (0, 0)),
            pl.BlockSpec((1, ATTN_DV), lambda b, h, i: (0, 0)),
        ],
        out_specs=pl.BlockSpec((bq, ATTN_DV), lambda b, h, i: (b * nq + i, h)),
        out_shape=jax.ShapeDtypeStruct((T, H * ATTN_DV), BF16),
        scratch_shapes=[
            pltpu.VMEM((2 * bq, ATTN_DV), BF16),
            pltpu.VMEM((S, 2 * ATTN_DV), BF16),
            pltpu.VMEM((2 * bq, bk), F32),
            pltpu.VMEM((2 * bq, max(bq - bk, LANES)), F32),
            pltpu.VMEM((2 * bq, LANES), F32),
            pltpu.VMEM((2 * bq, 2 * ATTN_DV), F32),
        ],
        compiler_params=_cparams(("arbitrary", "arbitrary", "arbitrary")),
        name="attention",
    )(q, k, v, lam, subln_g)


def _lru_kernel(tc, cw, x_ref, gate_ref, cw_ref, cb_ref, wg_ref, bg_ref, lam_ref, og_ref,
                o_ref, ext_ref, a_ref, b_ref, hc_ref):
    c = pl.program_id(1)

    @pl.when(c == 0)
    def _():
        ext_ref[0:SUBLANES, :] = jnp.zeros((SUBLANES, cw), F32)
        hc_ref[...] = jnp.zeros_like(hc_ref)

    ext_ref[SUBLANES:SUBLANES + tc, :] = x_ref[...]
    xc = cb_ref[...] + cw_ref[CONV_W - 1:CONV_W, :] * x_ref[...]
    for w in range(CONV_W - 1):
        sh = CONV_W - 1 - w
        xc = xc + cw_ref[w:w + 1, :] * ext_ref[SUBLANES - sh:SUBLANES - sh + tc, :]
    tail = ext_ref[tc:tc + SUBLANES, :]
    ext_ref[0:SUBLANES, :] = tail

    g = _dot(xc.astype(BF16), wg_ref[...]) + bg_ref[...]
    r = _sigmoid(g[:, :cw])
    ig = _sigmoid(g[:, cw:])
    nl = -lam_ref[...]
    softplus = jnp.maximum(nl, 0.0) + jnp.log1p(jnp.exp(-jnp.abs(nl)))
    log_a = (-LRU_C) * r * softplus
    th = jnp.tanh(log_a)
    a_ref[...] = jnp.exp(log_a)
    b_ref[...] = jnp.sqrt(-2.0 * th / (1.0 - th)) * (ig * xc)

    row = lax.broadcasted_iota(I32, (SUBLANES, cw), 0)
    has_src = {d: row >= d for d in (1, 2, 4)}

    def tile_scan(n, hc):
        rows = pl.ds(pl.multiple_of(n * SUBLANES, SUBLANES), SUBLANES)
        a = a_ref[rows, :]
        b = b_ref[rows, :]
        for d in (1, 2, 4):
            b = jnp.where(has_src[d], a * pltpu.roll(b, d, 0), 0.0) + b
            a = jnp.where(has_src[d], a * pltpu.roll(a, d, 0), a)
        h = b + a * hc
        b_ref[rows, :] = h
        return jnp.broadcast_to(h[SUBLANES - 1:SUBLANES, :], h.shape)

    hc_ref[...] = lax.fori_loop(0, tc // SUBLANES, tile_scan, hc_ref[...], unroll=4)

    gt = gate_ref[...]
    gelu = 0.5 * gt * (1.0 + jnp.tanh(math.sqrt(2.0 / math.pi) * (gt + 0.044715 * gt * gt * gt)))
    y = b_ref[...] * gelu
    ms = jnp.mean(y * y, axis=-1, keepdims=True)
    o_ref[...] = (y * lax.rsqrt(ms + EPS) * og_ref[...]).astype(BF16)


def _lru(xr, gr, conv_w, conv_b, w_gates, b_gates, lru_lambda, out_g, B, S, tc):
    T, cw = xr.shape
    nc = S // tc
    const = lambda shape: pl.BlockSpec(shape, lambda b, c: (0,) * len(shape))
    row = pl.BlockSpec((tc, cw), lambda b, c: (b * nc + c, 0))
    return pl.pallas_call(
        functools.partial(_lru_kernel, tc, cw),
        grid=(B, nc),
        in_specs=[row, row, const((CONV_W, cw)), const((1, cw)), const((cw, 2 * cw)),
                  const((1, 2 * cw)), const((1, cw)), const((1, cw))],
        out_specs=row,
        out_shape=jax.ShapeDtypeStruct((T, cw), BF16),
        scratch_shapes=[
            pltpu.VMEM((tc + SUBLANES, cw), F32),
            pltpu.VMEM((tc, cw), F32),
            pltpu.VMEM((tc, cw), F32),
            pltpu.VMEM((SUBLANES, cw), F32),
        ],
        compiler_params=_cparams(("arbitrary", "arbitrary")),
        name="lru",
    )(xr, gr, conv_w, conv_b, w_gates, b_gates, lru_lambda, out_g)


def _out_proj_kernel(tm, aw, attn_ref, lru_ref, x_ref, mod_ref, w_ref, g_ref, rw_ref, rb_ref, tri_ref,
                     x1_ref, h2_ref, ti_ref, gt_ref, rk_ref, cnt_ref):
    mix = _dot(attn_ref[...], w_ref[0:aw, :]) + _dot(lru_ref[...], w_ref[aw:, :])
    x1 = x_ref[...] + mod_ref[2:3, :] * mix
    x1_ref[...] = x1
    h2 = _rms_modulate(x1, g_ref[...], mod_ref[3:4, :], mod_ref[4:5, :])
    for j in range(h2.shape[1] // LANES):
        h2_ref[pl.ds(j, tm, stride=SUBLANES), :] = h2[:, j * LANES:(j + 1) * LANES]

    h_hi, h_lo = _split_hi_lo(h2)
    w_hi, w_lo = _split_hi_lo(rw_ref[...])
    lg = _dot_tb(w_hi, h_hi) + _dot_tb(w_hi, h_lo) + _dot_tb(w_lo, h_hi) + rb_ref[...]

    eidx = lax.broadcasted_iota(I32, lg.shape, 0)
    picked = jnp.zeros(lg.shape, F32)
    vals, idxs = [], []
    for _ in range(TOP_K):
        m = jnp.max(lg, axis=0, keepdims=True)
        idx = jnp.min(jnp.where(lg == m, eidx, N_EXPERTS), axis=0, keepdims=True)
        sel = eidx == idx
        lg = jnp.where(sel, -jnp.inf, lg)
        picked = picked + sel.astype(F32)
        vals.append(m)
        idxs.append(idx)
    es = [jnp.exp(v - vals[0]) for v in vals]
    den = es[0] + es[1] + es[2] + es[3]
    gt_ref[...] = jnp.concatenate([e / den for e in es], axis=0)
    ti_ref[...] = jnp.concatenate(idxs, axis=0)

    before = _dot(picked.astype(BF16), tri_ref[...])
    ranks = [jnp.sum(jnp.where(eidx == idx, before, 0.0), axis=0, keepdims=True) for idx in idxs]
    rk_ref[...] = jnp.concatenate(ranks, axis=0).astype(I32)
    cnt = jnp.sum(picked, axis=1, keepdims=True)
    cnt_ref[...] = jnp.broadcast_to(cnt, cnt_ref.shape).astype(I32)


def _out_proj(attn, lru, x2, mod3, w_out_b, norm_g, router_wt, router_b, S, tm):
    T, D = x2.shape
    aw = attn.shape[1]
    lw = lru.shape[1]
    tps = S // tm
    tri = (jnp.arange(tm, dtype=I32)[:, None] < jnp.arange(tm, dtype=I32)[None, :]).astype(BF16)
    const = lambda shape: pl.BlockSpec(shape, lambda i: (0,) * len(shape))
    row = lambda w: pl.BlockSpec((tm, w), lambda i: (i, 0))
    col = pl.BlockSpec((TOP_K, tm), lambda i: (0, i))
    return pl.pallas_call(
        functools.partial(_out_proj_kernel, tm, aw),
        grid=(T // tm,),
        in_specs=[
            row(aw), row(lw), row(D),
            pl.BlockSpec((None, 6, D), lambda i: (i // tps, 0, 0)),
            const((D, D)), const((1, D)), const((N_EXPERTS, D)), const((N_EXPERTS, 1)),
            const((tm, tm)),
        ],
        out_specs=[
            row(D),
            pl.BlockSpec((tm * SUBLANES, LANES), lambda i: (i, 0)),
            col, col, col,
            pl.BlockSpec((None, N_EXPERTS, LANES), lambda i: (i, 0, 0)),
        ],
        out_shape=[
            jax.ShapeDtypeStruct((T, D), F32),
            jax.ShapeDtypeStruct((T * SUBLANES, LANES), F32),
            jax.ShapeDtypeStruct((TOP_K, T), I32),
            jax.ShapeDtypeStruct((TOP_K, T), F32),
            jax.ShapeDtypeStruct((TOP_K, T), I32),
            jax.ShapeDtypeStruct((T // tm, N_EXPERTS, LANES), I32),
        ],
        compiler_params=_cparams(("arbitrary",)),
        name="out_proj",
    )(attn, lru, x2, mod3, w_out_b, norm_g, router_wt, router_b, tri)


def _slots_kernel(off_ref, ti_ref, rk_ref, gt_ref, meta_ref):
    i = pl.program_id(0)
    ti = ti_ref[...]
    base = jnp.zeros(ti.shape, I32)
    for e in range(N_EXPERTS):
        base = jnp.where(ti == e, off_ref[i * N_EXPERTS + e], base)
    meta_ref[0:TOP_K, :] = (base + rk_ref[...]) * SUBLANES
    meta_ref[TOP_K:, :] = lax.bitcast_convert_type(gt_ref[...], I32)


def _slots(run_off, top_i, rank, gates, t):
    T = top_i.shape[1]
    col = pl.BlockSpec((TOP_K, t), lambda i, off: (0, i))
    grid_spec = pltpu.PrefetchScalarGridSpec(
        num_scalar_prefetch=1,
        grid=(T // t,),
        in_specs=[col, col, col],
        out_specs=pl.BlockSpec((None, 2 * TOP_K, t), lambda i, off: (i, 0, 0)),
    )
    return pl.pallas_call(
        _slots_kernel,
        grid_spec=grid_spec,
        out_shape=jax.ShapeDtypeStruct((T // t, 2 * TOP_K, t), I32),
        compiler_params=_cparams(("arbitrary",)),
        name="slots",
    )(run_off, top_i, rank, gates)


def _rows(ref, first_row, n_rows):
    return ref.at[pl.ds(pl.multiple_of(first_row * SUBLANES, SUBLANES), n_rows * SUBLANES), :]


def _row_at(ref, sublane_offset):
    return ref.at[pl.ds(pl.multiple_of(sublane_offset, SUBLANES), SUBLANES), :]


def _fetch_tile(meta_hbm, smem, sems, t, n_words):
    i = pl.program_id(0)
    n = pl.num_programs(0)
    slot = i % 2
    per_tile = 2 * TOP_K * t

    def copy(tile, s):
        src = meta_hbm.at[pl.ds(pl.multiple_of(tile * per_tile, per_tile), n_words)]
        return pltpu.make_async_copy(src, smem.at[pl.ds(pl.multiple_of(s * n_words, n_words), n_words)], sems.at[s])

    @pl.when(i == 0)
    def _():
        copy(0, 0).start()

    copy(i, slot).wait()

    @pl.when(i + 1 < n)
    def _():
        copy(i + 1, 1 - slot).start()

    return slot * n_words


def _run_copies(t, tile, cnt_ref, off_ref, dst_ref, make_copy):
    def expert(e, carry):
        n = cnt_ref[tile * N_EXPERTS + e]
        off = off_ref[tile * N_EXPERTS + e]
        dst = dst_ref[tile * N_EXPERTS + e]
        for b in reversed(range(t.bit_length())):
            done = (n >> (b + 1)) << (b + 1)

            @pl.when(((n >> b) & 1) == 1)
            def _():
                make_copy(off + done, dst + done, 1 << b).start()
        return carry

    lax.fori_loop(0, N_EXPERTS, expert, 0)


def _dispatch_kernel(t, cnt_ref, off_ref, dst_ref, meta_hbm, h_ref, xs_hbm, meta_smem, stage_ref, psem, sem):
    i = pl.program_id(0)
    n = pl.num_programs(0)
    slot = i % 2
    base = _fetch_tile(meta_hbm, meta_smem, psem, t, TOP_K * t)
    stage = stage_ref.at[slot]

    def place(r, carry):
        row = _rows(h_ref, r, 1)[...]
        for k in range(TOP_K):
            _row_at(stage, meta_smem[base + r + k * t])[...] = row
        return carry

    lax.fori_loop(0, t, place, 0, unroll=8)

    def all_runs(s):
        return pltpu.make_async_copy(stage_ref.at[s], _rows(xs_hbm, 0, TOP_K * t), sem.at[s])

    _run_copies(t, i, cnt_ref, off_ref, dst_ref,
                lambda a, b, m: pltpu.make_async_copy(_rows(stage, a, m), _rows(xs_hbm, b, m), sem.at[slot]))

    @pl.when(i > 0)
    def _():
        all_runs(1 - slot).wait()

    @pl.when(i == n - 1)
    def _():
        all_runs(slot).wait()


def _dispatch(tile_cnt, tile_off, tile_dst, meta, h2t, n_slots, t):
    n_tiles = meta.shape[0] // (2 * TOP_K * t)
    grid_spec = pltpu.PrefetchScalarGridSpec(
        num_scalar_prefetch=3,
        grid=(n_tiles,),
        in_specs=[pl.BlockSpec(memory_space=pl.ANY),
                  pl.BlockSpec((t * SUBLANES, LANES), lambda i, *_: (i, 0))],
        out_specs=pl.BlockSpec(memory_space=pl.ANY),
        scratch_shapes=[
            pltpu.SMEM((2 * TOP_K * t,), I32),
            pltpu.VMEM((2, TOP_K * t * SUBLANES, LANES), F32),
            pltpu.SemaphoreType.DMA((2,)),
            pltpu.SemaphoreType.DMA((2,)),
        ],
    )
    return pl.pallas_call(
        functools.partial(_dispatch_kernel, t),
        grid_spec=grid_spec,
        out_shape=jax.ShapeDtypeStruct((n_slots * SUBLANES, LANES), F32),
        compiler_params=_cparams(("arbitrary",)),
        name="dispatch",
    )(tile_cnt, tile_off, tile_dst, meta, h2t)


def _experts_kernel(ff, be_ref, nv_ref, xs_ref, wgu_ref, bgu_ref, wdn_ref, bdn_ref, ys_ref):
    i = pl.program_id(0)
    nv = nv_ref[i]
    n_slabs = xs_ref.shape[0] // MOE_BLOCK

    @pl.when(nv > 0)
    def _():
        x = jnp.concatenate(
            [xs_ref[pl.ds(j, MOE_BLOCK, stride=SUBLANES), :] for j in range(n_slabs)], axis=1)
        x = jnp.where(lax.broadcasted_iota(I32, x.shape, 0) < nv, x, 0.0)
        gu = _dot(x.astype(BF16), wgu_ref[...]) + bgu_ref[...]
        gate = jnp.minimum(gu[:, :ff], SWIGLU_LIMIT)
        up = jnp.clip(gu[:, ff:], -SWIGLU_LIMIT, SWIGLU_LIMIT)
        act = (up + 1.0) * (gate * _sigmoid(SWIGLU_ALPHA * gate))
        y = _dot(act.astype(BF16), wdn_ref[...]) + bdn_ref[...]
        for j in range(n_slabs):
            ys_ref[pl.ds(j, MOE_BLOCK, stride=SUBLANES), :] = y[:, j * LANES:(j + 1) * LANES]

    @pl.when(nv == 0)
    def _():
        ys_ref[...] = jnp.zeros_like(ys_ref)


def _experts(block_expert, block_valid, xs, w_gu_b, b_gu, w_dn_b, b_dn):
    n_blocks = block_expert.shape[0]
    E, D, ff2 = w_gu_b.shape
    ff = ff2 // 2
    rows = pl.BlockSpec((MOE_BLOCK * SUBLANES, LANES), lambda i, be, nv: (i, 0))
    grid_spec = pltpu.PrefetchScalarGridSpec(
        num_scalar_prefetch=2,
        grid=(n_blocks,),
        in_specs=[
            rows,
            pl.BlockSpec((None, D, ff2), lambda i, be, nv: (be[i], 0, 0)),
            pl.BlockSpec((None, 1, ff2), lambda i, be, nv: (be[i], 0, 0)),
            pl.BlockSpec((None, ff, D), lambda i, be, nv: (be[i], 0, 0)),
            pl.BlockSpec((None, 1, D), lambda i, be, nv: (be[i], 0, 0)),
        ],
        out_specs=rows,
    )
    return pl.pallas_call(
        functools.partial(_experts_kernel, ff),
        grid_spec=grid_spec,
        out_shape=jax.ShapeDtypeStruct(xs.shape, F32),
        compiler_params=_cparams(("arbitrary",)),
        name="experts",
    )(block_expert, block_valid, xs, w_gu_b, b_gu.reshape(E, 1, ff2), w_dn_b, b_dn.reshape(E, 1, D))


def _combine_kernel(t, cnt_ref, off_ref, dst_ref, meta_hbm, ys_hbm, x1_ref, mod_ref, o_ref,
                    meta_smem, stage_ref, y_ref, psem, sem):
    i = pl.program_id(0)
    n = pl.num_programs(0)
    slot = i % 2
    base = _fetch_tile(meta_hbm, meta_smem, psem, t, 2 * TOP_K * t)

    def fetch_runs(tile, s):
        stage = stage_ref.at[s]
        _run_copies(t, tile, cnt_ref, off_ref, dst_ref,
                    lambda a, b, m: pltpu.make_async_copy(_rows(ys_hbm, b, m), _rows(stage, a, m), sem.at[s]))

    @pl.when(i == 0)
    def _():
        fetch_runs(0, 0)

    @pl.when(i + 1 < n)
    def _():
        fetch_runs(i + 1, 1 - slot)

    pltpu.make_async_copy(_rows(ys_hbm, 0, TOP_K * t), stage_ref.at[slot], sem.at[slot]).wait()
    stage = stage_ref.at[slot]

    def gather(r, carry):
        acc = None
        for k in range(TOP_K):
            gate = lax.bitcast_convert_type(meta_smem[base + r + (TOP_K + k) * t], F32)
            term = gate * _row_at(stage, meta_smem[base + r + k * t])[...]
            acc = term if acc is None else acc + term
        _rows(y_ref, r, 1)[...] = acc
        return carry

    lax.fori_loop(0, t, gather, 0, unroll=8)

    n_slabs = o_ref.shape[1] // LANES
    y = jnp.concatenate([y_ref[pl.ds(j, t, stride=SUBLANES), :] for j in range(n_slabs)], axis=1)
    o_ref[...] = x1_ref[...] + mod_ref[5:6, :] * y


def _combine(tile_cnt, tile_off, tile_dst, meta, ys, x1, mod3, S, t):
    T, D = x1.shape
    n_tiles = T // t
    tps = S // t
    grid_spec = pltpu.PrefetchScalarGridSpec(
        num_scalar_prefetch=3,
        grid=(n_tiles,),
        in_specs=[
            pl.BlockSpec(memory_space=pl.ANY),
            pl.BlockSpec(memory_space=pl.ANY),
            pl.BlockSpec((t, D), lambda i, *_: (i, 0)),
            pl.BlockSpec((None, 6, D), lambda i, *_: (i // tps, 0, 0)),
        ],
        out_specs=pl.BlockSpec((t, D), lambda i, *_: (i, 0)),
        scratch_shapes=[
            pltpu.SMEM((2 * 2 * TOP_K * t,), I32),
            pltpu.VMEM((2, TOP_K * t * SUBLANES, LANES), F32),
            pltpu.VMEM((t * SUBLANES, LANES), F32),
            pltpu.SemaphoreType.DMA((2,)),
            pltpu.SemaphoreType.DMA((2,)),
        ],
    )
    return pl.pallas_call(
        functools.partial(_combine_kernel, t),
        grid_spec=grid_spec,
        out_shape=jax.ShapeDtypeStruct((T, D), F32),
        compiler_params=_cparams(("arbitrary",)),
        name="combine",
    )(tile_cnt, tile_off, tile_dst, meta, ys, x1, mod3)


def _tile(n, target):
    t = min(n, target)
    while n % t:
        t //= 2
    return t


def _block_diag(w):
    n, bw, _ = w.shape
    eye = jnp.eye(n, dtype=w.dtype)
    return (eye[:, None, :, None] * w[:, :, None, :]).reshape(n * bw, n * bw)


def _layer(l, x2, B, S, c, ada_w, ada_b, norm1_g, w_in, q_norm_g, k_norm_g, lambda_q1, lambda_k1, lambda_q2,
           lambda_k2, attn_subln_g, conv_w, conv_b, lru_wa, lru_ba, lru_wx, lru_bx, lru_lambda, lru_out_g,
           w_out, norm2_g, router_w, router_b, w_gate_up, b_gate_up, w_down, b_down):
    T, D = x2.shape
    lam_init = 0.8 - 0.6 * math.exp(-0.3 * l)
    aw = D // 2
    lw = D - aw
    heads = aw // ATTN_DV
    qkw = heads * 2 * ATTN_DK

    mod, lam = _adaln(c, ada_w, ada_b, lambda_q1, lambda_k1, lambda_q2, lambda_k2, lam_init)
    mod3 = mod.reshape(c.shape[0], 6, D)

    reps = qkw // ATTN_DK
    gq = (jnp.tile(q_norm_g, reps) * (ATTN_DK ** -0.5 * math.log2(math.e))).reshape(1, qkw)
    gk = jnp.tile(k_norm_g, reps).reshape(1, qkw)
    tm = _tile(S, 512)
    q, k, v, xr, gr = _in_proj(x2, mod3, norm1_g.reshape(1, D), w_in.astype(BF16), gq, gk, S, _tile(S, 1024),
                               qkw, aw, lw)

    attn = _attention(q, k, v, lam, attn_subln_g.reshape(1, ATTN_DV), B, S, _tile(S, 1024), _tile(S, 512),
                      1.0 - lam_init)

    w_gates = jnp.concatenate([_block_diag(lru_wa), _block_diag(lru_wx)], axis=1).astype(BF16)
    b_gates = jnp.concatenate([lru_ba, lru_bx]).reshape(1, 2 * lw)
    lru = _lru(xr, gr, conv_w, conv_b.reshape(1, lw), w_gates, b_gates, lru_lambda.reshape(1, lw),
               lru_out_g.reshape(1, lw), B, S, _tile(S, 512))

    x1, h2t, top_i, gates, rank, tile_cnt = _out_proj(
        attn, lru, x2, mod3, w_out.astype(BF16), norm2_g.reshape(1, D), router_w.T,
        router_b.reshape(N_EXPERTS, 1), S, tm)

    tile_cnt = tile_cnt[:, :, 0]
    counts = jnp.sum(tile_cnt, axis=0)
    padded = ((counts + MOE_BLOCK - 1) // MOE_BLOCK) * MOE_BLOCK
    pad_ends = jnp.cumsum(padded)
    pad_starts = pad_ends - padded
    n_blocks = (T * TOP_K) // MOE_BLOCK + N_EXPERTS
    blk_start = jnp.arange(n_blocks, dtype=I32) * MOE_BLOCK
    owner = blk_start[:, None] >= pad_ends[None, :]
    block_expert = jnp.minimum(jnp.sum(owner, axis=1), N_EXPERTS - 1).astype(I32)
    onehot = block_expert[:, None] == jnp.arange(N_EXPERTS, dtype=I32)[None, :]
    row_end = jnp.sum(jnp.where(onehot, (pad_starts + counts)[None, :], 0), axis=1)
    block_valid = jnp.clip(row_end - blk_start, 0, MOE_BLOCK).astype(I32)
    tile_off = jnp.cumsum(tile_cnt, axis=1) - tile_cnt
    tile_dst = pad_starts[None, :] + jnp.cumsum(tile_cnt, axis=0) - tile_cnt
    flat = lambda a: a.reshape(-1).astype(I32)

    meta = _slots(flat(tile_off), top_i, rank, gates, tm).reshape(-1)
    xs = _dispatch(flat(tile_cnt), flat(tile_off), flat(tile_dst), meta, h2t, n_blocks * MOE_BLOCK, tm)
    ys = _experts(block_expert, block_valid, xs, w_gate_up.astype(BF16), b_gate_up,
                  w_down.astype(BF16), b_down)
    return _combine(flat(tile_cnt), flat(tile_off), flat(tile_dst), meta, ys, x1, mod3, S, tm)


def kernel(x, c, ada_w, ada_b, norm1_g, w_in, q_norm_g, k_norm_g, lambda_q1, lambda_k1, lambda_q2, lambda_k2,
           attn_subln_g, conv_w, conv_b, lru_wa, lru_ba, lru_wx, lru_bx, lru_lambda, lru_out_g, w_out, norm2_g,
           router_w, router_b, w_gate_up, b_gate_up, w_down, b_down):
    B, S, D = x.shape
    params = (ada_w, ada_b, norm1_g, w_in, q_norm_g, k_norm_g, lambda_q1, lambda_k1, lambda_q2, lambda_k2,
              attn_subln_g, conv_w, conv_b, lru_wa, lru_ba, lru_wx, lru_bx, lru_lambda, lru_out_g, w_out,
              norm2_g, router_w, router_b, w_gate_up, b_gate_up, w_down, b_down)
    x2 = x.reshape(B * S, D)
    for l in range(ada_w.shape[0]):
        x2 = _layer(l, x2, B, S, c, *[p[l] for p in params])
    return x2.reshape(B, S, D)
```

```python
import functools
import math

import jax
import jax.numpy as jnp
from jax import lax
from jax.experimental import pallas as pl
from jax.experimental.pallas import tpu as pltpu

F32 = jnp.float32
BF16 = jnp.bfloat16
I32 = jnp.int32

ATTN_DK = 64
ATTN_DV = 2 * ATTN_DK
CONV_W = 4
LRU_C = 8.0
N_EXPERTS = 32
TOP_K = 4
SWIGLU_LIMIT = 7.0
SWIGLU_ALPHA = 1.702
MOE_BLOCK = 512
EPS = 1e-6
NEG_BIG = -1e30

LANES = 128
SUBLANES = 8
VMEM_LIMIT = 56 * 1024 * 1024


def _cparams(sem):
    return pltpu.CompilerParams(dimension_semantics=sem, vmem_limit_bytes=VMEM_LIMIT)


def _split_hi_lo(x):
    hi = x.astype(BF16)
    lo = (x - hi.astype(F32)).astype(BF16)
    return hi, lo


def _sigmoid(x):
    return 0.5 * jnp.tanh(0.5 * x) + 0.5


def _dot(a, b):
    return jnp.dot(a, b, preferred_element_type=F32)


def _dot_tb(a, b):
    return lax.dot_general(a, b, (((1,), (1,)), ((), ())), preferred_element_type=F32)


def _adaln_kernel(lam_init, c_ref, w_ref, b_ref, lq1, lk1, lq2, lk2, mod_ref, lam_ref):
    c = c_ref[...]
    s = c * jax.nn.sigmoid(c)
    s_hi, s_lo = _split_hi_lo(s)
    w = w_ref[...]
    w_hi, w_lo = _split_hi_lo(w)
    mod_ref[...] = _dot(s_hi, w_hi) + _dot(s_hi, w_lo) + _dot(s_lo, w_hi) + b_ref[...]
    d1 = jnp.sum(lq1[...] * lk1[...], axis=-1, keepdims=True)
    d2 = jnp.sum(lq2[...] * lk2[...], axis=-1, keepdims=True)
    lam = jnp.exp(d1) - jnp.exp(d2) + lam_init
    lam_ref[...] = jnp.broadcast_to(lam, lam_ref.shape)


def _adaln(c, ada_w, ada_b, lq1, lk1, lq2, lk2, lam_init):
    B, D = c.shape
    n = ada_w.shape[1] // D
    vec = lambda: pl.BlockSpec((1, ATTN_DK), lambda j: (0, 0))
    return pl.pallas_call(
        functools.partial(_adaln_kernel, lam_init),
        grid=(n,),
        in_specs=[
            pl.BlockSpec((B, D), lambda j: (0, 0)),
            pl.BlockSpec((D, D), lambda j: (0, j)),
            pl.BlockSpec((1, D), lambda j: (0, j)),
            vec(), vec(), vec(), vec(),
        ],
        out_specs=[
            pl.BlockSpec((B, D), lambda j: (0, j)),
            pl.BlockSpec((1, LANES), lambda j: (0, 0)),
        ],
        out_shape=[
            jax.ShapeDtypeStruct((B, n * D), F32),
            jax.ShapeDtypeStruct((1, LANES), F32),
        ],
        compiler_params=_cparams(("arbitrary",)),
        name="adaln",
    )(c, ada_w, ada_b.reshape(1, -1), lq1.reshape(1, -1), lk1.reshape(1, -1),
      lq2.reshape(1, -1), lk2.reshape(1, -1))


def _rms_modulate(x, g, shift, scale):
    ms = jnp.mean(x * x, axis=-1, keepdims=True)
    y = x * lax.rsqrt(ms + EPS) * g
    return y * (1.0 + scale) + shift


def _group_rms_scale(q, group_ones):
    ss = _dot((q * q).astype(BF16), group_ones)
    return lax.rsqrt(ss * (1.0 / ATTN_DK) + EPS)


def _in_proj_kernel(qkw, aw, lw, x_ref, mod_ref, g_ref, w_ref, gq_ref, gk_ref, ones_ref,
                    q_ref, k_ref, v_ref, xr_ref, gr_ref):
    h = _rms_modulate(x_ref[...], g_ref[...], mod_ref[0:1, :], mod_ref[1:2, :])
    hb = h.astype(BF16)
    group_ones = ones_ref[...]
    o = 0
    q = _dot(hb, w_ref[:, o:o + qkw]); o += qkw
    q_ref[...] = (q * _group_rms_scale(q, group_ones) * gq_ref[...]).astype(BF16)
    k = _dot(hb, w_ref[:, o:o + qkw]); o += qkw
    k_ref[...] = (k * _group_rms_scale(k, group_ones) * gk_ref[...]).astype(BF16)
    v_ref[...] = _dot(hb, w_ref[:, o:o + aw]).astype(BF16); o += aw
    xr_ref[...] = _dot(hb, w_ref[:, o:o + lw]); o += lw
    gr_ref[...] = _dot(hb, w_ref[:, o:o + lw])


def _in_proj(x2, mod3, norm_g, w_in_b, gq, gk, S, tm, qkw, aw, lw):
    T, D = x2.shape
    tps = S // tm
    grp = jnp.arange(qkw, dtype=I32) // ATTN_DK
    group_ones = (grp[:, None] == grp[None, :]).astype(BF16)
    const = lambda shape: pl.BlockSpec(shape, lambda i: (0,) * len(shape))
    row = lambda w: pl.BlockSpec((tm, w), lambda i: (i, 0))
    return pl.pallas_call(
        functools.partial(_in_proj_kernel, qkw, aw, lw),
        grid=(T // tm,),
        in_specs=[
            row(D),
            pl.BlockSpec((None, 6, D), lambda i: (i // tps, 0, 0)),
            const((1, D)),
            const(w_in_b.shape),
            const((1, qkw)), const((1, qkw)),
            const((qkw, qkw)),
        ],
        out_specs=[row(qkw), row(qkw), row(aw), row(lw), row(lw)],
        out_shape=[
            jax.ShapeDtypeStruct((T, qkw), BF16),
            jax.ShapeDtypeStruct((T, qkw), BF16),
            jax.ShapeDtypeStruct((T, aw), BF16),
            jax.ShapeDtypeStruct((T, lw), F32),
            jax.ShapeDtypeStruct((T, lw), F32),
        ],
        compiler_params=_cparams(("arbitrary",)),
        name="in_proj",
    )(x2, mod3, norm_g, w_in_b, gq, gk, group_ones)


ATTN_ROW_CHUNK = 128


def _attn_kernel(bq, bk, out_scale, q_ref, k_ref, v_ref, lam_ref, g_ref, o_ref,
                 qq_ref, vp_ref, s_ref, sd_ref, m_ref, acc_ref):
    i = pl.program_id(2)
    rc = ATTN_ROW_CHUNK
    n_chunks = 2 * bq // rc

    @pl.when(i == 0)
    def _():
        col = lax.broadcasted_iota(I32, (vp_ref.shape[0], ATTN_DV), 1)
        vp_ref[:, :ATTN_DV] = v_ref[...]
        vp_ref[:, ATTN_DV:] = jnp.where(col == 0, 1.0, 0.0).astype(BF16)

    q = q_ref[...]
    lane = lax.broadcasted_iota(I32, q.shape, 1)
    zero = jnp.zeros_like(q)
    qq_ref[0:bq, :] = jnp.where(lane < ATTN_DK, q, zero)
    qq_ref[bq:, :] = jnp.where(lane >= ATTN_DK, q, zero)
    m_ref[...] = jnp.full(m_ref.shape, NEG_BIG, F32)
    acc_ref[...] = jnp.zeros_like(acc_ref)

    def chunk(r):
        return pl.ds(r * rc, rc)

    def scores(r, start, kw):
        return _dot_tb(qq_ref[chunk(r), :], k_ref[pl.ds(start, kw), :])

    def softmax_pv(s, r, start):
        rows = chunk(r)
        kw = s.shape[1]
        m_old = m_ref[rows, :]
        m_new = jnp.maximum(m_old, jnp.max(s, axis=-1, keepdims=True))
        alpha = jnp.exp2(m_old - m_new)
        p = jnp.exp2(s - jnp.tile(m_new, (1, kw // LANES)))
        pv = _dot(p.astype(BF16), vp_ref[pl.ds(start, kw), :])
        acc_ref[rows, :] = acc_ref[rows, :] * jnp.tile(alpha, (1, 2)) + pv
        m_ref[rows, :] = m_new

    for r in range(n_chunks):
        s_ref[chunk(r), :] = scores(r, 0, bk)

    def body(j, carry):
        start = pl.multiple_of(j * bk, bk)
        for r in range(n_chunks):
            s = s_ref[chunk(r), :]
            s_ref[chunk(r), :] = scores(r, start + bk, bk)
            softmax_pv(s, r, start)
        return carry

    n_full = i * (bq // bk)
    lax.fori_loop(0, n_full, body, 0)

    start = pl.multiple_of(n_full * bk, bk)
    tri = (lax.broadcasted_iota(I32, (rc, rc), 1) <= lax.broadcasted_iota(I32, (rc, rc), 0))
    first_key = [(r * rc) % bq for r in range(n_chunks)]
    for r in range(n_chunks):
        if first_key[r] + rc > bk:
            sd_ref[chunk(r), 0:first_key[r] + rc - bk] = scores(r, start + bk, first_key[r] + rc - bk)
    for r in range(n_chunks):
        q0 = first_key[r]
        kw = q0 + rc
        parts = [s_ref[chunk(r), 0:min(q0, bk)]] if q0 else []
        if q0 > bk:
            parts.append(sd_ref[chunk(r), 0:q0 - bk])
        last = s_ref[chunk(r), q0:kw] if kw <= bk else sd_ref[chunk(r), q0 - bk:kw - bk]
        parts.append(jnp.where(tri, last, NEG_BIG))
        softmax_pv(parts[0] if len(parts) == 1 else jnp.concatenate(parts, axis=1), r, start)

    acc = acc_ref[...]
    o = acc[:, :ATTN_DV] / acc[:, ATTN_DV:ATTN_DV + 1]
    a = o[:bq] - lam_ref[0:1, 0:1] * o[bq:]
    ms = jnp.mean(a * a, axis=-1, keepdims=True)
    o_ref[...] = (a * lax.rsqrt(ms + EPS) * g_ref[...] * out_scale).astype(BF16)


def _attention(q, k, v, lam, subln_g, B, S, bq, bk, out_scale):
    T, qkw = q.shape
    H = qkw // ATTN_DV
    nq = S // bq
    return pl.pallas_call(
        functools.partial(_attn_kernel, bq, bk, out_scale),
        grid=(B, H, nq),
        in_specs=[
            pl.BlockSpec((bq, ATTN_DV), lambda b, h, i: (b * nq + i, h)),
            pl.BlockSpec((S, ATTN_DV), lambda b, h, i: (b, h)),
            pl.BlockSpec((S, ATTN_DV), lambda b, h, i: (b, h)),
            pl.BlockSpec((1, LANES), lambda b, h, i: (0, 0)),
            pl.BlockSpec((1, ATTN_DV), lambda b, h, i: (0, 0)),
        ],
        out_specs=pl.BlockSpec((bq, ATTN_DV), lambda b, h, i: (b * nq + i, h)),
        out_shape=jax.ShapeDtypeStruct((T, H * ATTN_DV), BF16),
        scratch_shapes=[
            pltpu.VMEM((2 * bq, ATTN_DV), BF16),
            pltpu.VMEM((S, 2 * ATTN_DV), BF16),
            pltpu.VMEM((2 * bq, bk), F32),
            pltpu.VMEM((2 * bq, max(bq - bk, LANES)), F32),
            pltpu.VMEM((2 * bq, LANES), F32),
            pltpu.VMEM((2 * bq, 2 * ATTN_DV), F32),
        ],
        compiler_params=_cparams(("arbitrary", "arbitrary", "arbitrary")),
        name="attention",
    )(q, k, v, lam, subln_g)


def _lru_kernel(tc, cw, x_ref, gate_ref, cw_ref, cb_ref, wg_ref, bg_ref, lam_ref, og_ref,
                o_ref, ext_ref, a_ref, b_ref, hc_ref):
    c = pl.program_id(1)

    @pl.when(c == 0)
    def _():
        ext_ref[0:SUBLANES, :] = jnp.zeros((SUBLANES, cw), F32)
        hc_ref[...] = jnp.zeros_like(hc_ref)

    ext_ref[SUBLANES:SUBLANES + tc, :] = x_ref[...]
    xc = cb_ref[...] + cw_ref[CONV_W - 1:CONV_W, :] * x_ref[...]
    for w in range(CONV_W - 1):
        sh = CONV_W - 1 - w
        xc = xc + cw_ref[w:w + 1, :] * ext_ref[SUBLANES - sh:SUBLANES - sh + tc, :]
    tail = ext_ref[tc:tc + SUBLANES, :]
    ext_ref[0:SUBLANES, :] = tail

    g = _dot(xc.astype(BF16), wg_ref[...]) + bg_ref[...]
    r = _sigmoid(g[:, :cw])
    ig = _sigmoid(g[:, cw:])
    nl = -lam_ref[...]
    softplus = jnp.maximum(nl, 0.0) + jnp.log1p(jnp.exp(-jnp.abs(nl)))
    log_a = (-LRU_C) * r * softplus
    th = jnp.tanh(log_a)
    a_ref[...] = jnp.exp(log_a)
    b_ref[...] = jnp.sqrt(-2.0 * th / (1.0 - th)) * (ig * xc)

    row = lax.broadcasted_iota(I32, (SUBLANES, cw), 0)
    has_src = {d: row >= d for d in (1, 2, 4)}

    def tile_scan(n, hc):
        rows = pl.ds(pl.multiple_of(n * SUBLANES, SUBLANES), SUBLANES)
        a = a_ref[rows, :]
        b = b_ref[rows, :]
        for d in (1, 2, 4):
            b = jnp.where(has_src[d], a * pltpu.roll(b, d, 0), 0.0) + b
            a = jnp.where(has_src[d], a * pltpu.roll(a, d, 0), a)
        h = b + a * hc
        b_ref[rows, :] = h
        return jnp.broadcast_to(h[SUBLANES - 1:SUBLANES, :], h.shape)

    hc_ref[...] = lax.fori_loop(0, tc // SUBLANES, tile_scan, hc_ref[...], unroll=4)

    gt = gate_ref[...]
    gelu = 0.5 * gt * (1.0 + jnp.tanh(math.sqrt(2.0 / math.pi) * (gt + 0.044715 * gt * gt * gt)))
    y = b_ref[...] * gelu
    ms = jnp.mean(y * y, axis=-1, keepdims=True)
    o_ref[...] = (y * lax.rsqrt(ms + EPS) * og_ref[...]).astype(BF16)


def _lru(xr, gr, conv_w, conv_b, w_gates, b_gates, lru_lambda, out_g, B, S, tc):
    T, cw = xr.shape
    nc = S // tc
    const = lambda shape: pl.BlockSpec(shape, lambda b, c: (0,) * len(shape))
    row = pl.BlockSpec((tc, cw), lambda b, c: (b * nc + c, 0))
    return pl.pallas_call(
        functools.partial(_lru_kernel, tc, cw),
        grid=(B, nc),
        in_specs=[row, row, const((CONV_W, cw)), const((1, cw)), const((cw, 2 * cw)),
                  const((1, 2 * cw)), const((1, cw)), const((1, cw))],
        out_specs=row,
        out_shape=jax.ShapeDtypeStruct((T, cw), BF16),
        scratch_shapes=[
            pltpu.VMEM((tc + SUBLANES, cw), F32),
            pltpu.VMEM((tc, cw), F32),
            pltpu.VMEM((tc, cw), F32),
            pltpu.VMEM((SUBLANES, cw), F32),
        ],
        compiler_params=_cparams(("arbitrary", "arbitrary")),
        name="lru",
    )(xr, gr, conv_w, conv_b, w_gates, b_gates, lru_lambda, out_g)


def _out_proj_kernel(tm, aw, attn_ref, lru_ref, x_ref, mod_ref, w_ref, g_ref, rw_ref, rb_ref, tri_ref,
                     x1_ref, h2_ref, ti_ref, gt_ref, rk_ref, cnt_ref):
    mix = _dot(attn_ref[...], w_ref[0:aw, :]) + _dot(lru_ref[...], w_ref[aw:, :])
    x1 = x_ref[...] + mod_ref[2:3, :] * mix
    x1_ref[...] = x1
    h2 = _rms_modulate(x1, g_ref[...], mod_ref[3:4, :], mod_ref[4:5, :])
    for j in range(h2.shape[1] // LANES):
        h2_ref[pl.ds(j, tm, stride=SUBLANES), :] = h2[:, j * LANES:(j + 1) * LANES]

    h_hi, h_lo = _split_hi_lo(h2)
    w_hi, w_lo = _split_hi_lo(rw_ref[...])
    lg = _dot_tb(w_hi, h_hi) + _dot_tb(w_hi, h_lo) + _dot_tb(w_lo, h_hi) + rb_ref[...]

    eidx = lax.broadcasted_iota(I32, lg.shape, 0)
    picked = jnp.zeros(lg.shape, F32)
    vals, idxs = [], []
    for _ in range(TOP_K):
        m = jnp.max(lg, axis=0, keepdims=True)
        idx = jnp.min(jnp.where(lg == m, eidx, N_EXPERTS), axis=0, keepdims=True)
        sel = eidx == idx
        lg = jnp.where(sel, -jnp.inf, lg)
        picked = picked + sel.astype(F32)
        vals.append(m)
        idxs.append(idx)
    es = [jnp.exp(v - vals[0]) for v in vals]
    den = es[0] + es[1] + es[2] + es[3]
    gt_ref[...] = jnp.concatenate([e / den for e in es], axis=0)
    ti_ref[...] = jnp.concatenate(idxs, axis=0)

    before = _dot(picked.astype(BF16), tri_ref[...])
    ranks = [jnp.sum(jnp.where(eidx == idx, before, 0.0), axis=0, keepdims=True) for idx in idxs]
    rk_ref[...] = jnp.concatenate(ranks, axis=0).astype(I32)
    cnt = jnp.sum(picked, axis=1, keepdims=True)
    cnt_ref[...] = jnp.broadcast_to(cnt, cnt_ref.shape).astype(I32)


def _out_proj(attn, lru, x2, mod3, w_out_b, norm_g, router_wt, router_b, S, tm):
    T, D = x2.shape
    aw = attn.shape[1]
    lw = lru.shape[1]
    tps = S // tm
    tri = (jnp.arange(tm, dtype=I32)[:, None] < jnp.arange(tm, dtype=I32)[None, :]).astype(BF16)
    const = lambda shape: pl.BlockSpec(shape, lambda i: (0,) * len(shape))
    row = lambda w: pl.BlockSpec((tm, w), lambda i: (i, 0))
    col = pl.BlockSpec((TOP_K, tm), lambda i: (0, i))
    return pl.pallas_call(
        functools.partial(_out_proj_kernel, tm, aw),
        grid=(T // tm,),
        in_specs=[
            row(aw), row(lw), row(D),
            pl.BlockSpec((None, 6, D), lambda i: (i // tps, 0, 0)),
            const((D, D)), const((1, D)), const((N_EXPERTS, D)), const((N_EXPERTS, 1)),
            const((tm, tm)),
        ],
        out_specs=[
            row(D),
            pl.BlockSpec((tm * SUBLANES, LANES), lambda i: (i, 0)),
            col, col, col,
            pl.BlockSpec((None, N_EXPERTS, LANES), lambda i: (i, 0, 0)),
        ],
        out_shape=[
            jax.ShapeDtypeStruct((T, D), F32),
            jax.ShapeDtypeStruct((T * SUBLANES, LANES), F32),
            jax.ShapeDtypeStruct((TOP_K, T), I32),
            jax.ShapeDtypeStruct((TOP_K, T), F32),
            jax.ShapeDtypeStruct((TOP_K, T), I32),
            jax.ShapeDtypeStruct((T // tm, N_EXPERTS, LANES), I32),
        ],
        compiler_params=_cparams(("arbitrary",)),
        name="out_proj",
    )(attn, lru, x2, mod3, w_out_b, norm_g, router_wt, router_b, tri)


def _slots_kernel(off_ref, ti_ref, rk_ref, gt_ref, meta_ref):
    i = pl.program_id(0)
    ti = ti_ref[...]
    base = jnp.zeros(ti.shape, I32)
    for e in range(N_EXPERTS):
        base = jnp.where(ti == e, off_ref[i * N_EXPERTS + e], base)
    meta_ref[0:TOP_K, :] = base + rk_ref[...] * SUBLANES
    meta_ref[TOP_K:, :] = lax.bitcast_convert_type(gt_ref[...], I32)


def _slots(run_off, top_i, rank, gates, t):
    T = top_i.shape[1]
    col = pl.BlockSpec((TOP_K, t), lambda i, off: (0, i))
    grid_spec = pltpu.PrefetchScalarGridSpec(
        num_scalar_prefetch=1,
        grid=(T // t,),
        in_specs=[col, col, col],
        out_specs=pl.BlockSpec((None, 2 * TOP_K, t), lambda i, off: (i, 0, 0)),
    )
    return pl.pallas_call(
        _slots_kernel,
        grid_spec=grid_spec,
        out_shape=jax.ShapeDtypeStruct((T // t, 2 * TOP_K, t), I32),
        compiler_params=_cparams(("arbitrary",)),
        name="slots",
    )(run_off, top_i, rank, gates)


def _rows(ref, first_row, n_rows):
    return ref.at[pl.ds(pl.multiple_of(first_row * SUBLANES, SUBLANES), n_rows * SUBLANES), :]


def _row_at(ref, sublane_offset):
    return ref.at[pl.ds(pl.multiple_of(sublane_offset, SUBLANES), SUBLANES), :]


def _fetch_tile(meta_hbm, smem, sems, t, n_words):
    i = pl.program_id(0)
    n = pl.num_programs(0)
    slot = i % 2
    per_tile = 2 * TOP_K * t

    def copy(tile, s):
        src = meta_hbm.at[pl.ds(pl.multiple_of(tile * per_tile, per_tile), n_words)]
        return pltpu.make_async_copy(src, smem.at[pl.ds(pl.multiple_of(s * n_words, n_words), n_words)], sems.at[s])

    @pl.when(i == 0)
    def _():
        copy(0, 0).start()

    copy(i, slot).wait()

    @pl.when(i + 1 < n)
    def _():
        copy(i + 1, 1 - slot).start()

    return slot * n_words


def _run_copies(t, tile, cnt_ref, off_ref, dst_ref, make_copy):
    def expert(e, carry):
        n = cnt_ref[tile * N_EXPERTS + e]
        off = off_ref[tile * N_EXPERTS + e]
        dst = dst_ref[tile * N_EXPERTS + e]
        for b in reversed(range(t.bit_length())):
            size = SUBLANES << b
            done = n & ~(2 * size - 1)

            @pl.when((n & size) != 0)
            def _():
                make_copy(off + done, dst + done, size).start()
        return carry

    lax.fori_loop(0, N_EXPERTS, expert, 0)


def _span(ref, first_sublane, n_sublanes):
    return ref.at[pl.ds(pl.multiple_of(first_sublane, SUBLANES), n_sublanes), :]


def _dispatch_kernel(t, cnt_ref, off_ref, dst_ref, meta_hbm, h_ref, xs_hbm, meta_smem, stage_ref, psem, sem):
    i = pl.program_id(0)
    n = pl.num_programs(0)
    slot = i % 2
    base = _fetch_tile(meta_hbm, meta_smem, psem, t, TOP_K * t)
    stage = stage_ref.at[slot]

    def place(r, carry):
        row = _rows(h_ref, r, 1)[...]
        for k in range(TOP_K):
            _row_at(stage, meta_smem[base + r + k * t])[...] = row
        return carry

    lax.fori_loop(0, t, place, 0, unroll=8)

    def all_runs(s):
        return pltpu.make_async_copy(stage_ref.at[s], _rows(xs_hbm, 0, TOP_K * t), sem.at[s])

    _run_copies(t, i, cnt_ref, off_ref, dst_ref,
                lambda a, b, m: pltpu.make_async_copy(_span(stage, a, m), _span(xs_hbm, b, m), sem.at[slot]))

    @pl.when(i > 0)
    def _():
        all_runs(1 - slot).wait()

    @pl.when(i == n - 1)
    def _():
        all_runs(slot).wait()


def _dispatch(tile_cnt, tile_off, tile_dst, meta, h2t, n_slots, t):
    n_tiles = meta.shape[0] // (2 * TOP_K * t)
    grid_spec = pltpu.PrefetchScalarGridSpec(
        num_scalar_prefetch=3,
        grid=(n_tiles,),
        in_specs=[pl.BlockSpec(memory_space=pl.ANY),
                  pl.BlockSpec((t * SUBLANES, LANES), lambda i, *_: (i, 0))],
        out_specs=pl.BlockSpec(memory_space=pl.ANY),
        scratch_shapes=[
            pltpu.SMEM((2 * TOP_K * t,), I32),
            pltpu.VMEM((2, TOP_K * t * SUBLANES, LANES), F32),
            pltpu.SemaphoreType.DMA((2,)),
            pltpu.SemaphoreType.DMA((2,)),
        ],
    )
    return pl.pallas_call(
        functools.partial(_dispatch_kernel, t),
        grid_spec=grid_spec,
        out_shape=jax.ShapeDtypeStruct((n_slots * SUBLANES, LANES), F32),
        compiler_params=_cparams(("arbitrary",)),
        name="dispatch",
    )(tile_cnt, tile_off, tile_dst, meta, h2t)


def _experts_kernel(ff, be_ref, nv_ref, xs_ref, wgu_ref, bgu_ref, wdn_ref, bdn_ref, ys_ref,
                    wgu_b, wdn_b, y_prev):
    i = pl.program_id(0)
    nv = nv_ref[i]
    n_slabs = xs_ref.shape[0] // MOE_BLOCK

    @pl.when(i == 0)
    def _():
        y_prev[...] = jnp.zeros_like(y_prev)

    @pl.when((i == 0) | (be_ref[i] != be_ref[jnp.maximum(i - 1, 0)]))
    def _():
        wgu_b[...] = wgu_ref[...].astype(BF16)
        wdn_b[...] = wdn_ref[...].astype(BF16)

    def flush():
        for j in range(n_slabs):
            ys_ref[pl.ds(j, MOE_BLOCK, stride=SUBLANES), :] = y_prev[:, j * LANES:(j + 1) * LANES]

    @pl.when(nv > 0)
    def _():
        flush()
        x = jnp.concatenate(
            [xs_ref[pl.ds(j, MOE_BLOCK, stride=SUBLANES), :] for j in range(n_slabs)], axis=1)
        x = jnp.where(lax.broadcasted_iota(I32, x.shape, 0) < nv, x, 0.0)
        gu = _dot(x.astype(BF16), wgu_b[...]) + bgu_ref[...]
        gate = jnp.minimum(gu[:, :ff], SWIGLU_LIMIT)
        up = jnp.clip(gu[:, ff:], -SWIGLU_LIMIT, SWIGLU_LIMIT)
        act = (up + 1.0) * (gate * _sigmoid(SWIGLU_ALPHA * gate))
        y_prev[...] = _dot(act.astype(BF16), wdn_b[...]) + bdn_ref[...]

    @pl.when(nv == 0)
    def _():
        flush()


def _experts(block_expert, block_valid, xs, w_gu, b_gu, w_dn, b_dn):
    n_blocks = block_expert.shape[0]
    E, D, ff2 = w_gu.shape
    ff = ff2 // 2
    block_expert = jnp.concatenate([block_expert, block_expert[-1:]])
    block_valid = jnp.concatenate([block_valid, jnp.zeros((1,), I32)])
    last = n_blocks - 1
    grid_spec = pltpu.PrefetchScalarGridSpec(
        num_scalar_prefetch=2,
        grid=(n_blocks + 1,),
        in_specs=[
            pl.BlockSpec((MOE_BLOCK * SUBLANES, LANES), lambda i, be, nv: (jnp.minimum(i, last), 0)),
            pl.BlockSpec((None, D, ff2), lambda i, be, nv: (be[i], 0, 0)),
            pl.BlockSpec((None, 1, ff2), lambda i, be, nv: (be[i], 0, 0)),
            pl.BlockSpec((None, ff, D), lambda i, be, nv: (be[i], 0, 0)),
            pl.BlockSpec((None, 1, D), lambda i, be, nv: (be[i], 0, 0)),
        ],
        out_specs=pl.BlockSpec((MOE_BLOCK * SUBLANES, LANES), lambda i, be, nv: (jnp.maximum(i - 1, 0), 0)),
        scratch_shapes=[
            pltpu.VMEM((D, ff2), BF16),
            pltpu.VMEM((ff, D), BF16),
            pltpu.VMEM((MOE_BLOCK, D), F32),
        ],
    )
    return pl.pallas_call(
        functools.partial(_experts_kernel, ff),
        grid_spec=grid_spec,
        out_shape=jax.ShapeDtypeStruct(xs.shape, F32),
        compiler_params=_cparams(("arbitrary",)),
        name="experts",
    )(block_expert, block_valid, xs, w_gu, b_gu.reshape(E, 1, ff2), w_dn, b_dn.reshape(E, 1, D))


def _combine_kernel(t, cnt_ref, off_ref, dst_ref, meta_hbm, ys_hbm, x1_ref, mod_ref, o_ref,
                    meta_smem, stage_ref, y_ref, psem, sem):
    i = pl.program_id(0)
    n = pl.num_programs(0)
    slot = i % 2
    base = _fetch_tile(meta_hbm, meta_smem, psem, t, 2 * TOP_K * t)

    def fetch_runs(tile, s):
        stage = stage_ref.at[s]
        _run_copies(t, tile, cnt_ref, off_ref, dst_ref,
                    lambda a, b, m: pltpu.make_async_copy(_span(ys_hbm, b, m), _span(stage, a, m), sem.at[s]))

    @pl.when(i == 0)
    def _():
        fetch_runs(0, 0)

    @pl.when(i + 1 < n)
    def _():
        fetch_runs(i + 1, 1 - slot)

    pltpu.make_async_copy(_rows(ys_hbm, 0, TOP_K * t), stage_ref.at[slot], sem.at[slot]).wait()
    stage = stage_ref.at[slot]

    def gather(r, carry):
        acc = None
        for k in range(TOP_K):
            gate = lax.bitcast_convert_type(meta_smem[base + r + (TOP_K + k) * t], F32)
            term = gate * _row_at(stage, meta_smem[base + r + k * t])[...]
            acc = term if acc is None else acc + term
        _rows(y_ref, r, 1)[...] = acc
        return carry

    lax.fori_loop(0, t, gather, 0, unroll=8)

    n_slabs = o_ref.shape[1] // LANES
    y = jnp.concatenate([y_ref[pl.ds(j, t, stride=SUBLANES), :] for j in range(n_slabs)], axis=1)
    o_ref[...] = x1_ref[...] + mod_ref[5:6, :] * y


def _combine(tile_cnt, tile_off, tile_dst, meta, ys, x1, mod3, S, t):
    T, D = x1.shape
    n_tiles = T // t
    tps = S // t
    grid_spec = pltpu.PrefetchScalarGridSpec(
        num_scalar_prefetch=3,
        grid=(n_tiles,),
        in_specs=[
            pl.BlockSpec(memory_space=pl.ANY),
            pl.BlockSpec(memory_space=pl.ANY),
            pl.BlockSpec((t, D), lambda i, *_: (i, 0)),
            pl.BlockSpec((None, 6, D), lambda i, *_: (i // tps, 0, 0)),
        ],
        out_specs=pl.BlockSpec((t, D), lambda i, *_: (i, 0)),
        scratch_shapes=[
            pltpu.SMEM((2 * 2 * TOP_K * t,), I32),
            pltpu.VMEM((2, TOP_K * t * SUBLANES, LANES), F32),
            pltpu.VMEM((t * SUBLANES, LANES), F32),
            pltpu.SemaphoreType.DMA((2,)),
            pltpu.SemaphoreType.DMA((2,)),
        ],
    )
    return pl.pallas_call(
        functools.partial(_combine_kernel, t),
        grid_spec=grid_spec,
        out_shape=jax.ShapeDtypeStruct((T, D), F32),
        compiler_params=_cparams(("arbitrary",)),
        name="combine",
    )(tile_cnt, tile_off, tile_dst, meta, ys, x1, mod3)


def _tile(n, target):
    t = min(n, target)
    while n % t:
        t //= 2
    return t


def _block_diag(w):
    n, bw, _ = w.shape
    eye = jnp.eye(n, dtype=w.dtype)
    return (eye[:, None, :, None] * w[:, :, None, :]).reshape(n * bw, n * bw)


def _layer(l, x2, B, S, c, ada_w, ada_b, norm1_g, w_in, q_norm_g, k_norm_g, lambda_q1, lambda_k1, lambda_q2,
           lambda_k2, attn_subln_g, conv_w, conv_b, lru_wa, lru_ba, lru_wx, lru_bx, lru_lambda, lru_out_g,
           w_out, norm2_g, router_w, router_b, w_gate_up, b_gate_up, w_down, b_down):
    T, D = x2.shape
    lam_init = 0.8 - 0.6 * math.exp(-0.3 * l)
    aw = D // 2
    lw = D - aw
    heads = aw // ATTN_DV
    qkw = heads * 2 * ATTN_DK

    mod, lam = _adaln(c, ada_w, ada_b, lambda_q1, lambda_k1, lambda_q2, lambda_k2, lam_init)
    mod3 = mod.reshape(c.shape[0], 6, D)

    reps = qkw // ATTN_DK
    gq = (jnp.tile(q_norm_g, reps) * (ATTN_DK ** -0.5 * math.log2(math.e))).reshape(1, qkw)
    gk = jnp.tile(k_norm_g, reps).reshape(1, qkw)
    tm = _tile(S, 512)
    q, k, v, xr, gr = _in_proj(x2, mod3, norm1_g.reshape(1, D), w_in.astype(BF16), gq, gk, S, _tile(S, 1024),
                               qkw, aw, lw)

    attn = _attention(q, k, v, lam, attn_subln_g.reshape(1, ATTN_DV), B, S, _tile(S, 1024), _tile(S, 512),
                      1.0 - lam_init)

    w_gates = jnp.concatenate([_block_diag(lru_wa), _block_diag(lru_wx)], axis=1).astype(BF16)
    b_gates = jnp.concatenate([lru_ba, lru_bx]).reshape(1, 2 * lw)
    lru = _lru(xr, gr, conv_w, conv_b.reshape(1, lw), w_gates, b_gates, lru_lambda.reshape(1, lw),
               lru_out_g.reshape(1, lw), B, S, _tile(S, 512))

    x1, h2t, top_i, gates, rank, tile_cnt = _out_proj(
        attn, lru, x2, mod3, w_out.astype(BF16), norm2_g.reshape(1, D), router_w.T,
        router_b.reshape(N_EXPERTS, 1), S, tm)

    tile_cnt = tile_cnt[:, :, 0]
    counts = jnp.sum(tile_cnt, axis=0)
    padded = ((counts + MOE_BLOCK - 1) // MOE_BLOCK) * MOE_BLOCK
    pad_ends = jnp.cumsum(padded)
    pad_starts = pad_ends - padded
    n_blocks = (T * TOP_K) // MOE_BLOCK + N_EXPERTS
    blk_start = jnp.arange(n_blocks, dtype=I32) * MOE_BLOCK
    owner = blk_start[:, None] >= pad_ends[None, :]
    block_expert = jnp.minimum(jnp.sum(owner, axis=1), N_EXPERTS - 1).astype(I32)
    onehot = block_expert[:, None] == jnp.arange(N_EXPERTS, dtype=I32)[None, :]
    row_end = jnp.sum(jnp.where(onehot, (pad_starts + counts)[None, :], 0), axis=1)
    block_valid = jnp.clip(row_end - blk_start, 0, MOE_BLOCK).astype(I32)
    tile_off = jnp.cumsum(tile_cnt, axis=1) - tile_cnt
    tile_dst = pad_starts[None, :] + jnp.cumsum(tile_cnt, axis=0) - tile_cnt
    flat = lambda a: (a * SUBLANES).reshape(-1).astype(I32)

    meta = _slots(flat(tile_off), top_i, rank, gates, tm).reshape(-1)
    xs = _dispatch(flat(tile_cnt), flat(tile_off), flat(tile_dst), meta, h2t, n_blocks * MOE_BLOCK, tm)
    ys = _experts(block_expert, block_valid, xs, w_gate_up, b_gate_up, w_down, b_down)
    return _combine(flat(tile_cnt), flat(tile_off), flat(tile_dst), meta, ys, x1, mod3, S, tm)


def kernel(x, c, ada_w, ada_b, norm1_g, w_in, q_norm_g, k_norm_g, lambda_q1, lambda_k1, lambda_q2, lambda_k2,
           attn_subln_g, conv_w, conv_b, lru_wa, lru_ba, lru_wx, lru_bx, lru_lambda, lru_out_g, w_out, norm2_g,
           router_w, router_b, w_gate_up, b_gate_up, w_down, b_down):
    B, S, D = x.shape
    params = (ada_w, ada_b, norm1_g, w_in, q_norm_g, k_norm_g, lambda_q1, lambda_k1, lambda_q2, lambda_k2,
              attn_subln_g, conv_w, conv_b, lru_wa, lru_ba, lru_wx, lru_bx, lru_lambda, lru_out_g, w_out,
              norm2_g, router_w, router_b, w_gate_up, b_gate_up, w_down, b_down)
    x2 = x.reshape(B * S, D)
    for l in range(ada_w.shape[0]):
        x2 = _layer(l, x2, B, S, c, *[p[l] for p in params])
    return x2.reshape(B, S, D)
```

```python
import functools
import math

import jax
import jax.numpy as jnp
from jax import lax
from jax.experimental import pallas as pl
from jax.experimental.pallas import tpu as pltpu

F32 = jnp.float32
BF16 = jnp.bfloat16
I32 = jnp.int32

ATTN_DK = 64
ATTN_DV = 2 * ATTN_DK
CONV_W = 4
LRU_C = 8.0
N_EXPERTS = 32
TOP_K = 4
SWIGLU_LIMIT = 7.0
SWIGLU_ALPHA = 1.702
MOE_BLOCK = 512
EPS = 1e-6
NEG_BIG = -1e30

LANES = 128
SUBLANES = 8
VMEM_LIMIT = 56 * 1024 * 1024


def _cparams(sem):
    return pltpu.CompilerParams(dimension_semantics=sem, vmem_limit_bytes=VMEM_LIMIT)


def _split_hi_lo(x):
    hi = x.astype(BF16)
    lo = (x - hi.astype(F32)).astype(BF16)
    return hi, lo


def _sigmoid(x):
    return 0.5 * jnp.tanh(0.5 * x) + 0.5


def _dot(a, b):
    return jnp.dot(a, b, preferred_element_type=F32)


def _dot_tb(a, b):
    return lax.dot_general(a, b, (((1,), (1,)), ((), ())), preferred_element_type=F32)


def _adaln_kernel(lam_init, c_ref, w_ref, b_ref, lq1, lk1, lq2, lk2, mod_ref, lam_ref):
    c = c_ref[...]
    s = c * jax.nn.sigmoid(c)
    s_hi, s_lo = _split_hi_lo(s)
    w = w_ref[...]
    w_hi, w_lo = _split_hi_lo(w)
    mod_ref[...] = _dot(s_hi, w_hi) + _dot(s_hi, w_lo) + _dot(s_lo, w_hi) + b_ref[...]
    d1 = jnp.sum(lq1[...] * lk1[...], axis=-1, keepdims=True)
    d2 = jnp.sum(lq2[...] * lk2[...], axis=-1, keepdims=True)
    lam = jnp.exp(d1) - jnp.exp(d2) + lam_init
    lam_ref[...] = jnp.broadcast_to(lam, lam_ref.shape)


def _adaln(c, ada_w, ada_b, lq1, lk1, lq2, lk2, lam_init):
    B, D = c.shape
    n = ada_w.shape[1] // D
    vec = lambda: pl.BlockSpec((1, ATTN_DK), lambda j: (0, 0))
    return pl.pallas_call(
        functools.partial(_adaln_kernel, lam_init),
        grid=(n,),
        in_specs=[
            pl.BlockSpec((B, D), lambda j: (0, 0)),
            pl.BlockSpec((D, D), lambda j: (0, j)),
            pl.BlockSpec((1, D), lambda j: (0, j)),
            vec(), vec(), vec(), vec(),
        ],
        out_specs=[
            pl.BlockSpec((B, D), lambda j: (0, j)),
            pl.BlockSpec((1, LANES), lambda j: (0, 0)),
        ],
        out_shape=[
            jax.ShapeDtypeStruct((B, n * D), F32),
            jax.ShapeDtypeStruct((1, LANES), F32),
        ],
        compiler_params=_cparams(("arbitrary",)),
        name="adaln",
    )(c, ada_w, ada_b.reshape(1, -1), lq1.reshape(1, -1), lk1.reshape(1, -1),
      lq2.reshape(1, -1), lk2.reshape(1, -1))


def _rms_modulate(x, g, shift, scale):
    ms = jnp.mean(x * x, axis=-1, keepdims=True)
    y = x * lax.rsqrt(ms + EPS) * g
    return y * (1.0 + scale) + shift


def _group_rms_scale(q, group_ones):
    ss = _dot((q * q).astype(BF16), group_ones)
    return lax.rsqrt(ss * (1.0 / ATTN_DK) + EPS)


def _in_proj_kernel(qkw, aw, lw, x_ref, mod_ref, g_ref, w_ref, gq_ref, gk_ref, ones_ref,
                    q_ref, k_ref, v_ref, xr_ref, gr_ref):
    h = _rms_modulate(x_ref[...], g_ref[...], mod_ref[0:1, :], mod_ref[1:2, :])
    hb = h.astype(BF16)
    group_ones = ones_ref[...]
    o = 0
    q = _dot(hb, w_ref[:, o:o + qkw]); o += qkw
    q_ref[...] = (q * _group_rms_scale(q, group_ones) * gq_ref[...]).astype(BF16)
    k = _dot(hb, w_ref[:, o:o + qkw]); o += qkw
    k_ref[...] = (k * _group_rms_scale(k, group_ones) * gk_ref[...]).astype(BF16)
    v_ref[...] = _dot(hb, w_ref[:, o:o + aw]).astype(BF16); o += aw
    xr_ref[...] = _dot(hb, w_ref[:, o:o + lw]); o += lw
    gr_ref[...] = _dot(hb, w_ref[:, o:o + lw])


def _in_proj(x2, mod3, norm_g, w_in_b, gq, gk, S, tm, qkw, aw, lw):
    T, D = x2.shape
    tps = S // tm
    grp = jnp.arange(qkw, dtype=I32) // ATTN_DK
    group_ones = (grp[:, None] == grp[None, :]).astype(BF16)
    const = lambda shape: pl.BlockSpec(shape, lambda i: (0,) * len(shape))
    row = lambda w: pl.BlockSpec((tm, w), lambda i: (i, 0))
    return pl.pallas_call(
        functools.partial(_in_proj_kernel, qkw, aw, lw),
        grid=(T // tm,),
        in_specs=[
            row(D),
            pl.BlockSpec((None, 6, D), lambda i: (i // tps, 0, 0)),
            const((1, D)),
            const(w_in_b.shape),
            const((1, qkw)), const((1, qkw)),
            const((qkw, qkw)),
        ],
        out_specs=[row(qkw), row(qkw), row(aw), row(lw), row(lw)],
        out_shape=[
            jax.ShapeDtypeStruct((T, qkw), BF16),
            jax.ShapeDtypeStruct((T, qkw), BF16),
            jax.ShapeDtypeStruct((T, aw), BF16),
            jax.ShapeDtypeStruct((T, lw), F32),
            jax.ShapeDtypeStruct((T, lw), F32),
        ],
        compiler_params=_cparams(("arbitrary",)),
        name="in_proj",
    )(x2, mod3, norm_g, w_in_b, gq, gk, group_ones)


ATTN_ROW_CHUNK = 128


def _attn_kernel(bq, bk, out_scale, q_ref, k_ref, v_ref, lam_ref, g_ref, o_ref,
                 qq_ref, vp_ref, s_ref, sd_ref, m_ref, acc_ref):
    i = pl.program_id(2)
    rc = ATTN_ROW_CHUNK
    n_chunks = 2 * bq // rc

    @pl.when(i == 0)
    def _():
        col = lax.broadcasted_iota(I32, (vp_ref.shape[0], ATTN_DV), 1)
        vp_ref[:, :ATTN_DV] = v_ref[...]
        vp_ref[:, ATTN_DV:] = jnp.where(col == 0, 1.0, 0.0).astype(BF16)

    q = q_ref[...]
    lane = lax.broadcasted_iota(I32, q.shape, 1)
    zero = jnp.zeros_like(q)
    qq_ref[0:bq, :] = jnp.where(lane < ATTN_DK, q, zero)
    qq_ref[bq:, :] = jnp.where(lane >= ATTN_DK, q, zero)
    m_ref[...] = jnp.full(m_ref.shape, NEG_BIG, F32)
    acc_ref[...] = jnp.zeros_like(acc_ref)

    def chunk(r):
        return pl.ds(r * rc, rc)

    def scores(r, start, kw):
        return _dot_tb(qq_ref[chunk(r), :], k_ref[pl.ds(start, kw), :])

    def softmax_pv(s, r, start):
        rows = chunk(r)
        kw = s.shape[1]
        m_old = m_ref[rows, :]
        m_new = jnp.maximum(m_old, jnp.max(s, axis=-1, keepdims=True))
        alpha = jnp.exp2(m_old - m_new)
        p = jnp.exp2(s - jnp.tile(m_new, (1, kw // LANES)))
        pv = _dot(p.astype(BF16), vp_ref[pl.ds(start, kw), :])
        acc_ref[rows, :] = acc_ref[rows, :] * jnp.tile(alpha, (1, 2)) + pv
        m_ref[rows, :] = m_new

    for r in range(n_chunks):
        s_ref[chunk(r), :] = scores(r, 0, bk)

    def body(j, carry):
        start = pl.multiple_of(j * bk, bk)
        for r in range(n_chunks):
            s = s_ref[chunk(r), :]
            s_ref[chunk(r), :] = scores(r, start + bk, bk)
            softmax_pv(s, r, start)
        return carry

    n_full = i * (bq // bk)
    lax.fori_loop(0, n_full, body, 0)

    start = pl.multiple_of(n_full * bk, bk)
    tri = (lax.broadcasted_iota(I32, (rc, rc), 1) <= lax.broadcasted_iota(I32, (rc, rc), 0))
    first_key = [(r * rc) % bq for r in range(n_chunks)]
    for r in range(n_chunks):
        if first_key[r] + rc > bk:
            sd_ref[chunk(r), 0:first_key[r] + rc - bk] = scores(r, start + bk, first_key[r] + rc - bk)
    for r in range(n_chunks):
        q0 = first_key[r]
        kw = q0 + rc
        parts = [s_ref[chunk(r), 0:min(q0, bk)]] if q0 else []
        if q0 > bk:
            parts.append(sd_ref[chunk(r), 0:q0 - bk])
        last = s_ref[chunk(r), q0:kw] if kw <= bk else sd_ref[chunk(r), q0 - bk:kw - bk]
        parts.append(jnp.where(tri, last, NEG_BIG))
        softmax_pv(parts[0] if len(parts) == 1 else jnp.concatenate(parts, axis=1), r, start)

    acc = acc_ref[...]
    o = acc[:, :ATTN_DV] / acc[:, ATTN_DV:ATTN_DV + 1]
    a = o[:bq] - lam_ref[0:1, 0:1] * o[bq:]
    ms = jnp.mean(a * a, axis=-1, keepdims=True)
    o_ref[...] = (a * lax.rsqrt(ms + EPS) * g_ref[...] * out_scale).astype(BF16)


def _attention(q, k, v, lam, subln_g, B, S, bq, bk, out_scale):
    T, qkw = q.shape
    H = qkw // ATTN_DV
    nq = S // bq
    return pl.pallas_call(
        functools.partial(_attn_kernel, bq, bk, out_scale),
        grid=(B, H, nq),
        in_specs=[
            pl.BlockSpec((bq, ATTN_DV), lambda b, h, i: (b * nq + i, h)),
            pl.BlockSpec((S, ATTN_DV), lambda b, h, i: (b, h)),
            pl.BlockSpec((S, ATTN_DV), lambda b, h, i: (b, h)),
            pl.BlockSpec((1, LANES), lambda b, h, i: (0, 0)),
            pl.BlockSpec((1, ATTN_DV), lambda b, h, i: (0, 0)),
        ],
        out_specs=pl.BlockSpec((bq, ATTN_DV), lambda b, h, i: (b * nq + i, h)),
        out_shape=jax.ShapeDtypeStruct((T, H * ATTN_DV), BF16),
        scratch_shapes=[
            pltpu.VMEM((2 * bq, ATTN_DV), BF16),
            pltpu.VMEM((S, 2 * ATTN_DV), BF16),
            pltpu.VMEM((2 * bq, bk), F32),
            pltpu.VMEM((2 * bq, max(bq - bk, LANES)), F32),
            pltpu.VMEM((2 * bq, LANES), F32),
            pltpu.VMEM((2 * bq, 2 * ATTN_DV), F32),
        ],
        compiler_params=_cparams(("arbitrary", "arbitrary", "arbitrary")),
        name="attention",
    )(q, k, v, lam, subln_g)


def _lru_kernel(tc, cw, x_ref, gate_ref, cw_ref, cb_ref, wg_ref, bg_ref, lam_ref, og_ref,
                o_ref, ext_ref, a_ref, b_ref, hc_ref):
    c = pl.program_id(1)

    @pl.when(c == 0)
    def _():
        ext_ref[0:SUBLANES, :] = jnp.zeros((SUBLANES, cw), F32)
        hc_ref[...] = jnp.zeros_like(hc_ref)

    ext_ref[SUBLANES:SUBLANES + tc, :] = x_ref[...]
    xc = cb_ref[...] + cw_ref[CONV_W - 1:CONV_W, :] * x_ref[...]
    for w in range(CONV_W - 1):
        sh = CONV_W - 1 - w
        xc = xc + cw_ref[w:w + 1, :] * ext_ref[SUBLANES - sh:SUBLANES - sh + tc, :]
    tail = ext_ref[tc:tc + SUBLANES, :]
    ext_ref[0:SUBLANES, :] = tail

    g = _dot(xc.astype(BF16), wg_ref[...]) + bg_ref[...]
    t_r = jnp.tanh(g[:, :cw])
    t_i = jnp.tanh(g[:, cw:])
    nl = -lam_ref[...]
    softplus = jnp.maximum(nl, 0.0) + jnp.log1p(jnp.exp(-jnp.abs(nl)))
    c1 = (-0.5 * LRU_C) * softplus
    log_a = c1 * t_r + c1
    th = jnp.tanh(log_a)
    p = -2.0 * th
    root = jnp.where(p > 0.0, p * lax.rsqrt(p * (1.0 - th)), 0.0)
    a_ref[...] = jnp.exp(log_a)
    b_ref[...] = root * ((t_i + 1.0) * (0.5 * xc))

    row = lax.broadcasted_iota(I32, (SUBLANES, cw), 0)
    has_src = {d: row >= d for d in (1, 2, 4)}

    def tile_scan(n, hc):
        rows = pl.ds(pl.multiple_of(n * SUBLANES, SUBLANES), SUBLANES)
        a = a_ref[rows, :]
        b = b_ref[rows, :]
        for d in (1, 2, 4):
            b = jnp.where(has_src[d], a * pltpu.roll(b, d, 0), 0.0) + b
            a = jnp.where(has_src[d], a * pltpu.roll(a, d, 0), a)
        h = b + a * hc
        b_ref[rows, :] = h
        return jnp.broadcast_to(h[SUBLANES - 1:SUBLANES, :], h.shape)

    hc_ref[...] = lax.fori_loop(0, tc // SUBLANES, tile_scan, hc_ref[...], unroll=4)

    gt = gate_ref[...]
    gelu = 0.5 * gt * (1.0 + jnp.tanh(math.sqrt(2.0 / math.pi) * (gt + 0.044715 * gt * gt * gt)))
    y = b_ref[...] * gelu
    ms = jnp.mean(y * y, axis=-1, keepdims=True)
    o_ref[...] = (y * lax.rsqrt(ms + EPS) * og_ref[...]).astype(BF16)


def _lru(xr, gr, conv_w, conv_b, w_gates, b_gates, lru_lambda, out_g, B, S, tc):
    T, cw = xr.shape
    nc = S // tc
    const = lambda shape: pl.BlockSpec(shape, lambda b, c: (0,) * len(shape))
    row = pl.BlockSpec((tc, cw), lambda b, c: (b * nc + c, 0))
    return pl.pallas_call(
        functools.partial(_lru_kernel, tc, cw),
        grid=(B, nc),
        in_specs=[row, row, const((CONV_W, cw)), const((1, cw)), const((cw, 2 * cw)),
                  const((1, 2 * cw)), const((1, cw)), const((1, cw))],
        out_specs=row,
        out_shape=jax.ShapeDtypeStruct((T, cw), BF16),
        scratch_shapes=[
            pltpu.VMEM((tc + SUBLANES, cw), F32),
            pltpu.VMEM((tc, cw), F32),
            pltpu.VMEM((tc, cw), F32),
            pltpu.VMEM((SUBLANES, cw), F32),
        ],
        compiler_params=_cparams(("arbitrary", "arbitrary")),
        name="lru",
    )(xr, gr, conv_w, conv_b, w_gates, b_gates, lru_lambda, out_g)


def _out_proj_kernel(tm, aw, attn_ref, lru_ref, x_ref, mod_ref, w_ref, g_ref, rw_ref, rb_ref, tri_ref,
                     x1_ref, h2_ref, ti_ref, gt_ref, rk_ref, cnt_ref):
    mix = _dot(attn_ref[...], w_ref[0:aw, :]) + _dot(lru_ref[...], w_ref[aw:, :])
    x1 = x_ref[...] + mod_ref[2:3, :] * mix
    x1_ref[...] = x1
    h2 = _rms_modulate(x1, g_ref[...], mod_ref[3:4, :], mod_ref[4:5, :])
    for j in range(h2.shape[1] // LANES):
        h2_ref[pl.ds(j, tm, stride=SUBLANES), :] = h2[:, j * LANES:(j + 1) * LANES]

    h_hi, h_lo = _split_hi_lo(h2)
    w_hi, w_lo = _split_hi_lo(rw_ref[...])
    by_hi = _dot_tb(jnp.concatenate([w_hi, w_lo], axis=0), h_hi)
    lg = by_hi[:N_EXPERTS] + by_hi[N_EXPERTS:] + _dot_tb(w_hi, h_lo) + rb_ref[...]

    eidx = lax.broadcasted_iota(I32, lg.shape, 0)
    picked = jnp.zeros(lg.shape, F32)
    vals, idxs = [], []
    for _ in range(TOP_K):
        m = jnp.max(lg, axis=0, keepdims=True)
        idx = jnp.min(jnp.where(lg == m, eidx, N_EXPERTS), axis=0, keepdims=True)
        sel = eidx == idx
        lg = jnp.where(sel, -jnp.inf, lg)
        picked = picked + sel.astype(F32)
        vals.append(m)
        idxs.append(idx)
    es = [jnp.exp(v - vals[0]) for v in vals]
    den = es[0] + es[1] + es[2] + es[3]
    gt_ref[...] = jnp.concatenate([e / den for e in es], axis=0)
    ti_ref[...] = jnp.concatenate(idxs, axis=0)

    before = _dot(picked.astype(BF16), tri_ref[...])
    ranks = [jnp.sum(jnp.where(eidx == idx, before, 0.0), axis=0, keepdims=True) for idx in idxs]
    rk_ref[...] = jnp.concatenate(ranks, axis=0).astype(I32)
    cnt = jnp.sum(picked, axis=1, keepdims=True)
    cnt_ref[...] = jnp.broadcast_to(cnt, cnt_ref.shape).astype(I32)


def _out_proj(attn, lru, x2, mod3, w_out_b, norm_g, router_wt, router_b, S, tm):
    T, D = x2.shape
    aw = attn.shape[1]
    lw = lru.shape[1]
    tps = S // tm
    tri = (jnp.arange(tm, dtype=I32)[:, None] < jnp.arange(tm, dtype=I32)[None, :]).astype(BF16)
    const = lambda shape: pl.BlockSpec(shape, lambda i: (0,) * len(shape))
    row = lambda w: pl.BlockSpec((tm, w), lambda i: (i, 0))
    col = pl.BlockSpec((TOP_K, tm), lambda i: (0, i))
    return pl.pallas_call(
        functools.partial(_out_proj_kernel, tm, aw),
        grid=(T // tm,),
        in_specs=[
            row(aw), row(lw), row(D),
            pl.BlockSpec((None, 6, D), lambda i: (i // tps, 0, 0)),
            const((D, D)), const((1, D)), const((N_EXPERTS, D)), const((N_EXPERTS, 1)),
            const((tm, tm)),
        ],
        out_specs=[
            row(D),
            pl.BlockSpec((tm * SUBLANES, LANES), lambda i: (i, 0)),
            col, col, col,
            pl.BlockSpec((None, N_EXPERTS, LANES), lambda i: (i, 0, 0)),
        ],
        out_shape=[
            jax.ShapeDtypeStruct((T, D), F32),
            jax.ShapeDtypeStruct((T * SUBLANES, LANES), F32),
            jax.ShapeDtypeStruct((TOP_K, T), I32),
            jax.ShapeDtypeStruct((TOP_K, T), F32),
            jax.ShapeDtypeStruct((TOP_K, T), I32),
            jax.ShapeDtypeStruct((T // tm, N_EXPERTS, LANES), I32),
        ],
        compiler_params=_cparams(("arbitrary",)),
        name="out_proj",
    )(attn, lru, x2, mod3, w_out_b, norm_g, router_wt, router_b, tri)


def _slots_kernel(off_ref, ti_ref, rk_ref, gt_ref, meta_ref):
    i = pl.program_id(0)
    ti = ti_ref[...]
    base = jnp.zeros(ti.shape, I32)
    for e in range(N_EXPERTS):
        base = jnp.where(ti == e, off_ref[i * N_EXPERTS + e], base)
    meta_ref[0:TOP_K, :] = base + rk_ref[...] * SUBLANES
    meta_ref[TOP_K:, :] = lax.bitcast_convert_type(gt_ref[...], I32)


def _slots(run_off, top_i, rank, gates, t):
    T = top_i.shape[1]
    col = pl.BlockSpec((TOP_K, t), lambda i, off: (0, i))
    grid_spec = pltpu.PrefetchScalarGridSpec(
        num_scalar_prefetch=1,
        grid=(T // t,),
        in_specs=[col, col, col],
        out_specs=pl.BlockSpec((None, 2 * TOP_K, t), lambda i, off: (i, 0, 0)),
    )
    return pl.pallas_call(
        _slots_kernel,
        grid_spec=grid_spec,
        out_shape=jax.ShapeDtypeStruct((T // t, 2 * TOP_K, t), I32),
        compiler_params=_cparams(("arbitrary",)),
        name="slots",
    )(run_off, top_i, rank, gates)


def _rows(ref, first_row, n_rows):
    return ref.at[pl.ds(pl.multiple_of(first_row * SUBLANES, SUBLANES), n_rows * SUBLANES), :]


def _row_at(ref, sublane_offset):
    return ref.at[pl.ds(pl.multiple_of(sublane_offset, SUBLANES), SUBLANES), :]


def _fetch_tile(meta_hbm, smem, sems, t, n_words):
    i = pl.program_id(0)
    n = pl.num_programs(0)
    slot = i % 2
    per_tile = 2 * TOP_K * t

    def copy(tile, s):
        src = meta_hbm.at[pl.ds(pl.multiple_of(tile * per_tile, per_tile), n_words)]
        return pltpu.make_async_copy(src, smem.at[pl.ds(pl.multiple_of(s * n_words, n_words), n_words)], sems.at[s])

    @pl.when(i == 0)
    def _():
        copy(0, 0).start()

    copy(i, slot).wait()

    @pl.when(i + 1 < n)
    def _():
        copy(i + 1, 1 - slot).start()

    return slot * n_words


def _run_copies(t, tile, cnt_ref, off_ref, dst_ref, make_copy):
    def expert(e, carry):
        n = cnt_ref[tile * N_EXPERTS + e]
        off = off_ref[tile * N_EXPERTS + e]
        dst = dst_ref[tile * N_EXPERTS + e]
        for b in reversed(range(t.bit_length())):
            size = SUBLANES << b
            done = n & ~(2 * size - 1)

            @pl.when((n & size) != 0)
            def _():
                make_copy(off + done, dst + done, size).start()
        return carry

    lax.fori_loop(0, N_EXPERTS, expert, 0)


def _span(ref, first_sublane, n_sublanes):
    return ref.at[pl.ds(pl.multiple_of(first_sublane, SUBLANES), n_sublanes), :]


def _dispatch_kernel(t, cnt_ref, off_ref, dst_ref, meta_hbm, h_ref, xs_hbm, meta_smem, stage_ref, psem, sem):
    i = pl.program_id(0)
    n = pl.num_programs(0)
    slot = i % 2
    base = _fetch_tile(meta_hbm, meta_smem, psem, t, TOP_K * t)
    stage = stage_ref.at[slot]

    def place(r, carry):
        row = _rows(h_ref, r, 1)[...]
        for k in range(TOP_K):
            _row_at(stage, meta_smem[base + r + k * t])[...] = row
        return carry

    lax.fori_loop(0, t, place, 0, unroll=8)

    def all_runs(s):
        return pltpu.make_async_copy(stage_ref.at[s], _rows(xs_hbm, 0, TOP_K * t), sem.at[s])

    _run_copies(t, i, cnt_ref, off_ref, dst_ref,
                lambda a, b, m: pltpu.make_async_copy(_span(stage, a, m), _span(xs_hbm, b, m), sem.at[slot]))

    @pl.when(i > 0)
    def _():
        all_runs(1 - slot).wait()

    @pl.when(i == n - 1)
    def _():
        all_runs(slot).wait()


def _dispatch(tile_cnt, tile_off, tile_dst, meta, h2t, n_slots, t):
    n_tiles = meta.shape[0] // (2 * TOP_K * t)
    grid_spec = pltpu.PrefetchScalarGridSpec(
        num_scalar_prefetch=3,
        grid=(n_tiles,),
        in_specs=[pl.BlockSpec(memory_space=pl.ANY),
                  pl.BlockSpec((t * SUBLANES, LANES), lambda i, *_: (i, 0))],
        out_specs=pl.BlockSpec(memory_space=pl.ANY),
        scratch_shapes=[
            pltpu.SMEM((2 * TOP_K * t,), I32),
            pltpu.VMEM((2, TOP_K * t * SUBLANES, LANES), F32),
            pltpu.SemaphoreType.DMA((2,)),
            pltpu.SemaphoreType.DMA((2,)),
        ],
    )
    return pl.pallas_call(
        functools.partial(_dispatch_kernel, t),
        grid_spec=grid_spec,
        out_shape=jax.ShapeDtypeStruct((n_slots * SUBLANES, LANES), F32),
        compiler_params=_cparams(("arbitrary",)),
        name="dispatch",
    )(tile_cnt, tile_off, tile_dst, meta, h2t)


def _experts_kernel(ff, be_ref, nv_ref, xs_ref, wgu_ref, bgu_ref, wdn_ref, bdn_ref, ys_ref, wgu_b, wdn_b):
    i = pl.program_id(0)
    nv = nv_ref[i]
    n_slabs = xs_ref.shape[0] // MOE_BLOCK

    @pl.when((i == 0) | (be_ref[i] != be_ref[jnp.maximum(i - 1, 0)]))
    def _():
        wgu_b[...] = wgu_ref[...].astype(BF16)
        wdn_b[...] = wdn_ref[...].astype(BF16)

    @pl.when(nv > 0)
    def _():
        x = jnp.concatenate(
            [xs_ref[pl.ds(j, MOE_BLOCK, stride=SUBLANES), :] for j in range(n_slabs)], axis=1)
        x = jnp.where(lax.broadcasted_iota(I32, x.shape, 0) < nv, x, 0.0)
        gu = _dot(x.astype(BF16), wgu_b[...]) + bgu_ref[...]
        gate = jnp.minimum(gu[:, :ff], SWIGLU_LIMIT)
        up = jnp.clip(gu[:, ff:], -SWIGLU_LIMIT, SWIGLU_LIMIT)
        act = (up + 1.0) * (gate * _sigmoid(SWIGLU_ALPHA * gate))
        y = _dot(act.astype(BF16), wdn_b[...]) + bdn_ref[...]
        for j in range(n_slabs):
            ys_ref[pl.ds(j, MOE_BLOCK, stride=SUBLANES), :] = y[:, j * LANES:(j + 1) * LANES]

    @pl.when(nv == 0)
    def _():
        ys_ref[...] = jnp.zeros_like(ys_ref)


def _experts(block_expert, block_valid, xs, w_gu, b_gu, w_dn, b_dn):
    n_blocks = block_expert.shape[0]
    E, D, ff2 = w_gu.shape
    ff = ff2 // 2
    rows = pl.BlockSpec((MOE_BLOCK * SUBLANES, LANES), lambda i, be, nv: (i, 0))
    grid_spec = pltpu.PrefetchScalarGridSpec(
        num_scalar_prefetch=2,
        grid=(n_blocks,),
        in_specs=[
            rows,
            pl.BlockSpec((None, D, ff2), lambda i, be, nv: (be[i], 0, 0)),
            pl.BlockSpec((None, 1, ff2), lambda i, be, nv: (be[i], 0, 0)),
            pl.BlockSpec((None, ff, D), lambda i, be, nv: (be[i], 0, 0)),
            pl.BlockSpec((None, 1, D), lambda i, be, nv: (be[i], 0, 0)),
        ],
        out_specs=rows,
        scratch_shapes=[pltpu.VMEM((D, ff2), BF16), pltpu.VMEM((ff, D), BF16)],
    )
    return pl.pallas_call(
        functools.partial(_experts_kernel, ff),
        grid_spec=grid_spec,
        out_shape=jax.ShapeDtypeStruct(xs.shape, F32),
        compiler_params=_cparams(("arbitrary",)),
        name="experts",
    )(block_expert, block_valid, xs, w_gu, b_gu.reshape(E, 1, ff2), w_dn, b_dn.reshape(E, 1, D))


def _combine_kernel(t, cnt_ref, off_ref, dst_ref, meta_hbm, ys_hbm, x1_ref, mod_ref, o_ref,
                    meta_smem, stage_ref, y_ref, psem, sem):
    i = pl.program_id(0)
    n = pl.num_programs(0)
    slot = i % 2
    base = _fetch_tile(meta_hbm, meta_smem, psem, t, 2 * TOP_K * t)

    def fetch_runs(tile, s):
        stage = stage_ref.at[s]
        _run_copies(t, tile, cnt_ref, off_ref, dst_ref,
                    lambda a, b, m: pltpu.make_async_copy(_span(ys_hbm, b, m), _span(stage, a, m), sem.at[s]))

    @pl.when(i == 0)
    def _():
        fetch_runs(0, 0)

    @pl.when(i + 1 < n)
    def _():
        fetch_runs(i + 1, 1 - slot)

    pltpu.make_async_copy(_rows(ys_hbm, 0, TOP_K * t), stage_ref.at[slot], sem.at[slot]).wait()
    stage = stage_ref.at[slot]

    def gather(r, carry):
        acc = None
        for k in range(TOP_K):
            gate = lax.bitcast_convert_type(meta_smem[base + r + (TOP_K + k) * t], F32)
            term = gate * _row_at(stage, meta_smem[base + r + k * t])[...]
            acc = term if acc is None else acc + term
        _rows(y_ref, r, 1)[...] = acc
        return carry

    lax.fori_loop(0, t, gather, 0, unroll=8)

    n_slabs = o_ref.shape[1] // LANES
    y = jnp.concatenate([y_ref[pl.ds(j, t, stride=SUBLANES), :] for j in range(n_slabs)], axis=1)
    o_ref[...] = x1_ref[...] + mod_ref[5:6, :] * y


def _combine(tile_cnt, tile_off, tile_dst, meta, ys, x1, mod3, S, t):
    T, D = x1.shape
    n_tiles = T // t
    tps = S // t
    grid_spec = pltpu.PrefetchScalarGridSpec(
        num_scalar_prefetch=3,
        grid=(n_tiles,),
        in_specs=[
            pl.BlockSpec(memory_space=pl.ANY),
            pl.BlockSpec(memory_space=pl.ANY),
            pl.BlockSpec((t, D), lambda i, *_: (i, 0)),
            pl.BlockSpec((None, 6, D), lambda i, *_: (i // tps, 0, 0)),
        ],
        out_specs=pl.BlockSpec((t, D), lambda i, *_: (i, 0)),
        scratch_shapes=[
            pltpu.SMEM((2 * 2 * TOP_K * t,), I32),
            pltpu.VMEM((2, TOP_K * t * SUBLANES, LANES), F32),
            pltpu.VMEM((t * SUBLANES, LANES), F32),
            pltpu.SemaphoreType.DMA((2,)),
            pltpu.SemaphoreType.DMA((2,)),
        ],
    )
    return pl.pallas_call(
        functools.partial(_combine_kernel, t),
        grid_spec=grid_spec,
        out_shape=jax.ShapeDtypeStruct((T, D), F32),
        compiler_params=_cparams(("arbitrary",)),
        name="combine",
    )(tile_cnt, tile_off, tile_dst, meta, ys, x1, mod3)


def _tile(n, target):
    t = min(n, target)
    while n % t:
        t //= 2
    return t


def _block_diag(w):
    n, bw, _ = w.shape
    eye = jnp.eye(n, dtype=w.dtype)
    return (eye[:, None, :, None] * w[:, :, None, :]).reshape(n * bw, n * bw)


def _layer(l, x2, B, S, c, ada_w, ada_b, norm1_g, w_in, q_norm_g, k_norm_g, lambda_q1, lambda_k1, lambda_q2,
           lambda_k2, attn_subln_g, conv_w, conv_b, lru_wa, lru_ba, lru_wx, lru_bx, lru_lambda, lru_out_g,
           w_out, norm2_g, router_w, router_b, w_gate_up, b_gate_up, w_down, b_down):
    T, D = x2.shape
    lam_init = 0.8 - 0.6 * math.exp(-0.3 * l)
    aw = D // 2
    lw = D - aw
    heads = aw // ATTN_DV
    qkw = heads * 2 * ATTN_DK

    mod, lam = _adaln(c, ada_w, ada_b, lambda_q1, lambda_k1, lambda_q2, lambda_k2, lam_init)
    mod3 = mod.reshape(c.shape[0], 6, D)

    reps = qkw // ATTN_DK
    gq = (jnp.tile(q_norm_g, reps) * (ATTN_DK ** -0.5 * math.log2(math.e))).reshape(1, qkw)
    gk = jnp.tile(k_norm_g, reps).reshape(1, qkw)
    tm = _tile(S, 512)
    q, k, v, xr, gr = _in_proj(x2, mod3, norm1_g.reshape(1, D), w_in.astype(BF16), gq, gk, S, _tile(S, 1024),
                               qkw, aw, lw)

    attn = _attention(q, k, v, lam, attn_subln_g.reshape(1, ATTN_DV), B, S, _tile(S, 1024), _tile(S, 512),
                      1.0 - lam_init)

    w_gates = (0.5 * jnp.concatenate([_block_diag(lru_wa), _block_diag(lru_wx)], axis=1)).astype(BF16)
    b_gates = 0.5 * jnp.concatenate([lru_ba, lru_bx]).reshape(1, 2 * lw)
    lru = _lru(xr, gr, conv_w, conv_b.reshape(1, lw), w_gates, b_gates, lru_lambda.reshape(1, lw),
               lru_out_g.reshape(1, lw), B, S, _tile(S, 512))

    x1, h2t, top_i, gates, rank, tile_cnt = _out_proj(
        attn, lru, x2, mod3, w_out.astype(BF16), norm2_g.reshape(1, D), router_w.T,
        router_b.reshape(N_EXPERTS, 1), S, tm)

    tile_cnt = tile_cnt[:, :, 0]
    counts = jnp.sum(tile_cnt, axis=0)
    padded = ((counts + MOE_BLOCK - 1) // MOE_BLOCK) * MOE_BLOCK
    pad_ends = jnp.cumsum(padded)
    pad_starts = pad_ends - padded
    n_blocks = (T * TOP_K) // MOE_BLOCK + N_EXPERTS
    blk_start = jnp.arange(n_blocks, dtype=I32) * MOE_BLOCK
    owner = blk_start[:, None] >= pad_ends[None, :]
    block_expert = jnp.minimum(jnp.sum(owner, axis=1), N_EXPERTS - 1).astype(I32)
    onehot = block_expert[:, None] == jnp.arange(N_EXPERTS, dtype=I32)[None, :]
    row_end = jnp.sum(jnp.where(onehot, (pad_starts + counts)[None, :], 0), axis=1)
    block_valid = jnp.clip(row_end - blk_start, 0, MOE_BLOCK).astype(I32)
    tile_off = jnp.cumsum(tile_cnt, axis=1) - tile_cnt
    tile_dst = pad_starts[None, :] + jnp.cumsum(tile_cnt, axis=0) - tile_cnt
    flat = lambda a: (a * SUBLANES).reshape(-1).astype(I32)

    meta = _slots(flat(tile_off), top_i, rank, gates, tm).reshape(-1)
    xs = _dispatch(flat(tile_cnt), flat(tile_off), flat(tile_dst), meta, h2t, n_blocks * MOE_BLOCK, tm)
    ys = _experts(block_expert, block_valid, xs, w_gate_up, b_gate_up, w_down, b_down)
    return _combine(flat(tile_cnt), flat(tile_off), flat(tile_dst), meta, ys, x1, mod3, S, tm)


def kernel(x, c, ada_w, ada_b, norm1_g, w_in, q_norm_g, k_norm_g, lambda_q1, lambda_k1, lambda_q2, lambda_k2,
           attn_subln_g, conv_w, conv_b, lru_wa, lru_ba, lru_wx, lru_bx, lru_lambda, lru_out_g, w_out, norm2_g,
           router_w, router_b, w_gate_up, b_gate_up, w_down, b_down):
    B, S, D = x.shape
    params = (ada_w, ada_b, norm1_g, w_in, q_norm_g, k_norm_g, lambda_q1, lambda_k1, lambda_q2, lambda_k2,
              attn_subln_g, conv_w, conv_b, lru_wa, lru_ba, lru_wx, lru_bx, lru_lambda, lru_out_g, w_out,
              norm2_g, router_w, router_b, w_gate_up, b_gate_up, w_down, b_down)
    x2 = x.reshape(B * S, D)
    for l in range(ada_w.shape[0]):
        x2 = _layer(l, x2, B, S, c, *[p[l] for p in params])
    return x2.reshape(B, S, D)
```

```python
import functools
import math

import jax
import jax.numpy as jnp
from jax import lax
from jax.experimental import pallas as pl
from jax.experimental.pallas import tpu as pltpu

F32 = jnp.float32
BF16 = jnp.bfloat16
I32 = jnp.int32

ATTN_DK = 64
ATTN_DV = 2 * ATTN_DK
CONV_W = 4
LRU_C = 8.0
N_EXPERTS = 32
TOP_K = 4
SWIGLU_LIMIT = 7.0
SWIGLU_ALPHA = 1.702
MOE_BLOCK = 512
EPS = 1e-6
NEG_BIG = -1e30

LANES = 128
SUBLANES = 8
VMEM_LIMIT = 56 * 1024 * 1024


def _cparams(sem):
    return pltpu.CompilerParams(dimension_semantics=sem, vmem_limit_bytes=VMEM_LIMIT)


def _split_hi_lo(x):
    hi = x.astype(BF16)
    lo = (x - hi.astype(F32)).astype(BF16)
    return hi, lo


def _sigmoid(x):
    return 0.5 * jnp.tanh(0.5 * x) + 0.5


def _dot(a, b):
    return jnp.dot(a, b, preferred_element_type=F32)


def _dot_tb(a, b):
    return lax.dot_general(a, b, (((1,), (1,)), ((), ())), preferred_element_type=F32)


def _adaln_kernel(lam_init, c_ref, w_ref, b_ref, lq1, lk1, lq2, lk2, mod_ref, lam_ref):
    c = c_ref[...]
    s = c * jax.nn.sigmoid(c)
    s_hi, s_lo = _split_hi_lo(s)
    w = w_ref[...]
    w_hi, w_lo = _split_hi_lo(w)
    mod_ref[...] = _dot(s_hi, w_hi) + _dot(s_hi, w_lo) + _dot(s_lo, w_hi) + b_ref[...]
    d1 = jnp.sum(lq1[...] * lk1[...], axis=-1, keepdims=True)
    d2 = jnp.sum(lq2[...] * lk2[...], axis=-1, keepdims=True)
    lam = jnp.exp(d1) - jnp.exp(d2) + lam_init
    lam_ref[...] = jnp.broadcast_to(lam, lam_ref.shape)


def _adaln(c, ada_w, ada_b, lq1, lk1, lq2, lk2, lam_init):
    B, D = c.shape
    n = ada_w.shape[1] // D
    vec = lambda: pl.BlockSpec((1, ATTN_DK), lambda j: (0, 0))
    return pl.pallas_call(
        functools.partial(_adaln_kernel, lam_init),
        grid=(n,),
        in_specs=[
            pl.BlockSpec((B, D), lambda j: (0, 0)),
            pl.BlockSpec((D, D), lambda j: (0, j)),
            pl.BlockSpec((1, D), lambda j: (0, j)),
            vec(), vec(), vec(), vec(),
        ],
        out_specs=[
            pl.BlockSpec((B, D), lambda j: (0, j)),
            pl.BlockSpec((1, LANES), lambda j: (0, 0)),
        ],
        out_shape=[
            jax.ShapeDtypeStruct((B, n * D), F32),
            jax.ShapeDtypeStruct((1, LANES), F32),
        ],
        compiler_params=_cparams(("arbitrary",)),
        name="adaln",
    )(c, ada_w, ada_b.reshape(1, -1), lq1.reshape(1, -1), lk1.reshape(1, -1),
      lq2.reshape(1, -1), lk2.reshape(1, -1))


def _rms_modulate(x, g, shift, scale):
    ms = jnp.mean(x * x, axis=-1, keepdims=True)
    y = x * lax.rsqrt(ms + EPS) * g
    return y * (1.0 + scale) + shift


def _group_rms_scale(q, group_ones):
    ss = _dot((q * q).astype(BF16), group_ones)
    return lax.rsqrt(ss * (1.0 / ATTN_DK) + EPS)


def _in_proj_kernel(qkw, aw, lw, x_ref, mod_ref, g_ref, w_ref, gq_ref, gk_ref, ones_ref,
                    q_ref, k_ref, v_ref, xr_ref, gr_ref):
    h = _rms_modulate(x_ref[...], g_ref[...], mod_ref[0:1, :], mod_ref[1:2, :])
    hb = h.astype(BF16)
    group_ones = ones_ref[...]
    o = 0
    q = _dot(hb, w_ref[:, o:o + qkw]); o += qkw
    q_ref[...] = (q * _group_rms_scale(q, group_ones) * gq_ref[...]).astype(BF16)
    k = _dot(hb, w_ref[:, o:o + qkw]); o += qkw
    k_ref[...] = (k * _group_rms_scale(k, group_ones) * gk_ref[...]).astype(BF16)
    v_ref[...] = _dot(hb, w_ref[:, o:o + aw]).astype(BF16); o += aw
    xr_ref[...] = _dot(hb, w_ref[:, o:o + lw]); o += lw
    gr_ref[...] = _dot(hb, w_ref[:, o:o + lw])


def _in_proj(x2, mod3, norm_g, w_in_b, gq, gk, S, tm, qkw, aw, lw):
    T, D = x2.shape
    tps = S // tm
    grp = jnp.arange(qkw, dtype=I32) // ATTN_DK
    group_ones = (grp[:, None] == grp[None, :]).astype(BF16)
    const = lambda shape: pl.BlockSpec(shape, lambda i: (0,) * len(shape))
    row = lambda w: pl.BlockSpec((tm, w), lambda i: (i, 0))
    return pl.pallas_call(
        functools.partial(_in_proj_kernel, qkw, aw, lw),
        grid=(T // tm,),
        in_specs=[
            row(D),
            pl.BlockSpec((None, 6, D), lambda i: (i // tps, 0, 0)),
            const((1, D)),
            const(w_in_b.shape),
            const((1, qkw)), const((1, qkw)),
            const((qkw, qkw)),
        ],
        out_specs=[row(qkw), row(qkw), row(aw), row(lw), row(lw)],
        out_shape=[
            jax.ShapeDtypeStruct((T, qkw), BF16),
            jax.ShapeDtypeStruct((T, qkw), BF16),
            jax.ShapeDtypeStruct((T, aw), BF16),
            jax.ShapeDtypeStruct((T, lw), F32),
            jax.ShapeDtypeStruct((T, lw), F32),
        ],
        compiler_params=_cparams(("arbitrary",)),
        name="in_proj",
    )(x2, mod3, norm_g, w_in_b, gq, gk, group_ones)


ATTN_ROW_CHUNK = 128


def _attn_kernel(bq, bk, out_scale, q_ref, k_ref, v_ref, lam_ref, g_ref, o_ref,
                 qq_ref, vp_ref, s_ref, sd_ref, m_ref, acc_ref):
    i = pl.program_id(2)
    rc = ATTN_ROW_CHUNK
    n_chunks = 2 * bq // rc

    @pl.when(i == 0)
    def _():
        col = lax.broadcasted_iota(I32, (vp_ref.shape[0], ATTN_DV), 1)
        vp_ref[:, :ATTN_DV] = v_ref[...]
        vp_ref[:, ATTN_DV:] = jnp.where(col == 0, 1.0, 0.0).astype(BF16)

    q = q_ref[...]
    lane = lax.broadcasted_iota(I32, q.shape, 1)
    zero = jnp.zeros_like(q)
    qq_ref[0:bq, :] = jnp.where(lane < ATTN_DK, q, zero)
    qq_ref[bq:, :] = jnp.where(lane >= ATTN_DK, q, zero)
    m_ref[...] = jnp.full(m_ref.shape, NEG_BIG, F32)
    acc_ref[...] = jnp.zeros_like(acc_ref)

    def chunk(r):
        return pl.ds(r * rc, rc)

    def scores(r, start, kw):
        return _dot_tb(qq_ref[chunk(r), :], k_ref[pl.ds(start, kw), :])

    def softmax_pv(s, r, start):
        rows = chunk(r)
        kw = s.shape[1]
        m_old = m_ref[rows, :]
        m_new = jnp.maximum(m_old, jnp.max(s, axis=-1, keepdims=True))
        alpha = jnp.exp2(m_old - m_new)
        p = jnp.exp2(s - jnp.tile(m_new, (1, kw // LANES)))
        pv = _dot(p.astype(BF16), vp_ref[pl.ds(start, kw), :])
        acc_ref[rows, :] = acc_ref[rows, :] * jnp.tile(alpha, (1, 2)) + pv
        m_ref[rows, :] = m_new

    for r in range(n_chunks):
        s_ref[chunk(r), :] = scores(r, 0, bk)

    def body(j, carry):
        start = pl.multiple_of(j * bk, bk)
        for r in range(n_chunks):
            s = s_ref[chunk(r), :]
            s_ref[chunk(r), :] = scores(r, start + bk, bk)
            softmax_pv(s, r, start)
        return carry

    n_full = i * (bq // bk)
    lax.fori_loop(0, n_full, body, 0)

    start = pl.multiple_of(n_full * bk, bk)
    tri = (lax.broadcasted_iota(I32, (rc, rc), 1) <= lax.broadcasted_iota(I32, (rc, rc), 0))
    first_key = [(r * rc) % bq for r in range(n_chunks)]
    for r in range(n_chunks):
        if first_key[r] + rc > bk:
            sd_ref[chunk(r), 0:first_key[r] + rc - bk] = scores(r, start + bk, first_key[r] + rc - bk)
    for r in range(n_chunks):
        q0 = first_key[r]
        kw = q0 + rc
        parts = [s_ref[chunk(r), 0:min(q0, bk)]] if q0 else []
        if q0 > bk:
            parts.append(sd_ref[chunk(r), 0:q0 - bk])
        last = s_ref[chunk(r), q0:kw] if kw <= bk else sd_ref[chunk(r), q0 - bk:kw - bk]
        parts.append(jnp.where(tri, last, NEG_BIG))
        softmax_pv(parts[0] if len(parts) == 1 else jnp.concatenate(parts, axis=1), r, start)

    acc = acc_ref[...]
    o = acc[:, :ATTN_DV] / acc[:, ATTN_DV:ATTN_DV + 1]
    a = o[:bq] - lam_ref[0:1, 0:1] * o[bq:]
    ms = jnp.mean(a * a, axis=-1, keepdims=True)
    o_ref[...] = (a * lax.rsqrt(ms + EPS) * g_ref[...] * out_scale).astype(BF16)


def _attention(q, k, v, lam, subln_g, B, S, bq, bk, out_scale):
    T, qkw = q.shape
    H = qkw // ATTN_DV
    nq = S // bq
    return pl.pallas_call(
        functools.partial(_attn_kernel, bq, bk, out_scale),
        grid=(B, H, nq),
        in_specs=[
            pl.BlockSpec((bq, ATTN_DV), lambda b, h, i: (b * nq + i, h)),
            pl.BlockSpec((S, ATTN_DV), lambda b, h, i: (b, h)),
            pl.BlockSpec((S, ATTN_DV), lambda b, h, i: (b, h)),
            pl.BlockSpec((1, LANES), lambda b, h, i: (0, 0)),
            pl.BlockSpec((1, ATTN_DV), lambda b, h, i: (0, 0)),
        ],
        out_specs=pl.BlockSpec((bq, ATTN_DV), lambda b, h, i: (b * nq + i, h)),
        out_shape=jax.ShapeDtypeStruct((T, H * ATTN_DV), BF16),
        scratch_shapes=[
            pltpu.VMEM((2 * bq, ATTN_DV), BF16),
            pltpu.VMEM((S, 2 * ATTN_DV), BF16),
            pltpu.VMEM((2 * bq, bk), F32),
            pltpu.VMEM((2 * bq, max(bq - bk, LANES)), F32),
            pltpu.VMEM((2 * bq, LANES), F32),
            pltpu.VMEM((2 * bq, 2 * ATTN_DV), F32),
        ],
        compiler_params=_cparams(("arbitrary", "arbitrary", "arbitrary")),
        name="attention",
    )(q, k, v, lam, subln_g)


def _lru_kernel(tc, cw, x_ref, gate_ref, cw_ref, cb_ref, wg_ref, bg_ref, lam_ref, og_ref,
                o_ref, ext_ref, a_ref, b_ref, hc_ref):
    c = pl.program_id(1)

    @pl.when(c == 0)
    def _():
        ext_ref[0:SUBLANES, :] = jnp.zeros((SUBLANES, cw), F32)
        hc_ref[...] = jnp.zeros_like(hc_ref)

    ext_ref[SUBLANES:SUBLANES + tc, :] = x_ref[...]
    xc = cb_ref[...] + cw_ref[CONV_W - 1:CONV_W, :] * x_ref[...]
    for w in range(CONV_W - 1):
        sh = CONV_W - 1 - w
        xc = xc + cw_ref[w:w + 1, :] * ext_ref[SUBLANES - sh:SUBLANES - sh + tc, :]
    tail = ext_ref[tc:tc + SUBLANES, :]
    ext_ref[0:SUBLANES, :] = tail

    g = _dot(xc.astype(BF16), wg_ref[...]) + bg_ref[...]
    t_r = jnp.tanh(g[:, :cw])
    t_i = jnp.tanh(g[:, cw:])
    nl = -lam_ref[...]
    softplus = jnp.maximum(nl, 0.0) + jnp.log1p(jnp.exp(-jnp.abs(nl)))
    c1 = (-0.5 * LRU_C) * softplus
    log_a = c1 * t_r + c1
    th = jnp.tanh(log_a)
    p = -2.0 * th
    root = jnp.where(p > 0.0, p * lax.rsqrt(p * (1.0 - th)), 0.0)
    a_ref[...] = jnp.exp(log_a)
    b_ref[...] = root * ((t_i + 1.0) * (0.5 * xc))

    row = lax.broadcasted_iota(I32, (SUBLANES, cw), 0)
    has_src = {d: row >= d for d in (1, 2, 4)}

    def tile_scan(n, hc):
        rows = pl.ds(pl.multiple_of(n * SUBLANES, SUBLANES), SUBLANES)
        a = a_ref[rows, :]
        b = b_ref[rows, :]
        for d in (1, 2, 4):
            b = jnp.where(has_src[d], a * pltpu.roll(b, d, 0), 0.0) + b
            a = jnp.where(has_src[d], a * pltpu.roll(a, d, 0), a)
        h = b + a * hc
        b_ref[rows, :] = h
        return jnp.broadcast_to(h[SUBLANES - 1:SUBLANES, :], h.shape)

    hc_ref[...] = lax.fori_loop(0, tc // SUBLANES, tile_scan, hc_ref[...], unroll=4)

    gt = gate_ref[...]
    gelu = 0.5 * gt * (1.0 + jnp.tanh(math.sqrt(2.0 / math.pi) * (gt + 0.044715 * gt * gt * gt)))
    y = b_ref[...] * gelu
    ms = jnp.mean(y * y, axis=-1, keepdims=True)
    o_ref[...] = (y * lax.rsqrt(ms + EPS) * og_ref[...]).astype(BF16)


def _lru(xr, gr, conv_w, conv_b, w_gates, b_gates, lru_lambda, out_g, B, S, tc):
    T, cw = xr.shape
    nc = S // tc
    const = lambda shape: pl.BlockSpec(shape, lambda b, c: (0,) * len(shape))
    row = pl.BlockSpec((tc, cw), lambda b, c: (b * nc + c, 0))
    return pl.pallas_call(
        functools.partial(_lru_kernel, tc, cw),
        grid=(B, nc),
        in_specs=[row, row, const((CONV_W, cw)), const((1, cw)), const((cw, 2 * cw)),
                  const((1, 2 * cw)), const((1, cw)), const((1, cw))],
        out_specs=row,
        out_shape=jax.ShapeDtypeStruct((T, cw), BF16),
        scratch_shapes=[
            pltpu.VMEM((tc + SUBLANES, cw), F32),
            pltpu.VMEM((tc, cw), F32),
            pltpu.VMEM((tc, cw), F32),
            pltpu.VMEM((SUBLANES, cw), F32),
        ],
        compiler_params=_cparams(("arbitrary", "arbitrary")),
        name="lru",
    )(xr, gr, conv_w, conv_b, w_gates, b_gates, lru_lambda, out_g)


def _out_proj_kernel(tm, aw, attn_ref, lru_ref, x_ref, mod_ref, w_ref, g_ref, rw_ref, rb_ref, tri_ref,
                     x1_ref, h2_ref, ti_ref, gt_ref, rk_ref, cnt_ref):
    mix = _dot(attn_ref[...], w_ref[0:aw, :]) + _dot(lru_ref[...], w_ref[aw:, :])
    x1 = x_ref[...] + mod_ref[2:3, :] * mix
    x1_ref[...] = x1
    h2 = _rms_modulate(x1, g_ref[...], mod_ref[3:4, :], mod_ref[4:5, :])
    for j in range(h2.shape[1] // LANES):
        h2_ref[pl.ds(j, tm, stride=SUBLANES), :] = h2[:, j * LANES:(j + 1) * LANES]

    h_hi, h_lo = _split_hi_lo(h2)
    w_hi, w_lo = _split_hi_lo(rw_ref[...])
    by_hi = _dot_tb(jnp.concatenate([w_hi, w_lo], axis=0), h_hi)
    lg = by_hi[:N_EXPERTS] + by_hi[N_EXPERTS:] + _dot_tb(w_hi, h_lo) + rb_ref[...]

    eidx = lax.broadcasted_iota(I32, lg.shape, 0)
    picked = jnp.zeros(lg.shape, F32)
    vals, idxs = [], []
    for _ in range(TOP_K):
        m = jnp.max(lg, axis=0, keepdims=True)
        idx = jnp.min(jnp.where(lg == m, eidx, N_EXPERTS), axis=0, keepdims=True)
        sel = eidx == idx
        lg = jnp.where(sel, -jnp.inf, lg)
        picked = picked + sel.astype(F32)
        vals.append(m)
        idxs.append(idx)
    es = [jnp.exp(v - vals[0]) for v in vals]
    den = es[0] + es[1] + es[2] + es[3]
    gt_ref[...] = jnp.concatenate([e / den for e in es], axis=0)
    ti_ref[...] = jnp.concatenate(idxs, axis=0)

    before = _dot(picked.astype(BF16), tri_ref[...])
    ranks = [jnp.sum(jnp.where(eidx == idx, before, 0.0), axis=0, keepdims=True) for idx in idxs]
    rk_ref[...] = jnp.concatenate(ranks, axis=0).astype(I32)
    cnt = jnp.sum(picked, axis=1, keepdims=True)
    cnt_ref[...] = jnp.broadcast_to(cnt, cnt_ref.shape).astype(I32)


def _out_proj(attn, lru, x2, mod3, w_out_b, norm_g, router_wt, router_b, S, tm):
    T, D = x2.shape
    aw = attn.shape[1]
    lw = lru.shape[1]
    tps = S // tm
    tri = (jnp.arange(tm, dtype=I32)[:, None] < jnp.arange(tm, dtype=I32)[None, :]).astype(BF16)
    const = lambda shape: pl.BlockSpec(shape, lambda i: (0,) * len(shape))
    row = lambda w: pl.BlockSpec((tm, w), lambda i: (i, 0))
    col = pl.BlockSpec((TOP_K, tm), lambda i: (0, i))
    return pl.pallas_call(
        functools.partial(_out_proj_kernel, tm, aw),
        grid=(T // tm,),
        in_specs=[
            row(aw), row(lw), row(D),
            pl.BlockSpec((None, 6, D), lambda i: (i // tps, 0, 0)),
            const((D, D)), const((1, D)), const((N_EXPERTS, D)), const((N_EXPERTS, 1)),
            const((tm, tm)),
        ],
        out_specs=[
            row(D),
            pl.BlockSpec((tm * SUBLANES, LANES), lambda i: (i, 0)),
            col, col, col,
            pl.BlockSpec((None, N_EXPERTS, LANES), lambda i: (i, 0, 0)),
        ],
        out_shape=[
            jax.ShapeDtypeStruct((T, D), F32),
            jax.ShapeDtypeStruct((T * SUBLANES, LANES), F32),
            jax.ShapeDtypeStruct((TOP_K, T), I32),
            jax.ShapeDtypeStruct((TOP_K, T), F32),
            jax.ShapeDtypeStruct((TOP_K, T), I32),
            jax.ShapeDtypeStruct((T // tm, N_EXPERTS, LANES), I32),
        ],
        compiler_params=_cparams(("arbitrary",)),
        name="out_proj",
    )(attn, lru, x2, mod3, w_out_b, norm_g, router_wt, router_b, tri)


def _slots_kernel(off_ref, ti_ref, rk_ref, gt_ref, meta_ref):
    i = pl.program_id(0)
    ti = ti_ref[...]
    base = jnp.zeros(ti.shape, I32)
    for e in range(N_EXPERTS):
        base = jnp.where(ti == e, off_ref[i * N_EXPERTS + e], base)
    meta_ref[0:TOP_K, :] = base + rk_ref[...] * SUBLANES
    meta_ref[TOP_K:, :] = lax.bitcast_convert_type(gt_ref[...], I32)


def _slots(run_off, top_i, rank, gates, t):
    T = top_i.shape[1]
    col = pl.BlockSpec((TOP_K, t), lambda i, off: (0, i))
    grid_spec = pltpu.PrefetchScalarGridSpec(
        num_scalar_prefetch=1,
        grid=(T // t,),
        in_specs=[col, col, col],
        out_specs=pl.BlockSpec((None, 2 * TOP_K, t), lambda i, off: (i, 0, 0)),
    )
    return pl.pallas_call(
        _slots_kernel,
        grid_spec=grid_spec,
        out_shape=jax.ShapeDtypeStruct((T // t, 2 * TOP_K, t), I32),
        compiler_params=_cparams(("arbitrary",)),
        name="slots",
    )(run_off, top_i, rank, gates)


def _rows(ref, first_row, n_rows):
    return ref.at[pl.ds(pl.multiple_of(first_row * SUBLANES, SUBLANES), n_rows * SUBLANES), :]


def _row_at(ref, sublane_offset):
    return ref.at[pl.ds(pl.multiple_of(sublane_offset, SUBLANES), SUBLANES), :]


def _fetch_tile(meta_hbm, smem, sems, t, n_words):
    i = pl.program_id(0)
    n = pl.num_programs(0)
    slot = i % 2
    per_tile = 2 * TOP_K * t

    def copy(tile, s):
        src = meta_hbm.at[pl.ds(pl.multiple_of(tile * per_tile, per_tile), n_words)]
        return pltpu.make_async_copy(src, smem.at[pl.ds(pl.multiple_of(s * n_words, n_words), n_words)], sems.at[s])

    @pl.when(i == 0)
    def _():
        copy(0, 0).start()

    copy(i, slot).wait()

    @pl.when(i + 1 < n)
    def _():
        copy(i + 1, 1 - slot).start()

    return slot * n_words


def _run_copies(t, tile, cnt_ref, off_ref, dst_ref, make_copy):
    def expert(e, carry):
        n = cnt_ref[tile * N_EXPERTS + e]
        off = off_ref[tile * N_EXPERTS + e]
        dst = dst_ref[tile * N_EXPERTS + e]
        for b in reversed(range(t.bit_length())):
            size = SUBLANES << b
            done = n & ~(2 * size - 1)

            @pl.when((n & size) != 0)
            def _():
                make_copy(off + done, dst + done, size).start()
        return carry

    lax.fori_loop(0, N_EXPERTS, expert, 0)


def _span(ref, first_sublane, n_sublanes):
    return ref.at[pl.ds(pl.multiple_of(first_sublane, SUBLANES), n_sublanes), :]


def _dispatch_kernel(t, cnt_ref, off_ref, dst_ref, meta_hbm, h_ref, xs_hbm, meta_smem, stage_ref, psem, sem):
    i = pl.program_id(0)
    n = pl.num_programs(0)
    slot = i % 2
    base = _fetch_tile(meta_hbm, meta_smem, psem, t, TOP_K * t)
    stage = stage_ref.at[slot]

    def place(r, carry):
        row = _rows(h_ref, r, 1)[...]
        for k in range(TOP_K):
            _row_at(stage, meta_smem[base + r + k * t])[...] = row
        return carry

    lax.fori_loop(0, t, place, 0, unroll=8)

    def all_runs(s):
        return pltpu.make_async_copy(stage_ref.at[s], _rows(xs_hbm, 0, TOP_K * t), sem.at[s])

    _run_copies(t, i, cnt_ref, off_ref, dst_ref,
                lambda a, b, m: pltpu.make_async_copy(_span(stage, a, m), _span(xs_hbm, b, m), sem.at[slot]))

    @pl.when(i > 0)
    def _():
        all_runs(1 - slot).wait()

    @pl.when(i == n - 1)
    def _():
        all_runs(slot).wait()


def _dispatch(tile_cnt, tile_off, tile_dst, meta, h2t, n_slots, t):
    n_tiles = meta.shape[0] // (2 * TOP_K * t)
    grid_spec = pltpu.PrefetchScalarGridSpec(
        num_scalar_prefetch=3,
        grid=(n_tiles,),
        in_specs=[pl.BlockSpec(memory_space=pl.ANY),
                  pl.BlockSpec((t * SUBLANES, LANES), lambda i, *_: (i, 0))],
        out_specs=pl.BlockSpec(memory_space=pl.ANY),
        scratch_shapes=[
            pltpu.SMEM((2 * TOP_K * t,), I32),
            pltpu.VMEM((2, TOP_K * t * SUBLANES, LANES), F32),
            pltpu.SemaphoreType.DMA((2,)),
            pltpu.SemaphoreType.DMA((2,)),
        ],
    )
    return pl.pallas_call(
        functools.partial(_dispatch_kernel, t),
        grid_spec=grid_spec,
        out_shape=jax.ShapeDtypeStruct((n_slots * SUBLANES, LANES), F32),
        compiler_params=_cparams(("arbitrary",)),
        name="dispatch",
    )(tile_cnt, tile_off, tile_dst, meta, h2t)


def _experts_kernel(ff, be_ref, nv_ref, slot_ref, nxt_ref, xs_ref, wgu_hbm, bgu_ref, wdn_hbm, bdn_ref, ys_ref,
                    wgu_f, wdn_f, wgu_b, wdn_b, sem):
    i = pl.program_id(0)
    nv = nv_ref[i]
    n_slabs = xs_ref.shape[0] // MOE_BLOCK

    def fetch(e, s):
        return (pltpu.make_async_copy(wgu_hbm.at[e], wgu_f.at[s], sem.at[0, s]),
                pltpu.make_async_copy(wdn_hbm.at[e], wdn_f.at[s], sem.at[1, s]))

    @pl.when(i == 0)
    def _():
        for c in fetch(be_ref[0], 0):
            c.start()

    @pl.when((i == 0) | (be_ref[i] != be_ref[jnp.maximum(i - 1, 0)]))
    def _():
        s = slot_ref[i]
        for c in fetch(be_ref[i], s):
            c.wait()
        wgu_b[...] = wgu_f[s].astype(BF16)
        wdn_b[...] = wdn_f[s].astype(BF16)

        @pl.when(nxt_ref[i] >= 0)
        def _():
            for c in fetch(nxt_ref[i], 1 - s):
                c.start()

    @pl.when(nv > 0)
    def _():
        x = jnp.concatenate(
            [xs_ref[pl.ds(j, MOE_BLOCK, stride=SUBLANES), :] for j in range(n_slabs)], axis=1)
        x = jnp.where(lax.broadcasted_iota(I32, x.shape, 0) < nv, x, 0.0)
        gu = _dot(x.astype(BF16), wgu_b[...]) + bgu_ref[...]
        gate = jnp.minimum(gu[:, :ff], SWIGLU_LIMIT)
        up = jnp.clip(gu[:, ff:], -SWIGLU_LIMIT, SWIGLU_LIMIT)
        act = (up + 1.0) * (gate * _sigmoid(SWIGLU_ALPHA * gate))
        y = _dot(act.astype(BF16), wdn_b[...]) + bdn_ref[...]
        for j in range(n_slabs):
            ys_ref[pl.ds(j, MOE_BLOCK, stride=SUBLANES), :] = y[:, j * LANES:(j + 1) * LANES]

    @pl.when(nv == 0)
    def _():
        ys_ref[...] = jnp.zeros_like(ys_ref)


def _experts(block_expert, block_valid, xs, w_gu, b_gu, w_dn, b_dn):
    n_blocks = block_expert.shape[0]
    E, D, ff2 = w_gu.shape
    ff = ff2 // 2
    change = jnp.concatenate([jnp.zeros((1,), I32), (block_expert[1:] != block_expert[:-1]).astype(I32)])
    slot = jnp.cumsum(change) % 2
    later = block_expert[None, :] > block_expert[:, None]
    nxt = jnp.min(jnp.where(later, block_expert[None, :], E), axis=1)
    nxt = jnp.where(nxt < E, nxt, -1)
    rows = pl.BlockSpec((MOE_BLOCK * SUBLANES, LANES), lambda i, *_: (i, 0))
    grid_spec = pltpu.PrefetchScalarGridSpec(
        num_scalar_prefetch=4,
        grid=(n_blocks,),
        in_specs=[
            rows,
            pl.BlockSpec(memory_space=pl.ANY),
            pl.BlockSpec((None, 1, ff2), lambda i, be, *_: (be[i], 0, 0)),
            pl.BlockSpec(memory_space=pl.ANY),
            pl.BlockSpec((None, 1, D), lambda i, be, *_: (be[i], 0, 0)),
        ],
        out_specs=rows,
        scratch_shapes=[
            pltpu.VMEM((2, D, ff2), F32), pltpu.VMEM((2, ff, D), F32),
            pltpu.VMEM((D, ff2), BF16), pltpu.VMEM((ff, D), BF16),
            pltpu.SemaphoreType.DMA((2, 2)),
        ],
    )
    return pl.pallas_call(
        functools.partial(_experts_kernel, ff),
        grid_spec=grid_spec,
        out_shape=jax.ShapeDtypeStruct(xs.shape, F32),
        compiler_params=_cparams(("arbitrary",)),
        name="experts",
    )(block_expert, block_valid, slot.astype(I32), nxt.astype(I32), xs, w_gu, b_gu.reshape(E, 1, ff2), w_dn,
      b_dn.reshape(E, 1, D))


def _combine_kernel(t, cnt_ref, off_ref, dst_ref, meta_hbm, ys_hbm, x1_ref, mod_ref, o_ref,
                    meta_smem, stage_ref, y_ref, psem, sem):
    i = pl.program_id(0)
    n = pl.num_programs(0)
    slot = i % 2
    base = _fetch_tile(meta_hbm, meta_smem, psem, t, 2 * TOP_K * t)

    def fetch_runs(tile, s):
        stage = stage_ref.at[s]
        _run_copies(t, tile, cnt_ref, off_ref, dst_ref,
                    lambda a, b, m: pltpu.make_async_copy(_span(ys_hbm, b, m), _span(stage, a, m), sem.at[s]))

    @pl.when(i == 0)
    def _():
        fetch_runs(0, 0)

    @pl.when(i + 1 < n)
    def _():
        fetch_runs(i + 1, 1 - slot)

    pltpu.make_async_copy(_rows(ys_hbm, 0, TOP_K * t), stage_ref.at[slot], sem.at[slot]).wait()
    stage = stage_ref.at[slot]

    def gather(r, carry):
        acc = None
        for k in range(TOP_K):
            gate = lax.bitcast_convert_type(meta_smem[base + r + (TOP_K + k) * t], F32)
            term = gate * _row_at(stage, meta_smem[base + r + k * t])[...]
            acc = term if acc is None else acc + term
        _rows(y_ref, r, 1)[...] = acc
        return carry

    lax.fori_loop(0, t, gather, 0, unroll=8)

    n_slabs = o_ref.shape[1] // LANES
    y = jnp.concatenate([y_ref[pl.ds(j, t, stride=SUBLANES), :] for j in range(n_slabs)], axis=1)
    o_ref[...] = x1_ref[...] + mod_ref[5:6, :] * y


def _combine(tile_cnt, tile_off, tile_dst, meta, ys, x1, mod3, S, t):
    T, D = x1.shape
    n_tiles = T // t
    tps = S // t
    grid_spec = pltpu.PrefetchScalarGridSpec(
        num_scalar_prefetch=3,
        grid=(n_tiles,),
        in_specs=[
            pl.BlockSpec(memory_space=pl.ANY),
            pl.BlockSpec(memory_space=pl.ANY),
            pl.BlockSpec((t, D), lambda i, *_: (i, 0)),
            pl.BlockSpec((None, 6, D), lambda i, *_: (i // tps, 0, 0)),
        ],
        out_specs=pl.BlockSpec((t, D), lambda i, *_: (i, 0)),
        scratch_shapes=[
            pltpu.SMEM((2 * 2 * TOP_K * t,), I32),
            pltpu.VMEM((2, TOP_K * t * SUBLANES, LANES), F32),
            pltpu.VMEM((t * SUBLANES, LANES), F32),
            pltpu.SemaphoreType.DMA((2,)),
            pltpu.SemaphoreType.DMA((2,)),
        ],
    )
    return pl.pallas_call(
        functools.partial(_combine_kernel, t),
        grid_spec=grid_spec,
        out_shape=jax.ShapeDtypeStruct((T, D), F32),
        compiler_params=_cparams(("arbitrary",)),
        name="combine",
    )(tile_cnt, tile_off, tile_dst, meta, ys, x1, mod3)


def _tile(n, target):
    t = min(n, target)
    while n % t:
        t //= 2
    return t


def _block_diag(w):
    n, bw, _ = w.shape
    eye = jnp.eye(n, dtype=w.dtype)
    return (eye[:, None, :, None] * w[:, :, None, :]).reshape(n * bw, n * bw)


def _layer(l, x2, B, S, c, ada_w, ada_b, norm1_g, w_in, q_norm_g, k_norm_g, lambda_q1, lambda_k1, lambda_q2,
           lambda_k2, attn_subln_g, conv_w, conv_b, lru_wa, lru_ba, lru_wx, lru_bx, lru_lambda, lru_out_g,
           w_out, norm2_g, router_w, router_b, w_gate_up, b_gate_up, w_down, b_down):
    T, D = x2.shape
    lam_init = 0.8 - 0.6 * math.exp(-0.3 * l)
    aw = D // 2
    lw = D - aw
    heads = aw // ATTN_DV
    qkw = heads * 2 * ATTN_DK

    mod, lam = _adaln(c, ada_w, ada_b, lambda_q1, lambda_k1, lambda_q2, lambda_k2, lam_init)
    mod3 = mod.reshape(c.shape[0], 6, D)

    reps = qkw // ATTN_DK
    gq = (jnp.tile(q_norm_g, reps) * (ATTN_DK ** -0.5 * math.log2(math.e))).reshape(1, qkw)
    gk = jnp.tile(k_norm_g, reps).reshape(1, qkw)
    tm = _tile(S, 512)
    q, k, v, xr, gr = _in_proj(x2, mod3, norm1_g.reshape(1, D), w_in.astype(BF16), gq, gk, S, _tile(S, 1024),
                               qkw, aw, lw)

    attn = _attention(q, k, v, lam, attn_subln_g.reshape(1, ATTN_DV), B, S, _tile(S, 1024), _tile(S, 512),
                      1.0 - lam_init)

    w_gates = (0.5 * jnp.concatenate([_block_diag(lru_wa), _block_diag(lru_wx)], axis=1)).astype(BF16)
    b_gates = 0.5 * jnp.concatenate([lru_ba, lru_bx]).reshape(1, 2 * lw)
    lru = _lru(xr, gr, conv_w, conv_b.reshape(1, lw), w_gates, b_gates, lru_lambda.reshape(1, lw),
               lru_out_g.reshape(1, lw), B, S, _tile(S, 512))

    x1, h2t, top_i, gates, rank, tile_cnt = _out_proj(
        attn, lru, x2, mod3, w_out.astype(BF16), norm2_g.reshape(1, D), router_w.T,
        router_b.reshape(N_EXPERTS, 1), S, tm)

    tile_cnt = tile_cnt[:, :, 0]
    counts = jnp.sum(tile_cnt, axis=0)
    padded = ((counts + MOE_BLOCK - 1) // MOE_BLOCK) * MOE_BLOCK
    pad_ends = jnp.cumsum(padded)
    pad_starts = pad_ends - padded
    n_blocks = (T * TOP_K) // MOE_BLOCK + N_EXPERTS
    blk_start = jnp.arange(n_blocks, dtype=I32) * MOE_BLOCK
    owner = blk_start[:, None] >= pad_ends[None, :]
    block_expert = jnp.minimum(jnp.sum(owner, axis=1), N_EXPERTS - 1).astype(I32)
    onehot = block_expert[:, None] == jnp.arange(N_EXPERTS, dtype=I32)[None, :]
    row_end = jnp.sum(jnp.where(onehot, (pad_starts + counts)[None, :], 0), axis=1)
    block_valid = jnp.clip(row_end - blk_start, 0, MOE_BLOCK).astype(I32)
    tile_off = jnp.cumsum(tile_cnt, axis=1) - tile_cnt
    tile_dst = pad_starts[None, :] + jnp.cumsum(tile_cnt, axis=0) - tile_cnt
    flat = lambda a: (a * SUBLANES).reshape(-1).astype(I32)

    meta = _slots(flat(tile_off), top_i, rank, gates, tm).reshape(-1)
    xs = _dispatch(flat(tile_cnt), flat(tile_off), flat(tile_dst), meta, h2t, n_blocks * MOE_BLOCK, tm)
    ys = _experts(block_expert, block_valid, xs, w_gate_up, b_gate_up, w_down, b_down)
    return _combine(flat(tile_cnt), flat(tile_off), flat(tile_dst), meta, ys, x1, mod3, S, tm)


def kernel(x, c, ada_w, ada_b, norm1_g, w_in, q_norm_g, k_norm_g, lambda_q1, lambda_k1, lambda_q2, lambda_k2,
           attn_subln_g, conv_w, conv_b, lru_wa, lru_ba, lru_wx, lru_bx, lru_lambda, lru_out_g, w_out, norm2_g,
           router_w, router_b, w_gate_up, b_gate_up, w_down, b_down):
    B, S, D = x.shape
    params = (ada_w, ada_b, norm1_g, w_in, q_norm_g, k_norm_g, lambda_q1, lambda_k1, lambda_q2, lambda_k2,
              attn_subln_g, conv_w, conv_b, lru_wa, lru_ba, lru_wx, lru_bx, lru_lambda, lru_out_g, w_out,
              norm2_g, router_w, router_b, w_gate_up, b_gate_up, w_down, b_down)
    x2 = x.reshape(B * S, D)
    for l in range(ada_w.shape[0]):
        x2 = _layer(l, x2, B, S, c, *[p[l] for p in params])
    return x2.reshape(B, S, D)
```

```python
import functools
import math

import jax
import jax.numpy as jnp
from jax import lax
from jax.experimental import pallas as pl
from jax.experimental.pallas import tpu as pltpu

F32 = jnp.float32
BF16 = jnp.bfloat16
I32 = jnp.int32

ATTN_DK = 64
ATTN_DV = 2 * ATTN_DK
CONV_W = 4
LRU_C = 8.0
N_EXPERTS = 32
TOP_K = 4
SWIGLU_LIMIT = 7.0
SWIGLU_ALPHA = 1.702
MOE_BLOCK = 512
EPS = 1e-6
NEG_BIG = -1e30

LANES = 128
SUBLANES = 8
VMEM_LIMIT = 56 * 1024 * 1024


def _cparams(sem):
    return pltpu.CompilerParams(dimension_semantics=sem, vmem_limit_bytes=VMEM_LIMIT)


def _split_hi_lo(x):
    hi = x.astype(BF16)
    lo = (x - hi.astype(F32)).astype(BF16)
    return hi, lo


def _sigmoid(x):
    return 0.5 * jnp.tanh(0.5 * x) + 0.5


def _dot(a, b):
    return jnp.dot(a, b, preferred_element_type=F32)


def _dot_tb(a, b):
    return lax.dot_general(a, b, (((1,), (1,)), ((), ())), preferred_element_type=F32)


def _adaln_kernel(lam_init, c_ref, w_ref, b_ref, lq1, lk1, lq2, lk2, mod_ref, lam_ref):
    c = c_ref[...]
    s = c * jax.nn.sigmoid(c)
    s_hi, s_lo = _split_hi_lo(s)
    w = w_ref[...]
    w_hi, w_lo = _split_hi_lo(w)
    mod_ref[...] = _dot(s_hi, w_hi) + _dot(s_hi, w_lo) + _dot(s_lo, w_hi) + b_ref[...]
    d1 = jnp.sum(lq1[...] * lk1[...], axis=-1, keepdims=True)
    d2 = jnp.sum(lq2[...] * lk2[...], axis=-1, keepdims=True)
    lam = jnp.exp(d1) - jnp.exp(d2) + lam_init
    lam_ref[...] = jnp.broadcast_to(lam, lam_ref.shape)


def _adaln(c, ada_w, ada_b, lq1, lk1, lq2, lk2, lam_init):
    B, D = c.shape
    n = ada_w.shape[1] // D
    vec = lambda: pl.BlockSpec((1, ATTN_DK), lambda j: (0, 0))
    return pl.pallas_call(
        functools.partial(_adaln_kernel, lam_init),
        grid=(n,),
        in_specs=[
            pl.BlockSpec((B, D), lambda j: (0, 0)),
            pl.BlockSpec((D, D), lambda j: (0, j)),
            pl.BlockSpec((1, D), lambda j: (0, j)),
            vec(), vec(), vec(), vec(),
        ],
        out_specs=[
            pl.BlockSpec((B, D), lambda j: (0, j)),
            pl.BlockSpec((1, LANES), lambda j: (0, 0)),
        ],
        out_shape=[
            jax.ShapeDtypeStruct((B, n * D), F32),
            jax.ShapeDtypeStruct((1, LANES), F32),
        ],
        compiler_params=_cparams(("arbitrary",)),
        name="adaln",
    )(c, ada_w, ada_b.reshape(1, -1), lq1.reshape(1, -1), lk1.reshape(1, -1),
      lq2.reshape(1, -1), lk2.reshape(1, -1))


def _rms_modulate(x, g, shift, scale):
    ms = jnp.mean(x * x, axis=-1, keepdims=True)
    y = x * lax.rsqrt(ms + EPS) * g
    return y * (1.0 + scale) + shift


def _group_rms_scale(q, group_ones):
    ss = _dot((q * q).astype(BF16), group_ones)
    return lax.rsqrt(ss * (1.0 / ATTN_DK) + EPS)


def _in_proj_kernel(qkw, aw, lw, x_ref, mod_ref, g_ref, w_ref, gq_ref, gk_ref, ones_ref,
                    q_ref, k_ref, v_ref, xr_ref, gr_ref):
    h = _rms_modulate(x_ref[...], g_ref[...], mod_ref[0:1, :], mod_ref[1:2, :])
    hb = h.astype(BF16)
    group_ones = ones_ref[...]
    o = 0
    q = _dot(hb, w_ref[:, o:o + qkw]); o += qkw
    q_ref[...] = (q * _group_rms_scale(q, group_ones) * gq_ref[...]).astype(BF16)
    k = _dot(hb, w_ref[:, o:o + qkw]); o += qkw
    k_ref[...] = (k * _group_rms_scale(k, group_ones) * gk_ref[...]).astype(BF16)
    v_ref[...] = _dot(hb, w_ref[:, o:o + aw]).astype(BF16); o += aw
    xr_ref[...] = _dot(hb, w_ref[:, o:o + lw]); o += lw
    gr_ref[...] = _dot(hb, w_ref[:, o:o + lw])


def _in_proj(x2, mod3, norm_g, w_in_b, gq, gk, S, tm, qkw, aw, lw):
    T, D = x2.shape
    tps = S // tm
    grp = jnp.arange(qkw, dtype=I32) // ATTN_DK
    group_ones = (grp[:, None] == grp[None, :]).astype(BF16)
    const = lambda shape: pl.BlockSpec(shape, lambda i: (0,) * len(shape))
    row = lambda w: pl.BlockSpec((tm, w), lambda i: (i, 0))
    return pl.pallas_call(
        functools.partial(_in_proj_kernel, qkw, aw, lw),
        grid=(T // tm,),
        in_specs=[
            row(D),
            pl.BlockSpec((None, 6, D), lambda i: (i // tps, 0, 0)),
            const((1, D)),
            const(w_in_b.shape),
            const((1, qkw)), const((1, qkw)),
            const((qkw, qkw)),
        ],
        out_specs=[row(qkw), row(qkw), row(aw), row(lw), row(lw)],
        out_shape=[
            jax.ShapeDtypeStruct((T, qkw), BF16),
            jax.ShapeDtypeStruct((T, qkw), BF16),
            jax.ShapeDtypeStruct((T, aw), BF16),
            jax.ShapeDtypeStruct((T, lw), F32),
            jax.ShapeDtypeStruct((T, lw), F32),
        ],
        compiler_params=_cparams(("arbitrary",)),
        name="in_proj",
    )(x2, mod3, norm_g, w_in_b, gq, gk, group_ones)


ATTN_ROW_CHUNK = 128


def _attn_kernel(bq, bk, out_scale, q_ref, k_ref, v_ref, lam_ref, g_ref, o_ref,
                 qq_ref, vp_ref, s_ref, sd_ref, m_ref, acc_ref):
    i = pl.program_id(2)
    rc = ATTN_ROW_CHUNK
    n_chunks = 2 * bq // rc

    @pl.when(i == 0)
    def _():
        col = lax.broadcasted_iota(I32, (vp_ref.shape[0], ATTN_DV), 1)
        vp_ref[:, :ATTN_DV] = v_ref[...]
        vp_ref[:, ATTN_DV:] = jnp.where(col == 0, 1.0, 0.0).astype(BF16)

    q = q_ref[...]
    lane = lax.broadcasted_iota(I32, q.shape, 1)
    zero = jnp.zeros_like(q)
    qq_ref[0:bq, :] = jnp.where(lane < ATTN_DK, q, zero)
    qq_ref[bq:, :] = jnp.where(lane >= ATTN_DK, q, zero)
    m_ref[...] = jnp.full(m_ref.shape, NEG_BIG, F32)
    acc_ref[...] = jnp.zeros_like(acc_ref)

    def chunk(r):
        return pl.ds(r * rc, rc)

    def scores(r, start, kw):
        return _dot_tb(qq_ref[chunk(r), :], k_ref[pl.ds(start, kw), :])

    def softmax_pv(s, r, start):
        rows = chunk(r)
        kw = s.shape[1]
        m_old = m_ref[rows, :]
        m_new = jnp.maximum(m_old, jnp.max(s, axis=-1, keepdims=True))
        alpha = jnp.exp2(m_old - m_new)
        p = jnp.exp2(s - jnp.tile(m_new, (1, kw // LANES)))
        pv = _dot(p.astype(BF16), vp_ref[pl.ds(start, kw), :])
        acc_ref[rows, :] = acc_ref[rows, :] * jnp.tile(alpha, (1, 2)) + pv
        m_ref[rows, :] = m_new

    for r in range(n_chunks):
        s_ref[chunk(r), :] = scores(r, 0, bk)

    def body(j, carry):
        start = pl.multiple_of(j * bk, bk)
        for r in range(n_chunks):
            s = s_ref[chunk(r), :]
            s_ref[chunk(r), :] = scores(r, start + bk, bk)
            softmax_pv(s, r, start)
        return carry

    n_full = i * (bq // bk)
    lax.fori_loop(0, n_full, body, 0)

    start = pl.multiple_of(n_full * bk, bk)
    tri = (lax.broadcasted_iota(I32, (rc, rc), 1) <= lax.broadcasted_iota(I32, (rc, rc), 0))
    first_key = [(r * rc) % bq for r in range(n_chunks)]
    for r in range(n_chunks):
        if first_key[r] + rc > bk:
            sd_ref[chunk(r), 0:first_key[r] + rc - bk] = scores(r, start + bk, first_key[r] + rc - bk)
    for r in range(n_chunks):
        q0 = first_key[r]
        kw = q0 + rc
        parts = [s_ref[chunk(r), 0:min(q0, bk)]] if q0 else []
        if q0 > bk:
            parts.append(sd_ref[chunk(r), 0:q0 - bk])
        last = s_ref[chunk(r), q0:kw] if kw <= bk else sd_ref[chunk(r), q0 - bk:kw - bk]
        parts.append(jnp.where(tri, last, NEG_BIG))
        softmax_pv(parts[0] if len(parts) == 1 else jnp.concatenate(parts, axis=1), r, start)

    acc = acc_ref[...]
    o = acc[:, :ATTN_DV] / acc[:, ATTN_DV:ATTN_DV + 1]
    a = o[:bq] - lam_ref[0:1, 0:1] * o[bq:]
    ms = jnp.mean(a * a, axis=-1, keepdims=True)
    o_ref[...] = (a * lax.rsqrt(ms + EPS) * g_ref[...] * out_scale).astype(BF16)


def _attention(q, k, v, lam, subln_g, B, S, bq, bk, out_scale):
    T, qkw = q.shape
    H = qkw // ATTN_DV
    nq = S // bq
    return pl.pallas_call(
        functools.partial(_attn_kernel, bq, bk, out_scale),
        grid=(B, H, nq),
        in_specs=[
            pl.BlockSpec((bq, ATTN_DV), lambda b, h, i: (b * nq + i, h)),
            pl.BlockSpec((S, ATTN_DV), lambda b, h, i: (b, h)),
            pl.BlockSpec((S, ATTN_DV), lambda b, h, i: (b, h)),
            pl.BlockSpec((1, LANES), lambda b, h, i: (0, 0)),
            pl.BlockSpec((1, ATTN_DV), lambda b, h, i: (0, 0)),
        ],
        out_specs=pl.BlockSpec((bq, ATTN_DV), lambda b, h, i: (b * nq + i, h)),
        out_shape=jax.ShapeDtypeStruct((T, H * ATTN_DV), BF16),
        scratch_shapes=[
            pltpu.VMEM((2 * bq, ATTN_DV), BF16),
            pltpu.VMEM((S, 2 * ATTN_DV), BF16),
            pltpu.VMEM((2 * bq, bk), F32),
            pltpu.VMEM((2 * bq, max(bq - bk, LANES)), F32),
            pltpu.VMEM((2 * bq, LANES), F32),
            pltpu.VMEM((2 * bq, 2 * ATTN_DV), F32),
        ],
        compiler_params=_cparams(("arbitrary", "arbitrary", "arbitrary")),
        name="attention",
    )(q, k, v, lam, subln_g)


def _lru_kernel(tc, cw, x_ref, gate_ref, cw_ref, cb_ref, wg_ref, bg_ref, lam_ref, og_ref,
                o_ref, ext_ref, a_ref, b_ref, hc_ref):
    c = pl.program_id(1)

    @pl.when(c == 0)
    def _():
        ext_ref[0:SUBLANES, :] = jnp.zeros((SUBLANES, cw), F32)
        hc_ref[...] = jnp.zeros_like(hc_ref)

    ext_ref[SUBLANES:SUBLANES + tc, :] = x_ref[...]
    xc = cb_ref[...] + cw_ref[CONV_W - 1:CONV_W, :] * x_ref[...]
    for w in range(CONV_W - 1):
        sh = CONV_W - 1 - w
        xc = xc + cw_ref[w:w + 1, :] * ext_ref[SUBLANES - sh:SUBLANES - sh + tc, :]
    tail = ext_ref[tc:tc + SUBLANES, :]
    ext_ref[0:SUBLANES, :] = tail

    g = _dot(xc.astype(BF16), wg_ref[...]) + bg_ref[...]
    t_r = jnp.tanh(g[:, :cw])
    t_i = jnp.tanh(g[:, cw:])
    nl = -lam_ref[...]
    softplus = jnp.maximum(nl, 0.0) + jnp.log1p(jnp.exp(-jnp.abs(nl)))
    c1 = (-0.5 * LRU_C) * softplus
    log_a = c1 * t_r + c1
    th = jnp.tanh(log_a)
    p = -2.0 * th
    root = jnp.where(p > 0.0, p * lax.rsqrt(p * (1.0 - th)), 0.0)
    a_ref[...] = jnp.exp(log_a)
    b_ref[...] = root * ((t_i + 1.0) * (0.5 * xc))

    row = lax.broadcasted_iota(I32, (SUBLANES, cw), 0)
    has_src = {d: row >= d for d in (1, 2, 4)}

    def tile_scan(n, hc):
        rows = pl.ds(pl.multiple_of(n * SUBLANES, SUBLANES), SUBLANES)
        a = a_ref[rows, :]
        b = b_ref[rows, :]
        for d in (1, 2, 4):
            b = jnp.where(has_src[d], a * pltpu.roll(b, d, 0), 0.0) + b
            a = jnp.where(has_src[d], a * pltpu.roll(a, d, 0), a)
        h = b + a * hc
        b_ref[rows, :] = h
        return jnp.broadcast_to(h[SUBLANES - 1:SUBLANES, :], h.shape)

    hc_ref[...] = lax.fori_loop(0, tc // SUBLANES, tile_scan, hc_ref[...], unroll=4)

    gt = gate_ref[...]
    gelu = 0.5 * gt * (1.0 + jnp.tanh(math.sqrt(2.0 / math.pi) * (gt + 0.044715 * gt * gt * gt)))
    y = b_ref[...] * gelu
    ms = jnp.mean(y * y, axis=-1, keepdims=True)
    o_ref[...] = (y * lax.rsqrt(ms + EPS) * og_ref[...]).astype(BF16)


def _lru(xr, gr, conv_w, conv_b, w_gates, b_gates, lru_lambda, out_g, B, S, tc):
    T, cw = xr.shape
    nc = S // tc
    const = lambda shape: pl.BlockSpec(shape, lambda b, c: (0,) * len(shape))
    row = pl.BlockSpec((tc, cw), lambda b, c: (b * nc + c, 0))
    return pl.pallas_call(
        functools.partial(_lru_kernel, tc, cw),
        grid=(B, nc),
        in_specs=[row, row, const((CONV_W, cw)), const((1, cw)), const((cw, 2 * cw)),
                  const((1, 2 * cw)), const((1, cw)), const((1, cw))],
        out_specs=row,
        out_shape=jax.ShapeDtypeStruct((T, cw), BF16),
        scratch_shapes=[
            pltpu.VMEM((tc + SUBLANES, cw), F32),
            pltpu.VMEM((tc, cw), F32),
            pltpu.VMEM((tc, cw), F32),
            pltpu.VMEM((SUBLANES, cw), F32),
        ],
        compiler_params=_cparams(("arbitrary", "arbitrary")),
        name="lru",
    )(xr, gr, conv_w, conv_b, w_gates, b_gates, lru_lambda, out_g)


def _out_proj_kernel(tm, aw, attn_ref, lru_ref, x_ref, mod_ref, w_ref, g_ref, rw_ref, rb_ref, tri_ref, low_ref,
                     x1_ref, h2_ref, meta_ref, cnt_ref):
    mix = _dot(attn_ref[...], w_ref[0:aw, :]) + _dot(lru_ref[...], w_ref[aw:, :])
    x1 = x_ref[...] + mod_ref[2:3, :] * mix
    x1_ref[...] = x1
    h2 = _rms_modulate(x1, g_ref[...], mod_ref[3:4, :], mod_ref[4:5, :])
    for j in range(h2.shape[1] // LANES):
        h2_ref[pl.ds(j, tm, stride=SUBLANES), :] = h2[:, j * LANES:(j + 1) * LANES]

    h_hi, h_lo = _split_hi_lo(h2)
    w_hi, w_lo = _split_hi_lo(rw_ref[...])
    by_hi = _dot_tb(jnp.concatenate([w_hi, w_lo], axis=0), h_hi)
    lg = by_hi[:N_EXPERTS] + by_hi[N_EXPERTS:] + _dot_tb(w_hi, h_lo) + rb_ref[...]

    eidx = lax.broadcasted_iota(I32, lg.shape, 0)
    picked = jnp.zeros(lg.shape, F32)
    vals, idxs = [], []
    for _ in range(TOP_K):
        m = jnp.max(lg, axis=0, keepdims=True)
        idx = jnp.min(jnp.where(lg == m, eidx, N_EXPERTS), axis=0, keepdims=True)
        sel = eidx == idx
        lg = jnp.where(sel, -jnp.inf, lg)
        picked = picked + sel.astype(F32)
        vals.append(m)
        idxs.append(idx)
    es = [jnp.exp(v - vals[0]) for v in vals]
    den = es[0] + es[1] + es[2] + es[3]
    gates = jnp.concatenate([e / den for e in es], axis=0)

    cnt = jnp.sum(picked, axis=1, keepdims=True)
    cnt_b = jnp.broadcast_to(cnt, cnt_ref.shape)
    cnt_hi = jnp.floor(cnt_b * (1.0 / 256.0))
    cnt_lo = cnt_b - 256.0 * cnt_hi
    run_start = 256.0 * _dot(low_ref[...], cnt_hi.astype(BF16)) + _dot(low_ref[...], cnt_lo.astype(BF16))
    before = _dot(picked.astype(BF16), tri_ref[...]) + run_start[:, 0:1]
    pos = [jnp.sum(jnp.where(eidx == idx, before, 0.0), axis=0, keepdims=True) for idx in idxs]
    meta_ref[0:TOP_K, :] = jnp.concatenate(pos, axis=0).astype(I32) * SUBLANES
    meta_ref[TOP_K:, :] = lax.bitcast_convert_type(gates, I32)
    cnt_ref[...] = cnt_b.astype(I32)


def _out_proj(attn, lru, x2, mod3, w_out_b, norm_g, router_wt, router_b, S, tm):
    T, D = x2.shape
    aw = attn.shape[1]
    lw = lru.shape[1]
    tps = S // tm
    tri = (jnp.arange(tm, dtype=I32)[:, None] < jnp.arange(tm, dtype=I32)[None, :]).astype(BF16)
    experts = jnp.arange(N_EXPERTS, dtype=I32)
    low = (experts[None, :] < experts[:, None]).astype(BF16)
    const = lambda shape: pl.BlockSpec(shape, lambda i: (0,) * len(shape))
    row = lambda w: pl.BlockSpec((tm, w), lambda i: (i, 0))
    return pl.pallas_call(
        functools.partial(_out_proj_kernel, tm, aw),
        grid=(T // tm,),
        in_specs=[
            row(aw), row(lw), row(D),
            pl.BlockSpec((None, 6, D), lambda i: (i // tps, 0, 0)),
            const((D, D)), const((1, D)), const((N_EXPERTS, D)), const((N_EXPERTS, 1)),
            const((tm, tm)), const((N_EXPERTS, N_EXPERTS)),
        ],
        out_specs=[
            row(D),
            pl.BlockSpec((tm * SUBLANES, LANES), lambda i: (i, 0)),
            pl.BlockSpec((None, 2 * TOP_K, tm), lambda i: (i, 0, 0)),
            pl.BlockSpec((None, N_EXPERTS, LANES), lambda i: (i, 0, 0)),
        ],
        out_shape=[
            jax.ShapeDtypeStruct((T, D), F32),
            jax.ShapeDtypeStruct((T * SUBLANES, LANES), F32),
            jax.ShapeDtypeStruct((T // tm, 2 * TOP_K, tm), I32),
            jax.ShapeDtypeStruct((T // tm, N_EXPERTS, LANES), I32),
        ],
        compiler_params=_cparams(("arbitrary",)),
        name="out_proj",
    )(attn, lru, x2, mod3, w_out_b, norm_g, router_wt, router_b, tri, low)


def _rows(ref, first_row, n_rows):
    return ref.at[pl.ds(pl.multiple_of(first_row * SUBLANES, SUBLANES), n_rows * SUBLANES), :]


def _row_at(ref, sublane_offset):
    return ref.at[pl.ds(pl.multiple_of(sublane_offset, SUBLANES), SUBLANES), :]


def _fetch_tile(meta_hbm, smem, sems, t, n_words):
    i = pl.program_id(0)
    n = pl.num_programs(0)
    slot = i % 2
    per_tile = 2 * TOP_K * t

    def copy(tile, s):
        src = meta_hbm.at[pl.ds(pl.multiple_of(tile * per_tile, per_tile), n_words)]
        return pltpu.make_async_copy(src, smem.at[pl.ds(pl.multiple_of(s * n_words, n_words), n_words)], sems.at[s])

    @pl.when(i == 0)
    def _():
        copy(0, 0).start()

    copy(i, slot).wait()

    @pl.when(i + 1 < n)
    def _():
        copy(i + 1, 1 - slot).start()

    return slot * n_words


def _run_copies(t, tile, cnt_ref, off_ref, dst_ref, make_copy):
    def expert(e, carry):
        n = cnt_ref[tile * N_EXPERTS + e]
        off = off_ref[tile * N_EXPERTS + e]
        dst = dst_ref[tile * N_EXPERTS + e]
        for b in reversed(range(t.bit_length())):
            size = SUBLANES << b
            done = n & ~(2 * size - 1)

            @pl.when((n & size) != 0)
            def _():
                make_copy(off + done, dst + done, size).start()
        return carry

    lax.fori_loop(0, N_EXPERTS, expert, 0)


def _span(ref, first_sublane, n_sublanes):
    return ref.at[pl.ds(pl.multiple_of(first_sublane, SUBLANES), n_sublanes), :]


def _dispatch_kernel(t, cnt_ref, off_ref, dst_ref, meta_hbm, h_ref, xs_hbm, meta_smem, stage_ref, psem, sem):
    i = pl.program_id(0)
    n = pl.num_programs(0)
    slot = i % 2
    base = _fetch_tile(meta_hbm, meta_smem, psem, t, TOP_K * t)
    stage = stage_ref.at[slot]

    def place(r, carry):
        row = _rows(h_ref, r, 1)[...]
        for k in range(TOP_K):
            _row_at(stage, meta_smem[base + r + k * t])[...] = row
        return carry

    lax.fori_loop(0, t, place, 0, unroll=8)

    def all_runs(s):
        return pltpu.make_async_copy(stage_ref.at[s], _rows(xs_hbm, 0, TOP_K * t), sem.at[s])

    _run_copies(t, i, cnt_ref, off_ref, dst_ref,
                lambda a, b, m: pltpu.make_async_copy(_span(stage, a, m), _span(xs_hbm, b, m), sem.at[slot]))

    @pl.when(i > 0)
    def _():
        all_runs(1 - slot).wait()

    @pl.when(i == n - 1)
    def _():
        all_runs(slot).wait()


def _dispatch(tile_cnt, tile_off, tile_dst, meta, h2t, n_slots, t):
    n_tiles = meta.shape[0] // (2 * TOP_K * t)
    grid_spec = pltpu.PrefetchScalarGridSpec(
        num_scalar_prefetch=3,
        grid=(n_tiles,),
        in_specs=[pl.BlockSpec(memory_space=pl.ANY),
                  pl.BlockSpec((t * SUBLANES, LANES), lambda i, *_: (i, 0))],
        out_specs=pl.BlockSpec(memory_space=pl.ANY),
        scratch_shapes=[
            pltpu.SMEM((2 * TOP_K * t,), I32),
            pltpu.VMEM((2, TOP_K * t * SUBLANES, LANES), F32),
            pltpu.SemaphoreType.DMA((2,)),
            pltpu.SemaphoreType.DMA((2,)),
        ],
    )
    return pl.pallas_call(
        functools.partial(_dispatch_kernel, t),
        grid_spec=grid_spec,
        out_shape=jax.ShapeDtypeStruct((n_slots * SUBLANES, LANES), F32),
        compiler_params=_cparams(("arbitrary",)),
        name="dispatch",
    )(tile_cnt, tile_off, tile_dst, meta, h2t)


def _experts_kernel(ff, be_ref, nv_ref, slot_ref, nxt_ref, xs_ref, wgu_hbm, bgu_ref, wdn_hbm, bdn_ref, ys_ref,
                    wgu_f, wdn_f, wgu_b, wdn_b, sem):
    i = pl.program_id(0)
    nv = nv_ref[i]
    n_slabs = xs_ref.shape[0] // MOE_BLOCK

    def fetch(e, s):
        return (pltpu.make_async_copy(wgu_hbm.at[e], wgu_f.at[s], sem.at[0, s]),
                pltpu.make_async_copy(wdn_hbm.at[e], wdn_f.at[s], sem.at[1, s]))

    @pl.when(i == 0)
    def _():
        for c in fetch(be_ref[0], 0):
            c.start()

    @pl.when((i == 0) | (be_ref[i] != be_ref[jnp.maximum(i - 1, 0)]))
    def _():
        s = slot_ref[i]
        for c in fetch(be_ref[i], s):
            c.wait()
        wgu_b[...] = wgu_f[s].astype(BF16)
        wdn_b[...] = wdn_f[s].astype(BF16)

        @pl.when(nxt_ref[i] >= 0)
        def _():
            for c in fetch(nxt_ref[i], 1 - s):
                c.start()

    @pl.when(nv > 0)
    def _():
        x = jnp.concatenate(
            [xs_ref[pl.ds(j, MOE_BLOCK, stride=SUBLANES), :] for j in range(n_slabs)], axis=1)
        x = jnp.where(lax.broadcasted_iota(I32, x.shape, 0) < nv, x, 0.0)
        gu = _dot(x.astype(BF16), wgu_b[...]) + bgu_ref[...]
        gate = jnp.minimum(gu[:, :ff], SWIGLU_LIMIT)
        up = jnp.clip(gu[:, ff:], -SWIGLU_LIMIT, SWIGLU_LIMIT)
        act = (up + 1.0) * (gate * _sigmoid(SWIGLU_ALPHA * gate))
        y = _dot(act.astype(BF16), wdn_b[...]) + bdn_ref[...]
        for j in range(n_slabs):
            ys_ref[pl.ds(j, MOE_BLOCK, stride=SUBLANES), :] = y[:, j * LANES:(j + 1) * LANES]

    @pl.when(nv == 0)
    def _():
        ys_ref[...] = jnp.zeros_like(ys_ref)


def _experts(block_expert, block_valid, xs, w_gu, b_gu, w_dn, b_dn):
    n_blocks = block_expert.shape[0]
    E, D, ff2 = w_gu.shape
    ff = ff2 // 2
    change = jnp.concatenate([jnp.zeros((1,), I32), (block_expert[1:] != block_expert[:-1]).astype(I32)])
    slot = jnp.cumsum(change) % 2
    later = block_expert[None, :] > block_expert[:, None]
    nxt = jnp.min(jnp.where(later, block_expert[None, :], E), axis=1)
    nxt = jnp.where(nxt < E, nxt, -1)
    rows = pl.BlockSpec((MOE_BLOCK * SUBLANES, LANES), lambda i, *_: (i, 0))
    grid_spec = pltpu.PrefetchScalarGridSpec(
        num_scalar_prefetch=4,
        grid=(n_blocks,),
        in_specs=[
            rows,
            pl.BlockSpec(memory_space=pl.ANY),
            pl.BlockSpec((None, 1, ff2), lambda i, be, *_: (be[i], 0, 0)),
            pl.BlockSpec(memory_space=pl.ANY),
            pl.BlockSpec((None, 1, D), lambda i, be, *_: (be[i], 0, 0)),
        ],
        out_specs=rows,
        scratch_shapes=[
            pltpu.VMEM((2, D, ff2), F32), pltpu.VMEM((2, ff, D), F32),
            pltpu.VMEM((D, ff2), BF16), pltpu.VMEM((ff, D), BF16),
            pltpu.SemaphoreType.DMA((2, 2)),
        ],
    )
    return pl.pallas_call(
        functools.partial(_experts_kernel, ff),
        grid_spec=grid_spec,
        out_shape=jax.ShapeDtypeStruct(xs.shape, F32),
        compiler_params=_cparams(("arbitrary",)),
        name="experts",
    )(block_expert, block_valid, slot.astype(I32), nxt.astype(I32), xs, w_gu, b_gu.reshape(E, 1, ff2), w_dn,
      b_dn.reshape(E, 1, D))


def _combine_kernel(t, cnt_ref, off_ref, dst_ref, meta_hbm, ys_hbm, x1_ref, mod_ref, o_ref,
                    meta_smem, stage_ref, y_ref, psem, sem):
    i = pl.program_id(0)
    n = pl.num_programs(0)
    slot = i % 2
    base = _fetch_tile(meta_hbm, meta_smem, psem, t, 2 * TOP_K * t)

    def fetch_runs(tile, s):
        stage = stage_ref.at[s]
        _run_copies(t, tile, cnt_ref, off_ref, dst_ref,
                    lambda a, b, m: pltpu.make_async_copy(_span(ys_hbm, b, m), _span(stage, a, m), sem.at[s]))

    @pl.when(i == 0)
    def _():
        fetch_runs(0, 0)

    @pl.when(i + 1 < n)
    def _():
        fetch_runs(i + 1, 1 - slot)

    pltpu.make_async_copy(_rows(ys_hbm, 0, TOP_K * t), stage_ref.at[slot], sem.at[slot]).wait()
    stage = stage_ref.at[slot]

    def gather(r, carry):
        acc = None
        for k in range(TOP_K):
            gate = lax.bitcast_convert_type(meta_smem[base + r + (TOP_K + k) * t], F32)
            term = gate * _row_at(stage, meta_smem[base + r + k * t])[...]
            acc = term if acc is None else acc + term
        _rows(y_ref, r, 1)[...] = acc
        return carry

    lax.fori_loop(0, t, gather, 0, unroll=8)

    n_slabs = o_ref.shape[1] // LANES
    y = jnp.concatenate([y_ref[pl.ds(j, t, stride=SUBLANES), :] for j in range(n_slabs)], axis=1)
    o_ref[...] = x1_ref[...] + mod_ref[5:6, :] * y


def _combine(tile_cnt, tile_off, tile_dst, meta, ys, x1, mod3, S, t):
    T, D = x1.shape
    n_tiles = T // t
    tps = S // t
    grid_spec = pltpu.PrefetchScalarGridSpec(
        num_scalar_prefetch=3,
        grid=(n_tiles,),
        in_specs=[
            pl.BlockSpec(memory_space=pl.ANY),
            pl.BlockSpec(memory_space=pl.ANY),
            pl.BlockSpec((t, D), lambda i, *_: (i, 0)),
            pl.BlockSpec((None, 6, D), lambda i, *_: (i // tps, 0, 0)),
        ],
        out_specs=pl.BlockSpec((t, D), lambda i, *_: (i, 0)),
        scratch_shapes=[
            pltpu.SMEM((2 * 2 * TOP_K * t,), I32),
            pltpu.VMEM((2, TOP_K * t * SUBLANES, LANES), F32),
            pltpu.VMEM((t * SUBLANES, LANES), F32),
            pltpu.SemaphoreType.DMA((2,)),
            pltpu.SemaphoreType.DMA((2,)),
        ],
    )
    return pl.pallas_call(
        functools.partial(_combine_kernel, t),
        grid_spec=grid_spec,
        out_shape=jax.ShapeDtypeStruct((T, D), F32),
        compiler_params=_cparams(("arbitrary",)),
        name="combine",
    )(tile_cnt, tile_off, tile_dst, meta, ys, x1, mod3)


def _tile(n, target):
    t = min(n, target)
    while n % t:
        t //= 2
    return t


def _block_diag(w):
    n, bw, _ = w.shape
    eye = jnp.eye(n, dtype=w.dtype)
    return (eye[:, None, :, None] * w[:, :, None, :]).reshape(n * bw, n * bw)


def _layer(l, x2, B, S, c, ada_w, ada_b, norm1_g, w_in, q_norm_g, k_norm_g, lambda_q1, lambda_k1, lambda_q2,
           lambda_k2, attn_subln_g, conv_w, conv_b, lru_wa, lru_ba, lru_wx, lru_bx, lru_lambda, lru_out_g,
           w_out, norm2_g, router_w, router_b, w_gate_up, b_gate_up, w_down, b_down):
    T, D = x2.shape
    lam_init = 0.8 - 0.6 * math.exp(-0.3 * l)
    aw = D // 2
    lw = D - aw
    heads = aw // ATTN_DV
    qkw = heads * 2 * ATTN_DK

    mod, lam = _adaln(c, ada_w, ada_b, lambda_q1, lambda_k1, lambda_q2, lambda_k2, lam_init)
    mod3 = mod.reshape(c.shape[0], 6, D)

    reps = qkw // ATTN_DK
    gq = (jnp.tile(q_norm_g, reps) * (ATTN_DK ** -0.5 * math.log2(math.e))).reshape(1, qkw)
    gk = jnp.tile(k_norm_g, reps).reshape(1, qkw)
    tm = _tile(S, 512)
    q, k, v, xr, gr = _in_proj(x2, mod3, norm1_g.reshape(1, D), w_in.astype(BF16), gq, gk, S, _tile(S, 1024),
                               qkw, aw, lw)

    attn = _attention(q, k, v, lam, attn_subln_g.reshape(1, ATTN_DV), B, S, _tile(S, 1024), _tile(S, 512),
                      1.0 - lam_init)

    w_gates = (0.5 * jnp.concatenate([_block_diag(lru_wa), _block_diag(lru_wx)], axis=1)).astype(BF16)
    b_gates = 0.5 * jnp.concatenate([lru_ba, lru_bx]).reshape(1, 2 * lw)
    lru = _lru(xr, gr, conv_w, conv_b.reshape(1, lw), w_gates, b_gates, lru_lambda.reshape(1, lw),
               lru_out_g.reshape(1, lw), B, S, _tile(S, 1024))

    x1, h2t, meta, tile_cnt = _out_proj(
        attn, lru, x2, mod3, w_out.astype(BF16), norm2_g.reshape(1, D), router_w.T,
        router_b.reshape(N_EXPERTS, 1), S, tm)

    tile_cnt = tile_cnt[:, :, 0]
    counts = jnp.sum(tile_cnt, axis=0)
    padded = ((counts + MOE_BLOCK - 1) // MOE_BLOCK) * MOE_BLOCK
    pad_ends = jnp.cumsum(padded)
    pad_starts = pad_ends - padded
    n_blocks = (T * TOP_K) // MOE_BLOCK + N_EXPERTS
    blk_start = jnp.arange(n_blocks, dtype=I32) * MOE_BLOCK
    owner = blk_start[:, None] >= pad_ends[None, :]
    block_expert = jnp.minimum(jnp.sum(owner, axis=1), N_EXPERTS - 1).astype(I32)
    onehot = block_expert[:, None] == jnp.arange(N_EXPERTS, dtype=I32)[None, :]
    row_end = jnp.sum(jnp.where(onehot, (pad_starts + counts)[None, :], 0), axis=1)
    block_valid = jnp.clip(row_end - blk_start, 0, MOE_BLOCK).astype(I32)
    tile_off = jnp.cumsum(tile_cnt, axis=1) - tile_cnt
    tile_dst = pad_starts[None, :] + jnp.cumsum(tile_cnt, axis=0) - tile_cnt
    flat = lambda a: (a * SUBLANES).reshape(-1).astype(I32)

    meta = meta.reshape(-1)
    xs = _dispatch(flat(tile_cnt), flat(tile_off), flat(tile_dst), meta, h2t, n_blocks * MOE_BLOCK, tm)
    ys = _experts(block_expert, block_valid, xs, w_gate_up, b_gate_up, w_down, b_down)
    return _combine(flat(tile_cnt), flat(tile_off), flat(tile_dst), meta, ys, x1, mod3, S, tm)


def kernel(x, c, ada_w, ada_b, norm1_g, w_in, q_norm_g, k_norm_g, lambda_q1, lambda_k1, lambda_q2, lambda_k2,
           attn_subln_g, conv_w, conv_b, lru_wa, lru_ba, lru_wx, lru_bx, lru_lambda, lru_out_g, w_out, norm2_g,
           router_w, router_b, w_gate_up, b_gate_up, w_down, b_down):
    B, S, D = x.shape
    params = (ada_w, ada_b, norm1_g, w_in, q_norm_g, k_norm_g, lambda_q1, lambda_k1, lambda_q2, lambda_k2,
              attn_subln_g, conv_w, conv_b, lru_wa, lru_ba, lru_wx, lru_bx, lru_lambda, lru_out_g, w_out,
              norm2_g, router_w, router_b, w_gate_up, b_gate_up, w_down, b_down)
    x2 = x.reshape(B * S, D)
    for l in range(ada_w.shape[0]):
        x2 = _layer(l, x2, B, S, c, *[p[l] for p in params])
    return x2.reshape(B, S, D)
```

```python
import functools
import math

import jax
import jax.numpy as jnp
from jax import lax
from jax.experimental import pallas as pl
from jax.experimental.pallas import tpu as pltpu

F32 = jnp.float32
BF16 = jnp.bfloat16
I32 = jnp.int32

ATTN_DK = 64
ATTN_DV = 2 * ATTN_DK
CONV_W = 4
LRU_C = 8.0
N_EXPERTS = 32
TOP_K = 4
SWIGLU_LIMIT = 7.0
SWIGLU_ALPHA = 1.702
MOE_BLOCK = 512
EPS = 1e-6
NEG_BIG = -1e30

LANES = 128
SUBLANES = 8
VMEM_LIMIT = 56 * 1024 * 1024


def _cparams(sem):
    return pltpu.CompilerParams(dimension_semantics=sem, vmem_limit_bytes=VMEM_LIMIT)


def _split_hi_lo(x):
    hi = x.astype(BF16)
    lo = (x - hi.astype(F32)).astype(BF16)
    return hi, lo


def _sigmoid(x):
    return 0.5 * jnp.tanh(0.5 * x) + 0.5


def _dot(a, b):
    return jnp.dot(a, b, preferred_element_type=F32)


def _dot_tb(a, b):
    return lax.dot_general(a, b, (((1,), (1,)), ((), ())), preferred_element_type=F32)


def _adaln_kernel(lam_init, c_ref, w_ref, b_ref, lq1, lk1, lq2, lk2, mod_ref, lam_ref):
    c = c_ref[...]
    s = c * jax.nn.sigmoid(c)
    s_hi, s_lo = _split_hi_lo(s)
    w = w_ref[...]
    w_hi, w_lo = _split_hi_lo(w)
    mod_ref[...] = _dot(s_hi, w_hi) + _dot(s_hi, w_lo) + _dot(s_lo, w_hi) + b_ref[...]
    d1 = jnp.sum(lq1[...] * lk1[...], axis=-1, keepdims=True)
    d2 = jnp.sum(lq2[...] * lk2[...], axis=-1, keepdims=True)
    lam = jnp.exp(d1) - jnp.exp(d2) + lam_init
    lam_ref[...] = jnp.broadcast_to(lam, lam_ref.shape)


def _adaln(c, ada_w, ada_b, lq1, lk1, lq2, lk2, lam_init):
    B, D = c.shape
    n = ada_w.shape[1] // D
    vec = lambda: pl.BlockSpec((1, ATTN_DK), lambda j: (0, 0))
    return pl.pallas_call(
        functools.partial(_adaln_kernel, lam_init),
        grid=(n,),
        in_specs=[
            pl.BlockSpec((B, D), lambda j: (0, 0)),
            pl.BlockSpec((D, D), lambda j: (0, j)),
            pl.BlockSpec((1, D), lambda j: (0, j)),
            vec(), vec(), vec(), vec(),
        ],
        out_specs=[
            pl.BlockSpec((B, D), lambda j: (0, j)),
            pl.BlockSpec((1, LANES), lambda j: (0, 0)),
        ],
        out_shape=[
            jax.ShapeDtypeStruct((B, n * D), F32),
            jax.ShapeDtypeStruct((1, LANES), F32),
        ],
        compiler_params=_cparams(("arbitrary",)),
        name="adaln",
    )(c, ada_w, ada_b.reshape(1, -1), lq1.reshape(1, -1), lk1.reshape(1, -1),
      lq2.reshape(1, -1), lk2.reshape(1, -1))


def _rms_modulate(x, g, shift, scale):
    ms = jnp.mean(x * x, axis=-1, keepdims=True)
    y = x * lax.rsqrt(ms + EPS) * g
    return y * (1.0 + scale) + shift


def _group_rms_scale(q, group_ones):
    ss = _dot((q * q).astype(BF16), group_ones)
    return lax.rsqrt(ss * (1.0 / ATTN_DK) + EPS)


def _in_proj_kernel(qkw, aw, lw, x_ref, mod_ref, g_ref, w_ref, gq_ref, gk_ref, ones_ref,
                    q_ref, k_ref, v_ref, xr_ref, gr_ref):
    h = _rms_modulate(x_ref[...], g_ref[...], mod_ref[0:1, :], mod_ref[1:2, :])
    hb = h.astype(BF16)
    group_ones = ones_ref[...]
    o = 0
    q = _dot(hb, w_ref[:, o:o + qkw]); o += qkw
    q_ref[...] = (q * _group_rms_scale(q, group_ones) * gq_ref[...]).astype(BF16)
    k = _dot(hb, w_ref[:, o:o + qkw]); o += qkw
    k_ref[...] = (k * _group_rms_scale(k, group_ones) * gk_ref[...]).astype(BF16)
    v_ref[...] = _dot(hb, w_ref[:, o:o + aw]).astype(BF16); o += aw
    xr_ref[...] = _dot(hb, w_ref[:, o:o + lw]); o += lw
    gr_ref[...] = _dot(hb, w_ref[:, o:o + lw])


def _in_proj(x2, mod3, norm_g, w_in_b, gq, gk, S, tm, qkw, aw, lw):
    T, D = x2.shape
    tps = S // tm
    grp = jnp.arange(qkw, dtype=I32) // ATTN_DK
    group_ones = (grp[:, None] == grp[None, :]).astype(BF16)
    const = lambda shape: pl.BlockSpec(shape, lambda i: (0,) * len(shape))
    row = lambda w: pl.BlockSpec((tm, w), lambda i: (i, 0))
    return pl.pallas_call(
        functools.partial(_in_proj_kernel, qkw, aw, lw),
        grid=(T // tm,),
        in_specs=[
            row(D),
            pl.BlockSpec((None, 6, D), lambda i: (i // tps, 0, 0)),
            const((1, D)),
            const(w_in_b.shape),
            const((1, qkw)), const((1, qkw)),
            const((qkw, qkw)),
        ],
        out_specs=[row(qkw), row(qkw), row(aw), row(lw), row(lw)],
        out_shape=[
            jax.ShapeDtypeStruct((T, qkw), BF16),
            jax.ShapeDtypeStruct((T, qkw), BF16),
            jax.ShapeDtypeStruct((T, aw), BF16),
            jax.ShapeDtypeStruct((T, lw), F32),
            jax.ShapeDtypeStruct((T, lw), F32),
        ],
        compiler_params=_cparams(("arbitrary",)),
        name="in_proj",
    )(x2, mod3, norm_g, w_in_b, gq, gk, group_ones)


ATTN_ROW_CHUNK = 128


def _attn_kernel(bq, bk, out_scale, q_ref, k_ref, v_ref, lam_ref, g_ref, o_ref,
                 qq_ref, vp_ref, s_ref, sd_ref, m_ref, acc_ref):
    i = pl.program_id(2)
    rc = ATTN_ROW_CHUNK
    n_chunks = 2 * bq // rc

    @pl.when(i == 0)
    def _():
        col = lax.broadcasted_iota(I32, (vp_ref.shape[0], ATTN_DV), 1)
        vp_ref[:, :ATTN_DV] = v_ref[...]
        vp_ref[:, ATTN_DV:] = jnp.where(col == 0, 1.0, 0.0).astype(BF16)

    q = q_ref[...]
    lane = lax.broadcasted_iota(I32, q.shape, 1)
    zero = jnp.zeros_like(q)
    qq_ref[0:bq, :] = jnp.where(lane < ATTN_DK, q, zero)
    qq_ref[bq:, :] = jnp.where(lane >= ATTN_DK, q, zero)
    m_ref[...] = jnp.full(m_ref.shape, NEG_BIG, F32)
    acc_ref[...] = jnp.zeros_like(acc_ref)

    def chunk(r):
        return pl.ds(r * rc, rc)

    def scores(r, start, kw):
        return _dot_tb(qq_ref[chunk(r), :], k_ref[pl.ds(start, kw), :])

    def softmax_pv(s, r, start):
        rows = chunk(r)
        kw = s.shape[1]
        m_old = m_ref[rows, :]
        m_new = jnp.maximum(m_old, jnp.max(s, axis=-1, keepdims=True))
        alpha = jnp.exp2(m_old - m_new)
        p = jnp.exp2(s - jnp.tile(m_new, (1, kw // LANES)))
        pv = _dot(p.astype(BF16), vp_ref[pl.ds(start, kw), :])
        acc_ref[rows, :] = acc_ref[rows, :] * jnp.tile(alpha, (1, 2)) + pv
        m_ref[rows, :] = m_new

    for r in range(n_chunks):
        s_ref[chunk(r), :] = scores(r, 0, bk)

    def body(j, carry):
        start = pl.multiple_of(j * bk, bk)
        for r in range(n_chunks):
            s = s_ref[chunk(r), :]
            s_ref[chunk(r), :] = scores(r, start + bk, bk)
            softmax_pv(s, r, start)
        return carry

    n_full = i * (bq // bk)
    lax.fori_loop(0, n_full, body, 0)

    start = pl.multiple_of(n_full * bk, bk)
    tri = (lax.broadcasted_iota(I32, (rc, rc), 1) <= lax.broadcasted_iota(I32, (rc, rc), 0))
    first_key = [(r * rc) % bq for r in range(n_chunks)]
    for r in range(n_chunks):
        if first_key[r] + rc > bk:
            sd_ref[chunk(r), 0:first_key[r] + rc - bk] = scores(r, start + bk, first_key[r] + rc - bk)
    for r in range(n_chunks):
        q0 = first_key[r]
        kw = q0 + rc
        parts = [s_ref[chunk(r), 0:min(q0, bk)]] if q0 else []
        if q0 > bk:
            parts.append(sd_ref[chunk(r), 0:q0 - bk])
        last = s_ref[chunk(r), q0:kw] if kw <= bk else sd_ref[chunk(r), q0 - bk:kw - bk]
        parts.append(jnp.where(tri, last, NEG_BIG))
        softmax_pv(parts[0] if len(parts) == 1 else jnp.concatenate(parts, axis=1), r, start)

    acc = acc_ref[...]
    o = acc[:, :ATTN_DV] / acc[:, ATTN_DV:ATTN_DV + 1]
    a = o[:bq] - lam_ref[0:1, 0:1] * o[bq:]
    ms = jnp.mean(a * a, axis=-1, keepdims=True)
    o_ref[...] = (a * lax.rsqrt(ms + EPS) * g_ref[...] * out_scale).astype(BF16)


def _attention(q, k, v, lam, subln_g, B, S, bq, bk, out_scale):
    T, qkw = q.shape
    H = qkw // ATTN_DV
    nq = S // bq
    return pl.pallas_call(
        functools.partial(_attn_kernel, bq, bk, out_scale),
        grid=(B, H, nq),
        in_specs=[
            pl.BlockSpec((bq, ATTN_DV), lambda b, h, i: (b * nq + i, h)),
            pl.BlockSpec((S, ATTN_DV), lambda b, h, i: (b, h)),
            pl.BlockSpec((S, ATTN_DV), lambda b, h, i: (b, h)),
            pl.BlockSpec((1, LANES), lambda b, h, i: (0, 0)),
            pl.BlockSpec((1, ATTN_DV), lambda b, h, i: (0, 0)),
        ],
        out_specs=pl.BlockSpec((bq, ATTN_DV), lambda b, h, i: (b * nq + i, h)),
        out_shape=jax.ShapeDtypeStruct((T, H * ATTN_DV), BF16),
        scratch_shapes=[
            pltpu.VMEM((2 * bq, ATTN_DV), BF16),
            pltpu.VMEM((S, 2 * ATTN_DV), BF16),
            pltpu.VMEM((2 * bq, bk), F32),
            pltpu.VMEM((2 * bq, max(bq - bk, LANES)), F32),
            pltpu.VMEM((2 * bq, LANES), F32),
            pltpu.VMEM((2 * bq, 2 * ATTN_DV), F32),
        ],
        compiler_params=_cparams(("arbitrary", "arbitrary", "arbitrary")),
        name="attention",
    )(q, k, v, lam, subln_g)


def _lru_kernel(tc, cw, x_ref, gate_ref, cw_ref, cb_ref, wg_ref, bg_ref, lam_ref, og_ref,
                o_ref, ext_ref, a_ref, b_ref, hc_ref):
    c = pl.program_id(1)

    @pl.when(c == 0)
    def _():
        ext_ref[0:SUBLANES, :] = jnp.zeros((SUBLANES, cw), F32)
        hc_ref[...] = jnp.zeros_like(hc_ref)

    ext_ref[SUBLANES:SUBLANES + tc, :] = x_ref[...]
    xc = cb_ref[...] + cw_ref[CONV_W - 1:CONV_W, :] * x_ref[...]
    for w in range(CONV_W - 1):
        sh = CONV_W - 1 - w
        xc = xc + cw_ref[w:w + 1, :] * ext_ref[SUBLANES - sh:SUBLANES - sh + tc, :]
    tail = ext_ref[tc:tc + SUBLANES, :]
    ext_ref[0:SUBLANES, :] = tail

    g = _dot(xc.astype(BF16), wg_ref[...]) + bg_ref[...]
    t_r = jnp.tanh(g[:, :cw])
    t_i = jnp.tanh(g[:, cw:])
    nl = -lam_ref[...]
    softplus = jnp.maximum(nl, 0.0) + jnp.log1p(jnp.exp(-jnp.abs(nl)))
    c1 = (-0.5 * LRU_C) * softplus
    log_a = c1 * t_r + c1
    th = jnp.tanh(log_a)
    p = -2.0 * th
    root = jnp.where(p > 0.0, p * lax.rsqrt(p * (1.0 - th)), 0.0)
    a_ref[...] = jnp.exp(log_a)
    b_ref[...] = root * ((t_i + 1.0) * (0.5 * xc))

    row = lax.broadcasted_iota(I32, (SUBLANES, cw), 0)
    has_src = {d: row >= d for d in (1, 2, 4)}

    def tile_scan(n, hc):
        rows = pl.ds(pl.multiple_of(n * SUBLANES, SUBLANES), SUBLANES)
        a = a_ref[rows, :]
        b = b_ref[rows, :]
        for d in (1, 2, 4):
            b = jnp.where(has_src[d], a * pltpu.roll(b, d, 0), 0.0) + b
            a = jnp.where(has_src[d], a * pltpu.roll(a, d, 0), a)
        h = b + a * hc
        b_ref[rows, :] = h
        return jnp.broadcast_to(h[SUBLANES - 1:SUBLANES, :], h.shape)

    hc_ref[...] = lax.fori_loop(0, tc // SUBLANES, tile_scan, hc_ref[...], unroll=4)

    gt = gate_ref[...]
    gelu = 0.5 * gt * (1.0 + jnp.tanh(math.sqrt(2.0 / math.pi) * (gt + 0.044715 * gt * gt * gt)))
    y = b_ref[...] * gelu
    ms = jnp.mean(y * y, axis=-1, keepdims=True)
    o_ref[...] = (y * lax.rsqrt(ms + EPS) * og_ref[...]).astype(BF16)


def _lru(xr, gr, conv_w, conv_b, w_gates, b_gates, lru_lambda, out_g, B, S, tc):
    T, cw = xr.shape
    nc = S // tc
    const = lambda shape: pl.BlockSpec(shape, lambda b, c: (0,) * len(shape))
    row = pl.BlockSpec((tc, cw), lambda b, c: (b * nc + c, 0))
    return pl.pallas_call(
        functools.partial(_lru_kernel, tc, cw),
        grid=(B, nc),
        in_specs=[row, row, const((CONV_W, cw)), const((1, cw)), const((cw, 2 * cw)),
                  const((1, 2 * cw)), const((1, cw)), const((1, cw))],
        out_specs=row,
        out_shape=jax.ShapeDtypeStruct((T, cw), BF16),
        scratch_shapes=[
            pltpu.VMEM((tc + SUBLANES, cw), F32),
            pltpu.VMEM((tc, cw), F32),
            pltpu.VMEM((tc, cw), F32),
            pltpu.VMEM((SUBLANES, cw), F32),
        ],
        compiler_params=_cparams(("arbitrary", "arbitrary")),
        name="lru",
    )(xr, gr, conv_w, conv_b, w_gates, b_gates, lru_lambda, out_g)


def _out_proj_kernel(tm, aw, attn_ref, lru_ref, x_ref, mod_ref, w_ref, g_ref, rw_ref, rb_ref, tri_ref, low_ref,
                     x1_ref, h2_ref, meta_ref, cnt_ref):
    mix = _dot(attn_ref[...], w_ref[0:aw, :]) + _dot(lru_ref[...], w_ref[aw:, :])
    x1 = x_ref[...] + mod_ref[2:3, :] * mix
    x1_ref[...] = x1
    h2 = _rms_modulate(x1, g_ref[...], mod_ref[3:4, :], mod_ref[4:5, :])
    for j in range(h2.shape[1] // LANES):
        h2_ref[pl.ds(j, tm, stride=SUBLANES), :] = h2[:, j * LANES:(j + 1) * LANES]

    h_hi, h_lo = _split_hi_lo(h2)
    w_hi, w_lo = _split_hi_lo(rw_ref[...])
    by_hi = _dot_tb(jnp.concatenate([w_hi, w_lo], axis=0), h_hi)
    lg = by_hi[:N_EXPERTS] + by_hi[N_EXPERTS:] + _dot_tb(w_hi, h_lo) + rb_ref[...]

    eidx = lax.broadcasted_iota(I32, lg.shape, 0)
    picked = jnp.zeros(lg.shape, F32)
    vals, idxs = [], []
    for _ in range(TOP_K):
        m = jnp.max(lg, axis=0, keepdims=True)
        idx = jnp.min(jnp.where(lg == m, eidx, N_EXPERTS), axis=0, keepdims=True)
        sel = eidx == idx
        lg = jnp.where(sel, -jnp.inf, lg)
        picked = picked + sel.astype(F32)
        vals.append(m)
        idxs.append(idx)
    es = [jnp.exp(v - vals[0]) for v in vals]
    den = es[0] + es[1] + es[2] + es[3]
    gates = jnp.concatenate([e / den for e in es], axis=0)

    cnt = jnp.sum(picked, axis=1, keepdims=True)
    cnt_b = jnp.broadcast_to(cnt, cnt_ref.shape)
    cnt_hi = jnp.floor(cnt_b * (1.0 / 256.0))
    cnt_lo = cnt_b - 256.0 * cnt_hi
    run_start = 256.0 * _dot(low_ref[...], cnt_hi.astype(BF16)) + _dot(low_ref[...], cnt_lo.astype(BF16))
    before = _dot(picked.astype(BF16), tri_ref[...]) + run_start[:, 0:1]
    pos = [jnp.sum(jnp.where(eidx == idx, before, 0.0), axis=0, keepdims=True) for idx in idxs]
    meta_ref[0:TOP_K, :] = jnp.concatenate(pos, axis=0).astype(I32) * SUBLANES
    meta_ref[TOP_K:, :] = lax.bitcast_convert_type(gates, I32)
    cnt_ref[...] = cnt_b.astype(I32)


def _out_proj(attn, lru, x2, mod3, w_out_b, norm_g, router_wt, router_b, S, tm):
    T, D = x2.shape
    aw = attn.shape[1]
    lw = lru.shape[1]
    tps = S // tm
    tri = (jnp.arange(tm, dtype=I32)[:, None] < jnp.arange(tm, dtype=I32)[None, :]).astype(BF16)
    experts = jnp.arange(N_EXPERTS, dtype=I32)
    low = (experts[None, :] < experts[:, None]).astype(BF16)
    const = lambda shape: pl.BlockSpec(shape, lambda i: (0,) * len(shape))
    row = lambda w: pl.BlockSpec((tm, w), lambda i: (i, 0))
    return pl.pallas_call(
        functools.partial(_out_proj_kernel, tm, aw),
        grid=(T // tm,),
        in_specs=[
            row(aw), row(lw), row(D),
            pl.BlockSpec((None, 6, D), lambda i: (i // tps, 0, 0)),
            const((D, D)), const((1, D)), const((N_EXPERTS, D)), const((N_EXPERTS, 1)),
            const((tm, tm)), const((N_EXPERTS, N_EXPERTS)),
        ],
        out_specs=[
            row(D),
            pl.BlockSpec((tm * SUBLANES, LANES), lambda i: (i, 0)),
            pl.BlockSpec((None, 2 * TOP_K, tm), lambda i: (i, 0, 0)),
            pl.BlockSpec((None, N_EXPERTS, LANES), lambda i: (i, 0, 0)),
        ],
        out_shape=[
            jax.ShapeDtypeStruct((T, D), F32),
            jax.ShapeDtypeStruct((T * SUBLANES, LANES), F32),
            jax.ShapeDtypeStruct((T // tm, 2 * TOP_K, tm), I32),
            jax.ShapeDtypeStruct((T // tm, N_EXPERTS, LANES), I32),
        ],
        compiler_params=_cparams(("arbitrary",)),
        name="out_proj",
    )(attn, lru, x2, mod3, w_out_b, norm_g, router_wt, router_b, tri, low)


def _rows(ref, first_row, n_rows):
    return ref.at[pl.ds(pl.multiple_of(first_row * SUBLANES, SUBLANES), n_rows * SUBLANES), :]


def _row_at(ref, sublane_offset):
    return ref.at[pl.ds(pl.multiple_of(sublane_offset, SUBLANES), SUBLANES), :]


def _fetch_tile(meta_hbm, smem, sems, t, n_words):
    i = pl.program_id(0)
    n = pl.num_programs(0)
    slot = i % 2
    per_tile = 2 * TOP_K * t

    def copy(tile, s):
        src = meta_hbm.at[pl.ds(pl.multiple_of(tile * per_tile, per_tile), n_words)]
        return pltpu.make_async_copy(src, smem.at[pl.ds(pl.multiple_of(s * n_words, n_words), n_words)], sems.at[s])

    @pl.when(i == 0)
    def _():
        copy(0, 0).start()

    copy(i, slot).wait()

    @pl.when(i + 1 < n)
    def _():
        copy(i + 1, 1 - slot).start()

    return slot * n_words


def _run_copies(t, tile, cnt_ref, off_ref, dst_ref, make_copy):
    def expert(e, carry):
        n = cnt_ref[tile * N_EXPERTS + e]
        off = off_ref[tile * N_EXPERTS + e]
        dst = dst_ref[tile * N_EXPERTS + e]
        for b in reversed(range(t.bit_length())):
            size = SUBLANES << b
            done = n & ~(2 * size - 1)

            @pl.when((n & size) != 0)
            def _():
                make_copy(off + done, dst + done, size).start()
        return carry

    lax.fori_loop(0, N_EXPERTS, expert, 0)


def _span(ref, first_sublane, n_sublanes):
    return ref.at[pl.ds(pl.multiple_of(first_sublane, SUBLANES), n_sublanes), :]


def _dispatch_kernel(t, n_blocks, cnt_ref, off_ref, dst_ref, pad_ref, meta_hbm, h_ref, xs_hbm,
                     meta_smem, stage_ref, zero_ref, psem, sem, zsem):
    i = pl.program_id(0)
    n = pl.num_programs(0)
    slot = i % 2

    def zero_fill(go):
        def expert(e, carry):
            first = pad_ref[e]
            length = pad_ref[N_EXPERTS + e]
            for b in reversed(range((MOE_BLOCK - 1).bit_length())):
                size = SUBLANES << b
                done = length & ~(2 * size - 1)

                @pl.when((length & size) != 0)
                def _():
                    go(pltpu.make_async_copy(_span(zero_ref, 0, size), _span(xs_hbm, first + done, size), zsem))
            return carry

        lax.fori_loop(0, N_EXPERTS, expert, 0)

        def block(b, carry):
            go(pltpu.make_async_copy(zero_ref, _rows(xs_hbm, b * MOE_BLOCK, MOE_BLOCK), zsem))
            return carry

        lax.fori_loop(pad_ref[2 * N_EXPERTS], n_blocks, block, 0)

    @pl.when(i == 0)
    def _():
        zero_ref[...] = jnp.zeros_like(zero_ref)
        zero_fill(lambda c: c.start())

    base = _fetch_tile(meta_hbm, meta_smem, psem, t, TOP_K * t)
    stage = stage_ref.at[slot]

    def place(r, carry):
        row = _rows(h_ref, r, 1)[...]
        for k in range(TOP_K):
            _row_at(stage, meta_smem[base + r + k * t])[...] = row
        return carry

    lax.fori_loop(0, t, place, 0, unroll=8)

    def all_runs(s):
        return pltpu.make_async_copy(stage_ref.at[s], _rows(xs_hbm, 0, TOP_K * t), sem.at[s])

    _run_copies(t, i, cnt_ref, off_ref, dst_ref,
                lambda a, b, m: pltpu.make_async_copy(_span(stage, a, m), _span(xs_hbm, b, m), sem.at[slot]))

    @pl.when(i > 0)
    def _():
        all_runs(1 - slot).wait()

    @pl.when(i == n - 1)
    def _():
        all_runs(slot).wait()
        zero_fill(lambda c: c.wait())


def _dispatch(tile_cnt, tile_off, tile_dst, pad_info, meta, h2t, n_blocks, t):
    n_tiles = meta.shape[0] // (2 * TOP_K * t)
    n_slots = n_blocks * MOE_BLOCK
    grid_spec = pltpu.PrefetchScalarGridSpec(
        num_scalar_prefetch=4,
        grid=(n_tiles,),
        in_specs=[pl.BlockSpec(memory_space=pl.ANY),
                  pl.BlockSpec((t * SUBLANES, LANES), lambda i, *_: (i, 0))],
        out_specs=pl.BlockSpec(memory_space=pl.ANY),
        scratch_shapes=[
            pltpu.SMEM((2 * TOP_K * t,), I32),
            pltpu.VMEM((2, TOP_K * t * SUBLANES, LANES), F32),
            pltpu.VMEM((MOE_BLOCK * SUBLANES, LANES), F32),
            pltpu.SemaphoreType.DMA((2,)),
            pltpu.SemaphoreType.DMA((2,)),
            pltpu.SemaphoreType.DMA,
        ],
    )
    return pl.pallas_call(
        functools.partial(_dispatch_kernel, t, n_blocks),
        grid_spec=grid_spec,
        out_shape=jax.ShapeDtypeStruct((n_slots * SUBLANES, LANES), F32),
        compiler_params=_cparams(("arbitrary",)),
        name="dispatch",
    )(tile_cnt, tile_off, tile_dst, pad_info, meta, h2t)


def _experts_kernel(ff, be_ref, nv_ref, slot_ref, nxt_ref, xs_ref, wgu_hbm, bgu_ref, wdn_hbm, bdn_ref, ys_ref,
                    wgu_f, wdn_f, wgu_b, wdn_b, sem):
    i = pl.program_id(0)
    nv = nv_ref[i]
    n_slabs = xs_ref.shape[0] // MOE_BLOCK

    def fetch(e, s):
        return (pltpu.make_async_copy(wgu_hbm.at[e], wgu_f.at[s], sem.at[0, s]),
                pltpu.make_async_copy(wdn_hbm.at[e], wdn_f.at[s], sem.at[1, s]))

    @pl.when(i == 0)
    def _():
        for c in fetch(be_ref[0], 0):
            c.start()

    @pl.when((i == 0) | (be_ref[i] != be_ref[jnp.maximum(i - 1, 0)]))
    def _():
        s = slot_ref[i]
        for c in fetch(be_ref[i], s):
            c.wait()
        wgu_b[...] = wgu_f[s].astype(BF16)
        wdn_b[...] = wdn_f[s].astype(BF16)

        @pl.when(nxt_ref[i] >= 0)
        def _():
            for c in fetch(nxt_ref[i], 1 - s):
                c.start()

    @pl.when(nv > 0)
    def _():
        x = jnp.concatenate(
            [xs_ref[pl.ds(j, MOE_BLOCK, stride=SUBLANES), :] for j in range(n_slabs)], axis=1)
        gu = _dot(x.astype(BF16), wgu_b[...]) + bgu_ref[...]
        gate = jnp.minimum(gu[:, :ff], SWIGLU_LIMIT)
        up = jnp.clip(gu[:, ff:], -SWIGLU_LIMIT, SWIGLU_LIMIT)
        act = (up + 1.0) * (gate * _sigmoid(SWIGLU_ALPHA * gate))
        y = _dot(act.astype(BF16), wdn_b[...]) + bdn_ref[...]
        for j in range(n_slabs):
            ys_ref[pl.ds(j, MOE_BLOCK, stride=SUBLANES), :] = y[:, j * LANES:(j + 1) * LANES]

    @pl.when(nv == 0)
    def _():
        ys_ref[...] = jnp.zeros_like(ys_ref)


def _experts(block_expert, block_valid, xs, w_gu, b_gu, w_dn, b_dn):
    n_blocks = block_expert.shape[0]
    E, D, ff2 = w_gu.shape
    ff = ff2 // 2
    change = jnp.concatenate([jnp.zeros((1,), I32), (block_expert[1:] != block_expert[:-1]).astype(I32)])
    slot = jnp.cumsum(change) % 2
    later = block_expert[None, :] > block_expert[:, None]
    nxt = jnp.min(jnp.where(later, block_expert[None, :], E), axis=1)
    nxt = jnp.where(nxt < E, nxt, -1)
    rows = pl.BlockSpec((MOE_BLOCK * SUBLANES, LANES), lambda i, *_: (i, 0))
    grid_spec = pltpu.PrefetchScalarGridSpec(
        num_scalar_prefetch=4,
        grid=(n_blocks,),
        in_specs=[
            rows,
            pl.BlockSpec(memory_space=pl.ANY),
            pl.BlockSpec((None, 1, ff2), lambda i, be, *_: (be[i], 0, 0)),
            pl.BlockSpec(memory_space=pl.ANY),
            pl.BlockSpec((None, 1, D), lambda i, be, *_: (be[i], 0, 0)),
        ],
        out_specs=rows,
        scratch_shapes=[
            pltpu.VMEM((2, D, ff2), F32), pltpu.VMEM((2, ff, D), F32),
            pltpu.VMEM((D, ff2), BF16), pltpu.VMEM((ff, D), BF16),
            pltpu.SemaphoreType.DMA((2, 2)),
        ],
    )
    return pl.pallas_call(
        functools.partial(_experts_kernel, ff),
        grid_spec=grid_spec,
        out_shape=jax.ShapeDtypeStruct(xs.shape, F32),
        compiler_params=_cparams(("arbitrary",)),
        name="experts",
    )(block_expert, block_valid, slot.astype(I32), nxt.astype(I32), xs, w_gu, b_gu.reshape(E, 1, ff2), w_dn,
      b_dn.reshape(E, 1, D))


def _combine_kernel(t, cnt_ref, off_ref, dst_ref, meta_hbm, ys_hbm, x1_ref, mod_ref, o_ref,
                    meta_smem, stage_ref, y_ref, psem, sem):
    i = pl.program_id(0)
    n = pl.num_programs(0)
    slot = i % 2
    base = _fetch_tile(meta_hbm, meta_smem, psem, t, 2 * TOP_K * t)

    def fetch_runs(tile, s):
        stage = stage_ref.at[s]
        _run_copies(t, tile, cnt_ref, off_ref, dst_ref,
                    lambda a, b, m: pltpu.make_async_copy(_span(ys_hbm, b, m), _span(stage, a, m), sem.at[s]))

    @pl.when(i == 0)
    def _():
        fetch_runs(0, 0)

    @pl.when(i + 1 < n)
    def _():
        fetch_runs(i + 1, 1 - slot)

    pltpu.make_async_copy(_rows(ys_hbm, 0, TOP_K * t), stage_ref.at[slot], sem.at[slot]).wait()
    stage = stage_ref.at[slot]

    def gather(r, carry):
        acc = None
        for k in range(TOP_K):
            gate = lax.bitcast_convert_type(meta_smem[base + r + (TOP_K + k) * t], F32)
            term = gate * _row_at(stage, meta_smem[base + r + k * t])[...]
            acc = term if acc is None else acc + term
        _rows(y_ref, r, 1)[...] = acc
        return carry

    lax.fori_loop(0, t, gather, 0, unroll=8)

    n_slabs = o_ref.shape[1] // LANES
    y = jnp.concatenate([y_ref[pl.ds(j, t, stride=SUBLANES), :] for j in range(n_slabs)], axis=1)
    o_ref[...] = x1_ref[...] + mod_ref[5:6, :] * y


def _combine(tile_cnt, tile_off, tile_dst, meta, ys, x1, mod3, S, t):
    T, D = x1.shape
    n_tiles = T // t
    tps = S // t
    grid_spec = pltpu.PrefetchScalarGridSpec(
        num_scalar_prefetch=3,
        grid=(n_tiles,),
        in_specs=[
            pl.BlockSpec(memory_space=pl.ANY),
            pl.BlockSpec(memory_space=pl.ANY),
            pl.BlockSpec((t, D), lambda i, *_: (i, 0)),
            pl.BlockSpec((None, 6, D), lambda i, *_: (i // tps, 0, 0)),
        ],
        out_specs=pl.BlockSpec((t, D), lambda i, *_: (i, 0)),
        scratch_shapes=[
            pltpu.SMEM((2 * 2 * TOP_K * t,), I32),
            pltpu.VMEM((2, TOP_K * t * SUBLANES, LANES), F32),
            pltpu.VMEM((t * SUBLANES, LANES), F32),
            pltpu.SemaphoreType.DMA((2,)),
            pltpu.SemaphoreType.DMA((2,)),
        ],
    )
    return pl.pallas_call(
        functools.partial(_combine_kernel, t),
        grid_spec=grid_spec,
        out_shape=jax.ShapeDtypeStruct((T, D), F32),
        compiler_params=_cparams(("arbitrary",)),
        name="combine",
    )(tile_cnt, tile_off, tile_dst, meta, ys, x1, mod3)


def _tile(n, target):
    t = min(n, target)
    while n % t:
        t //= 2
    return t


def _block_diag(w):
    n, bw, _ = w.shape
    eye = jnp.eye(n, dtype=w.dtype)
    return (eye[:, None, :, None] * w[:, :, None, :]).reshape(n * bw, n * bw)


def _layer(l, x2, B, S, c, ada_w, ada_b, norm1_g, w_in, q_norm_g, k_norm_g, lambda_q1, lambda_k1, lambda_q2,
           lambda_k2, attn_subln_g, conv_w, conv_b, lru_wa, lru_ba, lru_wx, lru_bx, lru_lambda, lru_out_g,
           w_out, norm2_g, router_w, router_b, w_gate_up, b_gate_up, w_down, b_down):
    T, D = x2.shape
    lam_init = 0.8 - 0.6 * math.exp(-0.3 * l)
    aw = D // 2
    lw = D - aw
    heads = aw // ATTN_DV
    qkw = heads * 2 * ATTN_DK

    mod, lam = _adaln(c, ada_w, ada_b, lambda_q1, lambda_k1, lambda_q2, lambda_k2, lam_init)
    mod3 = mod.reshape(c.shape[0], 6, D)

    reps = qkw // ATTN_DK
    gq = (jnp.tile(q_norm_g, reps) * (ATTN_DK ** -0.5 * math.log2(math.e))).reshape(1, qkw)
    gk = jnp.tile(k_norm_g, reps).reshape(1, qkw)
    tm = _tile(S, 512)
    q, k, v, xr, gr = _in_proj(x2, mod3, norm1_g.reshape(1, D), w_in.astype(BF16), gq, gk, S, _tile(S, 1024),
                               qkw, aw, lw)

    attn = _attention(q, k, v, lam, attn_subln_g.reshape(1, ATTN_DV), B, S, _tile(S, 1024), _tile(S, 512),
                      1.0 - lam_init)

    w_gates = (0.5 * jnp.concatenate([_block_diag(lru_wa), _block_diag(lru_wx)], axis=1)).astype(BF16)
    b_gates = 0.5 * jnp.concatenate([lru_ba, lru_bx]).reshape(1, 2 * lw)
    lru = _lru(xr, gr, conv_w, conv_b.reshape(1, lw), w_gates, b_gates, lru_lambda.reshape(1, lw),
               lru_out_g.reshape(1, lw), B, S, _tile(S, 1024))

    x1, h2t, meta, tile_cnt = _out_proj(
        attn, lru, x2, mod3, w_out.astype(BF16), norm2_g.reshape(1, D), router_w.T,
        router_b.reshape(N_EXPERTS, 1), S, tm)

    tile_cnt = tile_cnt[:, :, 0]
    counts = jnp.sum(tile_cnt, axis=0)
    padded = ((counts + MOE_BLOCK - 1) // MOE_BLOCK) * MOE_BLOCK
    pad_ends = jnp.cumsum(padded)
    pad_starts = pad_ends - padded
    n_blocks = (T * TOP_K) // MOE_BLOCK + N_EXPERTS
    blk_start = jnp.arange(n_blocks, dtype=I32) * MOE_BLOCK
    owner = blk_start[:, None] >= pad_ends[None, :]
    block_expert = jnp.minimum(jnp.sum(owner, axis=1), N_EXPERTS - 1).astype(I32)
    onehot = block_expert[:, None] == jnp.arange(N_EXPERTS, dtype=I32)[None, :]
    row_end = jnp.sum(jnp.where(onehot, (pad_starts + counts)[None, :], 0), axis=1)
    block_valid = jnp.clip(row_end - blk_start, 0, MOE_BLOCK).astype(I32)
    tile_off = jnp.cumsum(tile_cnt, axis=1) - tile_cnt
    tile_dst = pad_starts[None, :] + jnp.cumsum(tile_cnt, axis=0) - tile_cnt
    flat = lambda a: (a * SUBLANES).reshape(-1).astype(I32)

    meta = meta.reshape(-1)
    pad_info = jnp.concatenate([(pad_starts + counts) * SUBLANES, (padded - counts) * SUBLANES,
                                pad_ends[-1:] // MOE_BLOCK]).astype(I32)
    xs = _dispatch(flat(tile_cnt), flat(tile_off), flat(tile_dst), pad_info, meta, h2t, n_blocks, tm)
    ys = _experts(block_expert, block_valid, xs, w_gate_up, b_gate_up, w_down, b_down)
    return _combine(flat(tile_cnt), flat(tile_off), flat(tile_dst), meta, ys, x1, mod3, S, tm)


def kernel(x, c, ada_w, ada_b, norm1_g, w_in, q_norm_g, k_norm_g, lambda_q1, lambda_k1, lambda_q2, lambda_k2,
           attn_subln_g, conv_w, conv_b, lru_wa, lru_ba, lru_wx, lru_bx, lru_lambda, lru_out_g, w_out, norm2_g,
           router_w, router_b, w_gate_up, b_gate_up, w_down, b_down):
    B, S, D = x.shape
    params = (ada_w, ada_b, norm1_g, w_in, q_norm_g, k_norm_g, lambda_q1, lambda_k1, lambda_q2, lambda_k2,
              attn_subln_g, conv_w, conv_b, lru_wa, lru_ba, lru_wx, lru_bx, lru_lambda, lru_out_g, w_out,
              norm2_g, router_w, router_b, w_gate_up, b_gate_up, w_down, b_down)
    x2 = x.reshape(B * S, D)
    for l in range(ada_w.shape[0]):
        x2 = _layer(l, x2, B, S, c, *[p[l] for p in params])
    return x2.reshape(B, S, D)
```

```python
import functools
import math

import jax
import jax.numpy as jnp
from jax import lax
from jax.experimental import pallas as pl
from jax.experimental.pallas import tpu as pltpu

F32 = jnp.float32
BF16 = jnp.bfloat16
I32 = jnp.int32

ATTN_DK = 64
ATTN_DV = 2 * ATTN_DK
CONV_W = 4
LRU_C = 8.0
N_EXPERTS = 32
TOP_K = 4
SWIGLU_LIMIT = 7.0
SWIGLU_ALPHA = 1.702
MOE_BLOCK = 512
EPS = 1e-6
NEG_BIG = -1e30

LANES = 128
SUBLANES = 8
VMEM_LIMIT = 56 * 1024 * 1024


def _cparams(sem):
    return pltpu.CompilerParams(dimension_semantics=sem, vmem_limit_bytes=VMEM_LIMIT)


def _split_hi_lo(x):
    hi = x.astype(BF16)
    lo = (x - hi.astype(F32)).astype(BF16)
    return hi, lo


def _sigmoid(x):
    return 0.5 * jnp.tanh(0.5 * x) + 0.5


def _dot(a, b):
    return jnp.dot(a, b, preferred_element_type=F32)


def _dot_tb(a, b):
    return lax.dot_general(a, b, (((1,), (1,)), ((), ())), preferred_element_type=F32)


def _adaln_kernel(lam_init, c_ref, w_ref, b_ref, lq1, lk1, lq2, lk2, mod_ref, lam_ref):
    c = c_ref[...]
    s = c * jax.nn.sigmoid(c)
    s_hi, s_lo = _split_hi_lo(s)
    w = w_ref[...]
    w_hi, w_lo = _split_hi_lo(w)
    mod_ref[...] = _dot(s_hi, w_hi) + _dot(s_hi, w_lo) + _dot(s_lo, w_hi) + b_ref[...]
    d1 = jnp.sum(lq1[...] * lk1[...], axis=-1, keepdims=True)
    d2 = jnp.sum(lq2[...] * lk2[...], axis=-1, keepdims=True)
    lam = jnp.exp(d1) - jnp.exp(d2) + lam_init
    lam_ref[...] = jnp.broadcast_to(lam, lam_ref.shape)


def _adaln(c, ada_w, ada_b, lq1, lk1, lq2, lk2, lam_init):
    B, D = c.shape
    n = ada_w.shape[1] // D
    vec = lambda: pl.BlockSpec((1, ATTN_DK), lambda j: (0, 0))
    return pl.pallas_call(
        functools.partial(_adaln_kernel, lam_init),
        grid=(n,),
        in_specs=[
            pl.BlockSpec((B, D), lambda j: (0, 0)),
            pl.BlockSpec((D, D), lambda j: (0, j)),
            pl.BlockSpec((1, D), lambda j: (0, j)),
            vec(), vec(), vec(), vec(),
        ],
        out_specs=[
            pl.BlockSpec((B, D), lambda j: (0, j)),
            pl.BlockSpec((1, LANES), lambda j: (0, 0)),
        ],
        out_shape=[
            jax.ShapeDtypeStruct((B, n * D), F32),
            jax.ShapeDtypeStruct((1, LANES), F32),
        ],
        compiler_params=_cparams(("arbitrary",)),
        name="adaln",
    )(c, ada_w, ada_b.reshape(1, -1), lq1.reshape(1, -1), lk1.reshape(1, -1),
      lq2.reshape(1, -1), lk2.reshape(1, -1))


def _rms_modulate(x, g, shift, scale):
    ms = jnp.mean(x * x, axis=-1, keepdims=True)
    y = x * lax.rsqrt(ms + EPS) * g
    return y * (1.0 + scale) + shift


def _group_rms_scale(q, group_ones):
    ss = _dot((q * q).astype(BF16), group_ones)
    return lax.rsqrt(ss * (1.0 / ATTN_DK) + EPS)


def _in_proj_kernel(qkw, aw, lw, x_ref, mod_ref, g_ref, w_ref, gq_ref, gk_ref, ones_ref,
                    q_ref, k_ref, v_ref, xr_ref, gr_ref):
    h = _rms_modulate(x_ref[...], g_ref[...], mod_ref[0:1, :], mod_ref[1:2, :])
    hb = h.astype(BF16)
    group_ones = ones_ref[...]
    o = 0
    q = _dot(hb, w_ref[:, o:o + qkw]); o += qkw
    q_ref[...] = (q * _group_rms_scale(q, group_ones) * gq_ref[...]).astype(BF16)
    k = _dot(hb, w_ref[:, o:o + qkw]); o += qkw
    k_ref[...] = (k * _group_rms_scale(k, group_ones) * gk_ref[...]).astype(BF16)
    v_ref[...] = _dot(hb, w_ref[:, o:o + aw]).astype(BF16); o += aw
    xr_ref[...] = _dot(hb, w_ref[:, o:o + lw]); o += lw
    gr_ref[...] = _dot(hb, w_ref[:, o:o + lw])


def _in_proj(x2, mod3, norm_g, w_in_b, gq, gk, S, tm, qkw, aw, lw):
    T, D = x2.shape
    tps = S // tm
    grp = jnp.arange(qkw, dtype=I32) // ATTN_DK
    group_ones = (grp[:, None] == grp[None, :]).astype(BF16)
    const = lambda shape: pl.BlockSpec(shape, lambda i: (0,) * len(shape))
    row = lambda w: pl.BlockSpec((tm, w), lambda i: (i, 0))
    return pl.pallas_call(
        functools.partial(_in_proj_kernel, qkw, aw, lw),
        grid=(T // tm,),
        in_specs=[
            row(D),
            pl.BlockSpec((None, 6, D), lambda i: (i // tps, 0, 0)),
            const((1, D)),
            const(w_in_b.shape),
            const((1, qkw)), const((1, qkw)),
            const((qkw, qkw)),
        ],
        out_specs=[row(qkw), row(qkw), row(aw), row(lw), row(lw)],
        out_shape=[
            jax.ShapeDtypeStruct((T, qkw), BF16),
            jax.ShapeDtypeStruct((T, qkw), BF16),
            jax.ShapeDtypeStruct((T, aw), BF16),
            jax.ShapeDtypeStruct((T, lw), F32),
            jax.ShapeDtypeStruct((T, lw), F32),
        ],
        compiler_params=_cparams(("arbitrary",)),
        name="in_proj",
    )(x2, mod3, norm_g, w_in_b, gq, gk, group_ones)


ATTN_ROW_CHUNK = 128


def _attn_kernel(bq, bk, out_scale, q_ref, k_ref, v_ref, lam_ref, g_ref, o_ref,
                 qq_ref, vp_ref, s_ref, sd_ref, m_ref, acc_ref):
    i = pl.program_id(2)
    rc = ATTN_ROW_CHUNK
    n_chunks = 2 * bq // rc

    @pl.when(i == 0)
    def _():
        col = lax.broadcasted_iota(I32, (vp_ref.shape[0], ATTN_DV), 1)
        vp_ref[:, :ATTN_DV] = v_ref[...]
        vp_ref[:, ATTN_DV:] = jnp.where(col == 0, 1.0, 0.0).astype(BF16)

    q = q_ref[...]
    lane = lax.broadcasted_iota(I32, q.shape, 1)
    zero = jnp.zeros_like(q)
    qq_ref[0:bq, :] = jnp.where(lane < ATTN_DK, q, zero)
    qq_ref[bq:, :] = jnp.where(lane >= ATTN_DK, q, zero)
    m_ref[...] = jnp.full(m_ref.shape, NEG_BIG, F32)
    acc_ref[...] = jnp.zeros_like(acc_ref)

    def chunk(r):
        return pl.ds(r * rc, rc)

    def scores(r, start, kw):
        return _dot_tb(qq_ref[chunk(r), :], k_ref[pl.ds(start, kw), :])

    def softmax_pv(s, r, start):
        rows = chunk(r)
        kw = s.shape[1]
        m_old = m_ref[rows, :]
        m_new = jnp.maximum(m_old, jnp.max(s, axis=-1, keepdims=True))
        alpha = jnp.exp2(m_old - m_new)
        p = jnp.exp2(s - jnp.tile(m_new, (1, kw // LANES)))
        pv = _dot(p.astype(BF16), vp_ref[pl.ds(start, kw), :])
        acc_ref[rows, :] = acc_ref[rows, :] * jnp.tile(alpha, (1, 2)) + pv
        m_ref[rows, :] = m_new

    for r in range(n_chunks):
        s_ref[chunk(r), :] = scores(r, 0, bk)

    def body(j, carry):
        start = pl.multiple_of(j * bk, bk)
        for r in range(n_chunks):
            s = s_ref[chunk(r), :]
            s_ref[chunk(r), :] = scores(r, start + bk, bk)
            softmax_pv(s, r, start)
        return carry

    n_full = i * (bq // bk)
    lax.fori_loop(0, n_full, body, 0)

    start = pl.multiple_of(n_full * bk, bk)
    tri = (lax.broadcasted_iota(I32, (rc, rc), 1) <= lax.broadcasted_iota(I32, (rc, rc), 0))
    first_key = [(r * rc) % bq for r in range(n_chunks)]
    for r in range(n_chunks):
        if first_key[r] + rc > bk:
            sd_ref[chunk(r), 0:first_key[r] + rc - bk] = scores(r, start + bk, first_key[r] + rc - bk)
    for r in range(n_chunks):
        q0 = first_key[r]
        kw = q0 + rc
        parts = [s_ref[chunk(r), 0:min(q0, bk)]] if q0 else []
        if q0 > bk:
            parts.append(sd_ref[chunk(r), 0:q0 - bk])
        last = s_ref[chunk(r), q0:kw] if kw <= bk else sd_ref[chunk(r), q0 - bk:kw - bk]
        parts.append(jnp.where(tri, last, NEG_BIG))
        softmax_pv(parts[0] if len(parts) == 1 else jnp.concatenate(parts, axis=1), r, start)

    acc = acc_ref[...]
    o = acc[:, :ATTN_DV] / acc[:, ATTN_DV:ATTN_DV + 1]
    a = o[:bq] - lam_ref[0:1, 0:1] * o[bq:]
    ms = jnp.mean(a * a, axis=-1, keepdims=True)
    o_ref[...] = (a * lax.rsqrt(ms + EPS) * g_ref[...] * out_scale).astype(BF16)


def _attention(q, k, v, lam, subln_g, B, S, bq, bk, out_scale):
    T, qkw = q.shape
    H = qkw // ATTN_DV
    nq = S // bq
    return pl.pallas_call(
        functools.partial(_attn_kernel, bq, bk, out_scale),
        grid=(B, H, nq),
        in_specs=[
            pl.BlockSpec((bq, ATTN_DV), lambda b, h, i: (b * nq + i, h)),
            pl.BlockSpec((S, ATTN_DV), lambda b, h, i: (b, h)),
            pl.BlockSpec((S, ATTN_DV), lambda b, h, i: (b, h)),
            pl.BlockSpec((1, LANES), lambda b, h, i: (0, 0)),
            pl.BlockSpec((1, ATTN_DV), lambda b, h, i: (0, 0)),
        ],
        out_specs=pl.BlockSpec((bq, ATTN_DV), lambda b, h, i: (b * nq + i, h)),
        out_shape=jax.ShapeDtypeStruct((T, H * ATTN_DV), BF16),
        scratch_shapes=[
            pltpu.VMEM((2 * bq, ATTN_DV), BF16),
            pltpu.VMEM((S, 2 * ATTN_DV), BF16),
            pltpu.VMEM((2 * bq, bk), F32),
            pltpu.VMEM((2 * bq, max(bq - bk, LANES)), F32),
            pltpu.VMEM((2 * bq, LANES), F32),
            pltpu.VMEM((2 * bq, 2 * ATTN_DV), F32),
        ],
        compiler_params=_cparams(("arbitrary", "arbitrary", "arbitrary")),
        name="attention",
    )(q, k, v, lam, subln_g)


def _lru_kernel(tc, cw, pitch, x_ref, gate_ref, cw_ref, cb_ref, wg_ref, bg_ref, lam_ref, og_ref,
                o_ref, ext_ref, a_ref, b_ref, hc_ref):
    c = pl.program_id(1)

    @pl.when(c == 0)
    def _():
        ext_ref[0:SUBLANES, :] = jnp.zeros((SUBLANES, cw), F32)
        hc_ref[...] = jnp.zeros_like(hc_ref)

    ext_ref[SUBLANES:SUBLANES + tc, :] = x_ref[...]
    xc = cb_ref[...] + cw_ref[CONV_W - 1:CONV_W, :] * x_ref[...]
    for w in range(CONV_W - 1):
        sh = CONV_W - 1 - w
        xc = xc + cw_ref[w:w + 1, :] * ext_ref[SUBLANES - sh:SUBLANES - sh + tc, :]
    tail = ext_ref[tc:tc + SUBLANES, :]
    ext_ref[0:SUBLANES, :] = tail

    g = _dot(xc.astype(BF16), wg_ref[...]) + bg_ref[...]
    t_r = jnp.tanh(g[:, :cw])
    t_i = jnp.tanh(g[:, cw:])
    nl = -lam_ref[...]
    softplus = jnp.maximum(nl, 0.0) + jnp.log1p(jnp.exp(-jnp.abs(nl)))
    c1 = (-0.5 * LRU_C) * softplus
    log_a = c1 * t_r + c1
    th = jnp.tanh(log_a)
    p = -2.0 * th
    root = jnp.where(p > 0.0, p * lax.rsqrt(p * (1.0 - th)), 0.0)
    a = jnp.exp(log_a)
    b = root * ((t_i + 1.0) * (0.5 * xc))

    steps = tc // SUBLANES
    n_lt = cw // LANES
    for q in range(SUBLANES):
        for l in range(n_lt):
            a_ref[l, q * pitch:q * pitch + steps, :] = a[q * steps:(q + 1) * steps, l * LANES:(l + 1) * LANES]
            b_ref[l, q * pitch:q * pitch + steps, :] = b[q * steps:(q + 1) * steps, l * LANES:(l + 1) * LANES]

    def scan_step(t, carry):
        out = []
        for l in range(n_lt):
            h, prod = carry[l]
            rows = pl.ds(t, SUBLANES, stride=pitch)
            a_t = a_ref[l, rows, :]
            h = a_t * h + b_ref[l, rows, :]
            prod = a_t * prod
            b_ref[l, rows, :] = h
            a_ref[l, rows, :] = prod
            out.append((h, prod))
        return tuple(out)

    init = tuple((jnp.zeros((SUBLANES, LANES), F32), jnp.ones((SUBLANES, LANES), F32)) for _ in range(n_lt))
    last = lax.fori_loop(0, steps, scan_step, init, unroll=4)

    h_cols = []
    for l in range(n_lt):
        h_last, p_last = last[l]
        entering = [hc_ref[0:1, l * LANES:(l + 1) * LANES]]
        for q in range(SUBLANES):
            entering.append(h_last[q:q + 1, :] + p_last[q:q + 1, :] * entering[q])
        hc_ref[:, l * LANES:(l + 1) * LANES] = jnp.broadcast_to(entering[SUBLANES], (SUBLANES, LANES))
        h_cols.append(jnp.concatenate(
            [b_ref[l, q * pitch:q * pitch + steps, :] + a_ref[l, q * pitch:q * pitch + steps, :] * entering[q]
             for q in range(SUBLANES)], axis=0))
    h = jnp.concatenate(h_cols, axis=1)

    gt = gate_ref[...]
    gelu = 0.5 * gt * (1.0 + jnp.tanh(math.sqrt(2.0 / math.pi) * (gt + 0.044715 * gt * gt * gt)))
    y = h * gelu
    ms = jnp.mean(y * y, axis=-1, keepdims=True)
    o_ref[...] = (y * lax.rsqrt(ms + EPS) * og_ref[...]).astype(BF16)


def _scan_pitch(steps):
    return steps if (steps // SUBLANES) % 2 else steps + SUBLANES


def _lru(xr, gr, conv_w, conv_b, w_gates, b_gates, lru_lambda, out_g, B, S, tc):
    T, cw = xr.shape
    nc = S // tc
    pitch = _scan_pitch(tc // SUBLANES)
    const = lambda shape: pl.BlockSpec(shape, lambda b, c: (0,) * len(shape))
    row = pl.BlockSpec((tc, cw), lambda b, c: (b * nc + c, 0))
    return pl.pallas_call(
        functools.partial(_lru_kernel, tc, cw, pitch),
        grid=(B, nc),
        in_specs=[row, row, const((CONV_W, cw)), const((1, cw)), const((cw, 2 * cw)),
                  const((1, 2 * cw)), const((1, cw)), const((1, cw))],
        out_specs=row,
        out_shape=jax.ShapeDtypeStruct((T, cw), BF16),
        scratch_shapes=[
            pltpu.VMEM((tc + SUBLANES, cw), F32),
            pltpu.VMEM((cw // LANES, SUBLANES * pitch, LANES), F32),
            pltpu.VMEM((cw // LANES, SUBLANES * pitch, LANES), F32),
            pltpu.VMEM((SUBLANES, cw), F32),
        ],
        compiler_params=_cparams(("arbitrary", "arbitrary")),
        name="lru",
    )(xr, gr, conv_w, conv_b, w_gates, b_gates, lru_lambda, out_g)


def _out_proj_kernel(tm, aw, attn_ref, lru_ref, x_ref, mod_ref, w_ref, g_ref, rw_ref, rb_ref, tri_ref, low_ref,
                     x1_ref, h2_ref, meta_ref, cnt_ref):
    mix = _dot(attn_ref[...], w_ref[0:aw, :]) + _dot(lru_ref[...], w_ref[aw:, :])
    x1 = x_ref[...] + mod_ref[2:3, :] * mix
    x1_ref[...] = x1
    h2 = _rms_modulate(x1, g_ref[...], mod_ref[3:4, :], mod_ref[4:5, :])
    for j in range(h2.shape[1] // LANES):
        h2_ref[pl.ds(j, tm, stride=SUBLANES), :] = h2[:, j * LANES:(j + 1) * LANES]

    h_hi, h_lo = _split_hi_lo(h2)
    w_hi, w_lo = _split_hi_lo(rw_ref[...])
    by_hi = _dot_tb(jnp.concatenate([w_hi, w_lo], axis=0), h_hi)
    lg = by_hi[:N_EXPERTS] + by_hi[N_EXPERTS:] + _dot_tb(w_hi, h_lo) + rb_ref[...]

    eidx = lax.broadcasted_iota(I32, lg.shape, 0)
    picked = jnp.zeros(lg.shape, F32)
    vals, idxs = [], []
    for _ in range(TOP_K):
        m = jnp.max(lg, axis=0, keepdims=True)
        idx = jnp.min(jnp.where(lg == m, eidx, N_EXPERTS), axis=0, keepdims=True)
        sel = eidx == idx
        lg = jnp.where(sel, -jnp.inf, lg)
        picked = picked + sel.astype(F32)
        vals.append(m)
        idxs.append(idx)
    es = [jnp.exp(v - vals[0]) for v in vals]
    den = es[0] + es[1] + es[2] + es[3]
    gates = jnp.concatenate([e / den for e in es], axis=0)

    cnt = jnp.sum(picked, axis=1, keepdims=True)
    cnt_b = jnp.broadcast_to(cnt, cnt_ref.shape)
    cnt_hi = jnp.floor(cnt_b * (1.0 / 256.0))
    cnt_lo = cnt_b - 256.0 * cnt_hi
    run_start = 256.0 * _dot(low_ref[...], cnt_hi.astype(BF16)) + _dot(low_ref[...], cnt_lo.astype(BF16))
    before = _dot(picked.astype(BF16), tri_ref[...]) + run_start[:, 0:1]
    pos = [jnp.sum(jnp.where(eidx == idx, before, 0.0), axis=0, keepdims=True) for idx in idxs]
    meta_ref[0:TOP_K, :] = jnp.concatenate(pos, axis=0).astype(I32) * SUBLANES
    meta_ref[TOP_K:, :] = lax.bitcast_convert_type(gates, I32)
    cnt_ref[...] = cnt_b.astype(I32)


def _out_proj(attn, lru, x2, mod3, w_out_b, norm_g, router_wt, router_b, S, tm):
    T, D = x2.shape
    aw = attn.shape[1]
    lw = lru.shape[1]
    tps = S // tm
    tri = (jnp.arange(tm, dtype=I32)[:, None] < jnp.arange(tm, dtype=I32)[None, :]).astype(BF16)
    experts = jnp.arange(N_EXPERTS, dtype=I32)
    low = (experts[None, :] < experts[:, None]).astype(BF16)
    const = lambda shape: pl.BlockSpec(shape, lambda i: (0,) * len(shape))
    row = lambda w: pl.BlockSpec((tm, w), lambda i: (i, 0))
    return pl.pallas_call(
        functools.partial(_out_proj_kernel, tm, aw),
        grid=(T // tm,),
        in_specs=[
            row(aw), row(lw), row(D),
            pl.BlockSpec((None, 6, D), lambda i: (i // tps, 0, 0)),
            const((D, D)), const((1, D)), const((N_EXPERTS, D)), const((N_EXPERTS, 1)),
            const((tm, tm)), const((N_EXPERTS, N_EXPERTS)),
        ],
        out_specs=[
            row(D),
            pl.BlockSpec((tm * SUBLANES, LANES), lambda i: (i, 0)),
            pl.BlockSpec((None, 2 * TOP_K, tm), lambda i: (i, 0, 0)),
            pl.BlockSpec((None, N_EXPERTS, LANES), lambda i: (i, 0, 0)),
        ],
        out_shape=[
            jax.ShapeDtypeStruct((T, D), F32),
            jax.ShapeDtypeStruct((T * SUBLANES, LANES), F32),
            jax.ShapeDtypeStruct((T // tm, 2 * TOP_K, tm), I32),
            jax.ShapeDtypeStruct((T // tm, N_EXPERTS, LANES), I32),
        ],
        compiler_params=_cparams(("arbitrary",)),
        name="out_proj",
    )(attn, lru, x2, mod3, w_out_b, norm_g, router_wt, router_b, tri, low)


def _rows(ref, first_row, n_rows):
    return ref.at[pl.ds(pl.multiple_of(first_row * SUBLANES, SUBLANES), n_rows * SUBLANES), :]


def _row_at(ref, sublane_offset):
    return ref.at[pl.ds(pl.multiple_of(sublane_offset, SUBLANES), SUBLANES), :]


def _fetch_tile(meta_hbm, smem, sems, t, n_words):
    i = pl.program_id(0)
    n = pl.num_programs(0)
    slot = i % 2
    per_tile = 2 * TOP_K * t

    def copy(tile, s):
        src = meta_hbm.at[pl.ds(pl.multiple_of(tile * per_tile, per_tile), n_words)]
        return pltpu.make_async_copy(src, smem.at[pl.ds(pl.multiple_of(s * n_words, n_words), n_words)], sems.at[s])

    @pl.when(i == 0)
    def _():
        copy(0, 0).start()

    copy(i, slot).wait()

    @pl.when(i + 1 < n)
    def _():
        copy(i + 1, 1 - slot).start()

    return slot * n_words


def _run_copies(t, tile, cnt_ref, off_ref, dst_ref, make_copy):
    def expert(e, carry):
        n = cnt_ref[tile * N_EXPERTS + e]
        off = off_ref[tile * N_EXPERTS + e]
        dst = dst_ref[tile * N_EXPERTS + e]
        for b in reversed(range(t.bit_length())):
            size = SUBLANES << b
            done = n & ~(2 * size - 1)

            @pl.when((n & size) != 0)
            def _():
                make_copy(off + done, dst + done, size).start()
        return carry

    lax.fori_loop(0, N_EXPERTS, expert, 0)


def _span(ref, first_sublane, n_sublanes):
    return ref.at[pl.ds(pl.multiple_of(first_sublane, SUBLANES), n_sublanes), :]


def _dispatch_kernel(t, n_blocks, cnt_ref, off_ref, dst_ref, pad_ref, meta_hbm, h_ref, xs_hbm,
                     meta_smem, stage_ref, zero_ref, psem, sem, zsem):
    i = pl.program_id(0)
    n = pl.num_programs(0)
    slot = i % 2

    def zero_fill(go):
        def expert(e, carry):
            first = pad_ref[e]
            length = pad_ref[N_EXPERTS + e]
            for b in reversed(range((MOE_BLOCK - 1).bit_length())):
                size = SUBLANES << b
                done = length & ~(2 * size - 1)

                @pl.when((length & size) != 0)
                def _():
                    go(pltpu.make_async_copy(_span(zero_ref, 0, size), _span(xs_hbm, first + done, size), zsem))
            return carry

        lax.fori_loop(0, N_EXPERTS, expert, 0)

        def block(b, carry):
            go(pltpu.make_async_copy(zero_ref, _rows(xs_hbm, b * MOE_BLOCK, MOE_BLOCK), zsem))
            return carry

        lax.fori_loop(pad_ref[2 * N_EXPERTS], n_blocks, block, 0)

    @pl.when(i == 0)
    def _():
        zero_ref[...] = jnp.zeros_like(zero_ref)
        zero_fill(lambda c: c.start())

    base = _fetch_tile(meta_hbm, meta_smem, psem, t, TOP_K * t)
    stage = stage_ref.at[slot]

    def place(r, carry):
        row = _rows(h_ref, r, 1)[...]
        for k in range(TOP_K):
            _row_at(stage, meta_smem[base + r + k * t])[...] = row
        return carry

    lax.fori_loop(0, t, place, 0, unroll=8)

    def all_runs(s):
        return pltpu.make_async_copy(stage_ref.at[s], _rows(xs_hbm, 0, TOP_K * t), sem.at[s])

    _run_copies(t, i, cnt_ref, off_ref, dst_ref,
                lambda a, b, m: pltpu.make_async_copy(_span(stage, a, m), _span(xs_hbm, b, m), sem.at[slot]))

    @pl.when(i > 0)
    def _():
        all_runs(1 - slot).wait()

    @pl.when(i == n - 1)
    def _():
        all_runs(slot).wait()
        zero_fill(lambda c: c.wait())


def _dispatch(tile_cnt, tile_off, tile_dst, pad_info, meta, h2t, n_blocks, t):
    n_tiles = meta.shape[0] // (2 * TOP_K * t)
    n_slots = n_blocks * MOE_BLOCK
    grid_spec = pltpu.PrefetchScalarGridSpec(
        num_scalar_prefetch=4,
        grid=(n_tiles,),
        in_specs=[pl.BlockSpec(memory_space=pl.ANY),
                  pl.BlockSpec((t * SUBLANES, LANES), lambda i, *_: (i, 0))],
        out_specs=pl.BlockSpec(memory_space=pl.ANY),
        scratch_shapes=[
            pltpu.SMEM((2 * TOP_K * t,), I32),
            pltpu.VMEM((2, TOP_K * t * SUBLANES, LANES), F32),
            pltpu.VMEM((MOE_BLOCK * SUBLANES, LANES), F32),
            pltpu.SemaphoreType.DMA((2,)),
            pltpu.SemaphoreType.DMA((2,)),
            pltpu.SemaphoreType.DMA,
        ],
    )
    return pl.pallas_call(
        functools.partial(_dispatch_kernel, t, n_blocks),
        grid_spec=grid_spec,
        out_shape=jax.ShapeDtypeStruct((n_slots * SUBLANES, LANES), F32),
        compiler_params=_cparams(("arbitrary",)),
        name="dispatch",
    )(tile_cnt, tile_off, tile_dst, pad_info, meta, h2t)


def _experts_kernel(ff, be_ref, nv_ref, slot_ref, nxt_ref, xs_ref, wgu_hbm, bgu_ref, wdn_hbm, bdn_ref, ys_ref,
                    wgu_f, wdn_f, wgu_b, wdn_b, sem):
    i = pl.program_id(0)
    nv = nv_ref[i]
    n_slabs = xs_ref.shape[0] // MOE_BLOCK

    def fetch(e, s):
        return (pltpu.make_async_copy(wgu_hbm.at[e], wgu_f.at[s], sem.at[0, s]),
                pltpu.make_async_copy(wdn_hbm.at[e], wdn_f.at[s], sem.at[1, s]))

    @pl.when(i == 0)
    def _():
        for c in fetch(be_ref[0], 0):
            c.start()

    @pl.when((i == 0) | (be_ref[i] != be_ref[jnp.maximum(i - 1, 0)]))
    def _():
        s = slot_ref[i]
        for c in fetch(be_ref[i], s):
            c.wait()
        wgu_b[...] = wgu_f[s].astype(BF16)
        wdn_b[...] = wdn_f[s].astype(BF16)

        @pl.when(nxt_ref[i] >= 0)
        def _():
            for c in fetch(nxt_ref[i], 1 - s):
                c.start()

    @pl.when(nv > 0)
    def _():
        x = jnp.concatenate(
            [xs_ref[pl.ds(j, MOE_BLOCK, stride=SUBLANES), :] for j in range(n_slabs)], axis=1)
        gu = _dot(x.astype(BF16), wgu_b[...]) + bgu_ref[...]
        gate = jnp.minimum(gu[:, :ff], SWIGLU_LIMIT)
        up = jnp.clip(gu[:, ff:], -SWIGLU_LIMIT, SWIGLU_LIMIT)
        act = (up + 1.0) * (gate * _sigmoid(SWIGLU_ALPHA * gate))
        y = _dot(act.astype(BF16), wdn_b[...]) + bdn_ref[...]
        for j in range(n_slabs):
            ys_ref[pl.ds(j, MOE_BLOCK, stride=SUBLANES), :] = y[:, j * LANES:(j + 1) * LANES]

    @pl.when(nv == 0)
    def _():
        ys_ref[...] = jnp.zeros_like(ys_ref)


def _experts(block_expert, block_valid, xs, w_gu, b_gu, w_dn, b_dn):
    n_blocks = block_expert.shape[0]
    E, D, ff2 = w_gu.shape
    ff = ff2 // 2
    change = jnp.concatenate([jnp.zeros((1,), I32), (block_expert[1:] != block_expert[:-1]).astype(I32)])
    slot = jnp.cumsum(change) % 2
    later = block_expert[None, :] > block_expert[:, None]
    nxt = jnp.min(jnp.where(later, block_expert[None, :], E), axis=1)
    nxt = jnp.where(nxt < E, nxt, -1)
    rows = pl.BlockSpec((MOE_BLOCK * SUBLANES, LANES), lambda i, *_: (i, 0))
    grid_spec = pltpu.PrefetchScalarGridSpec(
        num_scalar_prefetch=4,
        grid=(n_blocks,),
        in_specs=[
            rows,
            pl.BlockSpec(memory_space=pl.ANY),
            pl.BlockSpec((None, 1, ff2), lambda i, be, *_: (be[i], 0, 0)),
            pl.BlockSpec(memory_space=pl.ANY),
            pl.BlockSpec((None, 1, D), lambda i, be, *_: (be[i], 0, 0)),
        ],
        out_specs=rows,
        scratch_shapes=[
            pltpu.VMEM((2, D, ff2), F32), pltpu.VMEM((2, ff, D), F32),
            pltpu.VMEM((D, ff2), BF16), pltpu.VMEM((ff, D), BF16),
            pltpu.SemaphoreType.DMA((2, 2)),
        ],
    )
    return pl.pallas_call(
        functools.partial(_experts_kernel, ff),
        grid_spec=grid_spec,
        out_shape=jax.ShapeDtypeStruct(xs.shape, F32),
        compiler_params=_cparams(("arbitrary",)),
        name="experts",
    )(block_expert, block_valid, slot.astype(I32), nxt.astype(I32), xs, w_gu, b_gu.reshape(E, 1, ff2), w_dn,
      b_dn.reshape(E, 1, D))


def _combine_kernel(t, cnt_ref, off_ref, dst_ref, meta_hbm, ys_hbm, x1_ref, mod_ref, o_ref,
                    meta_smem, stage_ref, y_ref, psem, sem):
    i = pl.program_id(0)
    n = pl.num_programs(0)
    slot = i % 2
    base = _fetch_tile(meta_hbm, meta_smem, psem, t, 2 * TOP_K * t)

    def fetch_runs(tile, s):
        stage = stage_ref.at[s]
        _run_copies(t, tile, cnt_ref, off_ref, dst_ref,
                    lambda a, b, m: pltpu.make_async_copy(_span(ys_hbm, b, m), _span(stage, a, m), sem.at[s]))

    @pl.when(i == 0)
    def _():
        fetch_runs(0, 0)

    @pl.when(i + 1 < n)
    def _():
        fetch_runs(i + 1, 1 - slot)

    pltpu.make_async_copy(_rows(ys_hbm, 0, TOP_K * t), stage_ref.at[slot], sem.at[slot]).wait()
    stage = stage_ref.at[slot]

    def gather(r, carry):
        acc = None
        for k in range(TOP_K):
            gate = lax.bitcast_convert_type(meta_smem[base + r + (TOP_K + k) * t], F32)
            term = gate * _row_at(stage, meta_smem[base + r + k * t])[...]
            acc = term if acc is None else acc + term
        _rows(y_ref, r, 1)[...] = acc
        return carry

    lax.fori_loop(0, t, gather, 0, unroll=8)

    n_slabs = o_ref.shape[1] // LANES
    y = jnp.concatenate([y_ref[pl.ds(j, t, stride=SUBLANES), :] for j in range(n_slabs)], axis=1)
    o_ref[...] = x1_ref[...] + mod_ref[5:6, :] * y


def _combine(tile_cnt, tile_off, tile_dst, meta, ys, x1, mod3, S, t):
    T, D = x1.shape
    n_tiles = T // t
    tps = S // t
    grid_spec = pltpu.PrefetchScalarGridSpec(
        num_scalar_prefetch=3,
        grid=(n_tiles,),
        in_specs=[
            pl.BlockSpec(memory_space=pl.ANY),
            pl.BlockSpec(memory_space=pl.ANY),
            pl.BlockSpec((t, D), lambda i, *_: (i, 0)),
            pl.BlockSpec((None, 6, D), lambda i, *_: (i // tps, 0, 0)),
        ],
        out_specs=pl.BlockSpec((t, D), lambda i, *_: (i, 0)),
        scratch_shapes=[
            pltpu.SMEM((2 * 2 * TOP_K * t,), I32),
            pltpu.VMEM((2, TOP_K * t * SUBLANES, LANES), F32),
            pltpu.VMEM((t * SUBLANES, LANES), F32),
            pltpu.SemaphoreType.DMA((2,)),
            pltpu.SemaphoreType.DMA((2,)),
        ],
    )
    return pl.pallas_call(
        functools.partial(_combine_kernel, t),
        grid_spec=grid_spec,
        out_shape=jax.ShapeDtypeStruct((T, D), F32),
        compiler_params=_cparams(("arbitrary",)),
        name="combine",
    )(tile_cnt, tile_off, tile_dst, meta, ys, x1, mod3)


def _tile(n, target):
    t = min(n, target)
    while n % t:
        t //= 2
    return t


def _block_diag(w):
    n, bw, _ = w.shape
    eye = jnp.eye(n, dtype=w.dtype)
    return (eye[:, None, :, None] * w[:, :, None, :]).reshape(n * bw, n * bw)


def _layer(l, x2, B, S, c, ada_w, ada_b, norm1_g, w_in, q_norm_g, k_norm_g, lambda_q1, lambda_k1, lambda_q2,
           lambda_k2, attn_subln_g, conv_w, conv_b, lru_wa, lru_ba, lru_wx, lru_bx, lru_lambda, lru_out_g,
           w_out, norm2_g, router_w, router_b, w_gate_up, b_gate_up, w_down, b_down):
    T, D = x2.shape
    lam_init = 0.8 - 0.6 * math.exp(-0.3 * l)
    aw = D // 2
    lw = D - aw
    heads = aw // ATTN_DV
    qkw = heads * 2 * ATTN_DK

    mod, lam = _adaln(c, ada_w, ada_b, lambda_q1, lambda_k1, lambda_q2, lambda_k2, lam_init)
    mod3 = mod.reshape(c.shape[0], 6, D)

    reps = qkw // ATTN_DK
    gq = (jnp.tile(q_norm_g, reps) * (ATTN_DK ** -0.5 * math.log2(math.e))).reshape(1, qkw)
    gk = jnp.tile(k_norm_g, reps).reshape(1, qkw)
    tm = _tile(S, 512)
    q, k, v, xr, gr = _in_proj(x2, mod3, norm1_g.reshape(1, D), w_in.astype(BF16), gq, gk, S, _tile(S, 1024),
                               qkw, aw, lw)

    attn = _attention(q, k, v, lam, attn_subln_g.reshape(1, ATTN_DV), B, S, _tile(S, 1024), _tile(S, 512),
                      1.0 - lam_init)

    w_gates = (0.5 * jnp.concatenate([_block_diag(lru_wa), _block_diag(lru_wx)], axis=1)).astype(BF16)
    b_gates = 0.5 * jnp.concatenate([lru_ba, lru_bx]).reshape(1, 2 * lw)
    lru = _lru(xr, gr, conv_w, conv_b.reshape(1, lw), w_gates, b_gates, lru_lambda.reshape(1, lw),
               lru_out_g.reshape(1, lw), B, S, _tile(S, 1024))

    x1, h2t, meta, tile_cnt = _out_proj(
        attn, lru, x2, mod3, w_out.astype(BF16), norm2_g.reshape(1, D), router_w.T,
        router_b.reshape(N_EXPERTS, 1), S, tm)

    tile_cnt = tile_cnt[:, :, 0]
    counts = jnp.sum(tile_cnt, axis=0)
    padded = ((counts + MOE_BLOCK - 1) // MOE_BLOCK) * MOE_BLOCK
    pad_ends = jnp.cumsum(padded)
    pad_starts = pad_ends - padded
    n_blocks = (T * TOP_K) // MOE_BLOCK + N_EXPERTS
    blk_start = jnp.arange(n_blocks, dtype=I32) * MOE_BLOCK
    owner = blk_start[:, None] >= pad_ends[None, :]
    block_expert = jnp.minimum(jnp.sum(owner, axis=1), N_EXPERTS - 1).astype(I32)
    onehot = block_expert[:, None] == jnp.arange(N_EXPERTS, dtype=I32)[None, :]
    row_end = jnp.sum(jnp.where(onehot, (pad_starts + counts)[None, :], 0), axis=1)
    block_valid = jnp.clip(row_end - blk_start, 0, MOE_BLOCK).astype(I32)
    tile_off = jnp.cumsum(tile_cnt, axis=1) - tile_cnt
    tile_dst = pad_starts[None, :] + jnp.cumsum(tile_cnt, axis=0) - tile_cnt
    flat = lambda a: (a * SUBLANES).reshape(-1).astype(I32)

    meta = meta.reshape(-1)
    pad_info = jnp.concatenate([(pad_starts + counts) * SUBLANES, (padded - counts) * SUBLANES,
                                pad_ends[-1:] // MOE_BLOCK]).astype(I32)
    xs = _dispatch(flat(tile_cnt), flat(tile_off), flat(tile_dst), pad_info, meta, h2t, n_blocks, tm)
    ys = _experts(block_expert, block_valid, xs, w_gate_up, b_gate_up, w_down, b_down)
    return _combine(flat(tile_cnt), flat(tile_off), flat(tile_dst), meta, ys, x1, mod3, S, tm)


def kernel(x, c, ada_w, ada_b, norm1_g, w_in, q_norm_g, k_norm_g, lambda_q1, lambda_k1, lambda_q2, lambda_k2,
           attn_subln_g, conv_w, conv_b, lru_wa, lru_ba, lru_wx, lru_bx, lru_lambda, lru_out_g, w_out, norm2_g,
           router_w, router_b, w_gate_up, b_gate_up, w_down, b_down):
    B, S, D = x.shape
    params = (ada_w, ada_b, norm1_g, w_in, q_norm_g, k_norm_g, lambda_q1, lambda_k1, lambda_q2, lambda_k2,
              attn_subln_g, conv_w, conv_b, lru_wa, lru_ba, lru_wx, lru_bx, lru_lambda, lru_out_g, w_out,
              norm2_g, router_w, router_b, w_gate_up, b_gate_up, w_down, b_down)
    x2 = x.reshape(B * S, D)
    for l in range(ada_w.shape[0]):
        x2 = _layer(l, x2, B, S, c, *[p[l] for p in params])
    return x2.reshape(B, S, D)
```

```python
import functools
import math

import jax
import jax.numpy as jnp
from jax import lax
from jax.experimental import pallas as pl
from jax.experimental.pallas import tpu as pltpu

F32 = jnp.float32
BF16 = jnp.bfloat16
I32 = jnp.int32

ATTN_DK = 64
ATTN_DV = 2 * ATTN_DK
CONV_W = 4
LRU_C = 8.0
N_EXPERTS = 32
TOP_K = 4
SWIGLU_LIMIT = 7.0
SWIGLU_ALPHA = 1.702
MOE_BLOCK = 512
EPS = 1e-6
NEG_BIG = -1e30

LANES = 128
SUBLANES = 8
VMEM_LIMIT = 56 * 1024 * 1024


def _cparams(sem):
    return pltpu.CompilerParams(dimension_semantics=sem, vmem_limit_bytes=VMEM_LIMIT)


def _split_hi_lo(x):
    hi = x.astype(BF16)
    lo = (x - hi.astype(F32)).astype(BF16)
    return hi, lo


def _sigmoid(x):
    return 0.5 * jnp.tanh(0.5 * x) + 0.5


def _dot(a, b):
    return jnp.dot(a, b, preferred_element_type=F32)


def _dot_tb(a, b):
    return lax.dot_general(a, b, (((1,), (1,)), ((), ())), preferred_element_type=F32)


def _adaln_kernel(lam_init, c_ref, w_ref, b_ref, lq1, lk1, lq2, lk2, mod_ref, lam_ref):
    c = c_ref[...]
    s = c * jax.nn.sigmoid(c)
    s_hi, s_lo = _split_hi_lo(s)
    w = w_ref[...]
    w_hi, w_lo = _split_hi_lo(w)
    mod_ref[...] = _dot(s_hi, w_hi) + _dot(s_hi, w_lo) + _dot(s_lo, w_hi) + b_ref[...]
    d1 = jnp.sum(lq1[...] * lk1[...], axis=-1, keepdims=True)
    d2 = jnp.sum(lq2[...] * lk2[...], axis=-1, keepdims=True)
    lam = jnp.exp(d1) - jnp.exp(d2) + lam_init
    lam_ref[...] = jnp.broadcast_to(lam, lam_ref.shape)


def _adaln(c, ada_w, ada_b, lq1, lk1, lq2, lk2, lam_init):
    B, D = c.shape
    n = ada_w.shape[1] // D
    vec = lambda: pl.BlockSpec((1, ATTN_DK), lambda j: (0, 0))
    return pl.pallas_call(
        functools.partial(_adaln_kernel, lam_init),
        grid=(n,),
        in_specs=[
            pl.BlockSpec((B, D), lambda j: (0, 0)),
            pl.BlockSpec((D, D), lambda j: (0, j)),
            pl.BlockSpec((1, D), lambda j: (0, j)),
            vec(), vec(), vec(), vec(),
        ],
        out_specs=[
            pl.BlockSpec((B, D), lambda j: (0, j)),
            pl.BlockSpec((1, LANES), lambda j: (0, 0)),
        ],
        out_shape=[
            jax.ShapeDtypeStruct((B, n * D), F32),
            jax.ShapeDtypeStruct((1, LANES), F32),
        ],
        compiler_params=_cparams(("arbitrary",)),
        name="adaln",
    )(c, ada_w, ada_b.reshape(1, -1), lq1.reshape(1, -1), lk1.reshape(1, -1),
      lq2.reshape(1, -1), lk2.reshape(1, -1))


def _rms_modulate(x, g, shift, scale):
    ms = jnp.mean(x * x, axis=-1, keepdims=True)
    y = x * lax.rsqrt(ms + EPS) * g
    return y * (1.0 + scale) + shift


def _group_rms_scale(q, group_ones):
    ss = _dot((q * q).astype(BF16), group_ones)
    return lax.rsqrt(ss * (1.0 / ATTN_DK) + EPS)


def _in_proj_kernel(qkw, aw, lw, x_ref, mod_ref, g_ref, w_ref, gq_ref, gk_ref, ones_ref,
                    q_ref, k_ref, v_ref, xr_ref, gr_ref):
    h = _rms_modulate(x_ref[...], g_ref[...], mod_ref[0:1, :], mod_ref[1:2, :])
    hb = h.astype(BF16)
    group_ones = ones_ref[...]
    o = 0
    q = _dot(hb, w_ref[:, o:o + qkw]); o += qkw
    q_ref[...] = (q * _group_rms_scale(q, group_ones) * gq_ref[...]).astype(BF16)
    k = _dot(hb, w_ref[:, o:o + qkw]); o += qkw
    k_ref[...] = (k * _group_rms_scale(k, group_ones) * gk_ref[...]).astype(BF16)
    v_ref[...] = _dot(hb, w_ref[:, o:o + aw]).astype(BF16); o += aw
    xr_ref[...] = _dot(hb, w_ref[:, o:o + lw]); o += lw
    gr_ref[...] = _dot(hb, w_ref[:, o:o + lw])


def _in_proj(x2, mod3, norm_g, w_in_b, gq, gk, S, tm, qkw, aw, lw):
    T, D = x2.shape
    tps = S // tm
    grp = jnp.arange(qkw, dtype=I32) // ATTN_DK
    group_ones = (grp[:, None] == grp[None, :]).astype(BF16)
    const = lambda shape: pl.BlockSpec(shape, lambda i: (0,) * len(shape))
    row = lambda w: pl.BlockSpec((tm, w), lambda i: (i, 0))
    return pl.pallas_call(
        functools.partial(_in_proj_kernel, qkw, aw, lw),
        grid=(T // tm,),
        in_specs=[
            row(D),
            pl.BlockSpec((None, 6, D), lambda i: (i // tps, 0, 0)),
            const((1, D)),
            const(w_in_b.shape),
            const((1, qkw)), const((1, qkw)),
            const((qkw, qkw)),
        ],
        out_specs=[row(qkw), row(qkw), row(aw), row(lw), row(lw)],
        out_shape=[
            jax.ShapeDtypeStruct((T, qkw), BF16),
            jax.ShapeDtypeStruct((T, qkw), BF16),
            jax.ShapeDtypeStruct((T, aw), BF16),
            jax.ShapeDtypeStruct((T, lw), F32),
            jax.ShapeDtypeStruct((T, lw), F32),
        ],
        compiler_params=_cparams(("arbitrary",)),
        name="in_proj",
    )(x2, mod3, norm_g, w_in_b, gq, gk, group_ones)


ATTN_ROW_CHUNK = 128


def _attn_kernel(bq, bk, out_scale, q_ref, k_ref, v_ref, lam_ref, g_ref, o_ref,
                 qq_ref, vp_ref, s_ref, sd_ref, m_ref, acc_ref):
    i = pl.program_id(2)
    rc = ATTN_ROW_CHUNK
    n_chunks = 2 * bq // rc

    @pl.when(i == 0)
    def _():
        col = lax.broadcasted_iota(I32, (vp_ref.shape[0], ATTN_DV), 1)
        vp_ref[:, :ATTN_DV] = v_ref[...]
        vp_ref[:, ATTN_DV:] = jnp.where(col == 0, 1.0, 0.0).astype(BF16)

    q = q_ref[...]
    lane = lax.broadcasted_iota(I32, q.shape, 1)
    zero = jnp.zeros_like(q)
    qq_ref[0:bq, :] = jnp.where(lane < ATTN_DK, q, zero)
    qq_ref[bq:, :] = jnp.where(lane >= ATTN_DK, q, zero)
    m_ref[...] = jnp.full(m_ref.shape, NEG_BIG, F32)
    acc_ref[...] = jnp.zeros_like(acc_ref)

    def chunk(r):
        return pl.ds(r * rc, rc)

    def scores(r, start, kw):
        return _dot_tb(qq_ref[chunk(r), :], k_ref[pl.ds(start, kw), :])

    def softmax_pv(s, r, start):
        rows = chunk(r)
        kw = s.shape[1]
        m_old = m_ref[rows, :]
        m_new = jnp.maximum(m_old, jnp.max(s, axis=-1, keepdims=True))
        alpha = jnp.exp2(m_old - m_new)
        p = jnp.exp2(s - jnp.tile(m_new, (1, kw // LANES)))
        pv = _dot(p.astype(BF16), vp_ref[pl.ds(start, kw), :])
        acc_ref[rows, :] = acc_ref[rows, :] * jnp.tile(alpha, (1, 2)) + pv
        m_ref[rows, :] = m_new

    for r in range(n_chunks):
        s_ref[chunk(r), :] = scores(r, 0, bk)

    def body(j, carry):
        start = pl.multiple_of(j * bk, bk)
        for r in range(n_chunks):
            s = s_ref[chunk(r), :]
            s_ref[chunk(r), :] = scores(r, start + bk, bk)
            softmax_pv(s, r, start)
        return carry

    n_full = i * (bq // bk)
    lax.fori_loop(0, n_full, body, 0)

    start = pl.multiple_of(n_full * bk, bk)
    tri = (lax.broadcasted_iota(I32, (rc, rc), 1) <= lax.broadcasted_iota(I32, (rc, rc), 0))
    first_key = [(r * rc) % bq for r in range(n_chunks)]
    for r in range(n_chunks):
        if first_key[r] + rc > bk:
            sd_ref[chunk(r), 0:first_key[r] + rc - bk] = scores(r, start + bk, first_key[r] + rc - bk)
    for r in range(n_chunks):
        q0 = first_key[r]
        kw = q0 + rc
        parts = [s_ref[chunk(r), 0:min(q0, bk)]] if q0 else []
        if q0 > bk:
            parts.append(sd_ref[chunk(r), 0:q0 - bk])
        last = s_ref[chunk(r), q0:kw] if kw <= bk else sd_ref[chunk(r), q0 - bk:kw - bk]
        parts.append(jnp.where(tri, last, NEG_BIG))
        softmax_pv(parts[0] if len(parts) == 1 else jnp.concatenate(parts, axis=1), r, start)

    acc = acc_ref[...]
    o = acc[:, :ATTN_DV] / acc[:, ATTN_DV:ATTN_DV + 1]
    a = o[:bq] - lam_ref[0:1, 0:1] * o[bq:]
    ms = jnp.mean(a * a, axis=-1, keepdims=True)
    o_ref[...] = (a * lax.rsqrt(ms + EPS) * g_ref[...] * out_scale).astype(BF16)


def _attention(q, k, v, lam, subln_g, B, S, bq, bk, out_scale):
    T, qkw = q.shape
    H = qkw // ATTN_DV
    nq = S // bq
    return pl.pallas_call(
        functools.partial(_attn_kernel, bq, bk, out_scale),
        grid=(B, H, nq),
        in_specs=[
            pl.BlockSpec((bq, ATTN_DV), lambda b, h, i: (b * nq + i, h)),
            pl.BlockSpec((S, ATTN_DV), lambda b, h, i: (b, h)),
            pl.BlockSpec((S, ATTN_DV), lambda b, h, i: (b, h)),
            pl.BlockSpec((1, LANES), lambda b, h, i: (0, 0)),
            pl.BlockSpec((1, ATTN_DV), lambda b, h, i: (0, 0)),
        ],
        out_specs=pl.BlockSpec((bq, ATTN_DV), lambda b, h, i: (b * nq + i, h)),
        out_shape=jax.ShapeDtypeStruct((T, H * ATTN_DV), BF16),
        scratch_shapes=[
            pltpu.VMEM((2 * bq, ATTN_DV), BF16),
            pltpu.VMEM((S, 2 * ATTN_DV), BF16),
            pltpu.VMEM((2 * bq, bk), F32),
            pltpu.VMEM((2 * bq, max(bq - bk, LANES)), F32),
            pltpu.VMEM((2 * bq, LANES), F32),
            pltpu.VMEM((2 * bq, 2 * ATTN_DV), F32),
        ],
        compiler_params=_cparams(("arbitrary", "arbitrary", "arbitrary")),
        name="attention",
    )(q, k, v, lam, subln_g)


def _lru_kernel(tc, cw, pitch, x_ref, gate_ref, cw_ref, cb_ref, wg_ref, bg_ref, lam_ref, og_ref,
                o_ref, ext_ref, a_ref, b_ref, hc_ref):
    c = pl.program_id(1)

    @pl.when(c == 0)
    def _():
        ext_ref[0:SUBLANES, :] = jnp.zeros((SUBLANES, cw), F32)
        hc_ref[...] = jnp.zeros_like(hc_ref)

    ext_ref[SUBLANES:SUBLANES + tc, :] = x_ref[...]
    xc = cb_ref[...] + cw_ref[CONV_W - 1:CONV_W, :] * x_ref[...]
    for w in range(CONV_W - 1):
        sh = CONV_W - 1 - w
        xc = xc + cw_ref[w:w + 1, :] * ext_ref[SUBLANES - sh:SUBLANES - sh + tc, :]
    tail = ext_ref[tc:tc + SUBLANES, :]
    ext_ref[0:SUBLANES, :] = tail

    g = _dot(xc.astype(BF16), wg_ref[...]) + bg_ref[...]
    t_r = jnp.tanh(g[:, :cw])
    t_i = jnp.tanh(g[:, cw:])
    nl = -lam_ref[...]
    softplus = jnp.maximum(nl, 0.0) + jnp.log1p(jnp.exp(-jnp.abs(nl)))
    c1 = (-0.5 * LRU_C) * softplus
    log_a = c1 * t_r + c1
    th = jnp.tanh(log_a)
    p = -2.0 * th
    root = jnp.where(p > 0.0, p * lax.rsqrt(p * (1.0 - th)), 0.0)
    a = jnp.exp(log_a)
    b = root * ((t_i + 1.0) * (0.5 * xc))

    steps = tc // SUBLANES
    n_lt = cw // LANES
    for q in range(SUBLANES):
        for l in range(n_lt):
            a_ref[l, q * pitch:q * pitch + steps, :] = a[q * steps:(q + 1) * steps, l * LANES:(l + 1) * LANES]
            b_ref[l, q * pitch:q * pitch + steps, :] = b[q * steps:(q + 1) * steps, l * LANES:(l + 1) * LANES]

    def scan_step(t, carry):
        out = []
        for l in range(n_lt):
            h, prod = carry[l]
            rows = pl.ds(t, SUBLANES, stride=pitch)
            a_t = a_ref[l, rows, :]
            h = a_t * h + b_ref[l, rows, :]
            prod = a_t * prod
            b_ref[l, rows, :] = h
            a_ref[l, rows, :] = prod
            out.append((h, prod))
        return tuple(out)

    init = tuple((jnp.zeros((SUBLANES, LANES), F32), jnp.ones((SUBLANES, LANES), F32)) for _ in range(n_lt))
    last = lax.fori_loop(0, steps, scan_step, init, unroll=4)

    h_cols = []
    for l in range(n_lt):
        h_last, p_last = last[l]
        entering = [hc_ref[0:1, l * LANES:(l + 1) * LANES]]
        for q in range(SUBLANES):
            entering.append(h_last[q:q + 1, :] + p_last[q:q + 1, :] * entering[q])
        hc_ref[:, l * LANES:(l + 1) * LANES] = jnp.broadcast_to(entering[SUBLANES], (SUBLANES, LANES))
        h_cols.append(jnp.concatenate(
            [b_ref[l, q * pitch:q * pitch + steps, :] + a_ref[l, q * pitch:q * pitch + steps, :] * entering[q]
             for q in range(SUBLANES)], axis=0))
    h = jnp.concatenate(h_cols, axis=1)

    gt = gate_ref[...]
    gelu = 0.5 * gt * (1.0 + jnp.tanh(math.sqrt(2.0 / math.pi) * (gt + 0.044715 * gt * gt * gt)))
    y = h * gelu
    ms = jnp.mean(y * y, axis=-1, keepdims=True)
    o_ref[...] = (y * lax.rsqrt(ms + EPS) * og_ref[...]).astype(BF16)


def _scan_pitch(steps):
    return steps if (steps // SUBLANES) % 2 else steps + SUBLANES


def _lru(xr, gr, conv_w, conv_b, w_gates, b_gates, lru_lambda, out_g, B, S, tc):
    T, cw = xr.shape
    nc = S // tc
    pitch = _scan_pitch(tc // SUBLANES)
    const = lambda shape: pl.BlockSpec(shape, lambda b, c: (0,) * len(shape))
    row = pl.BlockSpec((tc, cw), lambda b, c: (b * nc + c, 0))
    return pl.pallas_call(
        functools.partial(_lru_kernel, tc, cw, pitch),
        grid=(B, nc),
        in_specs=[row, row, const((CONV_W, cw)), const((1, cw)), const((cw, 2 * cw)),
                  const((1, 2 * cw)), const((1, cw)), const((1, cw))],
        out_specs=row,
        out_shape=jax.ShapeDtypeStruct((T, cw), BF16),
        scratch_shapes=[
            pltpu.VMEM((tc + SUBLANES, cw), F32),
            pltpu.VMEM((cw // LANES, SUBLANES * pitch, LANES), F32),
            pltpu.VMEM((cw // LANES, SUBLANES * pitch, LANES), F32),
            pltpu.VMEM((SUBLANES, cw), F32),
        ],
        compiler_params=_cparams(("arbitrary", "arbitrary")),
        name="lru",
    )(xr, gr, conv_w, conv_b, w_gates, b_gates, lru_lambda, out_g)


def _out_proj_kernel(tm, aw, attn_ref, lru_ref, x_ref, mod_ref, w_ref, g_ref, rw_ref, rb_ref, tri_ref, low_ref,
                     x1_ref, h2_ref, meta_ref, cnt_ref):
    mix = _dot(attn_ref[...], w_ref[0:aw, :]) + _dot(lru_ref[...], w_ref[aw:, :])
    x1 = x_ref[...] + mod_ref[2:3, :] * mix
    x1_ref[...] = x1
    h2 = _rms_modulate(x1, g_ref[...], mod_ref[3:4, :], mod_ref[4:5, :])
    for j in range(h2.shape[1] // LANES):
        h2_ref[pl.ds(j, tm, stride=SUBLANES), :] = h2[:, j * LANES:(j + 1) * LANES]

    h_hi, h_lo = _split_hi_lo(h2)
    w_hi, w_lo = _split_hi_lo(rw_ref[...])
    by_hi = _dot_tb(jnp.concatenate([w_hi, w_lo], axis=0), h_hi)
    lg = by_hi[:N_EXPERTS] + by_hi[N_EXPERTS:] + _dot_tb(w_hi, h_lo) + rb_ref[...]

    eidx = lax.broadcasted_iota(I32, lg.shape, 0)
    picked = jnp.zeros(lg.shape, F32)
    vals, idxs = [], []
    for _ in range(TOP_K):
        m = jnp.max(lg, axis=0, keepdims=True)
        idx = jnp.min(jnp.where(lg == m, eidx, N_EXPERTS), axis=0, keepdims=True)
        sel = eidx == idx
        lg = jnp.where(sel, -jnp.inf, lg)
        picked = picked + sel.astype(F32)
        vals.append(m)
        idxs.append(idx)
    es = [jnp.exp(v - vals[0]) for v in vals]
    den = es[0] + es[1] + es[2] + es[3]
    gates = jnp.concatenate([e / den for e in es], axis=0)

    cnt = jnp.sum(picked, axis=1, keepdims=True)
    cnt_b = jnp.broadcast_to(cnt, cnt_ref.shape)
    cnt_hi = jnp.floor(cnt_b * (1.0 / 256.0))
    cnt_lo = cnt_b - 256.0 * cnt_hi
    run_start = 256.0 * _dot(low_ref[...], cnt_hi.astype(BF16)) + _dot(low_ref[...], cnt_lo.astype(BF16))
    before = _dot(picked.astype(BF16), tri_ref[...]) + run_start[:, 0:1]
    pos = [jnp.sum(jnp.where(eidx == idx, before, 0.0), axis=0, keepdims=True) for idx in idxs]
    meta_ref[0:TOP_K, :] = jnp.concatenate(pos, axis=0).astype(I32) * SUBLANES
    meta_ref[TOP_K:, :] = lax.bitcast_convert_type(gates, I32)
    cnt_ref[...] = cnt_b.astype(I32)


def _out_proj(attn, lru, x2, mod3, w_out_b, norm_g, router_wt, router_b, S, tm):
    T, D = x2.shape
    aw = attn.shape[1]
    lw = lru.shape[1]
    tps = S // tm
    tri = (jnp.arange(tm, dtype=I32)[:, None] < jnp.arange(tm, dtype=I32)[None, :]).astype(BF16)
    experts = jnp.arange(N_EXPERTS, dtype=I32)
    low = (experts[None, :] < experts[:, None]).astype(BF16)
    const = lambda shape: pl.BlockSpec(shape, lambda i: (0,) * len(shape))
    row = lambda w: pl.BlockSpec((tm, w), lambda i: (i, 0))
    return pl.pallas_call(
        functools.partial(_out_proj_kernel, tm, aw),
        grid=(T // tm,),
        in_specs=[
            row(aw), row(lw), row(D),
            pl.BlockSpec((None, 6, D), lambda i: (i // tps, 0, 0)),
            const((D, D)), const((1, D)), const((N_EXPERTS, D)), const((N_EXPERTS, 1)),
            const((tm, tm)), const((N_EXPERTS, N_EXPERTS)),
        ],
        out_specs=[
            row(D),
            pl.BlockSpec((tm * SUBLANES, LANES), lambda i: (i, 0)),
            pl.BlockSpec((None, 2 * TOP_K, tm), lambda i: (i, 0, 0)),
            pl.BlockSpec((None, N_EXPERTS, LANES), lambda i: (i, 0, 0)),
        ],
        out_shape=[
            jax.ShapeDtypeStruct((T, D), F32),
            jax.ShapeDtypeStruct((T * SUBLANES, LANES), F32),
            jax.ShapeDtypeStruct((T // tm, 2 * TOP_K, tm), I32),
            jax.ShapeDtypeStruct((T // tm, N_EXPERTS, LANES), I32),
        ],
        compiler_params=_cparams(("arbitrary",)),
        name="out_proj",
    )(attn, lru, x2, mod3, w_out_b, norm_g, router_wt, router_b, tri, low)


def _rows(ref, first_row, n_rows):
    return ref.at[pl.ds(pl.multiple_of(first_row * SUBLANES, SUBLANES), n_rows * SUBLANES), :]


def _row_at(ref, sublane_offset):
    return ref.at[pl.ds(pl.multiple_of(sublane_offset, SUBLANES), SUBLANES), :]


def _fetch_tile(meta_hbm, smems, sems, t):
    i = pl.program_id(0)
    n = pl.num_programs(0)
    slot = i % 2
    per_tile = 2 * TOP_K * t

    def copies(tile, s):
        return [pltpu.make_async_copy(meta_hbm.at[pl.ds(pl.multiple_of(tile * per_tile + j * t, t), t)],
                                      smem.at[pl.ds(pl.multiple_of(s * t, t), t)], sems.at[s])
                for j, smem in enumerate(smems)]

    @pl.when(i == 0)
    def _():
        for c in copies(0, 0):
            c.start()

    for c in copies(i, slot):
        c.wait()

    @pl.when(i + 1 < n)
    def _():
        for c in copies(i + 1, 1 - slot):
            c.start()

    return slot * t


def _run_copies(t, tile, cnt_ref, off_ref, dst_ref, make_copy):
    def expert(e, carry):
        n = cnt_ref[tile * N_EXPERTS + e]
        off = off_ref[tile * N_EXPERTS + e]
        dst = dst_ref[tile * N_EXPERTS + e]
        for b in reversed(range(t.bit_length())):
            size = SUBLANES << b
            done = n & ~(2 * size - 1)

            @pl.when((n & size) != 0)
            def _():
                make_copy(off + done, dst + done, size).start()
        return carry

    lax.fori_loop(0, N_EXPERTS, expert, 0)


def _span(ref, first_sublane, n_sublanes):
    return ref.at[pl.ds(pl.multiple_of(first_sublane, SUBLANES), n_sublanes), :]


def _dispatch_kernel(t, n_blocks, cnt_ref, off_ref, dst_ref, pad_ref, meta_hbm, h_ref, xs_hbm,
                     pos0, pos1, pos2, pos3, stage_ref, zero_ref, psem, sem, zsem):
    i = pl.program_id(0)
    n = pl.num_programs(0)
    slot = i % 2

    def zero_fill(go):
        def expert(e, carry):
            first = pad_ref[e]
            length = pad_ref[N_EXPERTS + e]
            for b in reversed(range((MOE_BLOCK - 1).bit_length())):
                size = SUBLANES << b
                done = length & ~(2 * size - 1)

                @pl.when((length & size) != 0)
                def _():
                    go(pltpu.make_async_copy(_span(zero_ref, 0, size), _span(xs_hbm, first + done, size), zsem))
            return carry

        lax.fori_loop(0, N_EXPERTS, expert, 0)

        def block(b, carry):
            go(pltpu.make_async_copy(zero_ref, _rows(xs_hbm, b * MOE_BLOCK, MOE_BLOCK), zsem))
            return carry

        lax.fori_loop(pad_ref[2 * N_EXPERTS], n_blocks, block, 0)

    @pl.when(i == 0)
    def _():
        zero_ref[...] = jnp.zeros_like(zero_ref)
        zero_fill(lambda c: c.start())

    pos = (pos0, pos1, pos2, pos3)
    base = _fetch_tile(meta_hbm, pos, psem, t)
    stage = stage_ref.at[slot]

    def place(r, carry):
        row = _rows(h_ref, r, 1)[...]
        for k in range(TOP_K):
            _row_at(stage, pos[k][base + r])[...] = row
        return carry

    lax.fori_loop(0, t, place, 0, unroll=8)

    def all_runs(s):
        return pltpu.make_async_copy(stage_ref.at[s], _rows(xs_hbm, 0, TOP_K * t), sem.at[s])

    _run_copies(t, i, cnt_ref, off_ref, dst_ref,
                lambda a, b, m: pltpu.make_async_copy(_span(stage, a, m), _span(xs_hbm, b, m), sem.at[slot]))

    @pl.when(i > 0)
    def _():
        all_runs(1 - slot).wait()

    @pl.when(i == n - 1)
    def _():
        all_runs(slot).wait()
        zero_fill(lambda c: c.wait())


def _dispatch(tile_cnt, tile_off, tile_dst, pad_info, meta, h2t, n_blocks, t):
    n_tiles = meta.shape[0] // (2 * TOP_K * t)
    n_slots = n_blocks * MOE_BLOCK
    grid_spec = pltpu.PrefetchScalarGridSpec(
        num_scalar_prefetch=4,
        grid=(n_tiles,),
        in_specs=[pl.BlockSpec(memory_space=pl.ANY),
                  pl.BlockSpec((t * SUBLANES, LANES), lambda i, *_: (i, 0))],
        out_specs=pl.BlockSpec(memory_space=pl.ANY),
        scratch_shapes=[pltpu.SMEM((2 * t,), I32)] * TOP_K + [
            pltpu.VMEM((2, TOP_K * t * SUBLANES, LANES), F32),
            pltpu.VMEM((MOE_BLOCK * SUBLANES, LANES), F32),
            pltpu.SemaphoreType.DMA((2,)),
            pltpu.SemaphoreType.DMA((2,)),
            pltpu.SemaphoreType.DMA,
        ],
    )
    return pl.pallas_call(
        functools.partial(_dispatch_kernel, t, n_blocks),
        grid_spec=grid_spec,
        out_shape=jax.ShapeDtypeStruct((n_slots * SUBLANES, LANES), F32),
        compiler_params=_cparams(("arbitrary",)),
        name="dispatch",
    )(tile_cnt, tile_off, tile_dst, pad_info, meta, h2t)


def _experts_kernel(ff, be_ref, nv_ref, slot_ref, nxt_ref, xs_ref, wgu_hbm, bgu_ref, wdn_hbm, bdn_ref, ys_ref,
                    wgu_f, wdn_f, wgu_b, wdn_b, sem):
    i = pl.program_id(0)
    nv = nv_ref[i]
    n_slabs = xs_ref.shape[0] // MOE_BLOCK

    def fetch(e, s):
        return (pltpu.make_async_copy(wgu_hbm.at[e], wgu_f.at[s], sem.at[0, s]),
                pltpu.make_async_copy(wdn_hbm.at[e], wdn_f.at[s], sem.at[1, s]))

    @pl.when(i == 0)
    def _():
        for c in fetch(be_ref[0], 0):
            c.start()

    @pl.when((i == 0) | (be_ref[i] != be_ref[jnp.maximum(i - 1, 0)]))
    def _():
        s = slot_ref[i]
        for c in fetch(be_ref[i], s):
            c.wait()
        wgu_b[...] = wgu_f[s].astype(BF16)
        wdn_b[...] = wdn_f[s].astype(BF16)

        @pl.when(nxt_ref[i] >= 0)
        def _():
            for c in fetch(nxt_ref[i], 1 - s):
                c.start()

    @pl.when(nv > 0)
    def _():
        x = jnp.concatenate(
            [xs_ref[pl.ds(j, MOE_BLOCK, stride=SUBLANES), :] for j in range(n_slabs)], axis=1)
        gu = _dot(x.astype(BF16), wgu_b[...]) + bgu_ref[...]
        gate = jnp.minimum(gu[:, :ff], SWIGLU_LIMIT)
        up = jnp.clip(gu[:, ff:], -SWIGLU_LIMIT, SWIGLU_LIMIT)
        act = (up + 1.0) * (gate * _sigmoid(SWIGLU_ALPHA * gate))
        y = _dot(act.astype(BF16), wdn_b[...]) + bdn_ref[...]
        for j in range(n_slabs):
            ys_ref[pl.ds(j, MOE_BLOCK, stride=SUBLANES), :] = y[:, j * LANES:(j + 1) * LANES]

    @pl.when(nv == 0)
    def _():
        ys_ref[...] = jnp.zeros_like(ys_ref)


def _experts(block_expert, block_valid, xs, w_gu, b_gu, w_dn, b_dn):
    n_blocks = block_expert.shape[0]
    E, D, ff2 = w_gu.shape
    ff = ff2 // 2
    change = jnp.concatenate([jnp.zeros((1,), I32), (block_expert[1:] != block_expert[:-1]).astype(I32)])
    slot = jnp.cumsum(change) % 2
    later = block_expert[None, :] > block_expert[:, None]
    nxt = jnp.min(jnp.where(later, block_expert[None, :], E), axis=1)
    nxt = jnp.where(nxt < E, nxt, -1)
    rows = pl.BlockSpec((MOE_BLOCK * SUBLANES, LANES), lambda i, *_: (i, 0))
    grid_spec = pltpu.PrefetchScalarGridSpec(
        num_scalar_prefetch=4,
        grid=(n_blocks,),
        in_specs=[
            rows,
            pl.BlockSpec(memory_space=pl.ANY),
            pl.BlockSpec((None, 1, ff2), lambda i, be, *_: (be[i], 0, 0)),
            pl.BlockSpec(memory_space=pl.ANY),
            pl.BlockSpec((None, 1, D), lambda i, be, *_: (be[i], 0, 0)),
        ],
        out_specs=rows,
        scratch_shapes=[
            pltpu.VMEM((2, D, ff2), F32), pltpu.VMEM((2, ff, D), F32),
            pltpu.VMEM((D, ff2), BF16), pltpu.VMEM((ff, D), BF16),
            pltpu.SemaphoreType.DMA((2, 2)),
        ],
    )
    return pl.pallas_call(
        functools.partial(_experts_kernel, ff),
        grid_spec=grid_spec,
        out_shape=jax.ShapeDtypeStruct(xs.shape, F32),
        compiler_params=_cparams(("arbitrary",)),
        name="experts",
    )(block_expert, block_valid, slot.astype(I32), nxt.astype(I32), xs, w_gu, b_gu.reshape(E, 1, ff2), w_dn,
      b_dn.reshape(E, 1, D))


def _combine_kernel(t, cnt_ref, off_ref, dst_ref, meta_hbm, ys_hbm, x1_ref, mod_ref, o_ref,
                    pos0, pos1, pos2, pos3, gate0, gate1, gate2, gate3, stage_ref, y_ref, psem, sem):
    i = pl.program_id(0)
    n = pl.num_programs(0)
    slot = i % 2
    pos = (pos0, pos1, pos2, pos3)
    gates = (gate0, gate1, gate2, gate3)
    base = _fetch_tile(meta_hbm, pos + gates, psem, t)

    def fetch_runs(tile, s):
        stage = stage_ref.at[s]
        _run_copies(t, tile, cnt_ref, off_ref, dst_ref,
                    lambda a, b, m: pltpu.make_async_copy(_span(ys_hbm, b, m), _span(stage, a, m), sem.at[s]))

    @pl.when(i == 0)
    def _():
        fetch_runs(0, 0)

    @pl.when(i + 1 < n)
    def _():
        fetch_runs(i + 1, 1 - slot)

    pltpu.make_async_copy(_rows(ys_hbm, 0, TOP_K * t), stage_ref.at[slot], sem.at[slot]).wait()
    stage = stage_ref.at[slot]

    def gather(r, carry):
        acc = None
        for k in range(TOP_K):
            gate = lax.bitcast_convert_type(gates[k][base + r], F32)
            term = gate * _row_at(stage, pos[k][base + r])[...]
            acc = term if acc is None else acc + term
        _rows(y_ref, r, 1)[...] = acc
        return carry

    lax.fori_loop(0, t, gather, 0, unroll=8)

    n_slabs = o_ref.shape[1] // LANES
    y = jnp.concatenate([y_ref[pl.ds(j, t, stride=SUBLANES), :] for j in range(n_slabs)], axis=1)
    o_ref[...] = x1_ref[...] + mod_ref[5:6, :] * y


def _combine(tile_cnt, tile_off, tile_dst, meta, ys, x1, mod3, S, t):
    T, D = x1.shape
    n_tiles = T // t
    tps = S // t
    grid_spec = pltpu.PrefetchScalarGridSpec(
        num_scalar_prefetch=3,
        grid=(n_tiles,),
        in_specs=[
            pl.BlockSpec(memory_space=pl.ANY),
            pl.BlockSpec(memory_space=pl.ANY),
            pl.BlockSpec((t, D), lambda i, *_: (i, 0)),
            pl.BlockSpec((None, 6, D), lambda i, *_: (i // tps, 0, 0)),
        ],
        out_specs=pl.BlockSpec((t, D), lambda i, *_: (i, 0)),
        scratch_shapes=[pltpu.SMEM((2 * t,), I32)] * (2 * TOP_K) + [
            pltpu.VMEM((2, TOP_K * t * SUBLANES, LANES), F32),
            pltpu.VMEM((t * SUBLANES, LANES), F32),
            pltpu.SemaphoreType.DMA((2,)),
            pltpu.SemaphoreType.DMA((2,)),
        ],
    )
    return pl.pallas_call(
        functools.partial(_combine_kernel, t),
        grid_spec=grid_spec,
        out_shape=jax.ShapeDtypeStruct((T, D), F32),
        compiler_params=_cparams(("arbitrary",)),
        name="combine",
    )(tile_cnt, tile_off, tile_dst, meta, ys, x1, mod3)


def _tile(n, target):
    t = min(n, target)
    while n % t:
        t //= 2
    return t


def _block_diag(w):
    n, bw, _ = w.shape
    eye = jnp.eye(n, dtype=w.dtype)
    return (eye[:, None, :, None] * w[:, :, None, :]).reshape(n * bw, n * bw)


def _layer(l, x2, B, S, c, ada_w, ada_b, norm1_g, w_in, q_norm_g, k_norm_g, lambda_q1, lambda_k1, lambda_q2,
           lambda_k2, attn_subln_g, conv_w, conv_b, lru_wa, lru_ba, lru_wx, lru_bx, lru_lambda, lru_out_g,
           w_out, norm2_g, router_w, router_b, w_gate_up, b_gate_up, w_down, b_down):
    T, D = x2.shape
    lam_init = 0.8 - 0.6 * math.exp(-0.3 * l)
    aw = D // 2
    lw = D - aw
    heads = aw // ATTN_DV
    qkw = heads * 2 * ATTN_DK

    mod, lam = _adaln(c, ada_w, ada_b, lambda_q1, lambda_k1, lambda_q2, lambda_k2, lam_init)
    mod3 = mod.reshape(c.shape[0], 6, D)

    reps = qkw // ATTN_DK
    gq = (jnp.tile(q_norm_g, reps) * (ATTN_DK ** -0.5 * math.log2(math.e))).reshape(1, qkw)
    gk = jnp.tile(k_norm_g, reps).reshape(1, qkw)
    tm = _tile(S, 512)
    q, k, v, xr, gr = _in_proj(x2, mod3, norm1_g.reshape(1, D), w_in.astype(BF16), gq, gk, S, _tile(S, 1024),
                               qkw, aw, lw)

    attn = _attention(q, k, v, lam, attn_subln_g.reshape(1, ATTN_DV), B, S, _tile(S, 1024), _tile(S, 512),
                      1.0 - lam_init)

    w_gates = (0.5 * jnp.concatenate([_block_diag(lru_wa), _block_diag(lru_wx)], axis=1)).astype(BF16)
    b_gates = 0.5 * jnp.concatenate([lru_ba, lru_bx]).reshape(1, 2 * lw)
    lru = _lru(xr, gr, conv_w, conv_b.reshape(1, lw), w_gates, b_gates, lru_lambda.reshape(1, lw),
               lru_out_g.reshape(1, lw), B, S, _tile(S, 1024))

    x1, h2t, meta, tile_cnt = _out_proj(
        attn, lru, x2, mod3, w_out.astype(BF16), norm2_g.reshape(1, D), router_w.T,
        router_b.reshape(N_EXPERTS, 1), S, tm)

    tile_cnt = tile_cnt[:, :, 0]
    counts = jnp.sum(tile_cnt, axis=0)
    padded = ((counts + MOE_BLOCK - 1) // MOE_BLOCK) * MOE_BLOCK
    pad_ends = jnp.cumsum(padded)
    pad_starts = pad_ends - padded
    n_blocks = (T * TOP_K) // MOE_BLOCK + N_EXPERTS
    blk_start = jnp.arange(n_blocks, dtype=I32) * MOE_BLOCK
    owner = blk_start[:, None] >= pad_ends[None, :]
    block_expert = jnp.minimum(jnp.sum(owner, axis=1), N_EXPERTS - 1).astype(I32)
    onehot = block_expert[:, None] == jnp.arange(N_EXPERTS, dtype=I32)[None, :]
    row_end = jnp.sum(jnp.where(onehot, (pad_starts + counts)[None, :], 0), axis=1)
    block_valid = jnp.clip(row_end - blk_start, 0, MOE_BLOCK).astype(I32)
    tile_off = jnp.cumsum(tile_cnt, axis=1) - tile_cnt
    tile_dst = pad_starts[None, :] + jnp.cumsum(tile_cnt, axis=0) - tile_cnt
    flat = lambda a: (a * SUBLANES).reshape(-1).astype(I32)

    meta = meta.reshape(-1)
    pad_info = jnp.concatenate([(pad_starts + counts) * SUBLANES, (padded - counts) * SUBLANES,
                                pad_ends[-1:] // MOE_BLOCK]).astype(I32)
    xs = _dispatch(flat(tile_cnt), flat(tile_off), flat(tile_dst), pad_info, meta, h2t, n_blocks, tm)
    ys = _experts(block_expert, block_valid, xs, w_gate_up, b_gate_up, w_down, b_down)
    return _combine(flat(tile_cnt), flat(tile_off), flat(tile_dst), meta, ys, x1, mod3, S, tm)


def kernel(x, c, ada_w, ada_b, norm1_g, w_in, q_norm_g, k_norm_g, lambda_q1, lambda_k1, lambda_q2, lambda_k2,
           attn_subln_g, conv_w, conv_b, lru_wa, lru_ba, lru_wx, lru_bx, lru_lambda, lru_out_g, w_out, norm2_g,
           router_w, router_b, w_gate_up, b_gate_up, w_down, b_down):
    B, S, D = x.shape
    params = (ada_w, ada_b, norm1_g, w_in, q_norm_g, k_norm_g, lambda_q1, lambda_k1, lambda_q2, lambda_k2,
              attn_subln_g, conv_w, conv_b, lru_wa, lru_ba, lru_wx, lru_bx, lru_lambda, lru_out_g, w_out,
              norm2_g, router_w, router_b, w_gate_up, b_gate_up, w_down, b_down)
    x2 = x.reshape(B * S, D)
    for l in range(ada_w.shape[0]):
        x2 = _layer(l, x2, B, S, c, *[p[l] for p in params])
    return x2.reshape(B, S, D)
```

```python
import functools
import math

import jax
import jax.numpy as jnp
from jax import lax
from jax.experimental import pallas as pl
from jax.experimental.pallas import tpu as pltpu

F32 = jnp.float32
BF16 = jnp.bfloat16
I32 = jnp.int32

ATTN_DK = 64
ATTN_DV = 2 * ATTN_DK
CONV_W = 4
LRU_C = 8.0
N_EXPERTS = 32
TOP_K = 4
SWIGLU_LIMIT = 7.0
SWIGLU_ALPHA = 1.702
MOE_BLOCK = 512
EPS = 1e-6
NEG_BIG = -1e30

IN_PROJ_ROWS = 1024
ATTN_Q_ROWS = 1024
ATTN_KV_ROWS = 512
LRU_CHUNK_ROWS = 1024
ROUTE_TILE_ROWS = 512
BF16_EXACT_INT = 256.0

LANES = 128
SUBLANES = 8
VMEM_LIMIT = 56 * 1024 * 1024


def _cparams(sem):
    return pltpu.CompilerParams(dimension_semantics=sem, vmem_limit_bytes=VMEM_LIMIT)


def _split_hi_lo(x):
    hi = x.astype(BF16)
    lo = (x - hi.astype(F32)).astype(BF16)
    return hi, lo


def _sigmoid(x):
    return 0.5 * jnp.tanh(0.5 * x) + 0.5


def _dot(a, b):
    return jnp.dot(a, b, preferred_element_type=F32)


def _dot_tb(a, b):
    return lax.dot_general(a, b, (((1,), (1,)), ((), ())), preferred_element_type=F32)


def _adaln_kernel(lam_init, c_ref, w_ref, b_ref, lq1, lk1, lq2, lk2, mod_ref, lam_ref):
    c = c_ref[...]
    s = c * jax.nn.sigmoid(c)
    s_hi, s_lo = _split_hi_lo(s)
    w = w_ref[...]
    w_hi, w_lo = _split_hi_lo(w)
    mod_ref[...] = _dot(s_hi, w_hi) + _dot(s_hi, w_lo) + _dot(s_lo, w_hi) + b_ref[...]
    d1 = jnp.sum(lq1[...] * lk1[...], axis=-1, keepdims=True)
    d2 = jnp.sum(lq2[...] * lk2[...], axis=-1, keepdims=True)
    lam = jnp.exp(d1) - jnp.exp(d2) + lam_init
    lam_ref[...] = jnp.broadcast_to(lam, lam_ref.shape)


def _adaln(c, ada_w, ada_b, lq1, lk1, lq2, lk2, lam_init):
    B, D = c.shape
    n = ada_w.shape[1] // D
    vec = lambda: pl.BlockSpec((1, ATTN_DK), lambda j: (0, 0))
    return pl.pallas_call(
        functools.partial(_adaln_kernel, lam_init),
        grid=(n,),
        in_specs=[
            pl.BlockSpec((B, D), lambda j: (0, 0)),
            pl.BlockSpec((D, D), lambda j: (0, j)),
            pl.BlockSpec((1, D), lambda j: (0, j)),
            vec(), vec(), vec(), vec(),
        ],
        out_specs=[
            pl.BlockSpec((B, D), lambda j: (0, j)),
            pl.BlockSpec((1, LANES), lambda j: (0, 0)),
        ],
        out_shape=[
            jax.ShapeDtypeStruct((B, n * D), F32),
            jax.ShapeDtypeStruct((1, LANES), F32),
        ],
        compiler_params=_cparams(("arbitrary",)),
        name="adaln",
    )(c, ada_w, ada_b.reshape(1, -1), lq1.reshape(1, -1), lk1.reshape(1, -1),
      lq2.reshape(1, -1), lk2.reshape(1, -1))


def _rms_modulate(x, g, shift, scale):
    ms = jnp.mean(x * x, axis=-1, keepdims=True)
    y = x * lax.rsqrt(ms + EPS) * g
    return y * (1.0 + scale) + shift


def _group_rms_scale(q, group_ones):
    ss = _dot((q * q).astype(BF16), group_ones)
    return lax.rsqrt(ss * (1.0 / ATTN_DK) + EPS)


def _in_proj_kernel(qkw, aw, lw, x_ref, mod_ref, g_ref, w_ref, gq_ref, gk_ref, ones_ref,
                    q_ref, k_ref, v_ref, xr_ref, gr_ref):
    h = _rms_modulate(x_ref[...], g_ref[...], mod_ref[0:1, :], mod_ref[1:2, :])
    hb = h.astype(BF16)
    group_ones = ones_ref[...]
    o = 0
    q = _dot(hb, w_ref[:, o:o + qkw]); o += qkw
    q_ref[...] = (q * _group_rms_scale(q, group_ones) * gq_ref[...]).astype(BF16)
    k = _dot(hb, w_ref[:, o:o + qkw]); o += qkw
    k_ref[...] = (k * _group_rms_scale(k, group_ones) * gk_ref[...]).astype(BF16)
    v_ref[...] = _dot(hb, w_ref[:, o:o + aw]).astype(BF16); o += aw
    xr_ref[...] = _dot(hb, w_ref[:, o:o + lw]); o += lw
    gr_ref[...] = _dot(hb, w_ref[:, o:o + lw])


def _in_proj(x2, mod3, norm_g, w_in_b, gq, gk, S, tm, qkw, aw, lw):
    T, D = x2.shape
    tps = S // tm
    grp = jnp.arange(qkw, dtype=I32) // ATTN_DK
    group_ones = (grp[:, None] == grp[None, :]).astype(BF16)
    const = lambda shape: pl.BlockSpec(shape, lambda i: (0,) * len(shape))
    row = lambda w: pl.BlockSpec((tm, w), lambda i: (i, 0))
    return pl.pallas_call(
        functools.partial(_in_proj_kernel, qkw, aw, lw),
        grid=(T // tm,),
        in_specs=[
            row(D),
            pl.BlockSpec((None, 6, D), lambda i: (i // tps, 0, 0)),
            const((1, D)),
            const(w_in_b.shape),
            const((1, qkw)), const((1, qkw)),
            const((qkw, qkw)),
        ],
        out_specs=[row(qkw), row(qkw), row(aw), row(lw), row(lw)],
        out_shape=[
            jax.ShapeDtypeStruct((T, qkw), BF16),
            jax.ShapeDtypeStruct((T, qkw), BF16),
            jax.ShapeDtypeStruct((T, aw), BF16),
            jax.ShapeDtypeStruct((T, lw), F32),
            jax.ShapeDtypeStruct((T, lw), F32),
        ],
        compiler_params=_cparams(("arbitrary",)),
        name="in_proj",
    )(x2, mod3, norm_g, w_in_b, gq, gk, group_ones)


ATTN_ROW_CHUNK = 128


def _attn_kernel(bq, bk, out_scale, q_ref, k_ref, v_ref, lam_ref, g_ref, o_ref,
                 qq_ref, vp_ref, s_ref, sd_ref, m_ref, acc_ref):
    i = pl.program_id(2)
    rc = ATTN_ROW_CHUNK
    n_chunks = 2 * bq // rc

    @pl.when(i == 0)
    def _():
        col = lax.broadcasted_iota(I32, (vp_ref.shape[0], ATTN_DV), 1)
        vp_ref[:, :ATTN_DV] = v_ref[...]
        vp_ref[:, ATTN_DV:] = jnp.where(col == 0, 1.0, 0.0).astype(BF16)

    q = q_ref[...]
    lane = lax.broadcasted_iota(I32, q.shape, 1)
    zero = jnp.zeros_like(q)
    qq_ref[0:bq, :] = jnp.where(lane < ATTN_DK, q, zero)
    qq_ref[bq:, :] = jnp.where(lane >= ATTN_DK, q, zero)
    m_ref[...] = jnp.full(m_ref.shape, NEG_BIG, F32)
    acc_ref[...] = jnp.zeros_like(acc_ref)

    def chunk(r):
        return pl.ds(r * rc, rc)

    def scores(r, start, kw):
        return _dot_tb(qq_ref[chunk(r), :], k_ref[pl.ds(start, kw), :])

    def softmax_pv(s, r, start):
        rows = chunk(r)
        kw = s.shape[1]
        m_old = m_ref[rows, :]
        m_new = jnp.maximum(m_old, jnp.max(s, axis=-1, keepdims=True))
        alpha = jnp.exp2(m_old - m_new)
        p = jnp.exp2(s - jnp.tile(m_new, (1, kw // LANES)))
        pv = _dot(p.astype(BF16), vp_ref[pl.ds(start, kw), :])
        acc_ref[rows, :] = acc_ref[rows, :] * jnp.tile(alpha, (1, 2)) + pv
        m_ref[rows, :] = m_new

    for r in range(n_chunks):
        s_ref[chunk(r), :] = scores(r, 0, bk)

    def body(j, carry):
        start = pl.multiple_of(j * bk, bk)
        for r in range(n_chunks):
            s = s_ref[chunk(r), :]
            s_ref[chunk(r), :] = scores(r, start + bk, bk)
            softmax_pv(s, r, start)
        return carry

    n_full = i * (bq // bk)
    lax.fori_loop(0, n_full, body, 0)

    start = pl.multiple_of(n_full * bk, bk)
    tri = (lax.broadcasted_iota(I32, (rc, rc), 1) <= lax.broadcasted_iota(I32, (rc, rc), 0))
    first_key = [(r * rc) % bq for r in range(n_chunks)]
    for r in range(n_chunks):
        if first_key[r] + rc > bk:
            sd_ref[chunk(r), 0:first_key[r] + rc - bk] = scores(r, start + bk, first_key[r] + rc - bk)
    for r in range(n_chunks):
        q0 = first_key[r]
        kw = q0 + rc
        parts = [s_ref[chunk(r), 0:min(q0, bk)]] if q0 else []
        if q0 > bk:
            parts.append(sd_ref[chunk(r), 0:q0 - bk])
        last = s_ref[chunk(r), q0:kw] if kw <= bk else sd_ref[chunk(r), q0 - bk:kw - bk]
        parts.append(jnp.where(tri, last, NEG_BIG))
        softmax_pv(parts[0] if len(parts) == 1 else jnp.concatenate(parts, axis=1), r, start)

    acc = acc_ref[...]
    o = acc[:, :ATTN_DV] / acc[:, ATTN_DV:ATTN_DV + 1]
    a = o[:bq] - lam_ref[0:1, 0:1] * o[bq:]
    ms = jnp.mean(a * a, axis=-1, keepdims=True)
    o_ref[...] = (a * lax.rsqrt(ms + EPS) * g_ref[...] * out_scale).astype(BF16)


def _attention(q, k, v, lam, subln_g, B, S, bq, bk, out_scale):
    T, qkw = q.shape
    H = qkw // ATTN_DV
    nq = S // bq
    return pl.pallas_call(
        functools.partial(_attn_kernel, bq, bk, out_scale),
        grid=(B, H, nq),
        in_specs=[
            pl.BlockSpec((bq, ATTN_DV), lambda b, h, i: (b * nq + i, h)),
            pl.BlockSpec((S, ATTN_DV), lambda b, h, i: (b, h)),
            pl.BlockSpec((S, ATTN_DV), lambda b, h, i: (b, h)),
            pl.BlockSpec((1, LANES), lambda b, h, i: (0, 0)),
            pl.BlockSpec((1, ATTN_DV), lambda b, h, i: (0, 0)),
        ],
        out_specs=pl.BlockSpec((bq, ATTN_DV), lambda b, h, i: (b * nq + i, h)),
        out_shape=jax.ShapeDtypeStruct((T, H * ATTN_DV), BF16),
        scratch_shapes=[
            pltpu.VMEM((2 * bq, ATTN_DV), BF16),
            pltpu.VMEM((S, 2 * ATTN_DV), BF16),
            pltpu.VMEM((2 * bq, bk), F32),
            pltpu.VMEM((2 * bq, max(bq - bk, LANES)), F32),
            pltpu.VMEM((2 * bq, LANES), F32),
            pltpu.VMEM((2 * bq, 2 * ATTN_DV), F32),
        ],
        compiler_params=_cparams(("arbitrary", "arbitrary", "arbitrary")),
        name="attention",
    )(q, k, v, lam, subln_g)


def _lru_kernel(tc, cw, pitch, x_ref, gate_ref, cw_ref, cb_ref, wg_ref, bg_ref, lam_ref, og_ref,
                o_ref, ext_ref, a_ref, b_ref, hc_ref):
    c = pl.program_id(1)

    @pl.when(c == 0)
    def _():
        ext_ref[0:SUBLANES, :] = jnp.zeros((SUBLANES, cw), F32)
        hc_ref[...] = jnp.zeros_like(hc_ref)

    ext_ref[SUBLANES:SUBLANES + tc, :] = x_ref[...]
    xc = cb_ref[...] + cw_ref[CONV_W - 1:CONV_W, :] * x_ref[...]
    for w in range(CONV_W - 1):
        sh = CONV_W - 1 - w
        xc = xc + cw_ref[w:w + 1, :] * ext_ref[SUBLANES - sh:SUBLANES - sh + tc, :]
    tail = ext_ref[tc:tc + SUBLANES, :]
    ext_ref[0:SUBLANES, :] = tail

    g = _dot(xc.astype(BF16), wg_ref[...]) + bg_ref[...]
    t_r = jnp.tanh(g[:, :cw])
    t_i = jnp.tanh(g[:, cw:])
    nl = -lam_ref[...]
    softplus = jnp.maximum(nl, 0.0) + jnp.log1p(jnp.exp(-jnp.abs(nl)))
    c1 = (-0.5 * LRU_C) * softplus
    log_a = c1 * t_r + c1
    th = jnp.tanh(log_a)
    p = -2.0 * th
    root = jnp.where(p > 0.0, p * lax.rsqrt(p * (1.0 - th)), 0.0)
    a = jnp.exp(log_a)
    b = root * ((t_i + 1.0) * (0.5 * xc))

    steps = tc // SUBLANES
    n_lt = cw // LANES
    for q in range(SUBLANES):
        for l in range(n_lt):
            a_ref[l, q * pitch:q * pitch + steps, :] = a[q * steps:(q + 1) * steps, l * LANES:(l + 1) * LANES]
            b_ref[l, q * pitch:q * pitch + steps, :] = b[q * steps:(q + 1) * steps, l * LANES:(l + 1) * LANES]

    def scan_step(t, carry):
        out = []
        for l in range(n_lt):
            h, prod = carry[l]
            rows = pl.ds(t, SUBLANES, stride=pitch)
            a_t = a_ref[l, rows, :]
            h = a_t * h + b_ref[l, rows, :]
            prod = a_t * prod
            b_ref[l, rows, :] = h
            a_ref[l, rows, :] = prod
            out.append((h, prod))
        return tuple(out)

    init = tuple((jnp.zeros((SUBLANES, LANES), F32), jnp.ones((SUBLANES, LANES), F32)) for _ in range(n_lt))
    last = lax.fori_loop(0, steps, scan_step, init, unroll=4)

    h_cols = []
    for l in range(n_lt):
        h_last, p_last = last[l]
        entering = [hc_ref[0:1, l * LANES:(l + 1) * LANES]]
        for q in range(SUBLANES):
            entering.append(h_last[q:q + 1, :] + p_last[q:q + 1, :] * entering[q])
        hc_ref[:, l * LANES:(l + 1) * LANES] = jnp.broadcast_to(entering[SUBLANES], (SUBLANES, LANES))
        h_cols.append(jnp.concatenate(
            [b_ref[l, q * pitch:q * pitch + steps, :] + a_ref[l, q * pitch:q * pitch + steps, :] * entering[q]
             for q in range(SUBLANES)], axis=0))
    h = jnp.concatenate(h_cols, axis=1)

    gt = gate_ref[...]
    gelu = 0.5 * gt * (1.0 + jnp.tanh(math.sqrt(2.0 / math.pi) * (gt + 0.044715 * gt * gt * gt)))
    y = h * gelu
    ms = jnp.mean(y * y, axis=-1, keepdims=True)
    o_ref[...] = (y * lax.rsqrt(ms + EPS) * og_ref[...]).astype(BF16)


def _scan_pitch(steps):
    return steps if (steps // SUBLANES) % 2 else steps + SUBLANES


def _lru(xr, gr, conv_w, conv_b, w_gates, b_gates, lru_lambda, out_g, B, S, tc):
    T, cw = xr.shape
    nc = S // tc
    pitch = _scan_pitch(tc // SUBLANES)
    const = lambda shape: pl.BlockSpec(shape, lambda b, c: (0,) * len(shape))
    row = pl.BlockSpec((tc, cw), lambda b, c: (b * nc + c, 0))
    return pl.pallas_call(
        functools.partial(_lru_kernel, tc, cw, pitch),
        grid=(B, nc),
        in_specs=[row, row, const((CONV_W, cw)), const((1, cw)), const((cw, 2 * cw)),
                  const((1, 2 * cw)), const((1, cw)), const((1, cw))],
        out_specs=row,
        out_shape=jax.ShapeDtypeStruct((T, cw), BF16),
        scratch_shapes=[
            pltpu.VMEM((tc + SUBLANES, cw), F32),
            pltpu.VMEM((cw // LANES, SUBLANES * pitch, LANES), F32),
            pltpu.VMEM((cw // LANES, SUBLANES * pitch, LANES), F32),
            pltpu.VMEM((SUBLANES, cw), F32),
        ],
        compiler_params=_cparams(("arbitrary", "arbitrary")),
        name="lru",
    )(xr, gr, conv_w, conv_b, w_gates, b_gates, lru_lambda, out_g)


def _out_proj_kernel(tm, aw, attn_ref, lru_ref, x_ref, mod_ref, w_ref, g_ref, rw_ref, rb_ref, tri_ref, low_ref,
                     x1_ref, h2_ref, meta_ref, cnt_ref):
    mix = _dot(attn_ref[...], w_ref[0:aw, :]) + _dot(lru_ref[...], w_ref[aw:, :])
    x1 = x_ref[...] + mod_ref[2:3, :] * mix
    x1_ref[...] = x1
    h2 = _rms_modulate(x1, g_ref[...], mod_ref[3:4, :], mod_ref[4:5, :])
    for j in range(h2.shape[1] // LANES):
        h2_ref[pl.ds(j, tm, stride=SUBLANES), :] = h2[:, j * LANES:(j + 1) * LANES]

    h_hi, h_lo = _split_hi_lo(h2)
    w_hi, w_lo = _split_hi_lo(rw_ref[...])
    by_hi = _dot_tb(jnp.concatenate([w_hi, w_lo], axis=0), h_hi)
    lg = by_hi[:N_EXPERTS] + by_hi[N_EXPERTS:] + _dot_tb(w_hi, h_lo) + rb_ref[...]

    eidx = lax.broadcasted_iota(I32, lg.shape, 0)
    picked = jnp.zeros(lg.shape, F32)
    vals, idxs = [], []
    for _ in range(TOP_K):
        m = jnp.max(lg, axis=0, keepdims=True)
        idx = jnp.min(jnp.where(lg == m, eidx, N_EXPERTS), axis=0, keepdims=True)
        sel = eidx == idx
        lg = jnp.where(sel, -jnp.inf, lg)
        picked = picked + sel.astype(F32)
        vals.append(m)
        idxs.append(idx)
    es = [jnp.exp(v - vals[0]) for v in vals]
    den = es[0] + es[1] + es[2] + es[3]
    gates = jnp.concatenate([e / den for e in es], axis=0)

    cnt = jnp.sum(picked, axis=1, keepdims=True)
    cnt_b = jnp.broadcast_to(cnt, cnt_ref.shape)
    cnt_hi = jnp.floor(cnt_b * (1.0 / BF16_EXACT_INT))
    cnt_lo = cnt_b - BF16_EXACT_INT * cnt_hi
    run_start = BF16_EXACT_INT * _dot(low_ref[...], cnt_hi.astype(BF16)) + _dot(low_ref[...], cnt_lo.astype(BF16))
    before = _dot(picked.astype(BF16), tri_ref[...]) + run_start[:, 0:1]
    pos = [jnp.sum(jnp.where(eidx == idx, before, 0.0), axis=0, keepdims=True) for idx in idxs]
    meta_ref[0:TOP_K, :] = jnp.concatenate(pos, axis=0).astype(I32) * SUBLANES
    meta_ref[TOP_K:, :] = lax.bitcast_convert_type(gates, I32)
    cnt_ref[...] = cnt_b.astype(I32)


def _out_proj(attn, lru, x2, mod3, w_out_b, norm_g, router_wt, router_b, S, tm):
    T, D = x2.shape
    aw = attn.shape[1]
    lw = lru.shape[1]
    tps = S // tm
    tri = (jnp.arange(tm, dtype=I32)[:, None] < jnp.arange(tm, dtype=I32)[None, :]).astype(BF16)
    experts = jnp.arange(N_EXPERTS, dtype=I32)
    low = (experts[None, :] < experts[:, None]).astype(BF16)
    const = lambda shape: pl.BlockSpec(shape, lambda i: (0,) * len(shape))
    row = lambda w: pl.BlockSpec((tm, w), lambda i: (i, 0))
    return pl.pallas_call(
        functools.partial(_out_proj_kernel, tm, aw),
        grid=(T // tm,),
        in_specs=[
            row(aw), row(lw), row(D),
            pl.BlockSpec((None, 6, D), lambda i: (i // tps, 0, 0)),
            const((D, D)), const((1, D)), const((N_EXPERTS, D)), const((N_EXPERTS, 1)),
            const((tm, tm)), const((N_EXPERTS, N_EXPERTS)),
        ],
        out_specs=[
            row(D),
            pl.BlockSpec((tm * SUBLANES, LANES), lambda i: (i, 0)),
            pl.BlockSpec((None, 2 * TOP_K, tm), lambda i: (i, 0, 0)),
            pl.BlockSpec((None, N_EXPERTS, LANES), lambda i: (i, 0, 0)),
        ],
        out_shape=[
            jax.ShapeDtypeStruct((T, D), F32),
            jax.ShapeDtypeStruct((T * SUBLANES, LANES), F32),
            jax.ShapeDtypeStruct((T // tm, 2 * TOP_K, tm), I32),
            jax.ShapeDtypeStruct((T // tm, N_EXPERTS, LANES), I32),
        ],
        compiler_params=_cparams(("arbitrary",)),
        name="out_proj",
    )(attn, lru, x2, mod3, w_out_b, norm_g, router_wt, router_b, tri, low)


def _rows(ref, first_row, n_rows):
    return ref.at[pl.ds(pl.multiple_of(first_row * SUBLANES, SUBLANES), n_rows * SUBLANES), :]


def _row_at(ref, sublane_offset):
    return ref.at[pl.ds(pl.multiple_of(sublane_offset, SUBLANES), SUBLANES), :]


def _fetch_tile(meta_hbm, smems, sems, t):
    i = pl.program_id(0)
    n = pl.num_programs(0)
    slot = i % 2
    per_tile = 2 * TOP_K * t

    def copies(tile, s):
        return [pltpu.make_async_copy(meta_hbm.at[pl.ds(pl.multiple_of(tile * per_tile + j * t, t), t)],
                                      smem.at[pl.ds(pl.multiple_of(s * t, t), t)], sems.at[s])
                for j, smem in enumerate(smems)]

    @pl.when(i == 0)
    def _():
        for c in copies(0, 0):
            c.start()

    for c in copies(i, slot):
        c.wait()

    @pl.when(i + 1 < n)
    def _():
        for c in copies(i + 1, 1 - slot):
            c.start()

    return slot * t


def _run_copies(t, tile, cnt_ref, off_ref, dst_ref, make_copy):
    def expert(e, carry):
        n = cnt_ref[tile * N_EXPERTS + e]
        off = off_ref[tile * N_EXPERTS + e]
        dst = dst_ref[tile * N_EXPERTS + e]
        for b in reversed(range(t.bit_length())):
            size = SUBLANES << b
            done = n & ~(2 * size - 1)

            @pl.when((n & size) != 0)
            def _():
                make_copy(off + done, dst + done, size).start()
        return carry

    lax.fori_loop(0, N_EXPERTS, expert, 0)


def _span(ref, first_sublane, n_sublanes):
    return ref.at[pl.ds(pl.multiple_of(first_sublane, SUBLANES), n_sublanes), :]


def _dispatch_kernel(t, n_blocks, cnt_ref, off_ref, dst_ref, pad_ref, meta_hbm, h_ref, xs_hbm,
                     pos0, pos1, pos2, pos3, stage_ref, zero_ref, psem, sem, zsem):
    i = pl.program_id(0)
    n = pl.num_programs(0)
    slot = i % 2

    def zero_fill(go):
        def expert(e, carry):
            first = pad_ref[e]
            length = pad_ref[N_EXPERTS + e]
            for b in reversed(range((MOE_BLOCK - 1).bit_length())):
                size = SUBLANES << b
                done = length & ~(2 * size - 1)

                @pl.when((length & size) != 0)
                def _():
                    go(pltpu.make_async_copy(_span(zero_ref, 0, size), _span(xs_hbm, first + done, size), zsem))
            return carry

        lax.fori_loop(0, N_EXPERTS, expert, 0)

        def block(b, carry):
            go(pltpu.make_async_copy(zero_ref, _rows(xs_hbm, b * MOE_BLOCK, MOE_BLOCK), zsem))
            return carry

        lax.fori_loop(pad_ref[2 * N_EXPERTS], n_blocks, block, 0)

    @pl.when(i == 0)
    def _():
        zero_ref[...] = jnp.zeros_like(zero_ref)
        zero_fill(lambda c: c.start())

    pos = (pos0, pos1, pos2, pos3)
    base = _fetch_tile(meta_hbm, pos, psem, t)
    stage = stage_ref.at[slot]

    def place(r, carry):
        row = _rows(h_ref, r, 1)[...]
        for k in range(TOP_K):
            _row_at(stage, pos[k][base + r])[...] = row
        return carry

    lax.fori_loop(0, t, place, 0, unroll=8)

    def all_runs(s):
        return pltpu.make_async_copy(stage_ref.at[s], _rows(xs_hbm, 0, TOP_K * t), sem.at[s])

    _run_copies(t, i, cnt_ref, off_ref, dst_ref,
                lambda a, b, m: pltpu.make_async_copy(_span(stage, a, m), _span(xs_hbm, b, m), sem.at[slot]))

    @pl.when(i > 0)
    def _():
        all_runs(1 - slot).wait()

    @pl.when(i == n - 1)
    def _():
        all_runs(slot).wait()
        zero_fill(lambda c: c.wait())


def _dispatch(tile_cnt, tile_off, tile_dst, pad_info, meta, h2t, n_blocks, t):
    n_tiles = meta.shape[0] // (2 * TOP_K * t)
    n_slots = n_blocks * MOE_BLOCK
    grid_spec = pltpu.PrefetchScalarGridSpec(
        num_scalar_prefetch=4,
        grid=(n_tiles,),
        in_specs=[pl.BlockSpec(memory_space=pl.ANY),
                  pl.BlockSpec((t * SUBLANES, LANES), lambda i, *_: (i, 0))],
        out_specs=pl.BlockSpec(memory_space=pl.ANY),
        scratch_shapes=[pltpu.SMEM((2 * t,), I32)] * TOP_K + [
            pltpu.VMEM((2, TOP_K * t * SUBLANES, LANES), F32),
            pltpu.VMEM((MOE_BLOCK * SUBLANES, LANES), F32),
            pltpu.SemaphoreType.DMA((2,)),
            pltpu.SemaphoreType.DMA((2,)),
            pltpu.SemaphoreType.DMA,
        ],
    )
    return pl.pallas_call(
        functools.partial(_dispatch_kernel, t, n_blocks),
        grid_spec=grid_spec,
        out_shape=jax.ShapeDtypeStruct((n_slots * SUBLANES, LANES), F32),
        compiler_params=_cparams(("arbitrary",)),
        name="dispatch",
    )(tile_cnt, tile_off, tile_dst, pad_info, meta, h2t)


def _experts_kernel(ff, be_ref, nv_ref, nxt_ref, xs_ref, wgu_hbm, bgu_ref, wdn_hbm, bdn_ref, ys_ref,
                    wgu_f, wdn_f, wgu_b, wdn_b, sem):
    i = pl.program_id(0)
    nv = nv_ref[i]
    n_slabs = xs_ref.shape[0] // MOE_BLOCK

    def fetch(e):
        return (pltpu.make_async_copy(wgu_hbm.at[e], wgu_f, sem.at[0]),
                pltpu.make_async_copy(wdn_hbm.at[e], wdn_f, sem.at[1]))

    @pl.when(i == 0)
    def _():
        for c in fetch(be_ref[0]):
            c.start()

    @pl.when((i == 0) | (be_ref[i] != be_ref[jnp.maximum(i - 1, 0)]))
    def _():
        for c in fetch(be_ref[i]):
            c.wait()
        wgu_b[...] = wgu_f[...].astype(BF16)
        wdn_b[...] = wdn_f[...].astype(BF16)

        @pl.when(nxt_ref[i] >= 0)
        def _():
            for c in fetch(nxt_ref[i]):
                c.start()

    @pl.when(nv > 0)
    def _():
        x = jnp.concatenate(
            [xs_ref[pl.ds(j, MOE_BLOCK, stride=SUBLANES), :] for j in range(n_slabs)], axis=1)
        gu = _dot(x.astype(BF16), wgu_b[...]) + bgu_ref[...]
        gate = jnp.minimum(gu[:, :ff], SWIGLU_LIMIT)
        up = jnp.clip(gu[:, ff:], -SWIGLU_LIMIT, SWIGLU_LIMIT)
        act = (up + 1.0) * (gate * _sigmoid(SWIGLU_ALPHA * gate))
        y = _dot(act.astype(BF16), wdn_b[...]) + bdn_ref[...]
        for j in range(n_slabs):
            ys_ref[pl.ds(j, MOE_BLOCK, stride=SUBLANES), :] = y[:, j * LANES:(j + 1) * LANES]

    @pl.when(nv == 0)
    def _():
        ys_ref[...] = jnp.zeros_like(ys_ref)


def _experts(block_expert, block_valid, xs, w_gu, b_gu, w_dn, b_dn):
    n_blocks = block_expert.shape[0]
    E, D, ff2 = w_gu.shape
    ff = ff2 // 2
    later = block_expert[None, :] > block_expert[:, None]
    nxt = jnp.min(jnp.where(later, block_expert[None, :], E), axis=1)
    nxt = jnp.where(nxt < E, nxt, -1)
    rows = pl.BlockSpec((MOE_BLOCK * SUBLANES, LANES), lambda i, *_: (i, 0))
    grid_spec = pltpu.PrefetchScalarGridSpec(
        num_scalar_prefetch=3,
        grid=(n_blocks,),
        in_specs=[
            rows,
            pl.BlockSpec(memory_space=pl.ANY),
            pl.BlockSpec((None, 1, ff2), lambda i, be, *_: (be[i], 0, 0)),
            pl.BlockSpec(memory_space=pl.ANY),
            pl.BlockSpec((None, 1, D), lambda i, be, *_: (be[i], 0, 0)),
        ],
        out_specs=rows,
        scratch_shapes=[
            pltpu.VMEM((D, ff2), F32), pltpu.VMEM((ff, D), F32),
            pltpu.VMEM((D, ff2), BF16), pltpu.VMEM((ff, D), BF16),
            pltpu.SemaphoreType.DMA((2,)),
        ],
    )
    return pl.pallas_call(
        functools.partial(_experts_kernel, ff),
        grid_spec=grid_spec,
        out_shape=jax.ShapeDtypeStruct(xs.shape, F32),
        compiler_params=_cparams(("arbitrary",)),
        name="experts",
    )(block_expert, block_valid, nxt.astype(I32), xs, w_gu, b_gu.reshape(E, 1, ff2), w_dn,
      b_dn.reshape(E, 1, D))


def _combine_kernel(t, cnt_ref, off_ref, dst_ref, meta_hbm, ys_hbm, x1_ref, mod_ref, o_ref,
                    pos0, pos1, pos2, pos3, gate0, gate1, gate2, gate3, stage_ref, y_ref, psem, sem):
    i = pl.program_id(0)
    n = pl.num_programs(0)
    slot = i % 2
    pos = (pos0, pos1, pos2, pos3)
    gates = (gate0, gate1, gate2, gate3)
    base = _fetch_tile(meta_hbm, pos + gates, psem, t)

    def fetch_runs(tile, s):
        stage = stage_ref.at[s]
        _run_copies(t, tile, cnt_ref, off_ref, dst_ref,
                    lambda a, b, m: pltpu.make_async_copy(_span(ys_hbm, b, m), _span(stage, a, m), sem.at[s]))

    @pl.when(i == 0)
    def _():
        fetch_runs(0, 0)

    @pl.when(i + 1 < n)
    def _():
        fetch_runs(i + 1, 1 - slot)

    pltpu.make_async_copy(_rows(ys_hbm, 0, TOP_K * t), stage_ref.at[slot], sem.at[slot]).wait()
    stage = stage_ref.at[slot]

    def gather(r, carry):
        acc = None
        for k in range(TOP_K):
            gate = lax.bitcast_convert_type(gates[k][base + r], F32)
            term = gate * _row_at(stage, pos[k][base + r])[...]
            acc = term if acc is None else acc + term
        _rows(y_ref, r, 1)[...] = acc
        return carry

    lax.fori_loop(0, t, gather, 0, unroll=8)

    n_slabs = o_ref.shape[1] // LANES
    y = jnp.concatenate([y_ref[pl.ds(j, t, stride=SUBLANES), :] for j in range(n_slabs)], axis=1)
    o_ref[...] = x1_ref[...] + mod_ref[5:6, :] * y


def _combine(tile_cnt, tile_off, tile_dst, meta, ys, x1, mod3, S, t):
    T, D = x1.shape
    n_tiles = T // t
    tps = S // t
    grid_spec = pltpu.PrefetchScalarGridSpec(
        num_scalar_prefetch=3,
        grid=(n_tiles,),
        in_specs=[
            pl.BlockSpec(memory_space=pl.ANY),
            pl.BlockSpec(memory_space=pl.ANY),
            pl.BlockSpec((t, D), lambda i, *_: (i, 0)),
            pl.BlockSpec((None, 6, D), lambda i, *_: (i // tps, 0, 0)),
        ],
        out_specs=pl.BlockSpec((t, D), lambda i, *_: (i, 0)),
        scratch_shapes=[pltpu.SMEM((2 * t,), I32)] * (2 * TOP_K) + [
            pltpu.VMEM((2, TOP_K * t * SUBLANES, LANES), F32),
            pltpu.VMEM((t * SUBLANES, LANES), F32),
            pltpu.SemaphoreType.DMA((2,)),
            pltpu.SemaphoreType.DMA((2,)),
        ],
    )
    return pl.pallas_call(
        functools.partial(_combine_kernel, t),
        grid_spec=grid_spec,
        out_shape=jax.ShapeDtypeStruct((T, D), F32),
        compiler_params=_cparams(("arbitrary",)),
        name="combine",
    )(tile_cnt, tile_off, tile_dst, meta, ys, x1, mod3)


def _tile(n, target):
    t = min(n, target)
    while n % t:
        t //= 2
    return t


def _block_diag(w):
    n, bw, _ = w.shape
    eye = jnp.eye(n, dtype=w.dtype)
    return (eye[:, None, :, None] * w[:, :, None, :]).reshape(n * bw, n * bw)


def _layer(l, x2, B, S, c, ada_w, ada_b, norm1_g, w_in, q_norm_g, k_norm_g, lambda_q1, lambda_k1, lambda_q2,
           lambda_k2, attn_subln_g, conv_w, conv_b, lru_wa, lru_ba, lru_wx, lru_bx, lru_lambda, lru_out_g,
           w_out, norm2_g, router_w, router_b, w_gate_up, b_gate_up, w_down, b_down):
    T, D = x2.shape
    lam_init = 0.8 - 0.6 * math.exp(-0.3 * l)
    aw = D // 2
    lw = D - aw
    heads = aw // ATTN_DV
    qkw = heads * 2 * ATTN_DK

    mod, lam = _adaln(c, ada_w, ada_b, lambda_q1, lambda_k1, lambda_q2, lambda_k2, lam_init)
    mod3 = mod.reshape(c.shape[0], 6, D)

    reps = qkw // ATTN_DK
    gq = (jnp.tile(q_norm_g, reps) * (ATTN_DK ** -0.5 * math.log2(math.e))).reshape(1, qkw)
    gk = jnp.tile(k_norm_g, reps).reshape(1, qkw)
    tm = _tile(S, ROUTE_TILE_ROWS)
    q, k, v, xr, gr = _in_proj(x2, mod3, norm1_g.reshape(1, D), w_in.astype(BF16), gq, gk, S,
                               _tile(S, IN_PROJ_ROWS), qkw, aw, lw)

    attn = _attention(q, k, v, lam, attn_subln_g.reshape(1, ATTN_DV), B, S, _tile(S, ATTN_Q_ROWS),
                      _tile(S, ATTN_KV_ROWS), 1.0 - lam_init)

    w_gates = (0.5 * jnp.concatenate([_block_diag(lru_wa), _block_diag(lru_wx)], axis=1)).astype(BF16)
    b_gates = 0.5 * jnp.concatenate([lru_ba, lru_bx]).reshape(1, 2 * lw)
    lru = _lru(xr, gr, conv_w, conv_b.reshape(1, lw), w_gates, b_gates, lru_lambda.reshape(1, lw),
               lru_out_g.reshape(1, lw), B, S, _tile(S, LRU_CHUNK_ROWS))

    x1, h2t, meta, tile_cnt = _out_proj(
        attn, lru, x2, mod3, w_out.astype(BF16), norm2_g.reshape(1, D), router_w.T,
        router_b.reshape(N_EXPERTS, 1), S, tm)

    tile_cnt = tile_cnt[:, :, 0]
    counts = jnp.sum(tile_cnt, axis=0)
    padded = ((counts + MOE_BLOCK - 1) // MOE_BLOCK) * MOE_BLOCK
    pad_ends = jnp.cumsum(padded)
    pad_starts = pad_ends - padded
    n_blocks = (T * TOP_K) // MOE_BLOCK + N_EXPERTS
    blk_start = jnp.arange(n_blocks, dtype=I32) * MOE_BLOCK
    owner = blk_start[:, None] >= pad_ends[None, :]
    block_expert = jnp.minimum(jnp.sum(owner, axis=1), N_EXPERTS - 1).astype(I32)
    onehot = block_expert[:, None] == jnp.arange(N_EXPERTS, dtype=I32)[None, :]
    row_end = jnp.sum(jnp.where(onehot, (pad_starts + counts)[None, :], 0), axis=1)
    block_valid = jnp.clip(row_end - blk_start, 0, MOE_BLOCK).astype(I32)
    tile_off = jnp.cumsum(tile_cnt, axis=1) - tile_cnt
    tile_dst = pad_starts[None, :] + jnp.cumsum(tile_cnt, axis=0) - tile_cnt
    flat = lambda a: (a * SUBLANES).reshape(-1).astype(I32)

    meta = meta.reshape(-1)
    pad_info = jnp.concatenate([(pad_starts + counts) * SUBLANES, (padded - counts) * SUBLANES,
                                pad_ends[-1:] // MOE_BLOCK]).astype(I32)
    xs = _dispatch(flat(tile_cnt), flat(tile_off), flat(tile_dst), pad_info, meta, h2t, n_blocks, tm)
    ys = _experts(block_expert, block_valid, xs, w_gate_up, b_gate_up, w_down, b_down)
    return _combine(flat(tile_cnt), flat(tile_off), flat(tile_dst), meta, ys, x1, mod3, S, tm)


def kernel(x, c, ada_w, ada_b, norm1_g, w_in, q_norm_g, k_norm_g, lambda_q1, lambda_k1, lambda_q2, lambda_k2,
           attn_subln_g, conv_w, conv_b, lru_wa, lru_ba, lru_wx, lru_bx, lru_lambda, lru_out_g, w_out, norm2_g,
           router_w, router_b, w_gate_up, b_gate_up, w_down, b_down):
    B, S, D = x.shape
    params = (ada_w, ada_b, norm1_g, w_in, q_norm_g, k_norm_g, lambda_q1, lambda_k1, lambda_q2, lambda_k2,
              attn_subln_g, conv_w, conv_b, lru_wa, lru_ba, lru_wx, lru_bx, lru_lambda, lru_out_g, w_out,
              norm2_g, router_w, router_b, w_gate_up, b_gate_up, w_down, b_down)
    x2 = x.reshape(B * S, D)
    for l in range(ada_w.shape[0]):
        x2 = _layer(l, x2, B, S, c, *[p[l] for p in params])
    return x2.reshape(B, S, D)
```

```python
import functools
import math

import jax
import jax.numpy as jnp
from jax import lax
from jax.experimental import pallas as pl
from jax.experimental.pallas import tpu as pltpu

F32 = jnp.float32
BF16 = jnp.bfloat16
I32 = jnp.int32

ATTN_DK = 64
ATTN_DV = 2 * ATTN_DK
CONV_W = 4
LRU_C = 8.0
N_EXPERTS = 32
TOP_K = 4
SWIGLU_LIMIT = 7.0
SWIGLU_ALPHA = 1.702
MOE_BLOCK = 512
EPS = 1e-6
NEG_BIG = -1e30

IN_PROJ_ROWS = 1024
ATTN_Q_ROWS = 1024
ATTN_KV_ROWS = 512
LRU_CHUNK_ROWS = 1024
ROUTE_TILE_ROWS = 512
BF16_EXACT_INT = 256.0

LANES = 128
SUBLANES = 8
VMEM_LIMIT = 56 * 1024 * 1024


def _cparams(sem):
    return pltpu.CompilerParams(dimension_semantics=sem, vmem_limit_bytes=VMEM_LIMIT)


def _split_hi_lo(x):
    hi = x.astype(BF16)
    lo = (x - hi.astype(F32)).astype(BF16)
    return hi, lo


def _sigmoid(x):
    return 0.5 * jnp.tanh(0.5 * x) + 0.5


def _dot(a, b):
    return jnp.dot(a, b, preferred_element_type=F32)


def _dot_tb(a, b):
    return lax.dot_general(a, b, (((1,), (1,)), ((), ())), preferred_element_type=F32)


def _adaln_kernel(lam_init, c_ref, w_ref, b_ref, lq1, lk1, lq2, lk2, mod_ref, lam_ref):
    c = c_ref[...]
    s = c * jax.nn.sigmoid(c)
    s_hi, s_lo = _split_hi_lo(s)
    w = w_ref[...]
    w_hi, w_lo = _split_hi_lo(w)
    mod_ref[...] = _dot(s_hi, w_hi) + _dot(s_hi, w_lo) + _dot(s_lo, w_hi) + b_ref[...]
    d1 = jnp.sum(lq1[...] * lk1[...], axis=-1, keepdims=True)
    d2 = jnp.sum(lq2[...] * lk2[...], axis=-1, keepdims=True)
    lam = jnp.exp(d1) - jnp.exp(d2) + lam_init
    lam_ref[...] = jnp.broadcast_to(lam, lam_ref.shape)


def _adaln(c, ada_w, ada_b, lq1, lk1, lq2, lk2, lam_init):
    B, D = c.shape
    n = ada_w.shape[1] // D
    vec = lambda: pl.BlockSpec((1, ATTN_DK), lambda j: (0, 0))
    return pl.pallas_call(
        functools.partial(_adaln_kernel, lam_init),
        grid=(n,),
        in_specs=[
            pl.BlockSpec((B, D), lambda j: (0, 0)),
            pl.BlockSpec((D, D), lambda j: (0, j)),
            pl.BlockSpec((1, D), lambda j: (0, j)),
            vec(), vec(), vec(), vec(),
        ],
        out_specs=[
            pl.BlockSpec((B, D), lambda j: (0, j)),
            pl.BlockSpec((1, LANES), lambda j: (0, 0)),
        ],
        out_shape=[
            jax.ShapeDtypeStruct((B, n * D), F32),
            jax.ShapeDtypeStruct((1, LANES), F32),
        ],
        compiler_params=_cparams(("arbitrary",)),
        name="adaln",
    )(c, ada_w, ada_b.reshape(1, -1), lq1.reshape(1, -1), lk1.reshape(1, -1),
      lq2.reshape(1, -1), lk2.reshape(1, -1))


def _rms_modulate(x, g, shift, scale):
    ms = jnp.mean(x * x, axis=-1, keepdims=True)
    y = x * lax.rsqrt(ms + EPS) * g
    return y * (1.0 + scale) + shift


def _group_rms_scale(q, group_ones):
    ss = _dot((q * q).astype(BF16), group_ones)
    return lax.rsqrt(ss * (1.0 / ATTN_DK) + EPS)


def _in_proj_kernel(qkw, aw, lw, x_ref, mod_ref, g_ref, w_ref, gq_ref, gk_ref, ones_ref,
                    q_ref, k_ref, v_ref, xr_ref, gr_ref):
    h = _rms_modulate(x_ref[...], g_ref[...], mod_ref[0:1, :], mod_ref[1:2, :])
    hb = h.astype(BF16)
    group_ones = ones_ref[...]
    o = 0
    q = _dot(hb, w_ref[:, o:o + qkw]); o += qkw
    q_ref[...] = (q * _group_rms_scale(q, group_ones) * gq_ref[...]).astype(BF16)
    k = _dot(hb, w_ref[:, o:o + qkw]); o += qkw
    k_ref[...] = (k * _group_rms_scale(k, group_ones) * gk_ref[...]).astype(BF16)
    v_ref[...] = _dot(hb, w_ref[:, o:o + aw]).astype(BF16); o += aw
    xr_ref[...] = _dot(hb, w_ref[:, o:o + lw]); o += lw
    gr_ref[...] = _dot(hb, w_ref[:, o:o + lw])


def _in_proj(x2, mod3, norm_g, w_in_b, gq, gk, S, tm, qkw, aw, lw):
    T, D = x2.shape
    tps = S // tm
    grp = jnp.arange(qkw, dtype=I32) // ATTN_DK
    group_ones = (grp[:, None] == grp[None, :]).astype(BF16)
    const = lambda shape: pl.BlockSpec(shape, lambda i: (0,) * len(shape))
    row = lambda w: pl.BlockSpec((tm, w), lambda i: (i, 0))
    return pl.pallas_call(
        functools.partial(_in_proj_kernel, qkw, aw, lw),
        grid=(T // tm,),
        in_specs=[
            row(D),
            pl.BlockSpec((None, 6, D), lambda i: (i // tps, 0, 0)),
            const((1, D)),
            const(w_in_b.shape),
            const((1, qkw)), const((1, qkw)),
            const((qkw, qkw)),
        ],
        out_specs=[row(qkw), row(qkw), row(aw), row(lw), row(lw)],
        out_shape=[
            jax.ShapeDtypeStruct((T, qkw), BF16),
            jax.ShapeDtypeStruct((T, qkw), BF16),
            jax.ShapeDtypeStruct((T, aw), BF16),
            jax.ShapeDtypeStruct((T, lw), F32),
            jax.ShapeDtypeStruct((T, lw), F32),
        ],
        compiler_params=_cparams(("arbitrary",)),
        name="in_proj",
    )(x2, mod3, norm_g, w_in_b, gq, gk, group_ones)


ATTN_LOOP_ROWS = 256
ATTN_DIAG_ROWS = 128


def _attn_kernel(bq, bk, out_scale, q_ref, k_ref, v_ref, lam_ref, g_ref, o_ref,
                 qq_ref, vp_ref, s_ref, sd_ref, m_ref, acc_ref):
    i = pl.program_id(2)
    n_rows = 2 * bq

    @pl.when(i == 0)
    def _():
        col = lax.broadcasted_iota(I32, (vp_ref.shape[0], ATTN_DV), 1)
        vp_ref[:, :ATTN_DV] = v_ref[...]
        vp_ref[:, ATTN_DV:] = jnp.where(col == 0, 1.0, 0.0).astype(BF16)

    q = q_ref[...]
    lane = lax.broadcasted_iota(I32, q.shape, 1)
    zero = jnp.zeros_like(q)
    qq_ref[0:bq, :] = jnp.where(lane < ATTN_DK, q, zero)
    qq_ref[bq:, :] = jnp.where(lane >= ATTN_DK, q, zero)
    m_ref[...] = jnp.full(m_ref.shape, NEG_BIG, F32)
    acc_ref[...] = jnp.zeros_like(acc_ref)

    def scores(rows, start, kw):
        return _dot_tb(qq_ref[rows, :], k_ref[pl.ds(start, kw), :])

    def softmax_pv(s, rows, start):
        kw = s.shape[1]
        m_old = m_ref[rows, :]
        m_new = jnp.maximum(m_old, jnp.max(s, axis=-1, keepdims=True))
        alpha = jnp.exp2(m_old - m_new)
        p = jnp.exp2(s - jnp.tile(m_new, (1, kw // LANES)))
        pv = _dot(p.astype(BF16), vp_ref[pl.ds(start, kw), :])
        acc_ref[rows, :] = acc_ref[rows, :] * jnp.tile(alpha, (1, 2)) + pv
        m_ref[rows, :] = m_new

    loop_chunks = [pl.ds(r, ATTN_LOOP_ROWS) for r in range(0, n_rows, ATTN_LOOP_ROWS)]
    for rows in loop_chunks:
        s_ref[rows, :] = scores(rows, 0, bk)

    def body(j, carry):
        start = pl.multiple_of(j * bk, bk)
        for rows in loop_chunks:
            s = s_ref[rows, :]
            s_ref[rows, :] = scores(rows, start + bk, bk)
            softmax_pv(s, rows, start)
        return carry

    n_full = i * (bq // bk)
    lax.fori_loop(0, n_full, body, 0)

    start = pl.multiple_of(n_full * bk, bk)
    rc = ATTN_DIAG_ROWS
    tri = (lax.broadcasted_iota(I32, (rc, rc), 1) <= lax.broadcasted_iota(I32, (rc, rc), 0))
    diag_chunks = [(pl.ds(r, rc), r % bq) for r in range(0, n_rows, rc)]
    for rows, q0 in diag_chunks:
        if q0 + rc > bk:
            sd_ref[rows, 0:q0 + rc - bk] = scores(rows, start + bk, q0 + rc - bk)
    for rows, q0 in diag_chunks:
        kw = q0 + rc
        parts = [s_ref[rows, 0:min(q0, bk)]] if q0 else []
        if q0 > bk:
            parts.append(sd_ref[rows, 0:q0 - bk])
        last = s_ref[rows, q0:kw] if kw <= bk else sd_ref[rows, q0 - bk:kw - bk]
        parts.append(jnp.where(tri, last, NEG_BIG))
        softmax_pv(parts[0] if len(parts) == 1 else jnp.concatenate(parts, axis=1), rows, start)

    acc = acc_ref[...]
    o = acc[:, :ATTN_DV] / acc[:, ATTN_DV:ATTN_DV + 1]
    a = o[:bq] - lam_ref[0:1, 0:1] * o[bq:]
    ms = jnp.mean(a * a, axis=-1, keepdims=True)
    o_ref[...] = (a * lax.rsqrt(ms + EPS) * g_ref[...] * out_scale).astype(BF16)


def _attention(q, k, v, lam, subln_g, B, S, bq, bk, out_scale):
    T, qkw = q.shape
    H = qkw // ATTN_DV
    nq = S // bq
    return pl.pallas_call(
        functools.partial(_attn_kernel, bq, bk, out_scale),
        grid=(B, H, nq),
        in_specs=[
            pl.BlockSpec((bq, ATTN_DV), lambda b, h, i: (b * nq + i, h)),
            pl.BlockSpec((S, ATTN_DV), lambda b, h, i: (b, h)),
            pl.BlockSpec((S, ATTN_DV), lambda b, h, i: (b, h)),
            pl.BlockSpec((1, LANES), lambda b, h, i: (0, 0)),
            pl.BlockSpec((1, ATTN_DV), lambda b, h, i: (0, 0)),
        ],
        out_specs=pl.BlockSpec((bq, ATTN_DV), lambda b, h, i: (b * nq + i, h)),
        out_shape=jax.ShapeDtypeStruct((T, H * ATTN_DV), BF16),
        scratch_shapes=[
            pltpu.VMEM((2 * bq, ATTN_DV), BF16),
            pltpu.VMEM((S, 2 * ATTN_DV), BF16),
            pltpu.VMEM((2 * bq, bk), F32),
            pltpu.VMEM((2 * bq, max(bq - bk, LANES)), F32),
            pltpu.VMEM((2 * bq, LANES), F32),
            pltpu.VMEM((2 * bq, 2 * ATTN_DV), F32),
        ],
        compiler_params=_cparams(("arbitrary", "arbitrary", "arbitrary")),
        name="attention",
    )(q, k, v, lam, subln_g)


def _lru_kernel(tc, cw, pitch, x_ref, gate_ref, cw_ref, cb_ref, wg_ref, bg_ref, lam_ref, og_ref,
                o_ref, ext_ref, a_ref, b_ref, hc_ref):
    c = pl.program_id(1)

    @pl.when(c == 0)
    def _():
        ext_ref[0:SUBLANES, :] = jnp.zeros((SUBLANES, cw), F32)
        hc_ref[...] = jnp.zeros_like(hc_ref)

    ext_ref[SUBLANES:SUBLANES + tc, :] = x_ref[...]
    xc = cb_ref[...] + cw_ref[CONV_W - 1:CONV_W, :] * x_ref[...]
    for w in range(CONV_W - 1):
        sh = CONV_W - 1 - w
        xc = xc + cw_ref[w:w + 1, :] * ext_ref[SUBLANES - sh:SUBLANES - sh + tc, :]
    tail = ext_ref[tc:tc + SUBLANES, :]
    ext_ref[0:SUBLANES, :] = tail

    g = _dot(xc.astype(BF16), wg_ref[...]) + bg_ref[...]
    t_r = jnp.tanh(g[:, :cw])
    t_i = jnp.tanh(g[:, cw:])
    nl = -lam_ref[...]
    softplus = jnp.maximum(nl, 0.0) + jnp.log1p(jnp.exp(-jnp.abs(nl)))
    c1 = (-0.5 * LRU_C) * softplus
    log_a = c1 * t_r + c1
    th = jnp.tanh(log_a)
    p = -2.0 * th
    root = jnp.where(p > 0.0, p * lax.rsqrt(p * (1.0 - th)), 0.0)
    a = jnp.exp(log_a)
    b = root * ((t_i + 1.0) * (0.5 * xc))

    steps = tc // SUBLANES
    n_lt = cw // LANES
    for q in range(SUBLANES):
        for l in range(n_lt):
            a_ref[l, q * pitch:q * pitch + steps, :] = a[q * steps:(q + 1) * steps, l * LANES:(l + 1) * LANES]
            b_ref[l, q * pitch:q * pitch + steps, :] = b[q * steps:(q + 1) * steps, l * LANES:(l + 1) * LANES]

    def scan_step(t, carry):
        out = []
        for l in range(n_lt):
            h, prod = carry[l]
            rows = pl.ds(t, SUBLANES, stride=pitch)
            a_t = a_ref[l, rows, :]
            h = a_t * h + b_ref[l, rows, :]
            prod = a_t * prod
            b_ref[l, rows, :] = h
            a_ref[l, rows, :] = prod
            out.append((h, prod))
        return tuple(out)

    init = tuple((jnp.zeros((SUBLANES, LANES), F32), jnp.ones((SUBLANES, LANES), F32)) for _ in range(n_lt))
    last = lax.fori_loop(0, steps, scan_step, init, unroll=4)

    h_cols = []
    for l in range(n_lt):
        h_last, p_last = last[l]
        entering = [hc_ref[0:1, l * LANES:(l + 1) * LANES]]
        for q in range(SUBLANES):
            entering.append(h_last[q:q + 1, :] + p_last[q:q + 1, :] * entering[q])
        hc_ref[:, l * LANES:(l + 1) * LANES] = jnp.broadcast_to(entering[SUBLANES], (SUBLANES, LANES))
        h_cols.append(jnp.concatenate(
            [b_ref[l, q * pitch:q * pitch + steps, :] + a_ref[l, q * pitch:q * pitch + steps, :] * entering[q]
             for q in range(SUBLANES)], axis=0))
    h = jnp.concatenate(h_cols, axis=1)

    gt = gate_ref[...]
    gelu = 0.5 * gt * (1.0 + jnp.tanh(math.sqrt(2.0 / math.pi) * (gt + 0.044715 * gt * gt * gt)))
    y = h * gelu
    ms = jnp.mean(y * y, axis=-1, keepdims=True)
    o_ref[...] = (y * lax.rsqrt(ms + EPS) * og_ref[...]).astype(BF16)


def _scan_pitch(steps):
    return steps if (steps // SUBLANES) % 2 else steps + SUBLANES


def _lru(xr, gr, conv_w, conv_b, w_gates, b_gates, lru_lambda, out_g, B, S, tc):
    T, cw = xr.shape
    nc = S // tc
    pitch = _scan_pitch(tc // SUBLANES)
    const = lambda shape: pl.BlockSpec(shape, lambda b, c: (0,) * len(shape))
    row = pl.BlockSpec((tc, cw), lambda b, c: (b * nc + c, 0))
    return pl.pallas_call(
        functools.partial(_lru_kernel, tc, cw, pitch),
        grid=(B, nc),
        in_specs=[row, row, const((CONV_W, cw)), const((1, cw)), const((cw, 2 * cw)),
                  const((1, 2 * cw)), const((1, cw)), const((1, cw))],
        out_specs=row,
        out_shape=jax.ShapeDtypeStruct((T, cw), BF16),
        scratch_shapes=[
            pltpu.VMEM((tc + SUBLANES, cw), F32),
            pltpu.VMEM((cw // LANES, SUBLANES * pitch, LANES), F32),
            pltpu.VMEM((cw // LANES, SUBLANES * pitch, LANES), F32),
            pltpu.VMEM((SUBLANES, cw), F32),
        ],
        compiler_params=_cparams(("arbitrary", "arbitrary")),
        name="lru",
    )(xr, gr, conv_w, conv_b, w_gates, b_gates, lru_lambda, out_g)


def _out_proj_kernel(tm, aw, attn_ref, lru_ref, x_ref, mod_ref, w_ref, g_ref, rw_ref, rb_ref, tri_ref, low_ref,
                     x1_ref, h2_ref, meta_ref, cnt_ref):
    mix = _dot(attn_ref[...], w_ref[0:aw, :]) + _dot(lru_ref[...], w_ref[aw:, :])
    x1 = x_ref[...] + mod_ref[2:3, :] * mix
    x1_ref[...] = x1
    h2 = _rms_modulate(x1, g_ref[...], mod_ref[3:4, :], mod_ref[4:5, :])
    for j in range(h2.shape[1] // LANES):
        h2_ref[pl.ds(j, tm, stride=SUBLANES), :] = h2[:, j * LANES:(j + 1) * LANES]

    h_hi, h_lo = _split_hi_lo(h2)
    w_hi, w_lo = _split_hi_lo(rw_ref[...])
    by_hi = _dot_tb(jnp.concatenate([w_hi, w_lo], axis=0), h_hi)
    lg = by_hi[:N_EXPERTS] + by_hi[N_EXPERTS:] + _dot_tb(w_hi, h_lo) + rb_ref[...]

    eidx = lax.broadcasted_iota(I32, lg.shape, 0)
    picked = jnp.zeros(lg.shape, F32)
    vals, idxs = [], []
    for _ in range(TOP_K):
        m = jnp.max(lg, axis=0, keepdims=True)
        idx = jnp.min(jnp.where(lg == m, eidx, N_EXPERTS), axis=0, keepdims=True)
        sel = eidx == idx
        lg = jnp.where(sel, -jnp.inf, lg)
        picked = picked + sel.astype(F32)
        vals.append(m)
        idxs.append(idx)
    es = [jnp.exp(v - vals[0]) for v in vals]
    den = es[0] + es[1] + es[2] + es[3]
    gates = jnp.concatenate([e / den for e in es], axis=0)

    cnt = jnp.sum(picked, axis=1, keepdims=True)
    cnt_b = jnp.broadcast_to(cnt, cnt_ref.shape)
    cnt_hi = jnp.floor(cnt_b * (1.0 / BF16_EXACT_INT))
    cnt_lo = cnt_b - BF16_EXACT_INT * cnt_hi
    run_start = BF16_EXACT_INT * _dot(low_ref[...], cnt_hi.astype(BF16)) + _dot(low_ref[...], cnt_lo.astype(BF16))
    before = _dot(picked.astype(BF16), tri_ref[...]) + run_start[:, 0:1]
    pos = [jnp.sum(jnp.where(eidx == idx, before, 0.0), axis=0, keepdims=True) for idx in idxs]
    meta_ref[0:TOP_K, :] = jnp.concatenate(pos, axis=0).astype(I32) * SUBLANES
    meta_ref[TOP_K:, :] = lax.bitcast_convert_type(gates, I32)
    cnt_ref[...] = cnt_b.astype(I32)


def _out_proj(attn, lru, x2, mod3, w_out_b, norm_g, router_wt, router_b, S, tm):
    T, D = x2.shape
    aw = attn.shape[1]
    lw = lru.shape[1]
    tps = S // tm
    tri = (jnp.arange(tm, dtype=I32)[:, None] < jnp.arange(tm, dtype=I32)[None, :]).astype(BF16)
    experts = jnp.arange(N_EXPERTS, dtype=I32)
    low = (experts[None, :] < experts[:, None]).astype(BF16)
    const = lambda shape: pl.BlockSpec(shape, lambda i: (0,) * len(shape))
    row = lambda w: pl.BlockSpec((tm, w), lambda i: (i, 0))
    return pl.pallas_call(
        functools.partial(_out_proj_kernel, tm, aw),
        grid=(T // tm,),
        in_specs=[
            row(aw), row(lw), row(D),
            pl.BlockSpec((None, 6, D), lambda i: (i // tps, 0, 0)),
            const((D, D)), const((1, D)), const((N_EXPERTS, D)), const((N_EXPERTS, 1)),
            const((tm, tm)), const((N_EXPERTS, N_EXPERTS)),
        ],
        out_specs=[
            row(D),
            pl.BlockSpec((tm * SUBLANES, LANES), lambda i: (i, 0)),
            pl.BlockSpec((None, 2 * TOP_K, tm), lambda i: (i, 0, 0)),
            pl.BlockSpec((None, N_EXPERTS, LANES), lambda i: (i, 0, 0)),
        ],
        out_shape=[
            jax.ShapeDtypeStruct((T, D), F32),
            jax.ShapeDtypeStruct((T * SUBLANES, LANES), F32),
            jax.ShapeDtypeStruct((T // tm, 2 * TOP_K, tm), I32),
            jax.ShapeDtypeStruct((T // tm, N_EXPERTS, LANES), I32),
        ],
        compiler_params=_cparams(("arbitrary",)),
        name="out_proj",
    )(attn, lru, x2, mod3, w_out_b, norm_g, router_wt, router_b, tri, low)


def _rows(ref, first_row, n_rows):
    return ref.at[pl.ds(pl.multiple_of(first_row * SUBLANES, SUBLANES), n_rows * SUBLANES), :]


def _row_at(ref, sublane_offset):
    return ref.at[pl.ds(pl.multiple_of(sublane_offset, SUBLANES), SUBLANES), :]


def _fetch_tile(meta_hbm, smems, sems, t):
    i = pl.program_id(0)
    n = pl.num_programs(0)
    slot = i % 2
    per_tile = 2 * TOP_K * t

    def copies(tile, s):
        return [pltpu.make_async_copy(meta_hbm.at[pl.ds(pl.multiple_of(tile * per_tile + j * t, t), t)],
                                      smem.at[pl.ds(pl.multiple_of(s * t, t), t)], sems.at[s])
                for j, smem in enumerate(smems)]

    @pl.when(i == 0)
    def _():
        for c in copies(0, 0):
            c.start()

    for c in copies(i, slot):
        c.wait()

    @pl.when(i + 1 < n)
    def _():
        for c in copies(i + 1, 1 - slot):
            c.start()

    return slot * t


def _run_copies(t, tile, cnt_ref, off_ref, dst_ref, make_copy):
    def expert(e, carry):
        n = cnt_ref[tile * N_EXPERTS + e]
        off = off_ref[tile * N_EXPERTS + e]
        dst = dst_ref[tile * N_EXPERTS + e]
        for b in reversed(range(t.bit_length())):
            size = SUBLANES << b
            done = n & ~(2 * size - 1)

            @pl.when((n & size) != 0)
            def _():
                make_copy(off + done, dst + done, size).start()
        return carry

    lax.fori_loop(0, N_EXPERTS, expert, 0)


def _span(ref, first_sublane, n_sublanes):
    return ref.at[pl.ds(pl.multiple_of(first_sublane, SUBLANES), n_sublanes), :]


def _dispatch_kernel(t, n_blocks, cnt_ref, off_ref, dst_ref, pad_ref, meta_hbm, h_ref, xs_hbm,
                     pos0, pos1, pos2, pos3, stage_ref, zero_ref, psem, sem, zsem):
    i = pl.program_id(0)
    n = pl.num_programs(0)
    slot = i % 2

    def zero_fill(go):
        def expert(e, carry):
            first = pad_ref[e]
            length = pad_ref[N_EXPERTS + e]
            for b in reversed(range((MOE_BLOCK - 1).bit_length())):
                size = SUBLANES << b
                done = length & ~(2 * size - 1)

                @pl.when((length & size) != 0)
                def _():
                    go(pltpu.make_async_copy(_span(zero_ref, 0, size), _span(xs_hbm, first + done, size), zsem))
            return carry

        lax.fori_loop(0, N_EXPERTS, expert, 0)

        def block(b, carry):
            go(pltpu.make_async_copy(zero_ref, _rows(xs_hbm, b * MOE_BLOCK, MOE_BLOCK), zsem))
            return carry

        lax.fori_loop(pad_ref[2 * N_EXPERTS], n_blocks, block, 0)

    @pl.when(i == 0)
    def _():
        zero_ref[...] = jnp.zeros_like(zero_ref)
        zero_fill(lambda c: c.start())

    pos = (pos0, pos1, pos2, pos3)
    base = _fetch_tile(meta_hbm, pos, psem, t)
    stage = stage_ref.at[slot]

    def place(r, carry):
        row = _rows(h_ref, r, 1)[...]
        for k in range(TOP_K):
            _row_at(stage, pos[k][base + r])[...] = row
        return carry

    lax.fori_loop(0, t, place, 0, unroll=8)

    def all_runs(s):
        return pltpu.make_async_copy(stage_ref.at[s], _rows(xs_hbm, 0, TOP_K * t), sem.at[s])

    _run_copies(t, i, cnt_ref, off_ref, dst_ref,
                lambda a, b, m: pltpu.make_async_copy(_span(stage, a, m), _span(xs_hbm, b, m), sem.at[slot]))

    @pl.when(i > 0)
    def _():
        all_runs(1 - slot).wait()

    @pl.when(i == n - 1)
    def _():
        all_runs(slot).wait()
        zero_fill(lambda c: c.wait())


def _dispatch(tile_cnt, tile_off, tile_dst, pad_info, meta, h2t, n_blocks, t):
    n_tiles = meta.shape[0] // (2 * TOP_K * t)
    n_slots = n_blocks * MOE_BLOCK
    grid_spec = pltpu.PrefetchScalarGridSpec(
        num_scalar_prefetch=4,
        grid=(n_tiles,),
        in_specs=[pl.BlockSpec(memory_space=pl.ANY),
                  pl.BlockSpec((t * SUBLANES, LANES), lambda i, *_: (i, 0))],
        out_specs=pl.BlockSpec(memory_space=pl.ANY),
        scratch_shapes=[pltpu.SMEM((2 * t,), I32)] * TOP_K + [
            pltpu.VMEM((2, TOP_K * t * SUBLANES, LANES), F32),
            pltpu.VMEM((MOE_BLOCK * SUBLANES, LANES), F32),
            pltpu.SemaphoreType.DMA((2,)),
            pltpu.SemaphoreType.DMA((2,)),
            pltpu.SemaphoreType.DMA,
        ],
    )
    return pl.pallas_call(
        functools.partial(_dispatch_kernel, t, n_blocks),
        grid_spec=grid_spec,
        out_shape=jax.ShapeDtypeStruct((n_slots * SUBLANES, LANES), F32),
        compiler_params=_cparams(("arbitrary",)),
        name="dispatch",
    )(tile_cnt, tile_off, tile_dst, pad_info, meta, h2t)


def _experts_kernel(ff, be_ref, nv_ref, nxt_ref, xs_ref, wgu_hbm, bgu_ref, wdn_hbm, bdn_ref, ys_ref,
                    wgu_f, wdn_f, wgu_b, wdn_b, sem):
    i = pl.program_id(0)
    nv = nv_ref[i]
    n_slabs = xs_ref.shape[0] // MOE_BLOCK

    def fetch(e):
        return (pltpu.make_async_copy(wgu_hbm.at[e], wgu_f, sem.at[0]),
                pltpu.make_async_copy(wdn_hbm.at[e], wdn_f, sem.at[1]))

    @pl.when(i == 0)
    def _():
        for c in fetch(be_ref[0]):
            c.start()

    @pl.when((i == 0) | (be_ref[i] != be_ref[jnp.maximum(i - 1, 0)]))
    def _():
        for c in fetch(be_ref[i]):
            c.wait()
        wgu_b[...] = wgu_f[...].astype(BF16)
        wdn_b[...] = wdn_f[...].astype(BF16)

        @pl.when(nxt_ref[i] >= 0)
        def _():
            for c in fetch(nxt_ref[i]):
                c.start()

    @pl.when(nv > 0)
    def _():
        x = jnp.concatenate(
            [xs_ref[pl.ds(j, MOE_BLOCK, stride=SUBLANES), :] for j in range(n_slabs)], axis=1)
        gu = _dot(x.astype(BF16), wgu_b[...]) + bgu_ref[...]
        gate = jnp.minimum(gu[:, :ff], SWIGLU_LIMIT)
        up = jnp.clip(gu[:, ff:], -SWIGLU_LIMIT, SWIGLU_LIMIT)
        act = (up + 1.0) * (gate * _sigmoid(SWIGLU_ALPHA * gate))
        y = _dot(act.astype(BF16), wdn_b[...]) + bdn_ref[...]
        for j in range(n_slabs):
            ys_ref[pl.ds(j, MOE_BLOCK, stride=SUBLANES), :] = y[:, j * LANES:(j + 1) * LANES]

    @pl.when(nv == 0)
    def _():
        ys_ref[...] = jnp.zeros_like(ys_ref)


def _experts(block_expert, block_valid, xs, w_gu, b_gu, w_dn, b_dn):
    n_blocks = block_expert.shape[0]
    E, D, ff2 = w_gu.shape
    ff = ff2 // 2
    later = block_expert[None, :] > block_expert[:, None]
    nxt = jnp.min(jnp.where(later, block_expert[None, :], E), axis=1)
    nxt = jnp.where(nxt < E, nxt, -1)
    rows = pl.BlockSpec((MOE_BLOCK * SUBLANES, LANES), lambda i, *_: (i, 0))
    grid_spec = pltpu.PrefetchScalarGridSpec(
        num_scalar_prefetch=3,
        grid=(n_blocks,),
        in_specs=[
            rows,
            pl.BlockSpec(memory_space=pl.ANY),
            pl.BlockSpec((None, 1, ff2), lambda i, be, *_: (be[i], 0, 0)),
            pl.BlockSpec(memory_space=pl.ANY),
            pl.BlockSpec((None, 1, D), lambda i, be, *_: (be[i], 0, 0)),
        ],
        out_specs=rows,
        scratch_shapes=[
            pltpu.VMEM((D, ff2), F32), pltpu.VMEM((ff, D), F32),
            pltpu.VMEM((D, ff2), BF16), pltpu.VMEM((ff, D), BF16),
            pltpu.SemaphoreType.DMA((2,)),
        ],
    )
    return pl.pallas_call(
        functools.partial(_experts_kernel, ff),
        grid_spec=grid_spec,
        out_shape=jax.ShapeDtypeStruct(xs.shape, F32),
        compiler_params=_cparams(("arbitrary",)),
        name="experts",
    )(block_expert, block_valid, nxt.astype(I32), xs, w_gu, b_gu.reshape(E, 1, ff2), w_dn,
      b_dn.reshape(E, 1, D))


def _combine_kernel(t, cnt_ref, off_ref, dst_ref, meta_hbm, ys_hbm, x1_ref, mod_ref, o_ref,
                    pos0, pos1, pos2, pos3, gate0, gate1, gate2, gate3, stage_ref, y_ref, psem, sem):
    i = pl.program_id(0)
    n = pl.num_programs(0)
    slot = i % 2
    pos = (pos0, pos1, pos2, pos3)
    gates = (gate0, gate1, gate2, gate3)
    base = _fetch_tile(meta_hbm, pos + gates, psem, t)

    def fetch_runs(tile, s):
        stage = stage_ref.at[s]
        _run_copies(t, tile, cnt_ref, off_ref, dst_ref,
                    lambda a, b, m: pltpu.make_async_copy(_span(ys_hbm, b, m), _span(stage, a, m), sem.at[s]))

    @pl.when(i == 0)
    def _():
        fetch_runs(0, 0)

    @pl.when(i + 1 < n)
    def _():
        fetch_runs(i + 1, 1 - slot)

    pltpu.make_async_copy(_rows(ys_hbm, 0, TOP_K * t), stage_ref.at[slot], sem.at[slot]).wait()
    stage = stage_ref.at[slot]

    def gather(r, carry):
        acc = None
        for k in range(TOP_K):
            gate = lax.bitcast_convert_type(gates[k][base + r], F32)
            term = gate * _row_at(stage, pos[k][base + r])[...]
            acc = term if acc is None else acc + term
        _rows(y_ref, r, 1)[...] = acc
        return carry

    lax.fori_loop(0, t, gather, 0, unroll=8)

    n_slabs = o_ref.shape[1] // LANES
    y = jnp.concatenate([y_ref[pl.ds(j, t, stride=SUBLANES), :] for j in range(n_slabs)], axis=1)
    o_ref[...] = x1_ref[...] + mod_ref[5:6, :] * y


def _combine(tile_cnt, tile_off, tile_dst, meta, ys, x1, mod3, S, t):
    T, D = x1.shape
    n_tiles = T // t
    tps = S // t
    grid_spec = pltpu.PrefetchScalarGridSpec(
        num_scalar_prefetch=3,
        grid=(n_tiles,),
        in_specs=[
            pl.BlockSpec(memory_space=pl.ANY),
            pl.BlockSpec(memory_space=pl.ANY),
            pl.BlockSpec((t, D), lambda i, *_: (i, 0)),
            pl.BlockSpec((None, 6, D), lambda i, *_: (i // tps, 0, 0)),
        ],
        out_specs=pl.BlockSpec((t, D), lambda i, *_: (i, 0)),
        scratch_shapes=[pltpu.SMEM((2 * t,), I32)] * (2 * TOP_K) + [
            pltpu.VMEM((2, TOP_K * t * SUBLANES, LANES), F32),
            pltpu.VMEM((t * SUBLANES, LANES), F32),
            pltpu.SemaphoreType.DMA((2,)),
            pltpu.SemaphoreType.DMA((2,)),
        ],
    )
    return pl.pallas_call(
        functools.partial(_combine_kernel, t),
        grid_spec=grid_spec,
        out_shape=jax.ShapeDtypeStruct((T, D), F32),
        compiler_params=_cparams(("arbitrary",)),
        name="combine",
    )(tile_cnt, tile_off, tile_dst, meta, ys, x1, mod3)


def _tile(n, target):
    t = min(n, target)
    while n % t:
        t //= 2
    return t


def _block_diag(w):
    n, bw, _ = w.shape
    eye = jnp.eye(n, dtype=w.dtype)
    return (eye[:, None, :, None] * w[:, :, None, :]).reshape(n * bw, n * bw)


def _layer(l, x2, B, S, c, ada_w, ada_b, norm1_g, w_in, q_norm_g, k_norm_g, lambda_q1, lambda_k1, lambda_q2,
           lambda_k2, attn_subln_g, conv_w, conv_b, lru_wa, lru_ba, lru_wx, lru_bx, lru_lambda, lru_out_g,
           w_out, norm2_g, router_w, router_b, w_gate_up, b_gate_up, w_down, b_down):
    T, D = x2.shape
    lam_init = 0.8 - 0.6 * math.exp(-0.3 * l)
    aw = D // 2
    lw = D - aw
    heads = aw // ATTN_DV
    qkw = heads * 2 * ATTN_DK

    mod, lam = _adaln(c, ada_w, ada_b, lambda_q1, lambda_k1, lambda_q2, lambda_k2, lam_init)
    mod3 = mod.reshape(c.shape[0], 6, D)

    reps = qkw // ATTN_DK
    gq = (jnp.tile(q_norm_g, reps) * (ATTN_DK ** -0.5 * math.log2(math.e))).reshape(1, qkw)
    gk = jnp.tile(k_norm_g, reps).reshape(1, qkw)
    tm = _tile(S, ROUTE_TILE_ROWS)
    q, k, v, xr, gr = _in_proj(x2, mod3, norm1_g.reshape(1, D), w_in.astype(BF16), gq, gk, S,
                               _tile(S, IN_PROJ_ROWS), qkw, aw, lw)

    attn = _attention(q, k, v, lam, attn_subln_g.reshape(1, ATTN_DV), B, S, _tile(S, ATTN_Q_ROWS),
                      _tile(S, ATTN_KV_ROWS), 1.0 - lam_init)

    w_gates = (0.5 * jnp.concatenate([_block_diag(lru_wa), _block_diag(lru_wx)], axis=1)).astype(BF16)
    b_gates = 0.5 * jnp.concatenate([lru_ba, lru_bx]).reshape(1, 2 * lw)
    lru = _lru(xr, gr, conv_w, conv_b.reshape(1, lw), w_gates, b_gates, lru_lambda.reshape(1, lw),
               lru_out_g.reshape(1, lw), B, S, _tile(S, LRU_CHUNK_ROWS))

    x1, h2t, meta, tile_cnt = _out_proj(
        attn, lru, x2, mod3, w_out.astype(BF16), norm2_g.reshape(1, D), router_w.T,
        router_b.reshape(N_EXPERTS, 1), S, tm)

    tile_cnt = tile_cnt[:, :, 0]
    counts = jnp.sum(tile_cnt, axis=0)
    padded = ((counts + MOE_BLOCK - 1) // MOE_BLOCK) * MOE_BLOCK
    pad_ends = jnp.cumsum(padded)
    pad_starts = pad_ends - padded
    n_blocks = (T * TOP_K) // MOE_BLOCK + N_EXPERTS
    blk_start = jnp.arange(n_blocks, dtype=I32) * MOE_BLOCK
    owner = blk_start[:, None] >= pad_ends[None, :]
    block_expert = jnp.minimum(jnp.sum(owner, axis=1), N_EXPERTS - 1).astype(I32)
    onehot = block_expert[:, None] == jnp.arange(N_EXPERTS, dtype=I32)[None, :]
    row_end = jnp.sum(jnp.where(onehot, (pad_starts + counts)[None, :], 0), axis=1)
    block_valid = jnp.clip(row_end - blk_start, 0, MOE_BLOCK).astype(I32)
    tile_off = jnp.cumsum(tile_cnt, axis=1) - tile_cnt
    tile_dst = pad_starts[None, :] + jnp.cumsum(tile_cnt, axis=0) - tile_cnt
    flat = lambda a: (a * SUBLANES).reshape(-1).astype(I32)

    meta = meta.reshape(-1)
    pad_info = jnp.concatenate([(pad_starts + counts) * SUBLANES, (padded - counts) * SUBLANES,
                                pad_ends[-1:] // MOE_BLOCK]).astype(I32)
    xs = _dispatch(flat(tile_cnt), flat(tile_off), flat(tile_dst), pad_info, meta, h2t, n_blocks, tm)
    ys = _experts(block_expert, block_valid, xs, w_gate_up, b_gate_up, w_down, b_down)
    return _combine(flat(tile_cnt), flat(tile_off), flat(tile_dst), meta, ys, x1, mod3, S, tm)


def kernel(x, c, ada_w, ada_b, norm1_g, w_in, q_norm_g, k_norm_g, lambda_q1, lambda_k1, lambda_q2, lambda_k2,
           attn_subln_g, conv_w, conv_b, lru_wa, lru_ba, lru_wx, lru_bx, lru_lambda, lru_out_g, w_out, norm2_g,
           router_w, router_b, w_gate_up, b_gate_up, w_down, b_down):
    B, S, D = x.shape
    params = (ada_w, ada_b, norm1_g, w_in, q_norm_g, k_norm_g, lambda_q1, lambda_k1, lambda_q2, lambda_k2,
              attn_subln_g, conv_w, conv_b, lru_wa, lru_ba, lru_wx, lru_bx, lru_lambda, lru_out_g, w_out,
              norm2_g, router_w, router_b, w_gate_up, b_gate_up, w_down, b_down)
    x2 = x.reshape(B * S, D)
    for l in range(ada_w.shape[0]):
        x2 = _layer(l, x2, B, S, c, *[p[l] for p in params])
    return x2.reshape(B, S, D)
```

```python
import functools
import math

import jax
import jax.numpy as jnp
from jax import lax
from jax.experimental import pallas as pl
from jax.experimental.pallas import tpu as pltpu

F32 = jnp.float32
BF16 = jnp.bfloat16
I32 = jnp.int32

ATTN_DK = 64
ATTN_DV = 2 * ATTN_DK
CONV_W = 4
LRU_C = 8.0
N_EXPERTS = 32
TOP_K = 4
SWIGLU_LIMIT = 7.0
SWIGLU_ALPHA = 1.702
MOE_BLOCK = 512
EPS = 1e-6
NEG_BIG = -1e30

IN_PROJ_ROWS = 1024
ATTN_Q_ROWS = 1024
ATTN_KV_ROWS = 512
LRU_CHUNK_ROWS = 1024
ROUTE_TILE_ROWS = 512
OUT_PROJ_TILES = 2
BF16_EXACT_INT = 256.0

LANES = 128
SUBLANES = 8
VMEM_LIMIT = 56 * 1024 * 1024


def _cparams(sem):
    return pltpu.CompilerParams(dimension_semantics=sem, vmem_limit_bytes=VMEM_LIMIT)


def _split_hi_lo(x):
    hi = x.astype(BF16)
    lo = (x - hi.astype(F32)).astype(BF16)
    return hi, lo


def _sigmoid(x):
    return 0.5 * jnp.tanh(0.5 * x) + 0.5


def _dot(a, b):
    return jnp.dot(a, b, preferred_element_type=F32)


def _dot_tb(a, b):
    return lax.dot_general(a, b, (((1,), (1,)), ((), ())), preferred_element_type=F32)


def _adaln_kernel(lam_init, c_ref, w_ref, b_ref, lq1, lk1, lq2, lk2, mod_ref, lam_ref):
    c = c_ref[...]
    s = c * jax.nn.sigmoid(c)
    s_hi, s_lo = _split_hi_lo(s)
    w = w_ref[...]
    w_hi, w_lo = _split_hi_lo(w)
    mod_ref[...] = _dot(s_hi, w_hi) + _dot(s_hi, w_lo) + _dot(s_lo, w_hi) + b_ref[...]
    d1 = jnp.sum(lq1[...] * lk1[...], axis=-1, keepdims=True)
    d2 = jnp.sum(lq2[...] * lk2[...], axis=-1, keepdims=True)
    lam = jnp.exp(d1) - jnp.exp(d2) + lam_init
    lam_ref[...] = jnp.broadcast_to(lam, lam_ref.shape)


def _adaln(c, ada_w, ada_b, lq1, lk1, lq2, lk2, lam_init):
    B, D = c.shape
    n = ada_w.shape[1] // D
    vec = lambda: pl.BlockSpec((1, ATTN_DK), lambda j: (0, 0))
    return pl.pallas_call(
        functools.partial(_adaln_kernel, lam_init),
        grid=(n,),
        in_specs=[
            pl.BlockSpec((B, D), lambda j: (0, 0)),
            pl.BlockSpec((D, D), lambda j: (0, j)),
            pl.BlockSpec((1, D), lambda j: (0, j)),
            vec(), vec(), vec(), vec(),
        ],
        out_specs=[
            pl.BlockSpec((B, D), lambda j: (0, j)),
            pl.BlockSpec((1, LANES), lambda j: (0, 0)),
        ],
        out_shape=[
            jax.ShapeDtypeStruct((B, n * D), F32),
            jax.ShapeDtypeStruct((1, LANES), F32),
        ],
        compiler_params=_cparams(("arbitrary",)),
        name="adaln",
    )(c, ada_w, ada_b.reshape(1, -1), lq1.reshape(1, -1), lk1.reshape(1, -1),
      lq2.reshape(1, -1), lk2.reshape(1, -1))


def _rms_modulate(x, g, shift, scale):
    ms = jnp.mean(x * x, axis=-1, keepdims=True)
    y = x * lax.rsqrt(ms + EPS) * g
    return y * (1.0 + scale) + shift


def _group_rms_scale(q, group_ones):
    ss = _dot((q * q).astype(BF16), group_ones)
    return lax.rsqrt(ss * (1.0 / ATTN_DK) + EPS)


def _in_proj_kernel(qkw, aw, lw, x_ref, mod_ref, g_ref, w_ref, gq_ref, gk_ref, ones_ref,
                    q_ref, k_ref, v_ref, xr_ref, gr_ref):
    h = _rms_modulate(x_ref[...], g_ref[...], mod_ref[0:1, :], mod_ref[1:2, :])
    hb = h.astype(BF16)
    group_ones = ones_ref[...]
    o = 0
    q = _dot(hb, w_ref[:, o:o + qkw]); o += qkw
    q_ref[...] = (q * _group_rms_scale(q, group_ones) * gq_ref[...]).astype(BF16)
    k = _dot(hb, w_ref[:, o:o + qkw]); o += qkw
    k_ref[...] = (k * _group_rms_scale(k, group_ones) * gk_ref[...]).astype(BF16)
    v_ref[...] = _dot(hb, w_ref[:, o:o + aw]).astype(BF16); o += aw
    xr_ref[...] = _dot(hb, w_ref[:, o:o + lw]); o += lw
    gr_ref[...] = _dot(hb, w_ref[:, o:o + lw])


def _in_proj(x2, mod3, norm_g, w_in_b, gq, gk, S, tm, qkw, aw, lw):
    T, D = x2.shape
    tps = S // tm
    grp = jnp.arange(qkw, dtype=I32) // ATTN_DK
    group_ones = (grp[:, None] == grp[None, :]).astype(BF16)
    const = lambda shape: pl.BlockSpec(shape, lambda i: (0,) * len(shape))
    row = lambda w: pl.BlockSpec((tm, w), lambda i: (i, 0))
    return pl.pallas_call(
        functools.partial(_in_proj_kernel, qkw, aw, lw),
        grid=(T // tm,),
        in_specs=[
            row(D),
            pl.BlockSpec((None, 6, D), lambda i: (i // tps, 0, 0)),
            const((1, D)),
            const(w_in_b.shape),
            const((1, qkw)), const((1, qkw)),
            const((qkw, qkw)),
        ],
        out_specs=[row(qkw), row(qkw), row(aw), row(lw), row(lw)],
        out_shape=[
            jax.ShapeDtypeStruct((T, qkw), BF16),
            jax.ShapeDtypeStruct((T, qkw), BF16),
            jax.ShapeDtypeStruct((T, aw), BF16),
            jax.ShapeDtypeStruct((T, lw), F32),
            jax.ShapeDtypeStruct((T, lw), F32),
        ],
        compiler_params=_cparams(("arbitrary",)),
        name="in_proj",
    )(x2, mod3, norm_g, w_in_b, gq, gk, group_ones)


ATTN_LOOP_ROWS = 256
ATTN_DIAG_ROWS = 128


def _attn_kernel(bq, bk, out_scale, q_ref, k_ref, v_ref, lam_ref, g_ref, o_ref,
                 qq_ref, vp_ref, s_ref, sd_ref, m_ref, acc_ref):
    i = pl.program_id(2)
    n_rows = 2 * bq

    @pl.when(i == 0)
    def _():
        col = lax.broadcasted_iota(I32, (vp_ref.shape[0], ATTN_DV), 1)
        vp_ref[:, :ATTN_DV] = v_ref[...]
        vp_ref[:, ATTN_DV:] = jnp.where(col == 0, 1.0, 0.0).astype(BF16)

    q = q_ref[...]
    lane = lax.broadcasted_iota(I32, q.shape, 1)
    zero = jnp.zeros_like(q)
    qq_ref[0:bq, :] = jnp.where(lane < ATTN_DK, q, zero)
    qq_ref[bq:, :] = jnp.where(lane >= ATTN_DK, q, zero)
    m_ref[...] = jnp.full(m_ref.shape, NEG_BIG, F32)
    acc_ref[...] = jnp.zeros_like(acc_ref)

    def scores(rows, start, kw):
        return _dot_tb(qq_ref[rows, :], k_ref[pl.ds(start, kw), :])

    def softmax_pv(s, rows, start):
        kw = s.shape[1]
        m_old = m_ref[rows, :]
        m_new = jnp.maximum(m_old, jnp.max(s, axis=-1, keepdims=True))
        alpha = jnp.exp2(m_old - m_new)
        p = jnp.exp2(s - jnp.tile(m_new, (1, kw // LANES)))
        pv = _dot(p.astype(BF16), vp_ref[pl.ds(start, kw), :])
        acc_ref[rows, :] = acc_ref[rows, :] * jnp.tile(alpha, (1, 2)) + pv
        m_ref[rows, :] = m_new

    loop_chunks = [pl.ds(r, ATTN_LOOP_ROWS) for r in range(0, n_rows, ATTN_LOOP_ROWS)]
    for rows in loop_chunks:
        s_ref[rows, :] = scores(rows, 0, bk)

    def body(j, carry):
        start = pl.multiple_of(j * bk, bk)
        for rows in loop_chunks:
            s = s_ref[rows, :]
            s_ref[rows, :] = scores(rows, start + bk, bk)
            softmax_pv(s, rows, start)
        return carry

    n_full = i * (bq // bk)
    lax.fori_loop(0, n_full, body, 0)

    start = pl.multiple_of(n_full * bk, bk)
    rc = ATTN_DIAG_ROWS
    tri = (lax.broadcasted_iota(I32, (rc, rc), 1) <= lax.broadcasted_iota(I32, (rc, rc), 0))
    diag_chunks = [(pl.ds(r, rc), r % bq) for r in range(0, n_rows, rc)]
    for rows, q0 in diag_chunks:
        if q0 + rc > bk:
            sd_ref[rows, 0:q0 + rc - bk] = scores(rows, start + bk, q0 + rc - bk)
    for rows, q0 in diag_chunks:
        kw = q0 + rc
        parts = [s_ref[rows, 0:min(q0, bk)]] if q0 else []
        if q0 > bk:
            parts.append(sd_ref[rows, 0:q0 - bk])
        last = s_ref[rows, q0:kw] if kw <= bk else sd_ref[rows, q0 - bk:kw - bk]
        parts.append(jnp.where(tri, last, NEG_BIG))
        softmax_pv(parts[0] if len(parts) == 1 else jnp.concatenate(parts, axis=1), rows, start)

    acc = acc_ref[...]
    o = acc[:, :ATTN_DV] / acc[:, ATTN_DV:ATTN_DV + 1]
    a = o[:bq] - lam_ref[0:1, 0:1] * o[bq:]
    ms = jnp.mean(a * a, axis=-1, keepdims=True)
    o_ref[...] = (a * lax.rsqrt(ms + EPS) * g_ref[...] * out_scale).astype(BF16)


def _attention(q, k, v, lam, subln_g, B, S, bq, bk, out_scale):
    T, qkw = q.shape
    H = qkw // ATTN_DV
    nq = S // bq
    return pl.pallas_call(
        functools.partial(_attn_kernel, bq, bk, out_scale),
        grid=(B, H, nq),
        in_specs=[
            pl.BlockSpec((bq, ATTN_DV), lambda b, h, i: (b * nq + i, h)),
            pl.BlockSpec((S, ATTN_DV), lambda b, h, i: (b, h)),
            pl.BlockSpec((S, ATTN_DV), lambda b, h, i: (b, h)),
            pl.BlockSpec((1, LANES), lambda b, h, i: (0, 0)),
            pl.BlockSpec((1, ATTN_DV), lambda b, h, i: (0, 0)),
        ],
        out_specs=pl.BlockSpec((bq, ATTN_DV), lambda b, h, i: (b * nq + i, h)),
        out_shape=jax.ShapeDtypeStruct((T, H * ATTN_DV), BF16),
        scratch_shapes=[
            pltpu.VMEM((2 * bq, ATTN_DV), BF16),
            pltpu.VMEM((S, 2 * ATTN_DV), BF16),
            pltpu.VMEM((2 * bq, bk), F32),
            pltpu.VMEM((2 * bq, max(bq - bk, LANES)), F32),
            pltpu.VMEM((2 * bq, LANES), F32),
            pltpu.VMEM((2 * bq, 2 * ATTN_DV), F32),
        ],
        compiler_params=_cparams(("arbitrary", "arbitrary", "arbitrary")),
        name="attention",
    )(q, k, v, lam, subln_g)


def _lru_kernel(tc, cw, pitch, x_ref, gate_ref, cw_ref, cb_ref, wg_ref, bg_ref, lam_ref, og_ref,
                o_ref, ext_ref, a_ref, b_ref, hc_ref):
    c = pl.program_id(1)

    @pl.when(c == 0)
    def _():
        ext_ref[0:SUBLANES, :] = jnp.zeros((SUBLANES, cw), F32)
        hc_ref[...] = jnp.zeros_like(hc_ref)

    ext_ref[SUBLANES:SUBLANES + tc, :] = x_ref[...]
    xc = cb_ref[...] + cw_ref[CONV_W - 1:CONV_W, :] * x_ref[...]
    for w in range(CONV_W - 1):
        sh = CONV_W - 1 - w
        xc = xc + cw_ref[w:w + 1, :] * ext_ref[SUBLANES - sh:SUBLANES - sh + tc, :]
    tail = ext_ref[tc:tc + SUBLANES, :]
    ext_ref[0:SUBLANES, :] = tail

    g = _dot(xc.astype(BF16), wg_ref[...]) + bg_ref[...]
    t_r = jnp.tanh(g[:, :cw])
    t_i = jnp.tanh(g[:, cw:])
    nl = -lam_ref[...]
    softplus = jnp.maximum(nl, 0.0) + jnp.log1p(jnp.exp(-jnp.abs(nl)))
    c1 = (-0.5 * LRU_C) * softplus
    log_a = c1 * t_r + c1
    th = jnp.tanh(log_a)
    p = -2.0 * th
    root = jnp.where(p > 0.0, p * lax.rsqrt(p * (1.0 - th)), 0.0)
    a = jnp.exp(log_a)
    b = root * ((t_i + 1.0) * (0.5 * xc))

    steps = tc // SUBLANES
    n_lt = cw // LANES
    for q in range(SUBLANES):
        for l in range(n_lt):
            a_ref[l, q * pitch:q * pitch + steps, :] = a[q * steps:(q + 1) * steps, l * LANES:(l + 1) * LANES]
            b_ref[l, q * pitch:q * pitch + steps, :] = b[q * steps:(q + 1) * steps, l * LANES:(l + 1) * LANES]

    def scan_step(t, carry):
        out = []
        for l in range(n_lt):
            h, prod = carry[l]
            rows = pl.ds(t, SUBLANES, stride=pitch)
            a_t = a_ref[l, rows, :]
            h = a_t * h + b_ref[l, rows, :]
            prod = a_t * prod
            b_ref[l, rows, :] = h
            a_ref[l, rows, :] = prod
            out.append((h, prod))
        return tuple(out)

    init = tuple((jnp.zeros((SUBLANES, LANES), F32), jnp.ones((SUBLANES, LANES), F32)) for _ in range(n_lt))
    last = lax.fori_loop(0, steps, scan_step, init, unroll=4)

    h_cols = []
    for l in range(n_lt):
        h_last, p_last = last[l]
        entering = [hc_ref[0:1, l * LANES:(l + 1) * LANES]]
        for q in range(SUBLANES):
            entering.append(h_last[q:q + 1, :] + p_last[q:q + 1, :] * entering[q])
        hc_ref[:, l * LANES:(l + 1) * LANES] = jnp.broadcast_to(entering[SUBLANES], (SUBLANES, LANES))
        h_cols.append(jnp.concatenate(
            [b_ref[l, q * pitch:q * pitch + steps, :] + a_ref[l, q * pitch:q * pitch + steps, :] * entering[q]
             for q in range(SUBLANES)], axis=0))
    h = jnp.concatenate(h_cols, axis=1)

    gt = gate_ref[...]
    gelu = 0.5 * gt * (1.0 + jnp.tanh(math.sqrt(2.0 / math.pi) * (gt + 0.044715 * gt * gt * gt)))
    y = h * gelu
    ms = jnp.mean(y * y, axis=-1, keepdims=True)
    o_ref[...] = (y * lax.rsqrt(ms + EPS) * og_ref[...]).astype(BF16)


def _scan_pitch(steps):
    return steps if (steps // SUBLANES) % 2 else steps + SUBLANES


def _lru(xr, gr, conv_w, conv_b, w_gates, b_gates, lru_lambda, out_g, B, S, tc):
    T, cw = xr.shape
    nc = S // tc
    pitch = _scan_pitch(tc // SUBLANES)
    const = lambda shape: pl.BlockSpec(shape, lambda b, c: (0,) * len(shape))
    row = pl.BlockSpec((tc, cw), lambda b, c: (b * nc + c, 0))
    return pl.pallas_call(
        functools.partial(_lru_kernel, tc, cw, pitch),
        grid=(B, nc),
        in_specs=[row, row, const((CONV_W, cw)), const((1, cw)), const((cw, 2 * cw)),
                  const((1, 2 * cw)), const((1, cw)), const((1, cw))],
        out_specs=row,
        out_shape=jax.ShapeDtypeStruct((T, cw), BF16),
        scratch_shapes=[
            pltpu.VMEM((tc + SUBLANES, cw), F32),
            pltpu.VMEM((cw // LANES, SUBLANES * pitch, LANES), F32),
            pltpu.VMEM((cw // LANES, SUBLANES * pitch, LANES), F32),
            pltpu.VMEM((SUBLANES, cw), F32),
        ],
        compiler_params=_cparams(("arbitrary", "arbitrary")),
        name="lru",
    )(xr, gr, conv_w, conv_b, w_gates, b_gates, lru_lambda, out_g)


def _out_proj_kernel(tm, n_sub, aw, *refs):
    for u in range(n_sub):
        _out_proj_tile(u, tm, aw, *refs)


def _out_proj_tile(u, tm, aw, attn_ref, lru_ref, x_ref, mod_ref, w_ref, g_ref, rw_ref, rb_ref, tri_ref, low_ref,
                   x1_ref, h2_ref, meta_ref, cnt_ref):
    rows = pl.ds(u * tm, tm)
    mix = _dot(attn_ref[rows, :], w_ref[0:aw, :]) + _dot(lru_ref[rows, :], w_ref[aw:, :])
    x1 = x_ref[rows, :] + mod_ref[2:3, :] * mix
    x1_ref[rows, :] = x1
    h2 = _rms_modulate(x1, g_ref[...], mod_ref[3:4, :], mod_ref[4:5, :])
    for j in range(h2.shape[1] // LANES):
        h2_ref[pl.ds(u * tm * SUBLANES + j, tm, stride=SUBLANES), :] = h2[:, j * LANES:(j + 1) * LANES]

    h_hi, h_lo = _split_hi_lo(h2)
    w_hi, w_lo = _split_hi_lo(rw_ref[...])
    by_hi = _dot_tb(jnp.concatenate([w_hi, w_lo], axis=0), h_hi)
    lg = by_hi[:N_EXPERTS] + by_hi[N_EXPERTS:] + _dot_tb(w_hi, h_lo) + rb_ref[...]

    eidx = lax.broadcasted_iota(I32, lg.shape, 0)
    picked = jnp.zeros(lg.shape, F32)
    vals, idxs = [], []
    for _ in range(TOP_K):
        m = jnp.max(lg, axis=0, keepdims=True)
        idx = jnp.min(jnp.where(lg == m, eidx, N_EXPERTS), axis=0, keepdims=True)
        sel = eidx == idx
        lg = jnp.where(sel, -jnp.inf, lg)
        picked = picked + sel.astype(F32)
        vals.append(m)
        idxs.append(idx)
    es = [jnp.exp(v - vals[0]) for v in vals]
    den = es[0] + es[1] + es[2] + es[3]
    gates = jnp.concatenate([e / den for e in es], axis=0)

    cnt = jnp.sum(picked, axis=1, keepdims=True)
    cnt_b = jnp.broadcast_to(cnt, cnt_ref.shape[1:])
    cnt_hi = jnp.floor(cnt_b * (1.0 / BF16_EXACT_INT))
    cnt_lo = cnt_b - BF16_EXACT_INT * cnt_hi
    run_start = BF16_EXACT_INT * _dot(low_ref[...], cnt_hi.astype(BF16)) + _dot(low_ref[...], cnt_lo.astype(BF16))
    before = _dot(picked.astype(BF16), tri_ref[...]) + run_start[:, 0:1]
    pos = [jnp.sum(jnp.where(eidx == idx, before, 0.0), axis=0, keepdims=True) for idx in idxs]
    meta_ref[u, 0:TOP_K, :] = jnp.concatenate(pos, axis=0).astype(I32) * SUBLANES
    meta_ref[u, TOP_K:, :] = lax.bitcast_convert_type(gates, I32)
    cnt_ref[u] = cnt_b.astype(I32)


def _out_proj(attn, lru, x2, mod3, w_out_b, norm_g, router_wt, router_b, S, tm, n_sub):
    T, D = x2.shape
    aw = attn.shape[1]
    lw = lru.shape[1]
    ts = tm * n_sub
    tps = S // ts
    tri = (jnp.arange(tm, dtype=I32)[:, None] < jnp.arange(tm, dtype=I32)[None, :]).astype(BF16)
    experts = jnp.arange(N_EXPERTS, dtype=I32)
    low = (experts[None, :] < experts[:, None]).astype(BF16)
    const = lambda shape: pl.BlockSpec(shape, lambda i: (0,) * len(shape))
    row = lambda w: pl.BlockSpec((ts, w), lambda i: (i, 0))
    return pl.pallas_call(
        functools.partial(_out_proj_kernel, tm, n_sub, aw),
        grid=(T // ts,),
        in_specs=[
            row(aw), row(lw), row(D),
            pl.BlockSpec((None, 6, D), lambda i: (i // tps, 0, 0)),
            const((D, D)), const((1, D)), const((N_EXPERTS, D)), const((N_EXPERTS, 1)),
            const((tm, tm)), const((N_EXPERTS, N_EXPERTS)),
        ],
        out_specs=[
            row(D),
            pl.BlockSpec((ts * SUBLANES, LANES), lambda i: (i, 0)),
            pl.BlockSpec((n_sub, 2 * TOP_K, tm), lambda i: (i, 0, 0)),
            pl.BlockSpec((n_sub, N_EXPERTS, LANES), lambda i: (i, 0, 0)),
        ],
        out_shape=[
            jax.ShapeDtypeStruct((T, D), F32),
            jax.ShapeDtypeStruct((T * SUBLANES, LANES), F32),
            jax.ShapeDtypeStruct((T // tm, 2 * TOP_K, tm), I32),
            jax.ShapeDtypeStruct((T // tm, N_EXPERTS, LANES), I32),
        ],
        compiler_params=_cparams(("arbitrary",)),
        name="out_proj",
    )(attn, lru, x2, mod3, w_out_b, norm_g, router_wt, router_b, tri, low)


def _rows(ref, first_row, n_rows):
    return ref.at[pl.ds(pl.multiple_of(first_row * SUBLANES, SUBLANES), n_rows * SUBLANES), :]


def _row_at(ref, sublane_offset):
    return ref.at[pl.ds(pl.multiple_of(sublane_offset, SUBLANES), SUBLANES), :]


def _fetch_tile(meta_hbm, smems, sems, t):
    i = pl.program_id(0)
    n = pl.num_programs(0)
    slot = i % 2
    per_tile = 2 * TOP_K * t

    def copies(tile, s):
        return [pltpu.make_async_copy(meta_hbm.at[pl.ds(pl.multiple_of(tile * per_tile + j * t, t), t)],
                                      smem.at[pl.ds(pl.multiple_of(s * t, t), t)], sems.at[s])
                for j, smem in enumerate(smems)]

    @pl.when(i == 0)
    def _():
        for c in copies(0, 0):
            c.start()

    for c in copies(i, slot):
        c.wait()

    @pl.when(i + 1 < n)
    def _():
        for c in copies(i + 1, 1 - slot):
            c.start()

    return slot * t


def _run_copies(t, tile, cnt_ref, off_ref, dst_ref, make_copy):
    def expert(e, carry):
        n = cnt_ref[tile * N_EXPERTS + e]
        off = off_ref[tile * N_EXPERTS + e]
        dst = dst_ref[tile * N_EXPERTS + e]
        for b in reversed(range(t.bit_length())):
            size = SUBLANES << b
            done = n & ~(2 * size - 1)

            @pl.when((n & size) != 0)
            def _():
                make_copy(off + done, dst + done, size).start()
        return carry

    lax.fori_loop(0, N_EXPERTS, expert, 0)


def _span(ref, first_sublane, n_sublanes):
    return ref.at[pl.ds(pl.multiple_of(first_sublane, SUBLANES), n_sublanes), :]


def _dispatch_kernel(t, n_blocks, cnt_ref, off_ref, dst_ref, pad_ref, meta_hbm, h_ref, xs_hbm,
                     pos0, pos1, pos2, pos3, stage_ref, zero_ref, psem, sem, zsem):
    i = pl.program_id(0)
    n = pl.num_programs(0)
    slot = i % 2

    def zero_fill(go):
        def expert(e, carry):
            first = pad_ref[e]
            length = pad_ref[N_EXPERTS + e]
            for b in reversed(range((MOE_BLOCK - 1).bit_length())):
                size = SUBLANES << b
                done = length & ~(2 * size - 1)

                @pl.when((length & size) != 0)
                def _():
                    go(pltpu.make_async_copy(_span(zero_ref, 0, size), _span(xs_hbm, first + done, size), zsem))
            return carry

        lax.fori_loop(0, N_EXPERTS, expert, 0)

        def block(b, carry):
            go(pltpu.make_async_copy(zero_ref, _rows(xs_hbm, b * MOE_BLOCK, MOE_BLOCK), zsem))
            return carry

        lax.fori_loop(pad_ref[2 * N_EXPERTS], n_blocks, block, 0)

    @pl.when(i == 0)
    def _():
        zero_ref[...] = jnp.zeros_like(zero_ref)
        zero_fill(lambda c: c.start())

    pos = (pos0, pos1, pos2, pos3)
    base = _fetch_tile(meta_hbm, pos, psem, t)
    stage = stage_ref.at[slot]

    def place(r, carry):
        row = _rows(h_ref, r, 1)[...]
        for k in range(TOP_K):
            _row_at(stage, pos[k][base + r])[...] = row
        return carry

    lax.fori_loop(0, t, place, 0, unroll=8)

    def all_runs(s):
        return pltpu.make_async_copy(stage_ref.at[s], _rows(xs_hbm, 0, TOP_K * t), sem.at[s])

    _run_copies(t, i, cnt_ref, off_ref, dst_ref,
                lambda a, b, m: pltpu.make_async_copy(_span(stage, a, m), _span(xs_hbm, b, m), sem.at[slot]))

    @pl.when(i > 0)
    def _():
        all_runs(1 - slot).wait()

    @pl.when(i == n - 1)
    def _():
        all_runs(slot).wait()
        zero_fill(lambda c: c.wait())


def _dispatch(tile_cnt, tile_off, tile_dst, pad_info, meta, h2t, n_blocks, t):
    n_tiles = meta.shape[0] // (2 * TOP_K * t)
    n_slots = n_blocks * MOE_BLOCK
    grid_spec = pltpu.PrefetchScalarGridSpec(
        num_scalar_prefetch=4,
        grid=(n_tiles,),
        in_specs=[pl.BlockSpec(memory_space=pl.ANY),
                  pl.BlockSpec((t * SUBLANES, LANES), lambda i, *_: (i, 0))],
        out_specs=pl.BlockSpec(memory_space=pl.ANY),
        scratch_shapes=[pltpu.SMEM((2 * t,), I32)] * TOP_K + [
            pltpu.VMEM((2, TOP_K * t * SUBLANES, LANES), F32),
            pltpu.VMEM((MOE_BLOCK * SUBLANES, LANES), F32),
            pltpu.SemaphoreType.DMA((2,)),
            pltpu.SemaphoreType.DMA((2,)),
            pltpu.SemaphoreType.DMA,
        ],
    )
    return pl.pallas_call(
        functools.partial(_dispatch_kernel, t, n_blocks),
        grid_spec=grid_spec,
        out_shape=jax.ShapeDtypeStruct((n_slots * SUBLANES, LANES), F32),
        compiler_params=_cparams(("arbitrary",)),
        name="dispatch",
    )(tile_cnt, tile_off, tile_dst, pad_info, meta, h2t)


def _experts_kernel(ff, be_ref, nv_ref, nxt_ref, xs_ref, wgu_hbm, bgu_ref, wdn_hbm, bdn_ref, ys_ref,
                    wgu_f, wdn_f, wgu_b, wdn_b, sem):
    i = pl.program_id(0)
    nv = nv_ref[i]
    n_slabs = xs_ref.shape[0] // MOE_BLOCK

    def fetch(e):
        return (pltpu.make_async_copy(wgu_hbm.at[e], wgu_f, sem.at[0]),
                pltpu.make_async_copy(wdn_hbm.at[e], wdn_f, sem.at[1]))

    @pl.when(i == 0)
    def _():
        for c in fetch(be_ref[0]):
            c.start()

    @pl.when((i == 0) | (be_ref[i] != be_ref[jnp.maximum(i - 1, 0)]))
    def _():
        for c in fetch(be_ref[i]):
            c.wait()
        wgu_b[...] = wgu_f[...].astype(BF16)
        wdn_b[...] = wdn_f[...].astype(BF16)

        @pl.when(nxt_ref[i] >= 0)
        def _():
            for c in fetch(nxt_ref[i]):
                c.start()

    @pl.when(nv > 0)
    def _():
        x = jnp.concatenate(
            [xs_ref[pl.ds(j, MOE_BLOCK, stride=SUBLANES), :] for j in range(n_slabs)], axis=1)
        gu = _dot(x.astype(BF16), wgu_b[...]) + bgu_ref[...]
        gate = jnp.minimum(gu[:, :ff], SWIGLU_LIMIT)
        up = jnp.clip(gu[:, ff:], -SWIGLU_LIMIT, SWIGLU_LIMIT)
        act = (up + 1.0) * (gate * _sigmoid(SWIGLU_ALPHA * gate))
        y = _dot(act.astype(BF16), wdn_b[...]) + bdn_ref[...]
        for j in range(n_slabs):
            ys_ref[pl.ds(j, MOE_BLOCK, stride=SUBLANES), :] = y[:, j * LANES:(j + 1) * LANES]

    @pl.when(nv == 0)
    def _():
        ys_ref[...] = jnp.zeros_like(ys_ref)


def _experts(block_expert, block_valid, xs, w_gu, b_gu, w_dn, b_dn):
    n_blocks = block_expert.shape[0]
    E, D, ff2 = w_gu.shape
    ff = ff2 // 2
    later = block_expert[None, :] > block_expert[:, None]
    nxt = jnp.min(jnp.where(later, block_expert[None, :], E), axis=1)
    nxt = jnp.where(nxt < E, nxt, -1)
    rows = pl.BlockSpec((MOE_BLOCK * SUBLANES, LANES), lambda i, *_: (i, 0))
    grid_spec = pltpu.PrefetchScalarGridSpec(
        num_scalar_prefetch=3,
        grid=(n_blocks,),
        in_specs=[
            rows,
            pl.BlockSpec(memory_space=pl.ANY),
            pl.BlockSpec((None, 1, ff2), lambda i, be, *_: (be[i], 0, 0)),
            pl.BlockSpec(memory_space=pl.ANY),
            pl.BlockSpec((None, 1, D), lambda i, be, *_: (be[i], 0, 0)),
        ],
        out_specs=rows,
        scratch_shapes=[
            pltpu.VMEM((D, ff2), F32), pltpu.VMEM((ff, D), F32),
            pltpu.VMEM((D, ff2), BF16), pltpu.VMEM((ff, D), BF16),
            pltpu.SemaphoreType.DMA((2,)),
        ],
    )
    return pl.pallas_call(
        functools.partial(_experts_kernel, ff),
        grid_spec=grid_spec,
        out_shape=jax.ShapeDtypeStruct(xs.shape, F32),
        compiler_params=_cparams(("arbitrary",)),
        name="experts",
    )(block_expert, block_valid, nxt.astype(I32), xs, w_gu, b_gu.reshape(E, 1, ff2), w_dn,
      b_dn.reshape(E, 1, D))


def _combine_kernel(t, cnt_ref, off_ref, dst_ref, meta_hbm, ys_hbm, x1_ref, mod_ref, o_ref,
                    pos0, pos1, pos2, pos3, gate0, gate1, gate2, gate3, stage_ref, y_ref, psem, sem):
    i = pl.program_id(0)
    n = pl.num_programs(0)
    slot = i % 2
    pos = (pos0, pos1, pos2, pos3)
    gates = (gate0, gate1, gate2, gate3)
    base = _fetch_tile(meta_hbm, pos + gates, psem, t)

    def fetch_runs(tile, s):
        stage = stage_ref.at[s]
        _run_copies(t, tile, cnt_ref, off_ref, dst_ref,
                    lambda a, b, m: pltpu.make_async_copy(_span(ys_hbm, b, m), _span(stage, a, m), sem.at[s]))

    @pl.when(i == 0)
    def _():
        fetch_runs(0, 0)

    @pl.when(i + 1 < n)
    def _():
        fetch_runs(i + 1, 1 - slot)

    pltpu.make_async_copy(_rows(ys_hbm, 0, TOP_K * t), stage_ref.at[slot], sem.at[slot]).wait()
    stage = stage_ref.at[slot]

    def gather(r, carry):
        acc = None
        for k in range(TOP_K):
            gate = lax.bitcast_convert_type(gates[k][base + r], F32)
            term = gate * _row_at(stage, pos[k][base + r])[...]
            acc = term if acc is None else acc + term
        _rows(y_ref, r, 1)[...] = acc
        return carry

    lax.fori_loop(0, t, gather, 0, unroll=8)

    n_slabs = o_ref.shape[1] // LANES
    y = jnp.concatenate([y_ref[pl.ds(j, t, stride=SUBLANES), :] for j in range(n_slabs)], axis=1)
    o_ref[...] = x1_ref[...] + mod_ref[5:6, :] * y


def _combine(tile_cnt, tile_off, tile_dst, meta, ys, x1, mod3, S, t):
    T, D = x1.shape
    n_tiles = T // t
    tps = S // t
    grid_spec = pltpu.PrefetchScalarGridSpec(
        num_scalar_prefetch=3,
        grid=(n_tiles,),
        in_specs=[
            pl.BlockSpec(memory_space=pl.ANY),
            pl.BlockSpec(memory_space=pl.ANY),
            pl.BlockSpec((t, D), lambda i, *_: (i, 0)),
            pl.BlockSpec((None, 6, D), lambda i, *_: (i // tps, 0, 0)),
        ],
        out_specs=pl.BlockSpec((t, D), lambda i, *_: (i, 0)),
        scratch_shapes=[pltpu.SMEM((2 * t,), I32)] * (2 * TOP_K) + [
            pltpu.VMEM((2, TOP_K * t * SUBLANES, LANES), F32),
            pltpu.VMEM((t * SUBLANES, LANES), F32),
            pltpu.SemaphoreType.DMA((2,)),
            pltpu.SemaphoreType.DMA((2,)),
        ],
    )
    return pl.pallas_call(
        functools.partial(_combine_kernel, t),
        grid_spec=grid_spec,
        out_shape=jax.ShapeDtypeStruct((T, D), F32),
        compiler_params=_cparams(("arbitrary",)),
        name="combine",
    )(tile_cnt, tile_off, tile_dst, meta, ys, x1, mod3)


def _tile(n, target):
    t = min(n, target)
    while n % t:
        t //= 2
    return t


def _block_diag(w):
    n, bw, _ = w.shape
    eye = jnp.eye(n, dtype=w.dtype)
    return (eye[:, None, :, None] * w[:, :, None, :]).reshape(n * bw, n * bw)


def _layer(l, x2, B, S, c, ada_w, ada_b, norm1_g, w_in, q_norm_g, k_norm_g, lambda_q1, lambda_k1, lambda_q2,
           lambda_k2, attn_subln_g, conv_w, conv_b, lru_wa, lru_ba, lru_wx, lru_bx, lru_lambda, lru_out_g,
           w_out, norm2_g, router_w, router_b, w_gate_up, b_gate_up, w_down, b_down):
    T, D = x2.shape
    lam_init = 0.8 - 0.6 * math.exp(-0.3 * l)
    aw = D // 2
    lw = D - aw
    heads = aw // ATTN_DV
    qkw = heads * 2 * ATTN_DK

    mod, lam = _adaln(c, ada_w, ada_b, lambda_q1, lambda_k1, lambda_q2, lambda_k2, lam_init)
    mod3 = mod.reshape(c.shape[0], 6, D)

    reps = qkw // ATTN_DK
    gq = (jnp.tile(q_norm_g, reps) * (ATTN_DK ** -0.5 * math.log2(math.e))).reshape(1, qkw)
    gk = jnp.tile(k_norm_g, reps).reshape(1, qkw)
    tm = _tile(S, ROUTE_TILE_ROWS)
    q, k, v, xr, gr = _in_proj(x2, mod3, norm1_g.reshape(1, D), w_in.astype(BF16), gq, gk, S,
                               _tile(S, IN_PROJ_ROWS), qkw, aw, lw)

    attn = _attention(q, k, v, lam, attn_subln_g.reshape(1, ATTN_DV), B, S, _tile(S, ATTN_Q_ROWS),
                      _tile(S, ATTN_KV_ROWS), 1.0 - lam_init)

    w_gates = (0.5 * jnp.concatenate([_block_diag(lru_wa), _block_diag(lru_wx)], axis=1)).astype(BF16)
    b_gates = 0.5 * jnp.concatenate([lru_ba, lru_bx]).reshape(1, 2 * lw)
    lru = _lru(xr, gr, conv_w, conv_b.reshape(1, lw), w_gates, b_gates, lru_lambda.reshape(1, lw),
               lru_out_g.reshape(1, lw), B, S, _tile(S, LRU_CHUNK_ROWS))

    x1, h2t, meta, tile_cnt = _out_proj(
        attn, lru, x2, mod3, w_out.astype(BF16), norm2_g.reshape(1, D), router_w.T,
        router_b.reshape(N_EXPERTS, 1), S, tm, _tile(S // tm, OUT_PROJ_TILES))

    tile_cnt = tile_cnt[:, :, 0]
    counts = jnp.sum(tile_cnt, axis=0)
    padded = ((counts + MOE_BLOCK - 1) // MOE_BLOCK) * MOE_BLOCK
    pad_ends = jnp.cumsum(padded)
    pad_starts = pad_ends - padded
    n_blocks = (T * TOP_K) // MOE_BLOCK + N_EXPERTS
    blk_start = jnp.arange(n_blocks, dtype=I32) * MOE_BLOCK
    owner = blk_start[:, None] >= pad_ends[None, :]
    block_expert = jnp.minimum(jnp.sum(owner, axis=1), N_EXPERTS - 1).astype(I32)
    onehot = block_expert[:, None] == jnp.arange(N_EXPERTS, dtype=I32)[None, :]
    row_end = jnp.sum(jnp.where(onehot, (pad_starts + counts)[None, :], 0), axis=1)
    block_valid = jnp.clip(row_end - blk_start, 0, MOE_BLOCK).astype(I32)
    tile_off = jnp.cumsum(tile_cnt, axis=1) - tile_cnt
    tile_dst = pad_starts[None, :] + jnp.cumsum(tile_cnt, axis=0) - tile_cnt
    flat = lambda a: (a * SUBLANES).reshape(-1).astype(I32)

    meta = meta.reshape(-1)
    pad_info = jnp.concatenate([(pad_starts + counts) * SUBLANES, (padded - counts) * SUBLANES,
                                pad_ends[-1:] // MOE_BLOCK]).astype(I32)
    xs = _dispatch(flat(tile_cnt), flat(tile_off), flat(tile_dst), pad_info, meta, h2t, n_blocks, tm)
    ys = _experts(block_expert, block_valid, xs, w_gate_up, b_gate_up, w_down, b_down)
    return _combine(flat(tile_cnt), flat(tile_off), flat(tile_dst), meta, ys, x1, mod3, S, tm)


def kernel(x, c, ada_w, ada_b, norm1_g, w_in, q_norm_g, k_norm_g, lambda_q1, lambda_k1, lambda_q2, lambda_k2,
           attn_subln_g, conv_w, conv_b, lru_wa, lru_ba, lru_wx, lru_bx, lru_lambda, lru_out_g, w_out, norm2_g,
           router_w, router_b, w_gate_up, b_gate_up, w_down, b_down):
    B, S, D = x.shape
    params = (ada_w, ada_b, norm1_g, w_in, q_norm_g, k_norm_g, lambda_q1, lambda_k1, lambda_q2, lambda_k2,
              attn_subln_g, conv_w, conv_b, lru_wa, lru_ba, lru_wx, lru_bx, lru_lambda, lru_out_g, w_out,
              norm2_g, router_w, router_b, w_gate_up, b_gate_up, w_down, b_down)
    x2 = x.reshape(B * S, D)
    for l in range(ada_w.shape[0]):
        x2 = _layer(l, x2, B, S, c, *[p[l] for p in params])
    return x2.reshape(B, S, D)
```

```python
import functools
import math

import jax
import jax.numpy as jnp
from jax import lax
from jax.experimental import pallas as pl
from jax.experimental.pallas import tpu as pltpu

F32 = jnp.float32
BF16 = jnp.bfloat16
I32 = jnp.int32

ATTN_DK = 64
ATTN_DV = 2 * ATTN_DK
CONV_W = 4
LRU_C = 8.0
N_EXPERTS = 32
TOP_K = 4
SWIGLU_LIMIT = 7.0
SWIGLU_ALPHA = 1.702
MOE_BLOCK = 512
EPS = 1e-6
NEG_BIG = -1e30

IN_PROJ_ROWS = 1024
ATTN_Q_ROWS = 1024
ATTN_KV_ROWS = 512
LRU_CHUNK_ROWS = 1024
ROUTE_TILE_ROWS = 512
OUT_PROJ_TILES = 2
BF16_EXACT_INT = 256.0

LANES = 128
SUBLANES = 8
MXU_DEPTH = 256
VMEM_LIMIT = 56 * 1024 * 1024


def _cparams(sem):
    return pltpu.CompilerParams(dimension_semantics=sem, vmem_limit_bytes=VMEM_LIMIT)


def _split_hi_lo(x):
    hi = x.astype(BF16)
    lo = (x - hi.astype(F32)).astype(BF16)
    return hi, lo


def _sigmoid(x):
    return 0.5 * jnp.tanh(0.5 * x) + 0.5


def _dot(a, b):
    return jnp.dot(a, b, preferred_element_type=F32)


def _dot_tb(a, b):
    return lax.dot_general(a, b, (((1,), (1,)), ((), ())), preferred_element_type=F32)


def _adaln_kernel(lam_init, c_ref, w_ref, b_ref, lq1, lk1, lq2, lk2, mod_ref, lam_ref):
    c = c_ref[...]
    s = c * jax.nn.sigmoid(c)
    s_hi, s_lo = _split_hi_lo(s)
    w = w_ref[...]
    w_hi, w_lo = _split_hi_lo(w)
    mod_ref[...] = _dot(s_hi, w_hi) + _dot(s_hi, w_lo) + _dot(s_lo, w_hi) + b_ref[...]
    d1 = jnp.sum(lq1[...] * lk1[...], axis=-1, keepdims=True)
    d2 = jnp.sum(lq2[...] * lk2[...], axis=-1, keepdims=True)
    lam = jnp.exp(d1) - jnp.exp(d2) + lam_init
    lam_ref[...] = jnp.broadcast_to(lam, lam_ref.shape)


def _adaln(c, ada_w, ada_b, lq1, lk1, lq2, lk2, lam_init):
    B, D = c.shape
    n = ada_w.shape[1] // D
    vec = lambda: pl.BlockSpec((1, ATTN_DK), lambda j: (0, 0))
    return pl.pallas_call(
        functools.partial(_adaln_kernel, lam_init),
        grid=(n,),
        in_specs=[
            pl.BlockSpec((B, D), lambda j: (0, 0)),
            pl.BlockSpec((D, D), lambda j: (0, j)),
            pl.BlockSpec((1, D), lambda j: (0, j)),
            vec(), vec(), vec(), vec(),
        ],
        out_specs=[
            pl.BlockSpec((B, D), lambda j: (0, j)),
            pl.BlockSpec((1, LANES), lambda j: (0, 0)),
        ],
        out_shape=[
            jax.ShapeDtypeStruct((B, n * D), F32),
            jax.ShapeDtypeStruct((1, LANES), F32),
        ],
        compiler_params=_cparams(("arbitrary",)),
        name="adaln",
    )(c, ada_w, ada_b.reshape(1, -1), lq1.reshape(1, -1), lk1.reshape(1, -1),
      lq2.reshape(1, -1), lk2.reshape(1, -1))


def _rms_modulate(x, g, shift, scale):
    ms = jnp.mean(x * x, axis=-1, keepdims=True)
    y = x * lax.rsqrt(ms + EPS) * g
    return y * (1.0 + scale) + shift


def _group_rms_scale(q, group_ones):
    sq = (q * q).astype(BF16)
    w = group_ones.shape[0]
    ss = jnp.concatenate([_dot(sq[:, c:c + w], group_ones) for c in range(0, q.shape[1], w)], axis=1)
    return lax.rsqrt(ss * (1.0 / ATTN_DK) + EPS)


def _in_proj_kernel(qkw, aw, lw, x_ref, mod_ref, g_ref, w_ref, gq_ref, gk_ref, ones_ref,
                    q_ref, k_ref, v_ref, xr_ref, gr_ref):
    h = _rms_modulate(x_ref[...], g_ref[...], mod_ref[0:1, :], mod_ref[1:2, :])
    hb = h.astype(BF16)
    group_ones = ones_ref[...]
    o = 0
    q = _dot(hb, w_ref[:, o:o + qkw]); o += qkw
    q_ref[...] = (q * _group_rms_scale(q, group_ones) * gq_ref[...]).astype(BF16)
    k = _dot(hb, w_ref[:, o:o + qkw]); o += qkw
    k_ref[...] = (k * _group_rms_scale(k, group_ones) * gk_ref[...]).astype(BF16)
    v_ref[...] = _dot(hb, w_ref[:, o:o + aw]).astype(BF16); o += aw
    xr_ref[...] = _dot(hb, w_ref[:, o:o + lw]); o += lw
    gr_ref[...] = _dot(hb, w_ref[:, o:o + lw])


def _in_proj(x2, mod3, norm_g, w_in_b, gq, gk, S, tm, qkw, aw, lw):
    T, D = x2.shape
    tps = S // tm
    grp = jnp.arange(MXU_DEPTH, dtype=I32) // ATTN_DK
    group_ones = (grp[:, None] == grp[None, :]).astype(BF16)
    const = lambda shape: pl.BlockSpec(shape, lambda i: (0,) * len(shape))
    row = lambda w: pl.BlockSpec((tm, w), lambda i: (i, 0))
    return pl.pallas_call(
        functools.partial(_in_proj_kernel, qkw, aw, lw),
        grid=(T // tm,),
        in_specs=[
            row(D),
            pl.BlockSpec((None, 6, D), lambda i: (i // tps, 0, 0)),
            const((1, D)),
            const(w_in_b.shape),
            const((1, qkw)), const((1, qkw)),
            const((MXU_DEPTH, MXU_DEPTH)),
        ],
        out_specs=[row(qkw), row(qkw), row(aw), row(lw), row(lw)],
        out_shape=[
            jax.ShapeDtypeStruct((T, qkw), BF16),
            jax.ShapeDtypeStruct((T, qkw), BF16),
            jax.ShapeDtypeStruct((T, aw), BF16),
            jax.ShapeDtypeStruct((T, lw), F32),
            jax.ShapeDtypeStruct((T, lw), F32),
        ],
        compiler_params=_cparams(("arbitrary",)),
        name="in_proj",
    )(x2, mod3, norm_g, w_in_b, gq, gk, group_ones)


ATTN_LOOP_ROWS = 256
ATTN_DIAG_ROWS = 128


def _attn_kernel(bq, bk, out_scale, q_ref, k_ref, v_ref, lam_ref, g_ref, o_ref,
                 qq_ref, vp_ref, s_ref, sd_ref, m_ref, acc_ref):
    i = pl.program_id(2)
    n_rows = 2 * bq

    @pl.when(i == 0)
    def _():
        col = lax.broadcasted_iota(I32, (vp_ref.shape[0], ATTN_DV), 1)
        vp_ref[:, :ATTN_DV] = v_ref[...]
        vp_ref[:, ATTN_DV:] = jnp.where(col == 0, 1.0, 0.0).astype(BF16)

    q = q_ref[...]
    lane = lax.broadcasted_iota(I32, q.shape, 1)
    zero = jnp.zeros_like(q)
    qq_ref[0:bq, :] = jnp.where(lane < ATTN_DK, q, zero)
    qq_ref[bq:, :] = jnp.where(lane >= ATTN_DK, q, zero)
    m_ref[...] = jnp.full(m_ref.shape, NEG_BIG, F32)
    acc_ref[...] = jnp.zeros_like(acc_ref)

    def scores(rows, start, kw):
        return _dot_tb(qq_ref[rows, :], k_ref[pl.ds(start, kw), :])

    def softmax_pv(s, rows, start):
        kw = s.shape[1]
        m_old = m_ref[rows, :]
        m_new = jnp.maximum(m_old, jnp.max(s, axis=-1, keepdims=True))
        alpha = jnp.exp2(m_old - m_new)
        p = jnp.exp2(s - jnp.tile(m_new, (1, kw // LANES)))
        pv = _dot(p.astype(BF16), vp_ref[pl.ds(start, kw), :])
        acc_ref[rows, :] = acc_ref[rows, :] * jnp.tile(alpha, (1, 2)) + pv
        m_ref[rows, :] = m_new

    loop_chunks = [pl.ds(r, ATTN_LOOP_ROWS) for r in range(0, n_rows, ATTN_LOOP_ROWS)]
    for rows in loop_chunks:
        s_ref[rows, :] = scores(rows, 0, bk)

    def body(j, carry):
        start = pl.multiple_of(j * bk, bk)
        for rows in loop_chunks:
            s = s_ref[rows, :]
            s_ref[rows, :] = scores(rows, start + bk, bk)
            softmax_pv(s, rows, start)
        return carry

    n_full = i * (bq // bk)
    lax.fori_loop(0, n_full, body, 0)

    start = pl.multiple_of(n_full * bk, bk)
    rc = ATTN_DIAG_ROWS
    tri = (lax.broadcasted_iota(I32, (rc, rc), 1) <= lax.broadcasted_iota(I32, (rc, rc), 0))
    diag_chunks = [(pl.ds(r, rc), r % bq) for r in range(0, n_rows, rc)]
    for rows, q0 in diag_chunks:
        if q0 + rc > bk:
            sd_ref[rows, 0:q0 + rc - bk] = scores(rows, start + bk, q0 + rc - bk)
    for rows, q0 in diag_chunks:
        kw = q0 + rc
        parts = [s_ref[rows, 0:min(q0, bk)]] if q0 else []
        if q0 > bk:
            parts.append(sd_ref[rows, 0:q0 - bk])
        last = s_ref[rows, q0:kw] if kw <= bk else sd_ref[rows, q0 - bk:kw - bk]
        parts.append(jnp.where(tri, last, NEG_BIG))
        softmax_pv(parts[0] if len(parts) == 1 else jnp.concatenate(parts, axis=1), rows, start)

    acc = acc_ref[...]
    o = acc[:, :ATTN_DV] / acc[:, ATTN_DV:ATTN_DV + 1]
    a = o[:bq] - lam_ref[0:1, 0:1] * o[bq:]
    ms = jnp.mean(a * a, axis=-1, keepdims=True)
    o_ref[...] = (a * lax.rsqrt(ms + EPS) * g_ref[...] * out_scale).astype(BF16)


def _attention(q, k, v, lam, subln_g, B, S, bq, bk, out_scale):
    T, qkw = q.shape
    H = qkw // ATTN_DV
    nq = S // bq
    return pl.pallas_call(
        functools.partial(_attn_kernel, bq, bk, out_scale),
        grid=(B, H, nq),
        in_specs=[
            pl.BlockSpec((bq, ATTN_DV), lambda b, h, i: (b * nq + i, h)),
            pl.BlockSpec((S, ATTN_DV), lambda b, h, i: (b, h)),
            pl.BlockSpec((S, ATTN_DV), lambda b, h, i: (b, h)),
            pl.BlockSpec((1, LANES), lambda b, h, i: (0, 0)),
            pl.BlockSpec((1, ATTN_DV), lambda b, h, i: (0, 0)),
        ],
        out_specs=pl.BlockSpec((bq, ATTN_DV), lambda b, h, i: (b * nq + i, h)),
        out_shape=jax.ShapeDtypeStruct((T, H * ATTN_DV), BF16),
        scratch_shapes=[
            pltpu.VMEM((2 * bq, ATTN_DV), BF16),
            pltpu.VMEM((S, 2 * ATTN_DV), BF16),
            pltpu.VMEM((2 * bq, bk), F32),
            pltpu.VMEM((2 * bq, max(bq - bk, LANES)), F32),
            pltpu.VMEM((2 * bq, LANES), F32),
            pltpu.VMEM((2 * bq, 2 * ATTN_DV), F32),
        ],
        compiler_params=_cparams(("arbitrary", "arbitrary", "arbitrary")),
        name="attention",
    )(q, k, v, lam, subln_g)


def _lru_kernel(tc, cw, pitch, x_ref, gate_ref, cw_ref, cb_ref, wg_ref, bg_ref, lam_ref, og_ref,
                o_ref, ext_ref, a_ref, b_ref, hc_ref):
    c = pl.program_id(1)

    @pl.when(c == 0)
    def _():
        ext_ref[0:SUBLANES, :] = jnp.zeros((SUBLANES, cw), F32)
        hc_ref[...] = jnp.zeros_like(hc_ref)

    ext_ref[SUBLANES:SUBLANES + tc, :] = x_ref[...]
    xc = cb_ref[...] + cw_ref[CONV_W - 1:CONV_W, :] * x_ref[...]
    for w in range(CONV_W - 1):
        sh = CONV_W - 1 - w
        xc = xc + cw_ref[w:w + 1, :] * ext_ref[SUBLANES - sh:SUBLANES - sh + tc, :]
    tail = ext_ref[tc:tc + SUBLANES, :]
    ext_ref[0:SUBLANES, :] = tail

    g = _dot(xc.astype(BF16), wg_ref[...]) + bg_ref[...]
    t_r = jnp.tanh(g[:, :cw])
    t_i = jnp.tanh(g[:, cw:])
    nl = -lam_ref[...]
    softplus = jnp.maximum(nl, 0.0) + jnp.log1p(jnp.exp(-jnp.abs(nl)))
    c1 = (-0.5 * LRU_C) * softplus
    log_a = c1 * t_r + c1
    th = jnp.tanh(log_a)
    p = -2.0 * th
    root = jnp.where(p > 0.0, p * lax.rsqrt(p * (1.0 - th)), 0.0)
    a = jnp.exp(log_a)
    b = root * ((t_i + 1.0) * (0.5 * xc))

    steps = tc // SUBLANES
    n_lt = cw // LANES
    for q in range(SUBLANES):
        for l in range(n_lt):
            a_ref[l, q * pitch:q * pitch + steps, :] = a[q * steps:(q + 1) * steps, l * LANES:(l + 1) * LANES]
            b_ref[l, q * pitch:q * pitch + steps, :] = b[q * steps:(q + 1) * steps, l * LANES:(l + 1) * LANES]

    def scan_step(t, carry):
        out = []
        for l in range(n_lt):
            h, prod = carry[l]
            rows = pl.ds(t, SUBLANES, stride=pitch)
            a_t = a_ref[l, rows, :]
            h = a_t * h + b_ref[l, rows, :]
            prod = a_t * prod
            b_ref[l, rows, :] = h
            a_ref[l, rows, :] = prod
            out.append((h, prod))
        return tuple(out)

    init = tuple((jnp.zeros((SUBLANES, LANES), F32), jnp.ones((SUBLANES, LANES), F32)) for _ in range(n_lt))
    last = lax.fori_loop(0, steps, scan_step, init, unroll=4)

    h_cols = []
    for l in range(n_lt):
        h_last, p_last = last[l]
        entering = [hc_ref[0:1, l * LANES:(l + 1) * LANES]]
        for q in range(SUBLANES):
            entering.append(h_last[q:q + 1, :] + p_last[q:q + 1, :] * entering[q])
        hc_ref[:, l * LANES:(l + 1) * LANES] = jnp.broadcast_to(entering[SUBLANES], (SUBLANES, LANES))
        h_cols.append(jnp.concatenate(
            [b_ref[l, q * pitch:q * pitch + steps, :] + a_ref[l, q * pitch:q * pitch + steps, :] * entering[q]
             for q in range(SUBLANES)], axis=0))
    h = jnp.concatenate(h_cols, axis=1)

    gt = gate_ref[...]
    gelu = 0.5 * gt * (1.0 + jnp.tanh(math.sqrt(2.0 / math.pi) * (gt + 0.044715 * gt * gt * gt)))
    y = h * gelu
    ms = jnp.mean(y * y, axis=-1, keepdims=True)
    o_ref[...] = (y * lax.rsqrt(ms + EPS) * og_ref[...]).astype(BF16)


def _scan_pitch(steps):
    return steps if (steps // SUBLANES) % 2 else steps + SUBLANES


def _lru(xr, gr, conv_w, conv_b, w_gates, b_gates, lru_lambda, out_g, B, S, tc):
    T, cw = xr.shape
    nc = S // tc
    pitch = _scan_pitch(tc // SUBLANES)
    const = lambda shape: pl.BlockSpec(shape, lambda b, c: (0,) * len(shape))
    row = pl.BlockSpec((tc, cw), lambda b, c: (b * nc + c, 0))
    return pl.pallas_call(
        functools.partial(_lru_kernel, tc, cw, pitch),
        grid=(B, nc),
        in_specs=[row, row, const((CONV_W, cw)), const((1, cw)), const((cw, 2 * cw)),
                  const((1, 2 * cw)), const((1, cw)), const((1, cw))],
        out_specs=row,
        out_shape=jax.ShapeDtypeStruct((T, cw), BF16),
        scratch_shapes=[
            pltpu.VMEM((tc + SUBLANES, cw), F32),
            pltpu.VMEM((cw // LANES, SUBLANES * pitch, LANES), F32),
            pltpu.VMEM((cw // LANES, SUBLANES * pitch, LANES), F32),
            pltpu.VMEM((SUBLANES, cw), F32),
        ],
        compiler_params=_cparams(("arbitrary", "arbitrary")),
        name="lru",
    )(xr, gr, conv_w, conv_b, w_gates, b_gates, lru_lambda, out_g)


def _out_proj_kernel(tm, n_sub, aw, *refs):
    for u in range(n_sub):
        _out_proj_tile(u, tm, aw, *refs)


def _out_proj_tile(u, tm, aw, attn_ref, lru_ref, x_ref, mod_ref, w_ref, g_ref, rw_ref, rb_ref, tri_ref, low_ref,
                   x1_ref, h2_ref, meta_ref, cnt_ref):
    rows = pl.ds(u * tm, tm)
    mix = _dot(attn_ref[rows, :], w_ref[0:aw, :]) + _dot(lru_ref[rows, :], w_ref[aw:, :])
    x1 = x_ref[rows, :] + mod_ref[2:3, :] * mix
    x1_ref[rows, :] = x1
    h2 = _rms_modulate(x1, g_ref[...], mod_ref[3:4, :], mod_ref[4:5, :])
    for j in range(h2.shape[1] // LANES):
        h2_ref[pl.ds(u * tm * SUBLANES + j, tm, stride=SUBLANES), :] = h2[:, j * LANES:(j + 1) * LANES]

    h_hi, h_lo = _split_hi_lo(h2)
    w_hi, w_lo = _split_hi_lo(rw_ref[...])
    by_hi = _dot_tb(jnp.concatenate([w_hi, w_lo], axis=0), h_hi)
    lg = by_hi[:N_EXPERTS] + by_hi[N_EXPERTS:] + _dot_tb(w_hi, h_lo) + rb_ref[...]

    eidx = lax.broadcasted_iota(I32, lg.shape, 0)
    picked = jnp.zeros(lg.shape, F32)
    vals, idxs = [], []
    for _ in range(TOP_K):
        m = jnp.max(lg, axis=0, keepdims=True)
        idx = jnp.min(jnp.where(lg == m, eidx, N_EXPERTS), axis=0, keepdims=True)
        sel = eidx == idx
        lg = jnp.where(sel, -jnp.inf, lg)
        picked = picked + sel.astype(F32)
        vals.append(m)
        idxs.append(idx)
    es = [jnp.exp(v - vals[0]) for v in vals]
    den = es[0] + es[1] + es[2] + es[3]
    gates = jnp.concatenate([e / den for e in es], axis=0)

    cnt = jnp.sum(picked, axis=1, keepdims=True)
    cnt_b = jnp.broadcast_to(cnt, cnt_ref.shape[1:])
    cnt_hi = jnp.floor(cnt_b * (1.0 / BF16_EXACT_INT))
    cnt_lo = cnt_b - BF16_EXACT_INT * cnt_hi
    run_start = BF16_EXACT_INT * _dot(low_ref[...], cnt_hi.astype(BF16)) + _dot(low_ref[...], cnt_lo.astype(BF16))
    before = _dot(picked.astype(BF16), tri_ref[...]) + run_start[:, 0:1]
    pos = [jnp.sum(jnp.where(eidx == idx, before, 0.0), axis=0, keepdims=True) for idx in idxs]
    meta_ref[u, 0:TOP_K, :] = jnp.concatenate(pos, axis=0).astype(I32) * SUBLANES
    meta_ref[u, TOP_K:, :] = lax.bitcast_convert_type(gates, I32)
    cnt_ref[u] = cnt_b.astype(I32)


def _out_proj(attn, lru, x2, mod3, w_out_b, norm_g, router_wt, router_b, S, tm, n_sub):
    T, D = x2.shape
    aw = attn.shape[1]
    lw = lru.shape[1]
    ts = tm * n_sub
    tps = S // ts
    tri = (jnp.arange(tm, dtype=I32)[:, None] < jnp.arange(tm, dtype=I32)[None, :]).astype(BF16)
    experts = jnp.arange(N_EXPERTS, dtype=I32)
    low = (experts[None, :] < experts[:, None]).astype(BF16)
    const = lambda shape: pl.BlockSpec(shape, lambda i: (0,) * len(shape))
    row = lambda w: pl.BlockSpec((ts, w), lambda i: (i, 0))
    return pl.pallas_call(
        functools.partial(_out_proj_kernel, tm, n_sub, aw),
        grid=(T // ts,),
        in_specs=[
            row(aw), row(lw), row(D),
            pl.BlockSpec((None, 6, D), lambda i: (i // tps, 0, 0)),
            const((D, D)), const((1, D)), const((N_EXPERTS, D)), const((N_EXPERTS, 1)),
            const((tm, tm)), const((N_EXPERTS, N_EXPERTS)),
        ],
        out_specs=[
            row(D),
            pl.BlockSpec((ts * SUBLANES, LANES), lambda i: (i, 0)),
            pl.BlockSpec((n_sub, 2 * TOP_K, tm), lambda i: (i, 0, 0)),
            pl.BlockSpec((n_sub, N_EXPERTS, LANES), lambda i: (i, 0, 0)),
        ],
        out_shape=[
            jax.ShapeDtypeStruct((T, D), F32),
            jax.ShapeDtypeStruct((T * SUBLANES, LANES), F32),
            jax.ShapeDtypeStruct((T // tm, 2 * TOP_K, tm), I32),
            jax.ShapeDtypeStruct((T // tm, N_EXPERTS, LANES), I32),
        ],
        compiler_params=_cparams(("arbitrary",)),
        name="out_proj",
    )(attn, lru, x2, mod3, w_out_b, norm_g, router_wt, router_b, tri, low)


def _rows(ref, first_row, n_rows):
    return ref.at[pl.ds(pl.multiple_of(first_row * SUBLANES, SUBLANES), n_rows * SUBLANES), :]


def _row_at(ref, sublane_offset):
    return ref.at[pl.ds(pl.multiple_of(sublane_offset, SUBLANES), SUBLANES), :]


def _fetch_tile(meta_hbm, smems, sems, t):
    i = pl.program_id(0)
    n = pl.num_programs(0)
    slot = i % 2
    per_tile = 2 * TOP_K * t

    def copies(tile, s):
        return [pltpu.make_async_copy(meta_hbm.at[pl.ds(pl.multiple_of(tile * per_tile + j * t, t), t)],
                                      smem.at[pl.ds(pl.multiple_of(s * t, t), t)], sems.at[s])
                for j, smem in enumerate(smems)]

    @pl.when(i == 0)
    def _():
        for c in copies(0, 0):
            c.start()

    for c in copies(i, slot):
        c.wait()

    @pl.when(i + 1 < n)
    def _():
        for c in copies(i + 1, 1 - slot):
            c.start()

    return slot * t


def _run_copies(t, tile, cnt_ref, off_ref, dst_ref, make_copy):
    def expert(e, carry):
        n = cnt_ref[tile * N_EXPERTS + e]
        off = off_ref[tile * N_EXPERTS + e]
        dst = dst_ref[tile * N_EXPERTS + e]
        for b in reversed(range(t.bit_length())):
            size = SUBLANES << b
            done = n & ~(2 * size - 1)

            @pl.when((n & size) != 0)
            def _():
                make_copy(off + done, dst + done, size).start()
        return carry

    lax.fori_loop(0, N_EXPERTS, expert, 0)


def _span(ref, first_sublane, n_sublanes):
    return ref.at[pl.ds(pl.multiple_of(first_sublane, SUBLANES), n_sublanes), :]


def _dispatch_kernel(t, n_blocks, cnt_ref, off_ref, dst_ref, pad_ref, meta_hbm, h_ref, xs_hbm,
                     pos0, pos1, pos2, pos3, stage_ref, zero_ref, psem, sem, zsem):
    i = pl.program_id(0)
    n = pl.num_programs(0)
    slot = i % 2

    def zero_fill(go):
        def expert(e, carry):
            first = pad_ref[e]
            length = pad_ref[N_EXPERTS + e]
            for b in reversed(range((MOE_BLOCK - 1).bit_length())):
                size = SUBLANES << b
                done = length & ~(2 * size - 1)

                @pl.when((length & size) != 0)
                def _():
                    go(pltpu.make_async_copy(_span(zero_ref, 0, size), _span(xs_hbm, first + done, size), zsem))
            return carry

        lax.fori_loop(0, N_EXPERTS, expert, 0)

        def block(b, carry):
            go(pltpu.make_async_copy(zero_ref, _rows(xs_hbm, b * MOE_BLOCK, MOE_BLOCK), zsem))
            return carry

        lax.fori_loop(pad_ref[2 * N_EXPERTS], n_blocks, block, 0)

    @pl.when(i == 0)
    def _():
        zero_ref[...] = jnp.zeros_like(zero_ref)
        zero_fill(lambda c: c.start())

    pos = (pos0, pos1, pos2, pos3)
    base = _fetch_tile(meta_hbm, pos, psem, t)
    stage = stage_ref.at[slot]

    def place(r, carry):
        row = _rows(h_ref, r, 1)[...]
        for k in range(TOP_K):
            _row_at(stage, pos[k][base + r])[...] = row
        return carry

    lax.fori_loop(0, t, place, 0, unroll=8)

    def all_runs(s):
        return pltpu.make_async_copy(stage_ref.at[s], _rows(xs_hbm, 0, TOP_K * t), sem.at[s])

    _run_copies(t, i, cnt_ref, off_ref, dst_ref,
                lambda a, b, m: pltpu.make_async_copy(_span(stage, a, m), _span(xs_hbm, b, m), sem.at[slot]))

    @pl.when(i > 0)
    def _():
        all_runs(1 - slot).wait()

    @pl.when(i == n - 1)
    def _():
        all_runs(slot).wait()
        zero_fill(lambda c: c.wait())


def _dispatch(tile_cnt, tile_off, tile_dst, pad_info, meta, h2t, n_blocks, t):
    n_tiles = meta.shape[0] // (2 * TOP_K * t)
    n_slots = n_blocks * MOE_BLOCK
    grid_spec = pltpu.PrefetchScalarGridSpec(
        num_scalar_prefetch=4,
        grid=(n_tiles,),
        in_specs=[pl.BlockSpec(memory_space=pl.ANY),
                  pl.BlockSpec((t * SUBLANES, LANES), lambda i, *_: (i, 0))],
        out_specs=pl.BlockSpec(memory_space=pl.ANY),
        scratch_shapes=[pltpu.SMEM((2 * t,), I32)] * TOP_K + [
            pltpu.VMEM((2, TOP_K * t * SUBLANES, LANES), F32),
            pltpu.VMEM((MOE_BLOCK * SUBLANES, LANES), F32),
            pltpu.SemaphoreType.DMA((2,)),
            pltpu.SemaphoreType.DMA((2,)),
            pltpu.SemaphoreType.DMA,
        ],
    )
    return pl.pallas_call(
        functools.partial(_dispatch_kernel, t, n_blocks),
        grid_spec=grid_spec,
        out_shape=jax.ShapeDtypeStruct((n_slots * SUBLANES, LANES), F32),
        compiler_params=_cparams(("arbitrary",)),
        name="dispatch",
    )(tile_cnt, tile_off, tile_dst, pad_info, meta, h2t)


def _experts_kernel(ff, be_ref, nv_ref, nxt_ref, xs_ref, wgu_hbm, bgu_ref, wdn_hbm, bdn_ref, ys_ref,
                    wgu_f, wdn_f, wgu_b, wdn_b, sem):
    i = pl.program_id(0)
    nv = nv_ref[i]
    n_slabs = xs_ref.shape[0] // MOE_BLOCK

    def fetch(e):
        return (pltpu.make_async_copy(wgu_hbm.at[e], wgu_f, sem.at[0]),
                pltpu.make_async_copy(wdn_hbm.at[e], wdn_f, sem.at[1]))

    @pl.when(i == 0)
    def _():
        for c in fetch(be_ref[0]):
            c.start()

    @pl.when((i == 0) | (be_ref[i] != be_ref[jnp.maximum(i - 1, 0)]))
    def _():
        for c in fetch(be_ref[i]):
            c.wait()
        wgu_b[...] = wgu_f[...].astype(BF16)
        wdn_b[...] = wdn_f[...].astype(BF16)

        @pl.when(nxt_ref[i] >= 0)
        def _():
            for c in fetch(nxt_ref[i]):
                c.start()

    @pl.when(nv > 0)
    def _():
        x = jnp.concatenate(
            [xs_ref[pl.ds(j, MOE_BLOCK, stride=SUBLANES), :] for j in range(n_slabs)], axis=1)
        gu = _dot(x.astype(BF16), wgu_b[...]) + bgu_ref[...]
        gate = jnp.minimum(gu[:, :ff], SWIGLU_LIMIT)
        up = jnp.clip(gu[:, ff:], -SWIGLU_LIMIT, SWIGLU_LIMIT)
        act = (up + 1.0) * (gate * _sigmoid(SWIGLU_ALPHA * gate))
        y = _dot(act.astype(BF16), wdn_b[...]) + bdn_ref[...]
        for j in range(n_slabs):
            ys_ref[pl.ds(j, MOE_BLOCK, stride=SUBLANES), :] = y[:, j * LANES:(j + 1) * LANES]

    @pl.when(nv == 0)
    def _():
        ys_ref[...] = jnp.zeros_like(ys_ref)


def _experts(block_expert, block_valid, xs, w_gu, b_gu, w_dn, b_dn):
    n_blocks = block_expert.shape[0]
    E, D, ff2 = w_gu.shape
    ff = ff2 // 2
    later = block_expert[None, :] > block_expert[:, None]
    nxt = jnp.min(jnp.where(later, block_expert[None, :], E), axis=1)
    nxt = jnp.where(nxt < E, nxt, -1)
    rows = pl.BlockSpec((MOE_BLOCK * SUBLANES, LANES), lambda i, *_: (i, 0))
    grid_spec = pltpu.PrefetchScalarGridSpec(
        num_scalar_prefetch=3,
        grid=(n_blocks,),
        in_specs=[
            rows,
            pl.BlockSpec(memory_space=pl.ANY),
            pl.BlockSpec((None, 1, ff2), lambda i, be, *_: (be[i], 0, 0)),
            pl.BlockSpec(memory_space=pl.ANY),
            pl.BlockSpec((None, 1, D), lambda i, be, *_: (be[i], 0, 0)),
        ],
        out_specs=rows,
        scratch_shapes=[
            pltpu.VMEM((D, ff2), F32), pltpu.VMEM((ff, D), F32),
            pltpu.VMEM((D, ff2), BF16), pltpu.VMEM((ff, D), BF16),
            pltpu.SemaphoreType.DMA((2,)),
        ],
    )
    return pl.pallas_call(
        functools.partial(_experts_kernel, ff),
        grid_spec=grid_spec,
        out_shape=jax.ShapeDtypeStruct(xs.shape, F32),
        compiler_params=_cparams(("arbitrary",)),
        name="experts",
    )(block_expert, block_valid, nxt.astype(I32), xs, w_gu, b_gu.reshape(E, 1, ff2), w_dn,
      b_dn.reshape(E, 1, D))


def _combine_kernel(t, cnt_ref, off_ref, dst_ref, meta_hbm, ys_hbm, x1_ref, mod_ref, o_ref,
                    pos0, pos1, pos2, pos3, gate0, gate1, gate2, gate3, stage_ref, y_ref, psem, sem):
    i = pl.program_id(0)
    n = pl.num_programs(0)
    slot = i % 2
    pos = (pos0, pos1, pos2, pos3)
    gates = (gate0, gate1, gate2, gate3)
    base = _fetch_tile(meta_hbm, pos + gates, psem, t)

    def fetch_runs(tile, s):
        stage = stage_ref.at[s]
        _run_copies(t, tile, cnt_ref, off_ref, dst_ref,
                    lambda a, b, m: pltpu.make_async_copy(_span(ys_hbm, b, m), _span(stage, a, m), sem.at[s]))

    @pl.when(i == 0)
    def _():
        fetch_runs(0, 0)

    @pl.when(i + 1 < n)
    def _():
        fetch_runs(i + 1, 1 - slot)

    pltpu.make_async_copy(_rows(ys_hbm, 0, TOP_K * t), stage_ref.at[slot], sem.at[slot]).wait()
    stage = stage_ref.at[slot]

    def gather(r, carry):
        acc = None
        for k in range(TOP_K):
            gate = lax.bitcast_convert_type(gates[k][base + r], F32)
            term = gate * _row_at(stage, pos[k][base + r])[...]
            acc = term if acc is None else acc + term
        _rows(y_ref, r, 1)[...] = acc
        return carry

    lax.fori_loop(0, t, gather, 0, unroll=8)

    n_slabs = o_ref.shape[1] // LANES
    y = jnp.concatenate([y_ref[pl.ds(j, t, stride=SUBLANES), :] for j in range(n_slabs)], axis=1)
    o_ref[...] = x1_ref[...] + mod_ref[5:6, :] * y


def _combine(tile_cnt, tile_off, tile_dst, meta, ys, x1, mod3, S, t):
    T, D = x1.shape
    n_tiles = T // t
    tps = S // t
    grid_spec = pltpu.PrefetchScalarGridSpec(
        num_scalar_prefetch=3,
        grid=(n_tiles,),
        in_specs=[
            pl.BlockSpec(memory_space=pl.ANY),
            pl.BlockSpec(memory_space=pl.ANY),
            pl.BlockSpec((t, D), lambda i, *_: (i, 0)),
            pl.BlockSpec((None, 6, D), lambda i, *_: (i // tps, 0, 0)),
        ],
        out_specs=pl.BlockSpec((t, D), lambda i, *_: (i, 0)),
        scratch_shapes=[pltpu.SMEM((2 * t,), I32)] * (2 * TOP_K) + [
            pltpu.VMEM((2, TOP_K * t * SUBLANES, LANES), F32),
            pltpu.VMEM((t * SUBLANES, LANES), F32),
            pltpu.SemaphoreType.DMA((2,)),
            pltpu.SemaphoreType.DMA((2,)),
        ],
    )
    return pl.pallas_call(
        functools.partial(_combine_kernel, t),
        grid_spec=grid_spec,
        out_shape=jax.ShapeDtypeStruct((T, D), F32),
        compiler_params=_cparams(("arbitrary",)),
        name="combine",
    )(tile_cnt, tile_off, tile_dst, meta, ys, x1, mod3)


def _tile(n, target):
    t = min(n, target)
    while n % t:
        t //= 2
    return t


def _block_diag(w):
    n, bw, _ = w.shape
    eye = jnp.eye(n, dtype=w.dtype)
    return (eye[:, None, :, None] * w[:, :, None, :]).reshape(n * bw, n * bw)


def _layer(l, x2, B, S, c, ada_w, ada_b, norm1_g, w_in, q_norm_g, k_norm_g, lambda_q1, lambda_k1, lambda_q2,
           lambda_k2, attn_subln_g, conv_w, conv_b, lru_wa, lru_ba, lru_wx, lru_bx, lru_lambda, lru_out_g,
           w_out, norm2_g, router_w, router_b, w_gate_up, b_gate_up, w_down, b_down):
    T, D = x2.shape
    lam_init = 0.8 - 0.6 * math.exp(-0.3 * l)
    aw = D // 2
    lw = D - aw
    heads = aw // ATTN_DV
    qkw = heads * 2 * ATTN_DK

    mod, lam = _adaln(c, ada_w, ada_b, lambda_q1, lambda_k1, lambda_q2, lambda_k2, lam_init)
    mod3 = mod.reshape(c.shape[0], 6, D)

    reps = qkw // ATTN_DK
    gq = (jnp.tile(q_norm_g, reps) * (ATTN_DK ** -0.5 * math.log2(math.e))).reshape(1, qkw)
    gk = jnp.tile(k_norm_g, reps).reshape(1, qkw)
    tm = _tile(S, ROUTE_TILE_ROWS)
    q, k, v, xr, gr = _in_proj(x2, mod3, norm1_g.reshape(1, D), w_in.astype(BF16), gq, gk, S,
                               _tile(S, IN_PROJ_ROWS), qkw, aw, lw)

    attn = _attention(q, k, v, lam, attn_subln_g.reshape(1, ATTN_DV), B, S, _tile(S, ATTN_Q_ROWS),
                      _tile(S, ATTN_KV_ROWS), 1.0 - lam_init)

    w_gates = (0.5 * jnp.concatenate([_block_diag(lru_wa), _block_diag(lru_wx)], axis=1)).astype(BF16)
    b_gates = 0.5 * jnp.concatenate([lru_ba, lru_bx]).reshape(1, 2 * lw)
    lru = _lru(xr, gr, conv_w, conv_b.reshape(1, lw), w_gates, b_gates, lru_lambda.reshape(1, lw),
               lru_out_g.reshape(1, lw), B, S, _tile(S, LRU_CHUNK_ROWS))

    x1, h2t, meta, tile_cnt = _out_proj(
        attn, lru, x2, mod3, w_out.astype(BF16), norm2_g.reshape(1, D), router_w.T,
        router_b.reshape(N_EXPERTS, 1), S, tm, _tile(S // tm, OUT_PROJ_TILES))

    tile_cnt = tile_cnt[:, :, 0]
    counts = jnp.sum(tile_cnt, axis=0)
    padded = ((counts + MOE_BLOCK - 1) // MOE_BLOCK) * MOE_BLOCK
    pad_ends = jnp.cumsum(padded)
    pad_starts = pad_ends - padded
    n_blocks = (T * TOP_K) // MOE_BLOCK + N_EXPERTS
    blk_start = jnp.arange(n_blocks, dtype=I32) * MOE_BLOCK
    owner = blk_start[:, None] >= pad_ends[None, :]
    block_expert = jnp.minimum(jnp.sum(owner, axis=1), N_EXPERTS - 1).astype(I32)
    onehot = block_expert[:, None] == jnp.arange(N_EXPERTS, dtype=I32)[None, :]
    row_end = jnp.sum(jnp.where(onehot, (pad_starts + counts)[None, :], 0), axis=1)
    block_valid = jnp.clip(row_end - blk_start, 0, MOE_BLOCK).astype(I32)
    tile_off = jnp.cumsum(tile_cnt, axis=1) - tile_cnt
    tile_dst = pad_starts[None, :] + jnp.cumsum(tile_cnt, axis=0) - tile_cnt
    flat = lambda a: (a * SUBLANES).reshape(-1).astype(I32)

    meta = meta.reshape(-1)
    pad_info = jnp.concatenate([(pad_starts + counts) * SUBLANES, (padded - counts) * SUBLANES,
                                pad_ends[-1:] // MOE_BLOCK]).astype(I32)
    xs = _dispatch(flat(tile_cnt), flat(tile_off), flat(tile_dst), pad_info, meta, h2t, n_blocks, tm)
    ys = _experts(block_expert, block_valid, xs, w_gate_up, b_gate_up, w_down, b_down)
    return _combine(flat(tile_cnt), flat(tile_off), flat(tile_dst), meta, ys, x1, mod3, S, tm)


def kernel(x, c, ada_w, ada_b, norm1_g, w_in, q_norm_g, k_norm_g, lambda_q1, lambda_k1, lambda_q2, lambda_k2,
           attn_subln_g, conv_w, conv_b, lru_wa, lru_ba, lru_wx, lru_bx, lru_lambda, lru_out_g, w_out, norm2_g,
           router_w, router_b, w_gate_up, b_gate_up, w_down, b_down):
    B, S, D = x.shape
    params = (ada_w, ada_b, norm1_g, w_in, q_norm_g, k_norm_g, lambda_q1, lambda_k1, lambda_q2, lambda_k2,
              attn_subln_g, conv_w, conv_b, lru_wa, lru_ba, lru_wx, lru_bx, lru_lambda, lru_out_g, w_out,
              norm2_g, router_w, router_b, w_gate_up, b_gate_up, w_down, b_down)
    x2 = x.reshape(B * S, D)
    for l in range(ada_w.shape[0]):
        x2 = _layer(l, x2, B, S, c, *[p[l] for p in params])
    return x2.reshape(B, S, D)
```

```python
import functools
import math

import jax
import jax.numpy as jnp
from jax import lax
from jax.experimental import pallas as pl
from jax.experimental.pallas import tpu as pltpu

F32 = jnp.float32
BF16 = jnp.bfloat16
I32 = jnp.int32

ATTN_DK = 64
ATTN_DV = 2 * ATTN_DK
CONV_W = 4
LRU_C = 8.0
N_EXPERTS = 32
TOP_K = 4
SWIGLU_LIMIT = 7.0
SWIGLU_ALPHA = 1.702
MOE_BLOCK = 512
EPS = 1e-6
NEG_BIG = -1e30

IN_PROJ_ROWS = 1024
ATTN_Q_ROWS = 1024
ATTN_KV_ROWS = 512
LRU_CHUNK_ROWS = 1024
ROUTE_TILE_ROWS = 512
OUT_PROJ_TILES = 2
BF16_EXACT_INT = 256.0

LANES = 128
SUBLANES = 8
MXU_DEPTH = 256
VMEM_LIMIT = 40 * 1024 * 1024


def _cparams(sem):
    return pltpu.CompilerParams(dimension_semantics=sem, vmem_limit_bytes=VMEM_LIMIT)


def _split_hi_lo(x):
    hi = x.astype(BF16)
    lo = (x - hi.astype(F32)).astype(BF16)
    return hi, lo


def _sigmoid(x):
    return 0.5 * jnp.tanh(0.5 * x) + 0.5


def _dot(a, b):
    return jnp.dot(a, b, preferred_element_type=F32)


def _dot_tb(a, b):
    return lax.dot_general(a, b, (((1,), (1,)), ((), ())), preferred_element_type=F32)


def _adaln_kernel(lam_init, c_ref, w_ref, b_ref, lq1, lk1, lq2, lk2, mod_ref, lam_ref):
    c = c_ref[...]
    s = c * jax.nn.sigmoid(c)
    s_hi, s_lo = _split_hi_lo(s)
    w = w_ref[...]
    w_hi, w_lo = _split_hi_lo(w)
    mod_ref[...] = _dot(s_hi, w_hi) + _dot(s_hi, w_lo) + _dot(s_lo, w_hi) + b_ref[...]
    d1 = jnp.sum(lq1[...] * lk1[...], axis=-1, keepdims=True)
    d2 = jnp.sum(lq2[...] * lk2[...], axis=-1, keepdims=True)
    lam = jnp.exp(d1) - jnp.exp(d2) + lam_init
    lam_ref[...] = jnp.broadcast_to(lam, lam_ref.shape)


def _adaln(c, ada_w, ada_b, lq1, lk1, lq2, lk2, lam_init):
    B, D = c.shape
    n = ada_w.shape[1] // D
    vec = lambda: pl.BlockSpec((1, ATTN_DK), lambda j: (0, 0))
    return pl.pallas_call(
        functools.partial(_adaln_kernel, lam_init),
        grid=(n,),
        in_specs=[
            pl.BlockSpec((B, D), lambda j: (0, 0)),
            pl.BlockSpec((D, D), lambda j: (0, j)),
            pl.BlockSpec((1, D), lambda j: (0, j)),
            vec(), vec(), vec(), vec(),
        ],
        out_specs=[
            pl.BlockSpec((B, D), lambda j: (0, j)),
            pl.BlockSpec((1, LANES), lambda j: (0, 0)),
        ],
        out_shape=[
            jax.ShapeDtypeStruct((B, n * D), F32),
            jax.ShapeDtypeStruct((1, LANES), F32),
        ],
        compiler_params=_cparams(("arbitrary",)),
        name="adaln",
    )(c, ada_w, ada_b.reshape(1, -1), lq1.reshape(1, -1), lk1.reshape(1, -1),
      lq2.reshape(1, -1), lk2.reshape(1, -1))


def _rms_modulate(x, g, shift, scale):
    ms = jnp.mean(x * x, axis=-1, keepdims=True)
    y = x * lax.rsqrt(ms + EPS) * g
    return y * (1.0 + scale) + shift


def _group_rms_scale(q, group_ones):
    sq = (q * q).astype(BF16)
    w = group_ones.shape[0]
    ss = jnp.concatenate([_dot(sq[:, c:c + w], group_ones) for c in range(0, q.shape[1], w)], axis=1)
    return lax.rsqrt(ss * (1.0 / ATTN_DK) + EPS)


def _in_proj_kernel(qkw, aw, lw, x_ref, mod_ref, g_ref, w_ref, gq_ref, gk_ref, ones_ref,
                    q_ref, k_ref, v_ref, xr_ref, gr_ref):
    h = _rms_modulate(x_ref[...], g_ref[...], mod_ref[0:1, :], mod_ref[1:2, :])
    hb = h.astype(BF16)
    group_ones = ones_ref[...]
    o = 0
    q = _dot(hb, w_ref[:, o:o + qkw]); o += qkw
    q_ref[...] = (q * _group_rms_scale(q, group_ones) * gq_ref[...]).astype(BF16)
    k = _dot(hb, w_ref[:, o:o + qkw]); o += qkw
    k_ref[...] = (k * _group_rms_scale(k, group_ones) * gk_ref[...]).astype(BF16)
    v_ref[...] = _dot(hb, w_ref[:, o:o + aw]).astype(BF16); o += aw
    xr_ref[...] = _dot(hb, w_ref[:, o:o + lw]); o += lw
    gr_ref[...] = _dot(hb, w_ref[:, o:o + lw])


def _in_proj(x2, mod3, norm_g, w_in_b, gq, gk, S, tm, qkw, aw, lw):
    T, D = x2.shape
    tps = S // tm
    grp = jnp.arange(MXU_DEPTH, dtype=I32) // ATTN_DK
    group_ones = (grp[:, None] == grp[None, :]).astype(BF16)
    const = lambda shape: pl.BlockSpec(shape, lambda i: (0,) * len(shape))
    row = lambda w: pl.BlockSpec((tm, w), lambda i: (i, 0))
    return pl.pallas_call(
        functools.partial(_in_proj_kernel, qkw, aw, lw),
        grid=(T // tm,),
        in_specs=[
            row(D),
            pl.BlockSpec((None, 6, D), lambda i: (i // tps, 0, 0)),
            const((1, D)),
            const(w_in_b.shape),
            const((1, qkw)), const((1, qkw)),
            const((MXU_DEPTH, MXU_DEPTH)),
        ],
        out_specs=[row(qkw), row(qkw), row(aw), row(lw), row(lw)],
        out_shape=[
            jax.ShapeDtypeStruct((T, qkw), BF16),
            jax.ShapeDtypeStruct((T, qkw), BF16),
            jax.ShapeDtypeStruct((T, aw), BF16),
            jax.ShapeDtypeStruct((T, lw), F32),
            jax.ShapeDtypeStruct((T, lw), F32),
        ],
        compiler_params=_cparams(("arbitrary",)),
        name="in_proj",
    )(x2, mod3, norm_g, w_in_b, gq, gk, group_ones)


ATTN_LOOP_ROWS = 256
ATTN_DIAG_ROWS = 128


def _attn_kernel(bq, bk, out_scale, q_ref, k_ref, v_ref, lam_ref, g_ref, o_ref,
                 qq_ref, vp_ref, s_ref, sd_ref, m_ref, acc_ref):
    i = pl.program_id(2)
    n_rows = 2 * bq

    @pl.when(i == 0)
    def _():
        col = lax.broadcasted_iota(I32, (vp_ref.shape[0], ATTN_DV), 1)
        vp_ref[:, :ATTN_DV] = v_ref[...]
        vp_ref[:, ATTN_DV:] = jnp.where(col == 0, 1.0, 0.0).astype(BF16)

    q = q_ref[...]
    lane = lax.broadcasted_iota(I32, q.shape, 1)
    zero = jnp.zeros_like(q)
    qq_ref[0:bq, :] = jnp.where(lane < ATTN_DK, q, zero)
    qq_ref[bq:, :] = jnp.where(lane >= ATTN_DK, q, zero)
    m_ref[...] = jnp.full(m_ref.shape, NEG_BIG, F32)
    acc_ref[...] = jnp.zeros_like(acc_ref)

    def scores(rows, start, kw):
        return _dot_tb(qq_ref[rows, :], k_ref[pl.ds(start, kw), :])

    def softmax_pv(s, rows, start):
        kw = s.shape[1]
        m_old = m_ref[rows, :]
        m_new = jnp.maximum(m_old, jnp.max(s, axis=-1, keepdims=True))
        alpha = jnp.exp2(m_old - m_new)
        p = jnp.exp2(s - jnp.tile(m_new, (1, kw // LANES)))
        pv = _dot(p.astype(BF16), vp_ref[pl.ds(start, kw), :])
        acc_ref[rows, :] = acc_ref[rows, :] * jnp.tile(alpha, (1, 2)) + pv
        m_ref[rows, :] = m_new

    loop_chunks = [pl.ds(r, ATTN_LOOP_ROWS) for r in range(0, n_rows, ATTN_LOOP_ROWS)]
    for rows in loop_chunks:
        s_ref[rows, :] = scores(rows, 0, bk)

    def body(j, carry):
        start = pl.multiple_of(j * bk, bk)
        for rows in loop_chunks:
            s = s_ref[rows, :]
            s_ref[rows, :] = scores(rows, start + bk, bk)
            softmax_pv(s, rows, start)
        return carry

    n_full = i * (bq // bk)
    lax.fori_loop(0, n_full, body, 0)

    start = pl.multiple_of(n_full * bk, bk)
    rc = ATTN_DIAG_ROWS
    tri = (lax.broadcasted_iota(I32, (rc, rc), 1) <= lax.broadcasted_iota(I32, (rc, rc), 0))
    diag_chunks = [(pl.ds(r, rc), r % bq) for r in range(0, n_rows, rc)]
    for rows, q0 in diag_chunks:
        if q0 + rc > bk:
            sd_ref[rows, 0:q0 + rc - bk] = scores(rows, start + bk, q0 + rc - bk)
    for rows, q0 in diag_chunks:
        kw = q0 + rc
        parts = [s_ref[rows, 0:min(q0, bk)]] if q0 else []
        if q0 > bk:
            parts.append(sd_ref[rows, 0:q0 - bk])
        last = s_ref[rows, q0:kw] if kw <= bk else sd_ref[rows, q0 - bk:kw - bk]
        parts.append(jnp.where(tri, last, NEG_BIG))
        softmax_pv(parts[0] if len(parts) == 1 else jnp.concatenate(parts, axis=1), rows, start)

    acc = acc_ref[...]
    o = acc[:, :ATTN_DV] / acc[:, ATTN_DV:ATTN_DV + 1]
    a = o[:bq] - lam_ref[0:1, 0:1] * o[bq:]
    ms = jnp.mean(a * a, axis=-1, keepdims=True)
    o_ref[...] = (a * lax.rsqrt(ms + EPS) * g_ref[...] * out_scale).astype(BF16)


def _attention(q, k, v, lam, subln_g, B, S, bq, bk, out_scale):
    T, qkw = q.shape
    H = qkw // ATTN_DV
    nq = S // bq
    return pl.pallas_call(
        functools.partial(_attn_kernel, bq, bk, out_scale),
        grid=(B, H, nq),
        in_specs=[
            pl.BlockSpec((bq, ATTN_DV), lambda b, h, i: (b * nq + i, h)),
            pl.BlockSpec((S, ATTN_DV), lambda b, h, i: (b, h)),
            pl.BlockSpec((S, ATTN_DV), lambda b, h, i: (b, h)),
            pl.BlockSpec((1, LANES), lambda b, h, i: (0, 0)),
            pl.BlockSpec((1, ATTN_DV), lambda b, h, i: (0, 0)),
        ],
        out_specs=pl.BlockSpec((bq, ATTN_DV), lambda b, h, i: (b * nq + i, h)),
        out_shape=jax.ShapeDtypeStruct((T, H * ATTN_DV), BF16),
        scratch_shapes=[
            pltpu.VMEM((2 * bq, ATTN_DV), BF16),
            pltpu.VMEM((S, 2 * ATTN_DV), BF16),
            pltpu.VMEM((2 * bq, bk), F32),
            pltpu.VMEM((2 * bq, max(bq - bk, LANES)), F32),
            pltpu.VMEM((2 * bq, LANES), F32),
            pltpu.VMEM((2 * bq, 2 * ATTN_DV), F32),
        ],
        compiler_params=_cparams(("arbitrary", "arbitrary", "arbitrary")),
        name="attention",
    )(q, k, v, lam, subln_g)


def _lru_kernel(tc, cw, pitch, x_ref, gate_ref, cw_ref, cb_ref, wg_ref, bg_ref, lam_ref, og_ref,
                o_ref, ext_ref, a_ref, b_ref, hc_ref):
    c = pl.program_id(1)

    @pl.when(c == 0)
    def _():
        ext_ref[0:SUBLANES, :] = jnp.zeros((SUBLANES, cw), F32)
        hc_ref[...] = jnp.zeros_like(hc_ref)

    ext_ref[SUBLANES:SUBLANES + tc, :] = x_ref[...]
    xc = cb_ref[...] + cw_ref[CONV_W - 1:CONV_W, :] * x_ref[...]
    for w in range(CONV_W - 1):
        sh = CONV_W - 1 - w
        xc = xc + cw_ref[w:w + 1, :] * ext_ref[SUBLANES - sh:SUBLANES - sh + tc, :]
    tail = ext_ref[tc:tc + SUBLANES, :]
    ext_ref[0:SUBLANES, :] = tail

    g = _dot(xc.astype(BF16), wg_ref[...]) + bg_ref[...]
    t_r = jnp.tanh(g[:, :cw])
    t_i = jnp.tanh(g[:, cw:])
    nl = -lam_ref[...]
    softplus = jnp.maximum(nl, 0.0) + jnp.log1p(jnp.exp(-jnp.abs(nl)))
    c1 = (-0.5 * LRU_C) * softplus
    log_a = c1 * t_r + c1
    th = jnp.tanh(log_a)
    p = -2.0 * th
    root = jnp.where(p > 0.0, p * lax.rsqrt(p * (1.0 - th)), 0.0)
    a = jnp.exp(log_a)
    b = root * ((t_i + 1.0) * (0.5 * xc))

    steps = tc // SUBLANES
    n_lt = cw // LANES
    for q in range(SUBLANES):
        for l in range(n_lt):
            a_ref[l, q * pitch:q * pitch + steps, :] = a[q * steps:(q + 1) * steps, l * LANES:(l + 1) * LANES]
            b_ref[l, q * pitch:q * pitch + steps, :] = b[q * steps:(q + 1) * steps, l * LANES:(l + 1) * LANES]

    def scan_step(t, carry):
        out = []
        for l in range(n_lt):
            h, prod = carry[l]
            rows = pl.ds(t, SUBLANES, stride=pitch)
            a_t = a_ref[l, rows, :]
            h = a_t * h + b_ref[l, rows, :]
            prod = a_t * prod
            b_ref[l, rows, :] = h
            a_ref[l, rows, :] = prod
            out.append((h, prod))
        return tuple(out)

    init = tuple((jnp.zeros((SUBLANES, LANES), F32), jnp.ones((SUBLANES, LANES), F32)) for _ in range(n_lt))
    last = lax.fori_loop(0, steps, scan_step, init, unroll=4)

    h_cols = []
    for l in range(n_lt):
        h_last, p_last = last[l]
        entering = [hc_ref[0:1, l * LANES:(l + 1) * LANES]]
        for q in range(SUBLANES):
            entering.append(h_last[q:q + 1, :] + p_last[q:q + 1, :] * entering[q])
        hc_ref[:, l * LANES:(l + 1) * LANES] = jnp.broadcast_to(entering[SUBLANES], (SUBLANES, LANES))
        h_cols.append(jnp.concatenate(
            [b_ref[l, q * pitch:q * pitch + steps, :] + a_ref[l, q * pitch:q * pitch + steps, :] * entering[q]
             for q in range(SUBLANES)], axis=0))
    h = jnp.concatenate(h_cols, axis=1)

    gt = gate_ref[...]
    gelu = 0.5 * gt * (1.0 + jnp.tanh(math.sqrt(2.0 / math.pi) * (gt + 0.044715 * gt * gt * gt)))
    y = h * gelu
    ms = jnp.mean(y * y, axis=-1, keepdims=True)
    o_ref[...] = (y * lax.rsqrt(ms + EPS) * og_ref[...]).astype(BF16)


def _scan_pitch(steps):
    return steps if (steps // SUBLANES) % 2 else steps + SUBLANES


def _lru(xr, gr, conv_w, conv_b, w_gates, b_gates, lru_lambda, out_g, B, S, tc):
    T, cw = xr.shape
    nc = S // tc
    pitch = _scan_pitch(tc // SUBLANES)
    const = lambda shape: pl.BlockSpec(shape, lambda b, c: (0,) * len(shape))
    row = pl.BlockSpec((tc, cw), lambda b, c: (b * nc + c, 0))
    return pl.pallas_call(
        functools.partial(_lru_kernel, tc, cw, pitch),
        grid=(B, nc),
        in_specs=[row, row, const((CONV_W, cw)), const((1, cw)), const((cw, 2 * cw)),
                  const((1, 2 * cw)), const((1, cw)), const((1, cw))],
        out_specs=row,
        out_shape=jax.ShapeDtypeStruct((T, cw), BF16),
        scratch_shapes=[
            pltpu.VMEM((tc + SUBLANES, cw), F32),
            pltpu.VMEM((cw // LANES, SUBLANES * pitch, LANES), F32),
            pltpu.VMEM((cw // LANES, SUBLANES * pitch, LANES), F32),
            pltpu.VMEM((SUBLANES, cw), F32),
        ],
        compiler_params=_cparams(("arbitrary", "arbitrary")),
        name="lru",
    )(xr, gr, conv_w, conv_b, w_gates, b_gates, lru_lambda, out_g)


def _out_proj_kernel(tm, n_sub, aw, *refs):
    for u in range(n_sub):
        _out_proj_tile(u, tm, aw, *refs)


def _out_proj_tile(u, tm, aw, attn_ref, lru_ref, x_ref, mod_ref, w_ref, g_ref, rw_ref, rb_ref, tri_ref, low_ref,
                   x1_ref, h2_ref, meta_ref, cnt_ref):
    rows = pl.ds(u * tm, tm)
    mix = _dot(attn_ref[rows, :], w_ref[0:aw, :]) + _dot(lru_ref[rows, :], w_ref[aw:, :])
    x1 = x_ref[rows, :] + mod_ref[2:3, :] * mix
    x1_ref[rows, :] = x1
    h2 = _rms_modulate(x1, g_ref[...], mod_ref[3:4, :], mod_ref[4:5, :])
    for j in range(h2.shape[1] // LANES):
        h2_ref[pl.ds(u * tm * SUBLANES + j, tm, stride=SUBLANES), :] = h2[:, j * LANES:(j + 1) * LANES]

    h_hi, h_lo = _split_hi_lo(h2)
    w_hi, w_lo = _split_hi_lo(rw_ref[...])
    by_hi = _dot_tb(jnp.concatenate([w_hi, w_lo], axis=0), h_hi)
    lg = by_hi[:N_EXPERTS] + by_hi[N_EXPERTS:] + _dot_tb(w_hi, h_lo) + rb_ref[...]

    eidx = lax.broadcasted_iota(I32, lg.shape, 0)
    picked = jnp.zeros(lg.shape, F32)
    vals, idxs = [], []
    for _ in range(TOP_K):
        m = jnp.max(lg, axis=0, keepdims=True)
        idx = jnp.min(jnp.where(lg == m, eidx, N_EXPERTS), axis=0, keepdims=True)
        sel = eidx == idx
        lg = jnp.where(sel, -jnp.inf, lg)
        picked = picked + sel.astype(F32)
        vals.append(m)
        idxs.append(idx)
    es = [jnp.exp(v - vals[0]) for v in vals]
    den = es[0] + es[1] + es[2] + es[3]
    gates = jnp.concatenate([e / den for e in es], axis=0)

    cnt = jnp.sum(picked, axis=1, keepdims=True)
    cnt_b = jnp.broadcast_to(cnt, cnt_ref.shape[1:])
    cnt_hi = jnp.floor(cnt_b * (1.0 / BF16_EXACT_INT))
    cnt_lo = cnt_b - BF16_EXACT_INT * cnt_hi
    run_start = BF16_EXACT_INT * _dot(low_ref[...], cnt_hi.astype(BF16)) + _dot(low_ref[...], cnt_lo.astype(BF16))
    before = _dot(picked.astype(BF16), tri_ref[...]) + run_start[:, 0:1]
    pos = [jnp.sum(jnp.where(eidx == idx, before, 0.0), axis=0, keepdims=True) for idx in idxs]
    meta_ref[u, 0:TOP_K, :] = jnp.concatenate(pos, axis=0).astype(I32) * SUBLANES
    meta_ref[u, TOP_K:, :] = lax.bitcast_convert_type(gates, I32)
    cnt_ref[u] = cnt_b.astype(I32)


def _out_proj(attn, lru, x2, mod3, w_out_b, norm_g, router_wt, router_b, S, tm, n_sub):
    T, D = x2.shape
    aw = attn.shape[1]
    lw = lru.shape[1]
    ts = tm * n_sub
    tps = S // ts
    tri = (jnp.arange(tm, dtype=I32)[:, None] < jnp.arange(tm, dtype=I32)[None, :]).astype(BF16)
    experts = jnp.arange(N_EXPERTS, dtype=I32)
    low = (experts[None, :] < experts[:, None]).astype(BF16)
    const = lambda shape: pl.BlockSpec(shape, lambda i: (0,) * len(shape))
    row = lambda w: pl.BlockSpec((ts, w), lambda i: (i, 0))
    return pl.pallas_call(
        functools.partial(_out_proj_kernel, tm, n_sub, aw),
        grid=(T // ts,),
        in_specs=[
            row(aw), row(lw), row(D),
            pl.BlockSpec((None, 6, D), lambda i: (i // tps, 0, 0)),
            const((D, D)), const((1, D)), const((N_EXPERTS, D)), const((N_EXPERTS, 1)),
            const((tm, tm)), const((N_EXPERTS, N_EXPERTS)),
        ],
        out_specs=[
            row(D),
            pl.BlockSpec((ts * SUBLANES, LANES), lambda i: (i, 0)),
            pl.BlockSpec((n_sub, 2 * TOP_K, tm), lambda i: (i, 0, 0)),
            pl.BlockSpec((n_sub, N_EXPERTS, LANES), lambda i: (i, 0, 0)),
        ],
        out_shape=[
            jax.ShapeDtypeStruct((T, D), F32),
            jax.ShapeDtypeStruct((T * SUBLANES, LANES), F32),
            jax.ShapeDtypeStruct((T // tm, 2 * TOP_K, tm), I32),
            jax.ShapeDtypeStruct((T // tm, N_EXPERTS, LANES), I32),
        ],
        compiler_params=_cparams(("arbitrary",)),
        name="out_proj",
    )(attn, lru, x2, mod3, w_out_b, norm_g, router_wt, router_b, tri, low)


def _rows(ref, first_row, n_rows):
    return ref.at[pl.ds(pl.multiple_of(first_row * SUBLANES, SUBLANES), n_rows * SUBLANES), :]


def _row_at(ref, sublane_offset):
    return ref.at[pl.ds(pl.multiple_of(sublane_offset, SUBLANES), SUBLANES), :]


def _fetch_tile(meta_hbm, smems, sems, t):
    i = pl.program_id(0)
    n = pl.num_programs(0)
    slot = i % 2
    per_tile = 2 * TOP_K * t

    def copies(tile, s):
        return [pltpu.make_async_copy(meta_hbm.at[pl.ds(pl.multiple_of(tile * per_tile + j * t, t), t)],
                                      smem.at[pl.ds(pl.multiple_of(s * t, t), t)], sems.at[s])
                for j, smem in enumerate(smems)]

    @pl.when(i == 0)
    def _():
        for c in copies(0, 0):
            c.start()

    for c in copies(i, slot):
        c.wait()

    @pl.when(i + 1 < n)
    def _():
        for c in copies(i + 1, 1 - slot):
            c.start()

    return slot * t


def _run_copies(t, tile, cnt_ref, off_ref, dst_ref, make_copy):
    def expert(e, carry):
        n = cnt_ref[tile * N_EXPERTS + e]
        off = off_ref[tile * N_EXPERTS + e]
        dst = dst_ref[tile * N_EXPERTS + e]
        for b in reversed(range(t.bit_length())):
            size = SUBLANES << b
            done = n & ~(2 * size - 1)

            @pl.when((n & size) != 0)
            def _():
                make_copy(off + done, dst + done, size).start()
        return carry

    lax.fori_loop(0, N_EXPERTS, expert, 0)


def _span(ref, first_sublane, n_sublanes):
    return ref.at[pl.ds(pl.multiple_of(first_sublane, SUBLANES), n_sublanes), :]


def _dispatch_kernel(t, n_blocks, cnt_ref, off_ref, dst_ref, pad_ref, meta_hbm, h_ref, xs_hbm,
                     pos0, pos1, pos2, pos3, stage_ref, zero_ref, psem, sem, zsem):
    i = pl.program_id(0)
    n = pl.num_programs(0)
    slot = i % 2

    def zero_fill(go):
        def expert(e, carry):
            first = pad_ref[e]
            length = pad_ref[N_EXPERTS + e]
            for b in reversed(range((MOE_BLOCK - 1).bit_length())):
                size = SUBLANES << b
                done = length & ~(2 * size - 1)

                @pl.when((length & size) != 0)
                def _():
                    go(pltpu.make_async_copy(_span(zero_ref, 0, size), _span(xs_hbm, first + done, size), zsem))
            return carry

        lax.fori_loop(0, N_EXPERTS, expert, 0)

        def block(b, carry):
            go(pltpu.make_async_copy(zero_ref, _rows(xs_hbm, b * MOE_BLOCK, MOE_BLOCK), zsem))
            return carry

        lax.fori_loop(pad_ref[2 * N_EXPERTS], n_blocks, block, 0)

    @pl.when(i == 0)
    def _():
        zero_ref[...] = jnp.zeros_like(zero_ref)
        zero_fill(lambda c: c.start())

    pos = (pos0, pos1, pos2, pos3)
    base = _fetch_tile(meta_hbm, pos, psem, t)
    stage = stage_ref.at[slot]

    def place(r, carry):
        row = _rows(h_ref, r, 1)[...]
        for k in range(TOP_K):
            _row_at(stage, pos[k][base + r])[...] = row
        return carry

    lax.fori_loop(0, t, place, 0, unroll=8)

    def all_runs(s):
        return pltpu.make_async_copy(stage_ref.at[s], _rows(xs_hbm, 0, TOP_K * t), sem.at[s])

    _run_copies(t, i, cnt_ref, off_ref, dst_ref,
                lambda a, b, m: pltpu.make_async_copy(_span(stage, a, m), _span(xs_hbm, b, m), sem.at[slot]))

    @pl.when(i > 0)
    def _():
        all_runs(1 - slot).wait()

    @pl.when(i == n - 1)
    def _():
        all_runs(slot).wait()
        zero_fill(lambda c: c.wait())


def _dispatch(tile_cnt, tile_off, tile_dst, pad_info, meta, h2t, n_blocks, t):
    n_tiles = meta.shape[0] // (2 * TOP_K * t)
    n_slots = n_blocks * MOE_BLOCK
    grid_spec = pltpu.PrefetchScalarGridSpec(
        num_scalar_prefetch=4,
        grid=(n_tiles,),
        in_specs=[pl.BlockSpec(memory_space=pl.ANY),
                  pl.BlockSpec((t * SUBLANES, LANES), lambda i, *_: (i, 0))],
        out_specs=pl.BlockSpec(memory_space=pl.ANY),
        scratch_shapes=[pltpu.SMEM((2 * t,), I32)] * TOP_K + [
            pltpu.VMEM((2, TOP_K * t * SUBLANES, LANES), F32),
            pltpu.VMEM((MOE_BLOCK * SUBLANES, LANES), F32),
            pltpu.SemaphoreType.DMA((2,)),
            pltpu.SemaphoreType.DMA((2,)),
            pltpu.SemaphoreType.DMA,
        ],
    )
    return pl.pallas_call(
        functools.partial(_dispatch_kernel, t, n_blocks),
        grid_spec=grid_spec,
        out_shape=jax.ShapeDtypeStruct((n_slots * SUBLANES, LANES), F32),
        compiler_params=_cparams(("arbitrary",)),
        name="dispatch",
    )(tile_cnt, tile_off, tile_dst, pad_info, meta, h2t)


def _experts_kernel(ff, be_ref, nv_ref, nxt_ref, xs_ref, wgu_hbm, bgu_ref, wdn_hbm, bdn_ref, ys_ref,
                    wgu_f, wdn_f, wgu_b, wdn_b, sem):
    i = pl.program_id(0)
    nv = nv_ref[i]
    n_slabs = xs_ref.shape[0] // MOE_BLOCK

    def fetch(e):
        return (pltpu.make_async_copy(wgu_hbm.at[e], wgu_f, sem.at[0]),
                pltpu.make_async_copy(wdn_hbm.at[e], wdn_f, sem.at[1]))

    @pl.when(i == 0)
    def _():
        for c in fetch(be_ref[0]):
            c.start()

    @pl.when((i == 0) | (be_ref[i] != be_ref[jnp.maximum(i - 1, 0)]))
    def _():
        for c in fetch(be_ref[i]):
            c.wait()
        wgu_b[...] = wgu_f[...].astype(BF16)
        wdn_b[...] = wdn_f[...].astype(BF16)

        @pl.when(nxt_ref[i] >= 0)
        def _():
            for c in fetch(nxt_ref[i]):
                c.start()

    @pl.when(nv > 0)
    def _():
        x = jnp.concatenate(
            [xs_ref[pl.ds(j, MOE_BLOCK, stride=SUBLANES), :] for j in range(n_slabs)], axis=1)
        gu = _dot(x.astype(BF16), wgu_b[...]) + bgu_ref[...]
        gate = jnp.minimum(gu[:, :ff], SWIGLU_LIMIT)
        up = jnp.clip(gu[:, ff:], -SWIGLU_LIMIT, SWIGLU_LIMIT)
        act = (up + 1.0) * (gate * _sigmoid(SWIGLU_ALPHA * gate))
        y = _dot(act.astype(BF16), wdn_b[...]) + bdn_ref[...]
        for j in range(n_slabs):
            ys_ref[pl.ds(j, MOE_BLOCK, stride=SUBLANES), :] = y[:, j * LANES:(j + 1) * LANES]

    @pl.when(nv == 0)
    def _():
        ys_ref[...] = jnp.zeros_like(ys_ref)


def _experts(block_expert, block_valid, xs, w_gu, b_gu, w_dn, b_dn):
    n_blocks = block_expert.shape[0]
    E, D, ff2 = w_gu.shape
    ff = ff2 // 2
    later = block_expert[None, :] > block_expert[:, None]
    nxt = jnp.min(jnp.where(later, block_expert[None, :], E), axis=1)
    nxt = jnp.where(nxt < E, nxt, -1)
    rows = pl.BlockSpec((MOE_BLOCK * SUBLANES, LANES), lambda i, *_: (i, 0))
    grid_spec = pltpu.PrefetchScalarGridSpec(
        num_scalar_prefetch=3,
        grid=(n_blocks,),
        in_specs=[
            rows,
            pl.BlockSpec(memory_space=pl.ANY),
            pl.BlockSpec((None, 1, ff2), lambda i, be, *_: (be[i], 0, 0)),
            pl.BlockSpec(memory_space=pl.ANY),
            pl.BlockSpec((None, 1, D), lambda i, be, *_: (be[i], 0, 0)),
        ],
        out_specs=rows,
        scratch_shapes=[
            pltpu.VMEM((D, ff2), F32), pltpu.VMEM((ff, D), F32),
            pltpu.VMEM((D, ff2), BF16), pltpu.VMEM((ff, D), BF16),
            pltpu.SemaphoreType.DMA((2,)),
        ],
    )
    return pl.pallas_call(
        functools.partial(_experts_kernel, ff),
        grid_spec=grid_spec,
        out_shape=jax.ShapeDtypeStruct(xs.shape, F32),
        compiler_params=_cparams(("arbitrary",)),
        name="experts",
    )(block_expert, block_valid, nxt.astype(I32), xs, w_gu, b_gu.reshape(E, 1, ff2), w_dn,
      b_dn.reshape(E, 1, D))


def _combine_kernel(t, cnt_ref, off_ref, dst_ref, meta_hbm, ys_hbm, x1_ref, mod_ref, o_ref,
                    pos0, pos1, pos2, pos3, gate0, gate1, gate2, gate3, stage_ref, y_ref, psem, sem):
    i = pl.program_id(0)
    n = pl.num_programs(0)
    slot = i % 2
    pos = (pos0, pos1, pos2, pos3)
    gates = (gate0, gate1, gate2, gate3)
    base = _fetch_tile(meta_hbm, pos + gates, psem, t)

    def fetch_runs(tile, s):
        stage = stage_ref.at[s]
        _run_copies(t, tile, cnt_ref, off_ref, dst_ref,
                    lambda a, b, m: pltpu.make_async_copy(_span(ys_hbm, b, m), _span(stage, a, m), sem.at[s]))

    @pl.when(i == 0)
    def _():
        fetch_runs(0, 0)

    @pl.when(i + 1 < n)
    def _():
        fetch_runs(i + 1, 1 - slot)

    pltpu.make_async_copy(_rows(ys_hbm, 0, TOP_K * t), stage_ref.at[slot], sem.at[slot]).wait()
    stage = stage_ref.at[slot]

    def gather(r, carry):
        acc = None
        for k in range(TOP_K):
            gate = lax.bitcast_convert_type(gates[k][base + r], F32)
            term = gate * _row_at(stage, pos[k][base + r])[...]
            acc = term if acc is None else acc + term
        _rows(y_ref, r, 1)[...] = acc
        return carry

    lax.fori_loop(0, t, gather, 0, unroll=8)

    n_slabs = o_ref.shape[1] // LANES
    y = jnp.concatenate([y_ref[pl.ds(j, t, stride=SUBLANES), :] for j in range(n_slabs)], axis=1)
    o_ref[...] = x1_ref[...] + mod_ref[5:6, :] * y


def _combine(tile_cnt, tile_off, tile_dst, meta, ys, x1, mod3, S, t):
    T, D = x1.shape
    n_tiles = T // t
    tps = S // t
    grid_spec = pltpu.PrefetchScalarGridSpec(
        num_scalar_prefetch=3,
        grid=(n_tiles,),
        in_specs=[
            pl.BlockSpec(memory_space=pl.ANY),
            pl.BlockSpec(memory_space=pl.ANY),
            pl.BlockSpec((t, D), lambda i, *_: (i, 0)),
            pl.BlockSpec((None, 6, D), lambda i, *_: (i // tps, 0, 0)),
        ],
        out_specs=pl.BlockSpec((t, D), lambda i, *_: (i, 0)),
        scratch_shapes=[pltpu.SMEM((2 * t,), I32)] * (2 * TOP_K) + [
            pltpu.VMEM((2, TOP_K * t * SUBLANES, LANES), F32),
            pltpu.VMEM((t * SUBLANES, LANES), F32),
            pltpu.SemaphoreType.DMA((2,)),
            pltpu.SemaphoreType.DMA((2,)),
        ],
    )
    return pl.pallas_call(
        functools.partial(_combine_kernel, t),
        grid_spec=grid_spec,
        out_shape=jax.ShapeDtypeStruct((T, D), F32),
        compiler_params=_cparams(("arbitrary",)),
        name="combine",
    )(tile_cnt, tile_off, tile_dst, meta, ys, x1, mod3)


def _tile(n, target):
    t = min(n, target)
    while n % t:
        t //= 2
    return t


def _block_diag(w):
    n, bw, _ = w.shape
    eye = jnp.eye(n, dtype=w.dtype)
    return (eye[:, None, :, None] * w[:, :, None, :]).reshape(n * bw, n * bw)


def _layer(l, x2, B, S, c, ada_w, ada_b, norm1_g, w_in, q_norm_g, k_norm_g, lambda_q1, lambda_k1, lambda_q2,
           lambda_k2, attn_subln_g, conv_w, conv_b, lru_wa, lru_ba, lru_wx, lru_bx, lru_lambda, lru_out_g,
           w_out, norm2_g, router_w, router_b, w_gate_up, b_gate_up, w_down, b_down):
    T, D = x2.shape
    lam_init = 0.8 - 0.6 * math.exp(-0.3 * l)
    aw = D // 2
    lw = D - aw
    heads = aw // ATTN_DV
    qkw = heads * 2 * ATTN_DK

    mod, lam = _adaln(c, ada_w, ada_b, lambda_q1, lambda_k1, lambda_q2, lambda_k2, lam_init)
    mod3 = mod.reshape(c.shape[0], 6, D)

    reps = qkw // ATTN_DK
    gq = (jnp.tile(q_norm_g, reps) * (ATTN_DK ** -0.5 * math.log2(math.e))).reshape(1, qkw)
    gk = jnp.tile(k_norm_g, reps).reshape(1, qkw)
    tm = _tile(S, ROUTE_TILE_ROWS)
    q, k, v, xr, gr = _in_proj(x2, mod3, norm1_g.reshape(1, D), w_in.astype(BF16), gq, gk, S,
                               _tile(S, IN_PROJ_ROWS), qkw, aw, lw)

    attn = _attention(q, k, v, lam, attn_subln_g.reshape(1, ATTN_DV), B, S, _tile(S, ATTN_Q_ROWS),
                      _tile(S, ATTN_KV_ROWS), 1.0 - lam_init)

    w_gates = (0.5 * jnp.concatenate([_block_diag(lru_wa), _block_diag(lru_wx)], axis=1)).astype(BF16)
    b_gates = 0.5 * jnp.concatenate([lru_ba, lru_bx]).reshape(1, 2 * lw)
    lru = _lru(xr, gr, conv_w, conv_b.reshape(1, lw), w_gates, b_gates, lru_lambda.reshape(1, lw),
               lru_out_g.reshape(1, lw), B, S, _tile(S, LRU_CHUNK_ROWS))

    x1, h2t, meta, tile_cnt = _out_proj(
        attn, lru, x2, mod3, w_out.astype(BF16), norm2_g.reshape(1, D), router_w.T,
        router_b.reshape(N_EXPERTS, 1), S, tm, _tile(S // tm, OUT_PROJ_TILES))

    tile_cnt = tile_cnt[:, :, 0]
    counts = jnp.sum(tile_cnt, axis=0)
    padded = ((counts + MOE_BLOCK - 1) // MOE_BLOCK) * MOE_BLOCK
    pad_ends = jnp.cumsum(padded)
    pad_starts = pad_ends - padded
    n_blocks = (T * TOP_K) // MOE_BLOCK + N_EXPERTS
    blk_start = jnp.arange(n_blocks, dtype=I32) * MOE_BLOCK
    owner = blk_start[:, None] >= pad_ends[None, :]
    block_expert = jnp.minimum(jnp.sum(owner, axis=1), N_EXPERTS - 1).astype(I32)
    onehot = block_expert[:, None] == jnp.arange(N_EXPERTS, dtype=I32)[None, :]
    row_end = jnp.sum(jnp.where(onehot, (pad_starts + counts)[None, :], 0), axis=1)
    block_valid = jnp.clip(row_end - blk_start, 0, MOE_BLOCK).astype(I32)
    tile_off = jnp.cumsum(tile_cnt, axis=1) - tile_cnt
    tile_dst = pad_starts[None, :] + jnp.cumsum(tile_cnt, axis=0) - tile_cnt
    flat = lambda a: (a * SUBLANES).reshape(-1).astype(I32)

    meta = meta.reshape(-1)
    pad_info = jnp.concatenate([(pad_starts + counts) * SUBLANES, (padded - counts) * SUBLANES,
                                pad_ends[-1:] // MOE_BLOCK]).astype(I32)
    xs = _dispatch(flat(tile_cnt), flat(tile_off), flat(tile_dst), pad_info, meta, h2t, n_blocks, tm)
    ys = _experts(block_expert, block_valid, xs, w_gate_up, b_gate_up, w_down, b_down)
    return _combine(flat(tile_cnt), flat(tile_off), flat(tile_dst), meta, ys, x1, mod3, S, tm)


def kernel(x, c, ada_w, ada_b, norm1_g, w_in, q_norm_g, k_norm_g, lambda_q1, lambda_k1, lambda_q2, lambda_k2,
           attn_subln_g, conv_w, conv_b, lru_wa, lru_ba, lru_wx, lru_bx, lru_lambda, lru_out_g, w_out, norm2_g,
           router_w, router_b, w_gate_up, b_gate_up, w_down, b_down):
    B, S, D = x.shape
    params = (ada_w, ada_b, norm1_g, w_in, q_norm_g, k_norm_g, lambda_q1, lambda_k1, lambda_q2, lambda_k2,
              attn_subln_g, conv_w, conv_b, lru_wa, lru_ba, lru_wx, lru_bx, lru_lambda, lru_out_g, w_out,
              norm2_g, router_w, router_b, w_gate_up, b_gate_up, w_down, b_down)
    x2 = x.reshape(B * S, D)
    for l in range(ada_w.shape[0]):
        x2 = _layer(l, x2, B, S, c, *[p[l] for p in params])
    return x2.reshape(B, S, D)
```

```python
import functools
import math

import jax
import jax.numpy as jnp
from jax import lax
from jax.experimental import pallas as pl
from jax.experimental.pallas import tpu as pltpu

F32 = jnp.float32
BF16 = jnp.bfloat16
I32 = jnp.int32

ATTN_DK = 64
ATTN_DV = 2 * ATTN_DK
CONV_W = 4
LRU_C = 8.0
N_EXPERTS = 32
TOP_K = 4
SWIGLU_LIMIT = 7.0
SWIGLU_ALPHA = 1.702
MOE_BLOCK = 512
EPS = 1e-6
NEG_BIG = -1e30

IN_PROJ_ROWS = 1024
ATTN_Q_ROWS = 1024
ATTN_KV_ROWS = 512
LRU_CHUNK_ROWS = 1024
ROUTE_TILE_ROWS = 512
OUT_PROJ_TILES = 2
BF16_EXACT_INT = 256.0

LANES = 128
SUBLANES = 8
MXU_DEPTH = 256
VMEM_LIMIT = 40 * 1024 * 1024


def _cparams(sem):
    return pltpu.CompilerParams(dimension_semantics=sem, vmem_limit_bytes=VMEM_LIMIT)


def _split_hi_lo(x):
    hi = x.astype(BF16)
    lo = (x - hi.astype(F32)).astype(BF16)
    return hi, lo


def _sigmoid(x):
    return 0.5 * jnp.tanh(0.5 * x) + 0.5


def _dot(a, b):
    return jnp.dot(a, b, preferred_element_type=F32)


def _dot_tb(a, b):
    return lax.dot_general(a, b, (((1,), (1,)), ((), ())), preferred_element_type=F32)


def _adaln_kernel(lam_init, c_ref, w_ref, b_ref, lq1, lk1, lq2, lk2, mod_ref, lam_ref):
    c = c_ref[...]
    s = c * jax.nn.sigmoid(c)
    s_hi, s_lo = _split_hi_lo(s)
    w = w_ref[...]
    w_hi, w_lo = _split_hi_lo(w)
    mod_ref[...] = _dot(s_hi, w_hi) + _dot(s_hi, w_lo) + _dot(s_lo, w_hi) + b_ref[...]
    d1 = jnp.sum(lq1[...] * lk1[...], axis=-1, keepdims=True)
    d2 = jnp.sum(lq2[...] * lk2[...], axis=-1, keepdims=True)
    lam = jnp.exp(d1) - jnp.exp(d2) + lam_init
    lam_ref[...] = jnp.broadcast_to(lam, lam_ref.shape)


def _adaln(c, ada_w, ada_b, lq1, lk1, lq2, lk2, lam_init):
    B, D = c.shape
    n = ada_w.shape[1] // D
    vec = lambda: pl.BlockSpec((1, ATTN_DK), lambda j: (0, 0))
    return pl.pallas_call(
        functools.partial(_adaln_kernel, lam_init),
        grid=(n,),
        in_specs=[
            pl.BlockSpec((B, D), lambda j: (0, 0)),
            pl.BlockSpec((D, D), lambda j: (0, j)),
            pl.BlockSpec((1, D), lambda j: (0, j)),
            vec(), vec(), vec(), vec(),
        ],
        out_specs=[
            pl.BlockSpec((B, D), lambda j: (0, j)),
            pl.BlockSpec((1, LANES), lambda j: (0, 0)),
        ],
        out_shape=[
            jax.ShapeDtypeStruct((B, n * D), F32),
            jax.ShapeDtypeStruct((1, LANES), F32),
        ],
        compiler_params=_cparams(("arbitrary",)),
        name="adaln",
    )(c, ada_w, ada_b.reshape(1, -1), lq1.reshape(1, -1), lk1.reshape(1, -1),
      lq2.reshape(1, -1), lk2.reshape(1, -1))


def _rms_modulate(x, g, shift, scale):
    ms = jnp.mean(x * x, axis=-1, keepdims=True)
    y = x * lax.rsqrt(ms + EPS) * g
    return y * (1.0 + scale) + shift


def _group_rms_scale(q, group_ones):
    sq = (q * q).astype(BF16)
    w = group_ones.shape[0]
    ss = jnp.concatenate([_dot(sq[:, c:c + w], group_ones) for c in range(0, q.shape[1], w)], axis=1)
    return lax.rsqrt(ss * (1.0 / ATTN_DK) + EPS)


def _in_proj_kernel(qkw, aw, lw, x_ref, mod_ref, g_ref, w_ref, gq_ref, gk_ref, ones_ref,
                    q_ref, k_ref, v_ref, xr_ref, gr_ref):
    h = _rms_modulate(x_ref[...], g_ref[...], mod_ref[0:1, :], mod_ref[1:2, :])
    hb = h.astype(BF16)
    group_ones = ones_ref[...]
    o = 0
    q = _dot(hb, w_ref[:, o:o + qkw]); o += qkw
    q_ref[...] = (q * _group_rms_scale(q, group_ones) * gq_ref[...]).astype(BF16)
    k = _dot(hb, w_ref[:, o:o + qkw]); o += qkw
    k_ref[...] = (k * _group_rms_scale(k, group_ones) * gk_ref[...]).astype(BF16)
    v_ref[...] = _dot(hb, w_ref[:, o:o + aw]).astype(BF16); o += aw
    xr_ref[...] = _dot(hb, w_ref[:, o:o + lw]); o += lw
    gr_ref[...] = _dot(hb, w_ref[:, o:o + lw])


def _in_proj(x2, mod3, norm_g, w_in_b, gq, gk, S, tm, qkw, aw, lw):
    T, D = x2.shape
    tps = S // tm
    grp = jnp.arange(MXU_DEPTH, dtype=I32) // ATTN_DK
    group_ones = (grp[:, None] == grp[None, :]).astype(BF16)
    const = lambda shape: pl.BlockSpec(shape, lambda i: (0,) * len(shape))
    row = lambda w: pl.BlockSpec((tm, w), lambda i: (i, 0))
    return pl.pallas_call(
        functools.partial(_in_proj_kernel, qkw, aw, lw),
        grid=(T // tm,),
        in_specs=[
            row(D),
            pl.BlockSpec((None, 6, D), lambda i: (i // tps, 0, 0)),
            const((1, D)),
            const(w_in_b.shape),
            const((1, qkw)), const((1, qkw)),
            const((MXU_DEPTH, MXU_DEPTH)),
        ],
        out_specs=[row(qkw), row(qkw), row(aw), row(lw), row(lw)],
        out_shape=[
            jax.ShapeDtypeStruct((T, qkw), BF16),
            jax.ShapeDtypeStruct((T, qkw), BF16),
            jax.ShapeDtypeStruct((T, aw), BF16),
            jax.ShapeDtypeStruct((T, lw), F32),
            jax.ShapeDtypeStruct((T, lw), F32),
        ],
        compiler_params=_cparams(("arbitrary",)),
        name="in_proj",
    )(x2, mod3, norm_g, w_in_b, gq, gk, group_ones)


ATTN_LOOP_ROWS = 256
ATTN_DIAG_ROWS = 128


def _attn_kernel(bq, bk, out_scale, q_ref, k_ref, v_ref, lam_ref, g_ref, o_ref,
                 qq_ref, vp_ref, s_ref, sd_ref, m_ref, acc_ref):
    n_rows = 2 * bq

    col = lax.broadcasted_iota(I32, (vp_ref.shape[0], ATTN_DV), 1)
    vp_ref[:, :ATTN_DV] = v_ref[...]
    vp_ref[:, ATTN_DV:] = jnp.where(col == 0, 1.0, 0.0).astype(BF16)

    def query_block(i, carry):
        q_rows = pl.ds(pl.multiple_of(i * bq, bq), bq)
        _attn_query_block(i, q_rows, bq, bk, n_rows, out_scale, q_ref, k_ref, lam_ref, g_ref, o_ref,
                          qq_ref, vp_ref, s_ref, sd_ref, m_ref, acc_ref)
        return carry

    lax.fori_loop(0, q_ref.shape[0] // bq, query_block, 0)


def _attn_query_block(i, q_rows, bq, bk, n_rows, out_scale, q_ref, k_ref, lam_ref, g_ref, o_ref,
                      qq_ref, vp_ref, s_ref, sd_ref, m_ref, acc_ref):
    q = q_ref[q_rows, :]
    lane = lax.broadcasted_iota(I32, q.shape, 1)
    zero = jnp.zeros_like(q)
    qq_ref[0:bq, :] = jnp.where(lane < ATTN_DK, q, zero)
    qq_ref[bq:, :] = jnp.where(lane >= ATTN_DK, q, zero)
    m_ref[...] = jnp.full(m_ref.shape, NEG_BIG, F32)
    acc_ref[...] = jnp.zeros_like(acc_ref)

    def scores(rows, start, kw):
        return _dot_tb(qq_ref[rows, :], k_ref[pl.ds(start, kw), :])

    def softmax_pv(s, rows, start):
        kw = s.shape[1]
        m_old = m_ref[rows, :]
        m_new = jnp.maximum(m_old, jnp.max(s, axis=-1, keepdims=True))
        alpha = jnp.exp2(m_old - m_new)
        p = jnp.exp2(s - jnp.tile(m_new, (1, kw // LANES)))
        pv = _dot(p.astype(BF16), vp_ref[pl.ds(start, kw), :])
        acc_ref[rows, :] = acc_ref[rows, :] * jnp.tile(alpha, (1, 2)) + pv
        m_ref[rows, :] = m_new

    loop_chunks = [pl.ds(r, ATTN_LOOP_ROWS) for r in range(0, n_rows, ATTN_LOOP_ROWS)]
    for rows in loop_chunks:
        s_ref[rows, :] = scores(rows, 0, bk)

    def body(j, carry):
        start = pl.multiple_of(j * bk, bk)
        for rows in loop_chunks:
            s = s_ref[rows, :]
            s_ref[rows, :] = scores(rows, start + bk, bk)
            softmax_pv(s, rows, start)
        return carry

    n_full = i * (bq // bk)
    lax.fori_loop(0, n_full, body, 0)

    start = pl.multiple_of(n_full * bk, bk)
    rc = ATTN_DIAG_ROWS
    tri = (lax.broadcasted_iota(I32, (rc, rc), 1) <= lax.broadcasted_iota(I32, (rc, rc), 0))
    diag_chunks = [(pl.ds(r, rc), r % bq) for r in range(0, n_rows, rc)]
    for rows, q0 in diag_chunks:
        if q0 + rc > bk:
            sd_ref[rows, 0:q0 + rc - bk] = scores(rows, start + bk, q0 + rc - bk)
    for rows, q0 in diag_chunks:
        kw = q0 + rc
        parts = [s_ref[rows, 0:min(q0, bk)]] if q0 else []
        if q0 > bk:
            parts.append(sd_ref[rows, 0:q0 - bk])
        last = s_ref[rows, q0:kw] if kw <= bk else sd_ref[rows, q0 - bk:kw - bk]
        parts.append(jnp.where(tri, last, NEG_BIG))
        softmax_pv(parts[0] if len(parts) == 1 else jnp.concatenate(parts, axis=1), rows, start)

    acc = acc_ref[...]
    o = acc[:, :ATTN_DV] / acc[:, ATTN_DV:ATTN_DV + 1]
    a = o[:bq] - lam_ref[0:1, 0:1] * o[bq:]
    ms = jnp.mean(a * a, axis=-1, keepdims=True)
    o_ref[q_rows, :] = (a * lax.rsqrt(ms + EPS) * g_ref[...] * out_scale).astype(BF16)


def _attention(q, k, v, lam, subln_g, B, S, bq, bk, out_scale):
    T, qkw = q.shape
    H = qkw // ATTN_DV
    nq = S // bq
    return pl.pallas_call(
        functools.partial(_attn_kernel, bq, bk, out_scale),
        grid=(B, H),
        in_specs=[
            pl.BlockSpec((S, ATTN_DV), lambda b, h: (b, h)),
            pl.BlockSpec((S, ATTN_DV), lambda b, h: (b, h)),
            pl.BlockSpec((S, ATTN_DV), lambda b, h: (b, h)),
            pl.BlockSpec((1, LANES), lambda b, h: (0, 0)),
            pl.BlockSpec((1, ATTN_DV), lambda b, h: (0, 0)),
        ],
        out_specs=pl.BlockSpec((S, ATTN_DV), lambda b, h: (b, h)),
        out_shape=jax.ShapeDtypeStruct((T, H * ATTN_DV), BF16),
        scratch_shapes=[
            pltpu.VMEM((2 * bq, ATTN_DV), BF16),
            pltpu.VMEM((S, 2 * ATTN_DV), BF16),
            pltpu.VMEM((2 * bq, bk), F32),
            pltpu.VMEM((2 * bq, max(bq - bk, LANES)), F32),
            pltpu.VMEM((2 * bq, LANES), F32),
            pltpu.VMEM((2 * bq, 2 * ATTN_DV), F32),
        ],
        compiler_params=_cparams(("arbitrary", "arbitrary")),
        name="attention",
    )(q, k, v, lam, subln_g)


def _lru_kernel(tc, cw, pitch, x_ref, gate_ref, cw_ref, cb_ref, wg_ref, bg_ref, lam_ref, og_ref,
                o_ref, ext_ref, a_ref, b_ref, hc_ref):
    c = pl.program_id(1)

    @pl.when(c == 0)
    def _():
        ext_ref[0:SUBLANES, :] = jnp.zeros((SUBLANES, cw), F32)
        hc_ref[...] = jnp.zeros_like(hc_ref)

    ext_ref[SUBLANES:SUBLANES + tc, :] = x_ref[...]
    xc = cb_ref[...] + cw_ref[CONV_W - 1:CONV_W, :] * x_ref[...]
    for w in range(CONV_W - 1):
        sh = CONV_W - 1 - w
        xc = xc + cw_ref[w:w + 1, :] * ext_ref[SUBLANES - sh:SUBLANES - sh + tc, :]
    tail = ext_ref[tc:tc + SUBLANES, :]
    ext_ref[0:SUBLANES, :] = tail

    g = _dot(xc.astype(BF16), wg_ref[...]) + bg_ref[...]
    t_r = jnp.tanh(g[:, :cw])
    t_i = jnp.tanh(g[:, cw:])
    nl = -lam_ref[...]
    softplus = jnp.maximum(nl, 0.0) + jnp.log1p(jnp.exp(-jnp.abs(nl)))
    c1 = (-0.5 * LRU_C) * softplus
    log_a = c1 * t_r + c1
    th = jnp.tanh(log_a)
    p = -2.0 * th
    root = jnp.where(p > 0.0, p * lax.rsqrt(p * (1.0 - th)), 0.0)
    a = jnp.exp(log_a)
    b = root * ((t_i + 1.0) * (0.5 * xc))

    steps = tc // SUBLANES
    n_lt = cw // LANES
    for q in range(SUBLANES):
        for l in range(n_lt):
            a_ref[l, q * pitch:q * pitch + steps, :] = a[q * steps:(q + 1) * steps, l * LANES:(l + 1) * LANES]
            b_ref[l, q * pitch:q * pitch + steps, :] = b[q * steps:(q + 1) * steps, l * LANES:(l + 1) * LANES]

    def scan_step(t, carry):
        out = []
        for l in range(n_lt):
            h, prod = carry[l]
            rows = pl.ds(t, SUBLANES, stride=pitch)
            a_t = a_ref[l, rows, :]
            h = a_t * h + b_ref[l, rows, :]
            prod = a_t * prod
            b_ref[l, rows, :] = h
            a_ref[l, rows, :] = prod
            out.append((h, prod))
        return tuple(out)

    init = tuple((jnp.zeros((SUBLANES, LANES), F32), jnp.ones((SUBLANES, LANES), F32)) for _ in range(n_lt))
    last = lax.fori_loop(0, steps, scan_step, init, unroll=4)

    h_cols = []
    for l in range(n_lt):
        h_last, p_last = last[l]
        entering = [hc_ref[0:1, l * LANES:(l + 1) * LANES]]
        for q in range(SUBLANES):
            entering.append(h_last[q:q + 1, :] + p_last[q:q + 1, :] * entering[q])
        hc_ref[:, l * LANES:(l + 1) * LANES] = jnp.broadcast_to(entering[SUBLANES], (SUBLANES, LANES))
        h_cols.append(jnp.concatenate(
            [b_ref[l, q * pitch:q * pitch + steps, :] + a_ref[l, q * pitch:q * pitch + steps, :] * entering[q]
             for q in range(SUBLANES)], axis=0))
    h = jnp.concatenate(h_cols, axis=1)

    gt = gate_ref[...]
    gelu = 0.5 * gt * (1.0 + jnp.tanh(math.sqrt(2.0 / math.pi) * (gt + 0.044715 * gt * gt * gt)))
    y = h * gelu
    ms = jnp.mean(y * y, axis=-1, keepdims=True)
    o_ref[...] = (y * lax.rsqrt(ms + EPS) * og_ref[...]).astype(BF16)


def _scan_pitch(steps):
    return steps if (steps // SUBLANES) % 2 else steps + SUBLANES


def _lru(xr, gr, conv_w, conv_b, w_gates, b_gates, lru_lambda, out_g, B, S, tc):
    T, cw = xr.shape
    nc = S // tc
    pitch = _scan_pitch(tc // SUBLANES)
    const = lambda shape: pl.BlockSpec(shape, lambda b, c: (0,) * len(shape))
    row = pl.BlockSpec((tc, cw), lambda b, c: (b * nc + c, 0))
    return pl.pallas_call(
        functools.partial(_lru_kernel, tc, cw, pitch),
        grid=(B, nc),
        in_specs=[row, row, const((CONV_W, cw)), const((1, cw)), const((cw, 2 * cw)),
                  const((1, 2 * cw)), const((1, cw)), const((1, cw))],
        out_specs=row,
        out_shape=jax.ShapeDtypeStruct((T, cw), BF16),
        scratch_shapes=[
            pltpu.VMEM((tc + SUBLANES, cw), F32),
            pltpu.VMEM((cw // LANES, SUBLANES * pitch, LANES), F32),
            pltpu.VMEM((cw // LANES, SUBLANES * pitch, LANES), F32),
            pltpu.VMEM((SUBLANES, cw), F32),
        ],
        compiler_params=_cparams(("arbitrary", "arbitrary")),
        name="lru",
    )(xr, gr, conv_w, conv_b, w_gates, b_gates, lru_lambda, out_g)


def _out_proj_kernel(tm, n_sub, aw, *refs):
    for u in range(n_sub):
        _out_proj_tile(u, tm, aw, *refs)


def _out_proj_tile(u, tm, aw, attn_ref, lru_ref, x_ref, mod_ref, w_ref, g_ref, rw_ref, rb_ref, tri_ref, low_ref,
                   x1_ref, h2_ref, meta_ref, cnt_ref):
    rows = pl.ds(u * tm, tm)
    mix = _dot(attn_ref[rows, :], w_ref[0:aw, :]) + _dot(lru_ref[rows, :], w_ref[aw:, :])
    x1 = x_ref[rows, :] + mod_ref[2:3, :] * mix
    x1_ref[rows, :] = x1
    h2 = _rms_modulate(x1, g_ref[...], mod_ref[3:4, :], mod_ref[4:5, :])
    for j in range(h2.shape[1] // LANES):
        h2_ref[pl.ds(u * tm * SUBLANES + j, tm, stride=SUBLANES), :] = h2[:, j * LANES:(j + 1) * LANES]

    h_hi, h_lo = _split_hi_lo(h2)
    w_hi, w_lo = _split_hi_lo(rw_ref[...])
    by_hi = _dot_tb(jnp.concatenate([w_hi, w_lo], axis=0), h_hi)
    lg = by_hi[:N_EXPERTS] + by_hi[N_EXPERTS:] + _dot_tb(w_hi, h_lo) + rb_ref[...]

    eidx = lax.broadcasted_iota(I32, lg.shape, 0)
    picked = jnp.zeros(lg.shape, F32)
    vals, idxs = [], []
    for _ in range(TOP_K):
        m = jnp.max(lg, axis=0, keepdims=True)
        idx = jnp.min(jnp.where(lg == m, eidx, N_EXPERTS), axis=0, keepdims=True)
        sel = eidx == idx
        lg = jnp.where(sel, -jnp.inf, lg)
        picked = picked + sel.astype(F32)
        vals.append(m)
        idxs.append(idx)
    es = [jnp.exp(v - vals[0]) for v in vals]
    den = es[0] + es[1] + es[2] + es[3]
    gates = jnp.concatenate([e / den for e in es], axis=0)

    cnt = jnp.sum(picked, axis=1, keepdims=True)
    cnt_b = jnp.broadcast_to(cnt, cnt_ref.shape[1:])
    cnt_hi = jnp.floor(cnt_b * (1.0 / BF16_EXACT_INT))
    cnt_lo = cnt_b - BF16_EXACT_INT * cnt_hi
    run_start = BF16_EXACT_INT * _dot(low_ref[...], cnt_hi.astype(BF16)) + _dot(low_ref[...], cnt_lo.astype(BF16))
    before = _dot(picked.astype(BF16), tri_ref[...]) + run_start[:, 0:1]
    pos = [jnp.sum(jnp.where(eidx == idx, before, 0.0), axis=0, keepdims=True) for idx in idxs]
    meta_ref[u, 0:TOP_K, :] = jnp.concatenate(pos, axis=0).astype(I32) * SUBLANES
    meta_ref[u, TOP_K:, :] = lax.bitcast_convert_type(gates, I32)
    cnt_ref[u] = cnt_b.astype(I32)


def _out_proj(attn, lru, x2, mod3, w_out_b, norm_g, router_wt, router_b, S, tm, n_sub):
    T, D = x2.shape
    aw = attn.shape[1]
    lw = lru.shape[1]
    ts = tm * n_sub
    tps = S // ts
    tri = (jnp.arange(tm, dtype=I32)[:, None] < jnp.arange(tm, dtype=I32)[None, :]).astype(BF16)
    experts = jnp.arange(N_EXPERTS, dtype=I32)
    low = (experts[None, :] < experts[:, None]).astype(BF16)
    const = lambda shape: pl.BlockSpec(shape, lambda i: (0,) * len(shape))
    row = lambda w: pl.BlockSpec((ts, w), lambda i: (i, 0))
    return pl.pallas_call(
        functools.partial(_out_proj_kernel, tm, n_sub, aw),
        grid=(T // ts,),
        in_specs=[
            row(aw), row(lw), row(D),
            pl.BlockSpec((None, 6, D), lambda i: (i // tps, 0, 0)),
            const((D, D)), const((1, D)), const((N_EXPERTS, D)), const((N_EXPERTS, 1)),
            const((tm, tm)), const((N_EXPERTS, N_EXPERTS)),
        ],
        out_specs=[
            row(D),
            pl.BlockSpec((ts * SUBLANES, LANES), lambda i: (i, 0)),
            pl.BlockSpec((n_sub, 2 * TOP_K, tm), lambda i: (i, 0, 0)),
            pl.BlockSpec((n_sub, N_EXPERTS, LANES), lambda i: (i, 0, 0)),
        ],
        out_shape=[
            jax.ShapeDtypeStruct((T, D), F32),
            jax.ShapeDtypeStruct((T * SUBLANES, LANES), F32),
            jax.ShapeDtypeStruct((T // tm, 2 * TOP_K, tm), I32),
            jax.ShapeDtypeStruct((T // tm, N_EXPERTS, LANES), I32),
        ],
        compiler_params=_cparams(("arbitrary",)),
        name="out_proj",
    )(attn, lru, x2, mod3, w_out_b, norm_g, router_wt, router_b, tri, low)


def _rows(ref, first_row, n_rows):
    return ref.at[pl.ds(pl.multiple_of(first_row * SUBLANES, SUBLANES), n_rows * SUBLANES), :]


def _row_at(ref, sublane_offset):
    return ref.at[pl.ds(pl.multiple_of(sublane_offset, SUBLANES), SUBLANES), :]


def _fetch_tile(meta_hbm, smems, sems, t):
    i = pl.program_id(0)
    n = pl.num_programs(0)
    slot = i % 2
    per_tile = 2 * TOP_K * t

    def copies(tile, s):
        return [pltpu.make_async_copy(meta_hbm.at[pl.ds(pl.multiple_of(tile * per_tile + j * t, t), t)],
                                      smem.at[pl.ds(pl.multiple_of(s * t, t), t)], sems.at[s])
                for j, smem in enumerate(smems)]

    @pl.when(i == 0)
    def _():
        for c in copies(0, 0):
            c.start()

    for c in copies(i, slot):
        c.wait()

    @pl.when(i + 1 < n)
    def _():
        for c in copies(i + 1, 1 - slot):
            c.start()

    return slot * t


def _run_copies(t, tile, cnt_ref, off_ref, dst_ref, make_copy):
    def expert(e, carry):
        n = cnt_ref[tile * N_EXPERTS + e]
        off = off_ref[tile * N_EXPERTS + e]
        dst = dst_ref[tile * N_EXPERTS + e]
        for b in reversed(range(t.bit_length())):
            size = SUBLANES << b
            done = n & ~(2 * size - 1)

            @pl.when((n & size) != 0)
            def _():
                make_copy(off + done, dst + done, size).start()
        return carry

    lax.fori_loop(0, N_EXPERTS, expert, 0)


def _span(ref, first_sublane, n_sublanes):
    return ref.at[pl.ds(pl.multiple_of(first_sublane, SUBLANES), n_sublanes), :]


def _dispatch_kernel(t, n_blocks, cnt_ref, off_ref, dst_ref, pad_ref, meta_hbm, h_ref, xs_hbm,
                     pos0, pos1, pos2, pos3, stage_ref, zero_ref, psem, sem, zsem):
    i = pl.program_id(0)
    n = pl.num_programs(0)
    slot = i % 2

    def zero_fill(go):
        def expert(e, carry):
            first = pad_ref[e]
            length = pad_ref[N_EXPERTS + e]
            for b in reversed(range((MOE_BLOCK - 1).bit_length())):
                size = SUBLANES << b
                done = length & ~(2 * size - 1)

                @pl.when((length & size) != 0)
                def _():
                    go(pltpu.make_async_copy(_span(zero_ref, 0, size), _span(xs_hbm, first + done, size), zsem))
            return carry

        lax.fori_loop(0, N_EXPERTS, expert, 0)

        def block(b, carry):
            go(pltpu.make_async_copy(zero_ref, _rows(xs_hbm, b * MOE_BLOCK, MOE_BLOCK), zsem))
            return carry

        lax.fori_loop(pad_ref[2 * N_EXPERTS], n_blocks, block, 0)

    @pl.when(i == 0)
    def _():
        zero_ref[...] = jnp.zeros_like(zero_ref)
        zero_fill(lambda c: c.start())

    pos = (pos0, pos1, pos2, pos3)
    base = _fetch_tile(meta_hbm, pos, psem, t)
    stage = stage_ref.at[slot]

    def place(r, carry):
        row = _rows(h_ref, r, 1)[...]
        for k in range(TOP_K):
            _row_at(stage, pos[k][base + r])[...] = row
        return carry

    lax.fori_loop(0, t, place, 0, unroll=8)

    def all_runs(s):
        return pltpu.make_async_copy(stage_ref.at[s], _rows(xs_hbm, 0, TOP_K * t), sem.at[s])

    _run_copies(t, i, cnt_ref, off_ref, dst_ref,
                lambda a, b, m: pltpu.make_async_copy(_span(stage, a, m), _span(xs_hbm, b, m), sem.at[slot]))

    @pl.when(i > 0)
    def _():
        all_runs(1 - slot).wait()

    @pl.when(i == n - 1)
    def _():
        all_runs(slot).wait()
        zero_fill(lambda c: c.wait())


def _dispatch(tile_cnt, tile_off, tile_dst, pad_info, meta, h2t, n_blocks, t):
    n_tiles = meta.shape[0] // (2 * TOP_K * t)
    n_slots = n_blocks * MOE_BLOCK
    grid_spec = pltpu.PrefetchScalarGridSpec(
        num_scalar_prefetch=4,
        grid=(n_tiles,),
        in_specs=[pl.BlockSpec(memory_space=pl.ANY),
                  pl.BlockSpec((t * SUBLANES, LANES), lambda i, *_: (i, 0))],
        out_specs=pl.BlockSpec(memory_space=pl.ANY),
        scratch_shapes=[pltpu.SMEM((2 * t,), I32)] * TOP_K + [
            pltpu.VMEM((2, TOP_K * t * SUBLANES, LANES), F32),
            pltpu.VMEM((MOE_BLOCK * SUBLANES, LANES), F32),
            pltpu.SemaphoreType.DMA((2,)),
            pltpu.SemaphoreType.DMA((2,)),
            pltpu.SemaphoreType.DMA,
        ],
    )
    return pl.pallas_call(
        functools.partial(_dispatch_kernel, t, n_blocks),
        grid_spec=grid_spec,
        out_shape=jax.ShapeDtypeStruct((n_slots * SUBLANES, LANES), F32),
        compiler_params=_cparams(("arbitrary",)),
        name="dispatch",
    )(tile_cnt, tile_off, tile_dst, pad_info, meta, h2t)


def _experts_kernel(ff, be_ref, nv_ref, nxt_ref, xs_ref, wgu_hbm, bgu_ref, wdn_hbm, bdn_ref, ys_ref,
                    wgu_f, wdn_f, wgu_b, wdn_b, sem):
    i = pl.program_id(0)
    nv = nv_ref[i]
    n_slabs = xs_ref.shape[0] // MOE_BLOCK

    def fetch(e):
        return (pltpu.make_async_copy(wgu_hbm.at[e], wgu_f, sem.at[0]),
                pltpu.make_async_copy(wdn_hbm.at[e], wdn_f, sem.at[1]))

    @pl.when(i == 0)
    def _():
        for c in fetch(be_ref[0]):
            c.start()

    @pl.when((i == 0) | (be_ref[i] != be_ref[jnp.maximum(i - 1, 0)]))
    def _():
        for c in fetch(be_ref[i]):
            c.wait()
        wgu_b[...] = wgu_f[...].astype(BF16)
        wdn_b[...] = wdn_f[...].astype(BF16)

        @pl.when(nxt_ref[i] >= 0)
        def _():
            for c in fetch(nxt_ref[i]):
                c.start()

    @pl.when(nv > 0)
    def _():
        x = jnp.concatenate(
            [xs_ref[pl.ds(j, MOE_BLOCK, stride=SUBLANES), :] for j in range(n_slabs)], axis=1)
        gu = _dot(x.astype(BF16), wgu_b[...]) + bgu_ref[...]
        gate = jnp.minimum(gu[:, :ff], SWIGLU_LIMIT)
        up = jnp.clip(gu[:, ff:], -SWIGLU_LIMIT, SWIGLU_LIMIT)
        act = (up + 1.0) * (gate * _sigmoid(SWIGLU_ALPHA * gate))
        y = _dot(act.astype(BF16), wdn_b[...]) + bdn_ref[...]
        for j in range(n_slabs):
            ys_ref[pl.ds(j, MOE_BLOCK, stride=SUBLANES), :] = y[:, j * LANES:(j + 1) * LANES]

    @pl.when(nv == 0)
    def _():
        ys_ref[...] = jnp.zeros_like(ys_ref)


def _experts(block_expert, block_valid, xs, w_gu, b_gu, w_dn, b_dn):
    n_blocks = block_expert.shape[0]
    E, D, ff2 = w_gu.shape
    ff = ff2 // 2
    later = block_expert[None, :] > block_expert[:, None]
    nxt = jnp.min(jnp.where(later, block_expert[None, :], E), axis=1)
    nxt = jnp.where(nxt < E, nxt, -1)
    rows = pl.BlockSpec((MOE_BLOCK * SUBLANES, LANES), lambda i, *_: (i, 0))
    grid_spec = pltpu.PrefetchScalarGridSpec(
        num_scalar_prefetch=3,
        grid=(n_blocks,),
        in_specs=[
            rows,
            pl.BlockSpec(memory_space=pl.ANY),
            pl.BlockSpec((None, 1, ff2), lambda i, be, *_: (be[i], 0, 0)),
            pl.BlockSpec(memory_space=pl.ANY),
            pl.BlockSpec((None, 1, D), lambda i, be, *_: (be[i], 0, 0)),
        ],
        out_specs=rows,
        scratch_shapes=[
            pltpu.VMEM((D, ff2), F32), pltpu.VMEM((ff, D), F32),
            pltpu.VMEM((D, ff2), BF16), pltpu.VMEM((ff, D), BF16),
            pltpu.SemaphoreType.DMA((2,)),
        ],
    )
    return pl.pallas_call(
        functools.partial(_experts_kernel, ff),
        grid_spec=grid_spec,
        out_shape=jax.ShapeDtypeStruct(xs.shape, F32),
        compiler_params=_cparams(("arbitrary",)),
        name="experts",
    )(block_expert, block_valid, nxt.astype(I32), xs, w_gu, b_gu.reshape(E, 1, ff2), w_dn,
      b_dn.reshape(E, 1, D))


def _combine_kernel(t, cnt_ref, off_ref, dst_ref, meta_hbm, ys_hbm, x1_ref, mod_ref, o_ref,
                    pos0, pos1, pos2, pos3, gate0, gate1, gate2, gate3, stage_ref, y_ref, psem, sem):
    i = pl.program_id(0)
    n = pl.num_programs(0)
    slot = i % 2
    pos = (pos0, pos1, pos2, pos3)
    gates = (gate0, gate1, gate2, gate3)
    base = _fetch_tile(meta_hbm, pos + gates, psem, t)

    def fetch_runs(tile, s):
        stage = stage_ref.at[s]
        _run_copies(t, tile, cnt_ref, off_ref, dst_ref,
                    lambda a, b, m: pltpu.make_async_copy(_span(ys_hbm, b, m), _span(stage, a, m), sem.at[s]))

    @pl.when(i == 0)
    def _():
        fetch_runs(0, 0)

    @pl.when(i + 1 < n)
    def _():
        fetch_runs(i + 1, 1 - slot)

    pltpu.make_async_copy(_rows(ys_hbm, 0, TOP_K * t), stage_ref.at[slot], sem.at[slot]).wait()
    stage = stage_ref.at[slot]

    def gather(r, carry):
        acc = None
        for k in range(TOP_K):
            gate = lax.bitcast_convert_type(gates[k][base + r], F32)
            term = gate * _row_at(stage, pos[k][base + r])[...]
            acc = term if acc is None else acc + term
        _rows(y_ref, r, 1)[...] = acc
        return carry

    lax.fori_loop(0, t, gather, 0, unroll=8)

    n_slabs = o_ref.shape[1] // LANES
    y = jnp.concatenate([y_ref[pl.ds(j, t, stride=SUBLANES), :] for j in range(n_slabs)], axis=1)
    o_ref[...] = x1_ref[...] + mod_ref[5:6, :] * y


def _combine(tile_cnt, tile_off, tile_dst, meta, ys, x1, mod3, S, t):
    T, D = x1.shape
    n_tiles = T // t
    tps = S // t
    grid_spec = pltpu.PrefetchScalarGridSpec(
        num_scalar_prefetch=3,
        grid=(n_tiles,),
        in_specs=[
            pl.BlockSpec(memory_space=pl.ANY),
            pl.BlockSpec(memory_space=pl.ANY),
            pl.BlockSpec((t, D), lambda i, *_: (i, 0)),
            pl.BlockSpec((None, 6, D), lambda i, *_: (i // tps, 0, 0)),
        ],
        out_specs=pl.BlockSpec((t, D), lambda i, *_: (i, 0)),
        scratch_shapes=[pltpu.SMEM((2 * t,), I32)] * (2 * TOP_K) + [
            pltpu.VMEM((2, TOP_K * t * SUBLANES, LANES), F32),
            pltpu.VMEM((t * SUBLANES, LANES), F32),
            pltpu.SemaphoreType.DMA((2,)),
            pltpu.SemaphoreType.DMA((2,)),
        ],
    )
    return pl.pallas_call(
        functools.partial(_combine_kernel, t),
        grid_spec=grid_spec,
        out_shape=jax.ShapeDtypeStruct((T, D), F32),
        compiler_params=_cparams(("arbitrary",)),
        name="combine",
    )(tile_cnt, tile_off, tile_dst, meta, ys, x1, mod3)


def _tile(n, target):
    t = min(n, target)
    while n % t:
        t //= 2
    return t


def _block_diag(w):
    n, bw, _ = w.shape
    eye = jnp.eye(n, dtype=w.dtype)
    return (eye[:, None, :, None] * w[:, :, None, :]).reshape(n * bw, n * bw)


def _layer(l, x2, B, S, c, ada_w, ada_b, norm1_g, w_in, q_norm_g, k_norm_g, lambda_q1, lambda_k1, lambda_q2,
           lambda_k2, attn_subln_g, conv_w, conv_b, lru_wa, lru_ba, lru_wx, lru_bx, lru_lambda, lru_out_g,
           w_out, norm2_g, router_w, router_b, w_gate_up, b_gate_up, w_down, b_down):
    T, D = x2.shape
    lam_init = 0.8 - 0.6 * math.exp(-0.3 * l)
    aw = D // 2
    lw = D - aw
    heads = aw // ATTN_DV
    qkw = heads * 2 * ATTN_DK

    mod, lam = _adaln(c, ada_w, ada_b, lambda_q1, lambda_k1, lambda_q2, lambda_k2, lam_init)
    mod3 = mod.reshape(c.shape[0], 6, D)

    reps = qkw // ATTN_DK
    gq = (jnp.tile(q_norm_g, reps) * (ATTN_DK ** -0.5 * math.log2(math.e))).reshape(1, qkw)
    gk = jnp.tile(k_norm_g, reps).reshape(1, qkw)
    tm = _tile(S, ROUTE_TILE_ROWS)
    q, k, v, xr, gr = _in_proj(x2, mod3, norm1_g.reshape(1, D), w_in.astype(BF16), gq, gk, S,
                               _tile(S, IN_PROJ_ROWS), qkw, aw, lw)

    attn = _attention(q, k, v, lam, attn_subln_g.reshape(1, ATTN_DV), B, S, _tile(S, ATTN_Q_ROWS),
                      _tile(S, ATTN_KV_ROWS), 1.0 - lam_init)

    w_gates = (0.5 * jnp.concatenate([_block_diag(lru_wa), _block_diag(lru_wx)], axis=1)).astype(BF16)
    b_gates = 0.5 * jnp.concatenate([lru_ba, lru_bx]).reshape(1, 2 * lw)
    lru = _lru(xr, gr, conv_w, conv_b.reshape(1, lw), w_gates, b_gates, lru_lambda.reshape(1, lw),
               lru_out_g.reshape(1, lw), B, S, _tile(S, LRU_CHUNK_ROWS))

    x1, h2t, meta, tile_cnt = _out_proj(
        attn, lru, x2, mod3, w_out.astype(BF16), norm2_g.reshape(1, D), router_w.T,
        router_b.reshape(N_EXPERTS, 1), S, tm, _tile(S // tm, OUT_PROJ_TILES))

    tile_cnt = tile_cnt[:, :, 0]
    counts = jnp.sum(tile_cnt, axis=0)
    padded = ((counts + MOE_BLOCK - 1) // MOE_BLOCK) * MOE_BLOCK
    pad_ends = jnp.cumsum(padded)
    pad_starts = pad_ends - padded
    n_blocks = (T * TOP_K) // MOE_BLOCK + N_EXPERTS
    blk_start = jnp.arange(n_blocks, dtype=I32) * MOE_BLOCK
    owner = blk_start[:, None] >= pad_ends[None, :]
    block_expert = jnp.minimum(jnp.sum(owner, axis=1), N_EXPERTS - 1).astype(I32)
    onehot = block_expert[:, None] == jnp.arange(N_EXPERTS, dtype=I32)[None, :]
    row_end = jnp.sum(jnp.where(onehot, (pad_starts + counts)[None, :], 0), axis=1)
    block_valid = jnp.clip(row_end - blk_start, 0, MOE_BLOCK).astype(I32)
    tile_off = jnp.cumsum(tile_cnt, axis=1) - tile_cnt
    tile_dst = pad_starts[None, :] + jnp.cumsum(tile_cnt, axis=0) - tile_cnt
    flat = lambda a: (a * SUBLANES).reshape(-1).astype(I32)

    meta = meta.reshape(-1)
    pad_info = jnp.concatenate([(pad_starts + counts) * SUBLANES, (padded - counts) * SUBLANES,
                                pad_ends[-1:] // MOE_BLOCK]).astype(I32)
    xs = _dispatch(flat(tile_cnt), flat(tile_off), flat(tile_dst), pad_info, meta, h2t, n_blocks, tm)
    ys = _experts(block_expert, block_valid, xs, w_gate_up, b_gate_up, w_down, b_down)
    return _combine(flat(tile_cnt), flat(tile_off), flat(tile_dst), meta, ys, x1, mod3, S, tm)


def kernel(x, c, ada_w, ada_b, norm1_g, w_in, q_norm_g, k_norm_g, lambda_q1, lambda_k1, lambda_q2, lambda_k2,
           attn_subln_g, conv_w, conv_b, lru_wa, lru_ba, lru_wx, lru_bx, lru_lambda, lru_out_g, w_out, norm2_g,
           router_w, router_b, w_gate_up, b_gate_up, w_down, b_down):
    B, S, D = x.shape
    params = (ada_w, ada_b, norm1_g, w_in, q_norm_g, k_norm_g, lambda_q1, lambda_k1, lambda_q2, lambda_k2,
              attn_subln_g, conv_w, conv_b, lru_wa, lru_ba, lru_wx, lru_bx, lru_lambda, lru_out_g, w_out,
              norm2_g, router_w, router_b, w_gate_up, b_gate_up, w_down, b_down)
    x2 = x.reshape(B * S, D)
    for l in range(ada_w.shape[0]):
        x2 = _layer(l, x2, B, S, c, *[p[l] for p in params])
    return x2.reshape(B, S, D)
```

```python
import functools
import math

import jax
import jax.numpy as jnp
from jax import lax
from jax.experimental import pallas as pl
from jax.experimental.pallas import tpu as pltpu

F32 = jnp.float32
BF16 = jnp.bfloat16
I32 = jnp.int32

ATTN_DK = 64
ATTN_DV = 2 * ATTN_DK
CONV_W = 4
LRU_C = 8.0
N_EXPERTS = 32
TOP_K = 4
SWIGLU_LIMIT = 7.0
SWIGLU_ALPHA = 1.702
MOE_BLOCK = 1024
EPS = 1e-6
NEG_BIG = -1e30

IN_PROJ_ROWS = 1024
ATTN_Q_ROWS = 1024
ATTN_KV_ROWS = 512
LRU_CHUNK_ROWS = 1024
ROUTE_TILE_ROWS = 512
OUT_PROJ_TILES = 2
BF16_EXACT_INT = 256.0

LANES = 128
SUBLANES = 8
MXU_DEPTH = 256
VMEM_LIMIT = 48 * 1024 * 1024


def _cparams(sem):
    return pltpu.CompilerParams(dimension_semantics=sem, vmem_limit_bytes=VMEM_LIMIT)


def _split_hi_lo(x):
    hi = x.astype(BF16)
    lo = (x - hi.astype(F32)).astype(BF16)
    return hi, lo


def _sigmoid(x):
    return 0.5 * jnp.tanh(0.5 * x) + 0.5


def _dot(a, b):
    return jnp.dot(a, b, preferred_element_type=F32)


def _dot_tb(a, b):
    return lax.dot_general(a, b, (((1,), (1,)), ((), ())), preferred_element_type=F32)


def _adaln_kernel(lam_init, c_ref, w_ref, b_ref, lq1, lk1, lq2, lk2, mod_ref, lam_ref):
    c = c_ref[...]
    s = c * jax.nn.sigmoid(c)
    s_hi, s_lo = _split_hi_lo(s)
    w = w_ref[...]
    w_hi, w_lo = _split_hi_lo(w)
    mod_ref[...] = _dot(s_hi, w_hi) + _dot(s_hi, w_lo) + _dot(s_lo, w_hi) + b_ref[...]
    d1 = jnp.sum(lq1[...] * lk1[...], axis=-1, keepdims=True)
    d2 = jnp.sum(lq2[...] * lk2[...], axis=-1, keepdims=True)
    lam = jnp.exp(d1) - jnp.exp(d2) + lam_init
    lam_ref[...] = jnp.broadcast_to(lam, lam_ref.shape)


def _adaln(c, ada_w, ada_b, lq1, lk1, lq2, lk2, lam_init):
    B, D = c.shape
    n = ada_w.shape[1] // D
    vec = lambda: pl.BlockSpec((1, ATTN_DK), lambda j: (0, 0))
    return pl.pallas_call(
        functools.partial(_adaln_kernel, lam_init),
        grid=(n,),
        in_specs=[
            pl.BlockSpec((B, D), lambda j: (0, 0)),
            pl.BlockSpec((D, D), lambda j: (0, j)),
            pl.BlockSpec((1, D), lambda j: (0, j)),
            vec(), vec(), vec(), vec(),
        ],
        out_specs=[
            pl.BlockSpec((B, D), lambda j: (0, j)),
            pl.BlockSpec((1, LANES), lambda j: (0, 0)),
        ],
        out_shape=[
            jax.ShapeDtypeStruct((B, n * D), F32),
            jax.ShapeDtypeStruct((1, LANES), F32),
        ],
        compiler_params=_cparams(("arbitrary",)),
        name="adaln",
    )(c, ada_w, ada_b.reshape(1, -1), lq1.reshape(1, -1), lk1.reshape(1, -1),
      lq2.reshape(1, -1), lk2.reshape(1, -1))


def _rms_modulate(x, g, shift, scale):
    ms = jnp.mean(x * x, axis=-1, keepdims=True)
    y = x * lax.rsqrt(ms + EPS) * g
    return y * (1.0 + scale) + shift


def _group_rms_scale(q, group_ones):
    sq = (q * q).astype(BF16)
    w = group_ones.shape[0]
    ss = jnp.concatenate([_dot(sq[:, c:c + w], group_ones) for c in range(0, q.shape[1], w)], axis=1)
    return lax.rsqrt(ss * (1.0 / ATTN_DK) + EPS)


def _in_proj_kernel(qkw, aw, lw, x_ref, mod_ref, g_ref, w_ref, gq_ref, gk_ref, ones_ref,
                    q_ref, k_ref, v_ref, xr_ref, gr_ref):
    h = _rms_modulate(x_ref[...], g_ref[...], mod_ref[0:1, :], mod_ref[1:2, :])
    hb = h.astype(BF16)
    group_ones = ones_ref[...]
    o = 0
    q = _dot(hb, w_ref[:, o:o + qkw]); o += qkw
    q_ref[...] = (q * _group_rms_scale(q, group_ones) * gq_ref[...]).astype(BF16)
    k = _dot(hb, w_ref[:, o:o + qkw]); o += qkw
    k_ref[...] = (k * _group_rms_scale(k, group_ones) * gk_ref[...]).astype(BF16)
    v_ref[...] = _dot(hb, w_ref[:, o:o + aw]).astype(BF16); o += aw
    xr_ref[...] = _dot(hb, w_ref[:, o:o + lw]); o += lw
    gr_ref[...] = _dot(hb, w_ref[:, o:o + lw])


def _in_proj(x2, mod3, norm_g, w_in_b, gq, gk, S, tm, qkw, aw, lw):
    T, D = x2.shape
    tps = S // tm
    grp = jnp.arange(MXU_DEPTH, dtype=I32) // ATTN_DK
    group_ones = (grp[:, None] == grp[None, :]).astype(BF16)
    const = lambda shape: pl.BlockSpec(shape, lambda i: (0,) * len(shape))
    row = lambda w: pl.BlockSpec((tm, w), lambda i: (i, 0))
    return pl.pallas_call(
        functools.partial(_in_proj_kernel, qkw, aw, lw),
        grid=(T // tm,),
        in_specs=[
            row(D),
            pl.BlockSpec((None, 6, D), lambda i: (i // tps, 0, 0)),
            const((1, D)),
            const(w_in_b.shape),
            const((1, qkw)), const((1, qkw)),
            const((MXU_DEPTH, MXU_DEPTH)),
        ],
        out_specs=[row(qkw), row(qkw), row(aw), row(lw), row(lw)],
        out_shape=[
            jax.ShapeDtypeStruct((T, qkw), BF16),
            jax.ShapeDtypeStruct((T, qkw), BF16),
            jax.ShapeDtypeStruct((T, aw), BF16),
            jax.ShapeDtypeStruct((T, lw), F32),
            jax.ShapeDtypeStruct((T, lw), F32),
        ],
        compiler_params=_cparams(("arbitrary",)),
        name="in_proj",
    )(x2, mod3, norm_g, w_in_b, gq, gk, group_ones)


ATTN_LOOP_ROWS = 256
ATTN_DIAG_ROWS = 128


def _attn_kernel(bq, bk, out_scale, q_ref, k_ref, v_ref, lam_ref, g_ref, o_ref,
                 qq_ref, vp_ref, s_ref, sd_ref, m_ref, acc_ref):
    n_rows = 2 * bq

    col = lax.broadcasted_iota(I32, (vp_ref.shape[0], ATTN_DV), 1)
    vp_ref[:, :ATTN_DV] = v_ref[...]
    vp_ref[:, ATTN_DV:] = jnp.where(col == 0, 1.0, 0.0).astype(BF16)

    def query_block(i, carry):
        q_rows = pl.ds(pl.multiple_of(i * bq, bq), bq)
        _attn_query_block(i, q_rows, bq, bk, n_rows, out_scale, q_ref, k_ref, lam_ref, g_ref, o_ref,
                          qq_ref, vp_ref, s_ref, sd_ref, m_ref, acc_ref)
        return carry

    lax.fori_loop(0, q_ref.shape[0] // bq, query_block, 0)


def _attn_query_block(i, q_rows, bq, bk, n_rows, out_scale, q_ref, k_ref, lam_ref, g_ref, o_ref,
                      qq_ref, vp_ref, s_ref, sd_ref, m_ref, acc_ref):
    q = q_ref[q_rows, :]
    lane = lax.broadcasted_iota(I32, q.shape, 1)
    zero = jnp.zeros_like(q)
    qq_ref[0:bq, :] = jnp.where(lane < ATTN_DK, q, zero)
    qq_ref[bq:, :] = jnp.where(lane >= ATTN_DK, q, zero)
    m_ref[...] = jnp.full(m_ref.shape, NEG_BIG, F32)
    acc_ref[...] = jnp.zeros_like(acc_ref)

    def scores(rows, start, kw):
        return _dot_tb(qq_ref[rows, :], k_ref[pl.ds(start, kw), :])

    def softmax_pv(s, rows, start):
        kw = s.shape[1]
        m_old = m_ref[rows, :]
        m_new = jnp.maximum(m_old, jnp.max(s, axis=-1, keepdims=True))
        alpha = jnp.exp2(m_old - m_new)
        p = jnp.exp2(s - jnp.tile(m_new, (1, kw // LANES)))
        pv = _dot(p.astype(BF16), vp_ref[pl.ds(start, kw), :])
        acc_ref[rows, :] = acc_ref[rows, :] * jnp.tile(alpha, (1, 2)) + pv
        m_ref[rows, :] = m_new

    loop_chunks = [pl.ds(r, ATTN_LOOP_ROWS) for r in range(0, n_rows, ATTN_LOOP_ROWS)]
    for rows in loop_chunks:
        s_ref[rows, :] = scores(rows, 0, bk)

    def body(j, carry):
        start = pl.multiple_of(j * bk, bk)
        for rows in loop_chunks:
            s = s_ref[rows, :]
            s_ref[rows, :] = scores(rows, start + bk, bk)
            softmax_pv(s, rows, start)
        return carry

    n_full = i * (bq // bk)
    lax.fori_loop(0, n_full, body, 0)

    start = pl.multiple_of(n_full * bk, bk)
    rc = ATTN_DIAG_ROWS
    tri = (lax.broadcasted_iota(I32, (rc, rc), 1) <= lax.broadcasted_iota(I32, (rc, rc), 0))
    diag_chunks = [(pl.ds(r, rc), r % bq) for r in range(0, n_rows, rc)]
    for rows, q0 in diag_chunks:
        if q0 + rc > bk:
            sd_ref[rows, 0:q0 + rc - bk] = scores(rows, start + bk, q0 + rc - bk)
    for rows, q0 in diag_chunks:
        kw = q0 + rc
        parts = [s_ref[rows, 0:min(q0, bk)]] if q0 else []
        if q0 > bk:
            parts.append(sd_ref[rows, 0:q0 - bk])
        last = s_ref[rows, q0:kw] if kw <= bk else sd_ref[rows, q0 - bk:kw - bk]
        parts.append(jnp.where(tri, last, NEG_BIG))
        softmax_pv(parts[0] if len(parts) == 1 else jnp.concatenate(parts, axis=1), rows, start)

    acc = acc_ref[...]
    o = acc[:, :ATTN_DV] / acc[:, ATTN_DV:ATTN_DV + 1]
    a = o[:bq] - lam_ref[0:1, 0:1] * o[bq:]
    ms = jnp.mean(a * a, axis=-1, keepdims=True)
    o_ref[q_rows, :] = (a * lax.rsqrt(ms + EPS) * g_ref[...] * out_scale).astype(BF16)


def _attention(q, k, v, lam, subln_g, B, S, bq, bk, out_scale):
    T, qkw = q.shape
    H = qkw // ATTN_DV
    nq = S // bq
    return pl.pallas_call(
        functools.partial(_attn_kernel, bq, bk, out_scale),
        grid=(B, H),
        in_specs=[
            pl.BlockSpec((S, ATTN_DV), lambda b, h: (b, h)),
            pl.BlockSpec((S, ATTN_DV), lambda b, h: (b, h)),
            pl.BlockSpec((S, ATTN_DV), lambda b, h: (b, h)),
            pl.BlockSpec((1, LANES), lambda b, h: (0, 0)),
            pl.BlockSpec((1, ATTN_DV), lambda b, h: (0, 0)),
        ],
        out_specs=pl.BlockSpec((S, ATTN_DV), lambda b, h: (b, h)),
        out_shape=jax.ShapeDtypeStruct((T, H * ATTN_DV), BF16),
        scratch_shapes=[
            pltpu.VMEM((2 * bq, ATTN_DV), BF16),
            pltpu.VMEM((S, 2 * ATTN_DV), BF16),
            pltpu.VMEM((2 * bq, bk), F32),
            pltpu.VMEM((2 * bq, max(bq - bk, LANES)), F32),
            pltpu.VMEM((2 * bq, LANES), F32),
            pltpu.VMEM((2 * bq, 2 * ATTN_DV), F32),
        ],
        compiler_params=_cparams(("arbitrary", "arbitrary")),
        name="attention",
    )(q, k, v, lam, subln_g)


def _lru_kernel(tc, cw, pitch, x_ref, gate_ref, cw_ref, cb_ref, wg_ref, bg_ref, lam_ref, og_ref,
                o_ref, ext_ref, a_ref, b_ref, hc_ref):
    c = pl.program_id(1)

    @pl.when(c == 0)
    def _():
        ext_ref[0:SUBLANES, :] = jnp.zeros((SUBLANES, cw), F32)
        hc_ref[...] = jnp.zeros_like(hc_ref)

    ext_ref[SUBLANES:SUBLANES + tc, :] = x_ref[...]
    xc = cb_ref[...] + cw_ref[CONV_W - 1:CONV_W, :] * x_ref[...]
    for w in range(CONV_W - 1):
        sh = CONV_W - 1 - w
        xc = xc + cw_ref[w:w + 1, :] * ext_ref[SUBLANES - sh:SUBLANES - sh + tc, :]
    tail = ext_ref[tc:tc + SUBLANES, :]
    ext_ref[0:SUBLANES, :] = tail

    g = _dot(xc.astype(BF16), wg_ref[...]) + bg_ref[...]
    t_r = jnp.tanh(g[:, :cw])
    t_i = jnp.tanh(g[:, cw:])
    nl = -lam_ref[...]
    softplus = jnp.maximum(nl, 0.0) + jnp.log1p(jnp.exp(-jnp.abs(nl)))
    c1 = (-0.5 * LRU_C) * softplus
    log_a = c1 * t_r + c1
    th = jnp.tanh(log_a)
    p = -2.0 * th
    root = jnp.where(p > 0.0, p * lax.rsqrt(p * (1.0 - th)), 0.0)
    a = jnp.exp(log_a)
    b = root * ((t_i + 1.0) * (0.5 * xc))

    steps = tc // SUBLANES
    n_lt = cw // LANES
    for q in range(SUBLANES):
        for l in range(n_lt):
            a_ref[l, q * pitch:q * pitch + steps, :] = a[q * steps:(q + 1) * steps, l * LANES:(l + 1) * LANES]
            b_ref[l, q * pitch:q * pitch + steps, :] = b[q * steps:(q + 1) * steps, l * LANES:(l + 1) * LANES]

    def scan_step(t, carry):
        out = []
        for l in range(n_lt):
            h, prod = carry[l]
            rows = pl.ds(t, SUBLANES, stride=pitch)
            a_t = a_ref[l, rows, :]
            h = a_t * h + b_ref[l, rows, :]
            prod = a_t * prod
            b_ref[l, rows, :] = h
            a_ref[l, rows, :] = prod
            out.append((h, prod))
        return tuple(out)

    init = tuple((jnp.zeros((SUBLANES, LANES), F32), jnp.ones((SUBLANES, LANES), F32)) for _ in range(n_lt))
    last = lax.fori_loop(0, steps, scan_step, init, unroll=4)

    h_cols = []
    for l in range(n_lt):
        h_last, p_last = last[l]
        entering = [hc_ref[0:1, l * LANES:(l + 1) * LANES]]
        for q in range(SUBLANES):
            entering.append(h_last[q:q + 1, :] + p_last[q:q + 1, :] * entering[q])
        hc_ref[:, l * LANES:(l + 1) * LANES] = jnp.broadcast_to(entering[SUBLANES], (SUBLANES, LANES))
        h_cols.append(jnp.concatenate(
            [b_ref[l, q * pitch:q * pitch + steps, :] + a_ref[l, q * pitch:q * pitch + steps, :] * entering[q]
             for q in range(SUBLANES)], axis=0))
    h = jnp.concatenate(h_cols, axis=1)

    gt = gate_ref[...]
    gelu = 0.5 * gt * (1.0 + jnp.tanh(math.sqrt(2.0 / math.pi) * (gt + 0.044715 * gt * gt * gt)))
    y = h * gelu
    ms = jnp.mean(y * y, axis=-1, keepdims=True)
    o_ref[...] = (y * lax.rsqrt(ms + EPS) * og_ref[...]).astype(BF16)


def _scan_pitch(steps):
    return steps if (steps // SUBLANES) % 2 else steps + SUBLANES


def _lru(xr, gr, conv_w, conv_b, w_gates, b_gates, lru_lambda, out_g, B, S, tc):
    T, cw = xr.shape
    nc = S // tc
    pitch = _scan_pitch(tc // SUBLANES)
    const = lambda shape: pl.BlockSpec(shape, lambda b, c: (0,) * len(shape))
    row = pl.BlockSpec((tc, cw), lambda b, c: (b * nc + c, 0))
    return pl.pallas_call(
        functools.partial(_lru_kernel, tc, cw, pitch),
        grid=(B, nc),
        in_specs=[row, row, const((CONV_W, cw)), const((1, cw)), const((cw, 2 * cw)),
                  const((1, 2 * cw)), const((1, cw)), const((1, cw))],
        out_specs=row,
        out_shape=jax.ShapeDtypeStruct((T, cw), BF16),
        scratch_shapes=[
            pltpu.VMEM((tc + SUBLANES, cw), F32),
            pltpu.VMEM((cw // LANES, SUBLANES * pitch, LANES), F32),
            pltpu.VMEM((cw // LANES, SUBLANES * pitch, LANES), F32),
            pltpu.VMEM((SUBLANES, cw), F32),
        ],
        compiler_params=_cparams(("arbitrary", "arbitrary")),
        name="lru",
    )(xr, gr, conv_w, conv_b, w_gates, b_gates, lru_lambda, out_g)


def _out_proj_kernel(tm, n_sub, aw, *refs):
    for u in range(n_sub):
        _out_proj_tile(u, tm, aw, *refs)


def _out_proj_tile(u, tm, aw, attn_ref, lru_ref, x_ref, mod_ref, w_ref, g_ref, rw_ref, rb_ref, tri_ref, low_ref,
                   x1_ref, h2_ref, meta_ref, cnt_ref):
    rows = pl.ds(u * tm, tm)
    mix = _dot(attn_ref[rows, :], w_ref[0:aw, :]) + _dot(lru_ref[rows, :], w_ref[aw:, :])
    x1 = x_ref[rows, :] + mod_ref[2:3, :] * mix
    x1_ref[rows, :] = x1
    h2 = _rms_modulate(x1, g_ref[...], mod_ref[3:4, :], mod_ref[4:5, :])
    for j in range(h2.shape[1] // LANES):
        h2_ref[pl.ds(u * tm * SUBLANES + j, tm, stride=SUBLANES), :] = h2[:, j * LANES:(j + 1) * LANES]

    h_hi, h_lo = _split_hi_lo(h2)
    w_hi, w_lo = _split_hi_lo(rw_ref[...])
    by_hi = _dot_tb(jnp.concatenate([w_hi, w_lo], axis=0), h_hi)
    lg = by_hi[:N_EXPERTS] + by_hi[N_EXPERTS:] + _dot_tb(w_hi, h_lo) + rb_ref[...]

    eidx = lax.broadcasted_iota(I32, lg.shape, 0)
    picked = jnp.zeros(lg.shape, F32)
    vals, idxs = [], []
    for _ in range(TOP_K):
        m = jnp.max(lg, axis=0, keepdims=True)
        idx = jnp.min(jnp.where(lg == m, eidx, N_EXPERTS), axis=0, keepdims=True)
        sel = eidx == idx
        lg = jnp.where(sel, -jnp.inf, lg)
        picked = picked + sel.astype(F32)
        vals.append(m)
        idxs.append(idx)
    es = [jnp.exp(v - vals[0]) for v in vals]
    den = es[0] + es[1] + es[2] + es[3]
    gates = jnp.concatenate([e / den for e in es], axis=0)

    cnt = jnp.sum(picked, axis=1, keepdims=True)
    cnt_b = jnp.broadcast_to(cnt, cnt_ref.shape[1:])
    cnt_hi = jnp.floor(cnt_b * (1.0 / BF16_EXACT_INT))
    cnt_lo = cnt_b - BF16_EXACT_INT * cnt_hi
    run_start = BF16_EXACT_INT * _dot(low_ref[...], cnt_hi.astype(BF16)) + _dot(low_ref[...], cnt_lo.astype(BF16))
    before = _dot(picked.astype(BF16), tri_ref[...]) + run_start[:, 0:1]
    pos = [jnp.sum(jnp.where(eidx == idx, before, 0.0), axis=0, keepdims=True) for idx in idxs]
    meta_ref[u, 0:TOP_K, :] = jnp.concatenate(pos, axis=0).astype(I32) * SUBLANES
    meta_ref[u, TOP_K:, :] = lax.bitcast_convert_type(gates, I32)
    cnt_ref[u] = cnt_b.astype(I32)


def _out_proj(attn, lru, x2, mod3, w_out_b, norm_g, router_wt, router_b, S, tm, n_sub):
    T, D = x2.shape
    aw = attn.shape[1]
    lw = lru.shape[1]
    ts = tm * n_sub
    tps = S // ts
    tri = (jnp.arange(tm, dtype=I32)[:, None] < jnp.arange(tm, dtype=I32)[None, :]).astype(BF16)
    experts = jnp.arange(N_EXPERTS, dtype=I32)
    low = (experts[None, :] < experts[:, None]).astype(BF16)
    const = lambda shape: pl.BlockSpec(shape, lambda i: (0,) * len(shape))
    row = lambda w: pl.BlockSpec((ts, w), lambda i: (i, 0))
    return pl.pallas_call(
        functools.partial(_out_proj_kernel, tm, n_sub, aw),
        grid=(T // ts,),
        in_specs=[
            row(aw), row(lw), row(D),
            pl.BlockSpec((None, 6, D), lambda i: (i // tps, 0, 0)),
            const((D, D)), const((1, D)), const((N_EXPERTS, D)), const((N_EXPERTS, 1)),
            const((tm, tm)), const((N_EXPERTS, N_EXPERTS)),
        ],
        out_specs=[
            row(D),
            pl.BlockSpec((ts * SUBLANES, LANES), lambda i: (i, 0)),
            pl.BlockSpec((n_sub, 2 * TOP_K, tm), lambda i: (i, 0, 0)),
            pl.BlockSpec((n_sub, N_EXPERTS, LANES), lambda i: (i, 0, 0)),
        ],
        out_shape=[
            jax.ShapeDtypeStruct((T, D), F32),
            jax.ShapeDtypeStruct((T * SUBLANES, LANES), F32),
            jax.ShapeDtypeStruct((T // tm, 2 * TOP_K, tm), I32),
            jax.ShapeDtypeStruct((T // tm, N_EXPERTS, LANES), I32),
        ],
        compiler_params=_cparams(("arbitrary",)),
        name="out_proj",
    )(attn, lru, x2, mod3, w_out_b, norm_g, router_wt, router_b, tri, low)


def _rows(ref, first_row, n_rows):
    return ref.at[pl.ds(pl.multiple_of(first_row * SUBLANES, SUBLANES), n_rows * SUBLANES), :]


def _row_at(ref, sublane_offset):
    return ref.at[pl.ds(pl.multiple_of(sublane_offset, SUBLANES), SUBLANES), :]


def _fetch_tile(meta_hbm, smems, sems, t):
    i = pl.program_id(0)
    n = pl.num_programs(0)
    slot = i % 2
    per_tile = 2 * TOP_K * t

    def copies(tile, s):
        return [pltpu.make_async_copy(meta_hbm.at[pl.ds(pl.multiple_of(tile * per_tile + j * t, t), t)],
                                      smem.at[pl.ds(pl.multiple_of(s * t, t), t)], sems.at[s])
                for j, smem in enumerate(smems)]

    @pl.when(i == 0)
    def _():
        for c in copies(0, 0):
            c.start()

    for c in copies(i, slot):
        c.wait()

    @pl.when(i + 1 < n)
    def _():
        for c in copies(i + 1, 1 - slot):
            c.start()

    return slot * t


def _run_copies(t, tile, cnt_ref, off_ref, dst_ref, make_copy):
    def expert(e, carry):
        n = cnt_ref[tile * N_EXPERTS + e]
        off = off_ref[tile * N_EXPERTS + e]
        dst = dst_ref[tile * N_EXPERTS + e]
        for b in reversed(range(t.bit_length())):
            size = SUBLANES << b
            done = n & ~(2 * size - 1)

            @pl.when((n & size) != 0)
            def _():
                make_copy(off + done, dst + done, size).start()
        return carry

    lax.fori_loop(0, N_EXPERTS, expert, 0)


def _span(ref, first_sublane, n_sublanes):
    return ref.at[pl.ds(pl.multiple_of(first_sublane, SUBLANES), n_sublanes), :]


def _dispatch_kernel(t, n_blocks, cnt_ref, off_ref, dst_ref, pad_ref, meta_hbm, h_ref, xs_hbm,
                     pos0, pos1, pos2, pos3, stage_ref, zero_ref, psem, sem, zsem):
    i = pl.program_id(0)
    n = pl.num_programs(0)
    slot = i % 2

    def zero_fill(go):
        def expert(e, carry):
            first = pad_ref[e]
            length = pad_ref[N_EXPERTS + e]
            for b in reversed(range((MOE_BLOCK - 1).bit_length())):
                size = SUBLANES << b
                done = length & ~(2 * size - 1)

                @pl.when((length & size) != 0)
                def _():
                    go(pltpu.make_async_copy(_span(zero_ref, 0, size), _span(xs_hbm, first + done, size), zsem))
            return carry

        lax.fori_loop(0, N_EXPERTS, expert, 0)

        def block(b, carry):
            go(pltpu.make_async_copy(zero_ref, _rows(xs_hbm, b * MOE_BLOCK, MOE_BLOCK), zsem))
            return carry

        lax.fori_loop(pad_ref[2 * N_EXPERTS], n_blocks, block, 0)

    @pl.when(i == 0)
    def _():
        zero_ref[...] = jnp.zeros_like(zero_ref)
        zero_fill(lambda c: c.start())

    pos = (pos0, pos1, pos2, pos3)
    base = _fetch_tile(meta_hbm, pos, psem, t)
    stage = stage_ref.at[slot]

    def place(r, carry):
        row = _rows(h_ref, r, 1)[...]
        for k in range(TOP_K):
            _row_at(stage, pos[k][base + r])[...] = row
        return carry

    lax.fori_loop(0, t, place, 0, unroll=8)

    def all_runs(s):
        return pltpu.make_async_copy(stage_ref.at[s], _rows(xs_hbm, 0, TOP_K * t), sem.at[s])

    _run_copies(t, i, cnt_ref, off_ref, dst_ref,
                lambda a, b, m: pltpu.make_async_copy(_span(stage, a, m), _span(xs_hbm, b, m), sem.at[slot]))

    @pl.when(i > 0)
    def _():
        all_runs(1 - slot).wait()

    @pl.when(i == n - 1)
    def _():
        all_runs(slot).wait()
        zero_fill(lambda c: c.wait())


def _dispatch(tile_cnt, tile_off, tile_dst, pad_info, meta, h2t, n_blocks, t):
    n_tiles = meta.shape[0] // (2 * TOP_K * t)
    n_slots = n_blocks * MOE_BLOCK
    grid_spec = pltpu.PrefetchScalarGridSpec(
        num_scalar_prefetch=4,
        grid=(n_tiles,),
        in_specs=[pl.BlockSpec(memory_space=pl.ANY),
                  pl.BlockSpec((t * SUBLANES, LANES), lambda i, *_: (i, 0))],
        out_specs=pl.BlockSpec(memory_space=pl.ANY),
        scratch_shapes=[pltpu.SMEM((2 * t,), I32)] * TOP_K + [
            pltpu.VMEM((2, TOP_K * t * SUBLANES, LANES), F32),
            pltpu.VMEM((MOE_BLOCK * SUBLANES, LANES), F32),
            pltpu.SemaphoreType.DMA((2,)),
            pltpu.SemaphoreType.DMA((2,)),
            pltpu.SemaphoreType.DMA,
        ],
    )
    return pl.pallas_call(
        functools.partial(_dispatch_kernel, t, n_blocks),
        grid_spec=grid_spec,
        out_shape=jax.ShapeDtypeStruct((n_slots * SUBLANES, LANES), F32),
        compiler_params=_cparams(("arbitrary",)),
        name="dispatch",
    )(tile_cnt, tile_off, tile_dst, pad_info, meta, h2t)


def _experts_kernel(ff, be_ref, nv_ref, nxt_ref, xs_ref, wgu_hbm, bgu_ref, wdn_hbm, bdn_ref, ys_ref,
                    wgu_f, wdn_f, wgu_b, wdn_b, sem):
    i = pl.program_id(0)
    nv = nv_ref[i]
    n_slabs = xs_ref.shape[0] // MOE_BLOCK

    def fetch(e):
        return (pltpu.make_async_copy(wgu_hbm.at[e], wgu_f, sem.at[0]),
                pltpu.make_async_copy(wdn_hbm.at[e], wdn_f, sem.at[1]))

    @pl.when(i == 0)
    def _():
        for c in fetch(be_ref[0]):
            c.start()

    @pl.when((i == 0) | (be_ref[i] != be_ref[jnp.maximum(i - 1, 0)]))
    def _():
        for c in fetch(be_ref[i]):
            c.wait()
        wgu_b[...] = wgu_f[...].astype(BF16)
        wdn_b[...] = wdn_f[...].astype(BF16)

        @pl.when(nxt_ref[i] >= 0)
        def _():
            for c in fetch(nxt_ref[i]):
                c.start()

    @pl.when(nv > 0)
    def _():
        x = jnp.concatenate(
            [xs_ref[pl.ds(j, MOE_BLOCK, stride=SUBLANES), :] for j in range(n_slabs)], axis=1)
        gu = _dot(x.astype(BF16), wgu_b[...]) + bgu_ref[...]
        gate = jnp.minimum(gu[:, :ff], SWIGLU_LIMIT)
        up = jnp.clip(gu[:, ff:], -SWIGLU_LIMIT, SWIGLU_LIMIT)
        act = (up + 1.0) * (gate * _sigmoid(SWIGLU_ALPHA * gate))
        y = _dot(act.astype(BF16), wdn_b[...]) + bdn_ref[...]
        for j in range(n_slabs):
            ys_ref[pl.ds(j, MOE_BLOCK, stride=SUBLANES), :] = y[:, j * LANES:(j + 1) * LANES]

    @pl.when(nv == 0)
    def _():
        ys_ref[...] = jnp.zeros_like(ys_ref)


def _experts(block_expert, block_valid, xs, w_gu, b_gu, w_dn, b_dn):
    n_blocks = block_expert.shape[0]
    E, D, ff2 = w_gu.shape
    ff = ff2 // 2
    later = block_expert[None, :] > block_expert[:, None]
    nxt = jnp.min(jnp.where(later, block_expert[None, :], E), axis=1)
    nxt = jnp.where(nxt < E, nxt, -1)
    rows = pl.BlockSpec((MOE_BLOCK * SUBLANES, LANES), lambda i, *_: (i, 0))
    grid_spec = pltpu.PrefetchScalarGridSpec(
        num_scalar_prefetch=3,
        grid=(n_blocks,),
        in_specs=[
            rows,
            pl.BlockSpec(memory_space=pl.ANY),
            pl.BlockSpec((None, 1, ff2), lambda i, be, *_: (be[i], 0, 0)),
            pl.BlockSpec(memory_space=pl.ANY),
            pl.BlockSpec((None, 1, D), lambda i, be, *_: (be[i], 0, 0)),
        ],
        out_specs=rows,
        scratch_shapes=[
            pltpu.VMEM((D, ff2), F32), pltpu.VMEM((ff, D), F32),
            pltpu.VMEM((D, ff2), BF16), pltpu.VMEM((ff, D), BF16),
            pltpu.SemaphoreType.DMA((2,)),
        ],
    )
    return pl.pallas_call(
        functools.partial(_experts_kernel, ff),
        grid_spec=grid_spec,
        out_shape=jax.ShapeDtypeStruct(xs.shape, F32),
        compiler_params=_cparams(("arbitrary",)),
        name="experts",
    )(block_expert, block_valid, nxt.astype(I32), xs, w_gu, b_gu.reshape(E, 1, ff2), w_dn,
      b_dn.reshape(E, 1, D))


def _combine_kernel(t, cnt_ref, off_ref, dst_ref, meta_hbm, ys_hbm, x1_ref, mod_ref, o_ref,
                    pos0, pos1, pos2, pos3, gate0, gate1, gate2, gate3, stage_ref, y_ref, psem, sem):
    i = pl.program_id(0)
    n = pl.num_programs(0)
    slot = i % 2
    pos = (pos0, pos1, pos2, pos3)
    gates = (gate0, gate1, gate2, gate3)
    base = _fetch_tile(meta_hbm, pos + gates, psem, t)

    def fetch_runs(tile, s):
        stage = stage_ref.at[s]
        _run_copies(t, tile, cnt_ref, off_ref, dst_ref,
                    lambda a, b, m: pltpu.make_async_copy(_span(ys_hbm, b, m), _span(stage, a, m), sem.at[s]))

    @pl.when(i == 0)
    def _():
        fetch_runs(0, 0)

    @pl.when(i + 1 < n)
    def _():
        fetch_runs(i + 1, 1 - slot)

    pltpu.make_async_copy(_rows(ys_hbm, 0, TOP_K * t), stage_ref.at[slot], sem.at[slot]).wait()
    stage = stage_ref.at[slot]

    def gather(r, carry):
        acc = None
        for k in range(TOP_K):
            gate = lax.bitcast_convert_type(gates[k][base + r], F32)
            term = gate * _row_at(stage, pos[k][base + r])[...]
            acc = term if acc is None else acc + term
        _rows(y_ref, r, 1)[...] = acc
        return carry

    lax.fori_loop(0, t, gather, 0, unroll=8)

    n_slabs = o_ref.shape[1] // LANES
    y = jnp.concatenate([y_ref[pl.ds(j, t, stride=SUBLANES), :] for j in range(n_slabs)], axis=1)
    o_ref[...] = x1_ref[...] + mod_ref[5:6, :] * y


def _combine(tile_cnt, tile_off, tile_dst, meta, ys, x1, mod3, S, t):
    T, D = x1.shape
    n_tiles = T // t
    tps = S // t
    grid_spec = pltpu.PrefetchScalarGridSpec(
        num_scalar_prefetch=3,
        grid=(n_tiles,),
        in_specs=[
            pl.BlockSpec(memory_space=pl.ANY),
            pl.BlockSpec(memory_space=pl.ANY),
            pl.BlockSpec((t, D), lambda i, *_: (i, 0)),
            pl.BlockSpec((None, 6, D), lambda i, *_: (i // tps, 0, 0)),
        ],
        out_specs=pl.BlockSpec((t, D), lambda i, *_: (i, 0)),
        scratch_shapes=[pltpu.SMEM((2 * t,), I32)] * (2 * TOP_K) + [
            pltpu.VMEM((2, TOP_K * t * SUBLANES, LANES), F32),
            pltpu.VMEM((t * SUBLANES, LANES), F32),
            pltpu.SemaphoreType.DMA((2,)),
            pltpu.SemaphoreType.DMA((2,)),
        ],
    )
    return pl.pallas_call(
        functools.partial(_combine_kernel, t),
        grid_spec=grid_spec,
        out_shape=jax.ShapeDtypeStruct((T, D), F32),
        compiler_params=_cparams(("arbitrary",)),
        name="combine",
    )(tile_cnt, tile_off, tile_dst, meta, ys, x1, mod3)


def _tile(n, target):
    t = min(n, target)
    while n % t:
        t //= 2
    return t


def _block_diag(w):
    n, bw, _ = w.shape
    eye = jnp.eye(n, dtype=w.dtype)
    return (eye[:, None, :, None] * w[:, :, None, :]).reshape(n * bw, n * bw)


def _layer(l, x2, B, S, c, ada_w, ada_b, norm1_g, w_in, q_norm_g, k_norm_g, lambda_q1, lambda_k1, lambda_q2,
           lambda_k2, attn_subln_g, conv_w, conv_b, lru_wa, lru_ba, lru_wx, lru_bx, lru_lambda, lru_out_g,
           w_out, norm2_g, router_w, router_b, w_gate_up, b_gate_up, w_down, b_down):
    T, D = x2.shape
    lam_init = 0.8 - 0.6 * math.exp(-0.3 * l)
    aw = D // 2
    lw = D - aw
    heads = aw // ATTN_DV
    qkw = heads * 2 * ATTN_DK

    mod, lam = _adaln(c, ada_w, ada_b, lambda_q1, lambda_k1, lambda_q2, lambda_k2, lam_init)
    mod3 = mod.reshape(c.shape[0], 6, D)

    reps = qkw // ATTN_DK
    gq = (jnp.tile(q_norm_g, reps) * (ATTN_DK ** -0.5 * math.log2(math.e))).reshape(1, qkw)
    gk = jnp.tile(k_norm_g, reps).reshape(1, qkw)
    tm = _tile(S, ROUTE_TILE_ROWS)
    q, k, v, xr, gr = _in_proj(x2, mod3, norm1_g.reshape(1, D), w_in.astype(BF16), gq, gk, S,
                               _tile(S, IN_PROJ_ROWS), qkw, aw, lw)

    attn = _attention(q, k, v, lam, attn_subln_g.reshape(1, ATTN_DV), B, S, _tile(S, ATTN_Q_ROWS),
                      _tile(S, ATTN_KV_ROWS), 1.0 - lam_init)

    w_gates = (0.5 * jnp.concatenate([_block_diag(lru_wa), _block_diag(lru_wx)], axis=1)).astype(BF16)
    b_gates = 0.5 * jnp.concatenate([lru_ba, lru_bx]).reshape(1, 2 * lw)
    lru = _lru(xr, gr, conv_w, conv_b.reshape(1, lw), w_gates, b_gates, lru_lambda.reshape(1, lw),
               lru_out_g.reshape(1, lw), B, S, _tile(S, LRU_CHUNK_ROWS))

    x1, h2t, meta, tile_cnt = _out_proj(
        attn, lru, x2, mod3, w_out.astype(BF16), norm2_g.reshape(1, D), router_w.T,
        router_b.reshape(N_EXPERTS, 1), S, tm, _tile(S // tm, OUT_PROJ_TILES))

    tile_cnt = tile_cnt[:, :, 0]
    counts = jnp.sum(tile_cnt, axis=0)
    padded = ((counts + MOE_BLOCK - 1) // MOE_BLOCK) * MOE_BLOCK
    pad_ends = jnp.cumsum(padded)
    pad_starts = pad_ends - padded
    n_blocks = (T * TOP_K) // MOE_BLOCK + N_EXPERTS
    blk_start = jnp.arange(n_blocks, dtype=I32) * MOE_BLOCK
    owner = blk_start[:, None] >= pad_ends[None, :]
    block_expert = jnp.minimum(jnp.sum(owner, axis=1), N_EXPERTS - 1).astype(I32)
    onehot = block_expert[:, None] == jnp.arange(N_EXPERTS, dtype=I32)[None, :]
    row_end = jnp.sum(jnp.where(onehot, (pad_starts + counts)[None, :], 0), axis=1)
    block_valid = jnp.clip(row_end - blk_start, 0, MOE_BLOCK).astype(I32)
    tile_off = jnp.cumsum(tile_cnt, axis=1) - tile_cnt
    tile_dst = pad_starts[None, :] + jnp.cumsum(tile_cnt, axis=0) - tile_cnt
    flat = lambda a: (a * SUBLANES).reshape(-1).astype(I32)

    meta = meta.reshape(-1)
    pad_info = jnp.concatenate([(pad_starts + counts) * SUBLANES, (padded - counts) * SUBLANES,
                                pad_ends[-1:] // MOE_BLOCK]).astype(I32)
    xs = _dispatch(flat(tile_cnt), flat(tile_off), flat(tile_dst), pad_info, meta, h2t, n_blocks, tm)
    ys = _experts(block_expert, block_valid, xs, w_gate_up, b_gate_up, w_down, b_down)
    return _combine(flat(tile_cnt), flat(tile_off), flat(tile_dst), meta, ys, x1, mod3, S, tm)


def kernel(x, c, ada_w, ada_b, norm1_g, w_in, q_norm_g, k_norm_g, lambda_q1, lambda_k1, lambda_q2, lambda_k2,
           attn_subln_g, conv_w, conv_b, lru_wa, lru_ba, lru_wx, lru_bx, lru_lambda, lru_out_g, w_out, norm2_g,
           router_w, router_b, w_gate_up, b_gate_up, w_down, b_down):
    B, S, D = x.shape
    params = (ada_w, ada_b, norm1_g, w_in, q_norm_g, k_norm_g, lambda_q1, lambda_k1, lambda_q2, lambda_k2,
              attn_subln_g, conv_w, conv_b, lru_wa, lru_ba, lru_wx, lru_bx, lru_lambda, lru_out_g, w_out,
              norm2_g, router_w, router_b, w_gate_up, b_gate_up, w_down, b_down)
    x2 = x.reshape(B * S, D)
    for l in range(ada_w.shape[0]):
        x2 = _layer(l, x2, B, S, c, *[p[l] for p in params])
    return x2.reshape(B, S, D)
```

```python
import functools
import math

import jax
import jax.numpy as jnp
from jax import lax
from jax.experimental import pallas as pl
from jax.experimental.pallas import tpu as pltpu

F32 = jnp.float32
BF16 = jnp.bfloat16
I32 = jnp.int32

ATTN_DK = 64
ATTN_DV = 2 * ATTN_DK
CONV_W = 4
LRU_C = 8.0
N_EXPERTS = 32
TOP_K = 4
SWIGLU_LIMIT = 7.0
SWIGLU_ALPHA = 1.702
MOE_BLOCK = 512
EPS = 1e-6
NEG_BIG = -1e30

IN_PROJ_ROWS = 1024
ATTN_Q_ROWS = 1024
ATTN_KV_ROWS = 512
LRU_CHUNK_ROWS = 1024
ROUTE_TILE_ROWS = 512
OUT_PROJ_TILES = 2
BF16_EXACT_INT = 256.0

LANES = 128
SUBLANES = 8
MXU_DEPTH = 256
VMEM_LIMIT = 40 * 1024 * 1024


def _cparams(sem):
    return pltpu.CompilerParams(dimension_semantics=sem, vmem_limit_bytes=VMEM_LIMIT)


def _split_hi_lo(x):
    hi = x.astype(BF16)
    lo = (x - hi.astype(F32)).astype(BF16)
    return hi, lo


def _sigmoid(x):
    return 0.5 * jnp.tanh(0.5 * x) + 0.5


def _dot(a, b):
    return jnp.dot(a, b, preferred_element_type=F32)


def _dot_tb(a, b):
    return lax.dot_general(a, b, (((1,), (1,)), ((), ())), preferred_element_type=F32)


def _adaln_kernel(lam_init, c_ref, w_ref, b_ref, lq1, lk1, lq2, lk2, mod_ref, lam_ref):
    c = c_ref[...]
    s = c * jax.nn.sigmoid(c)
    s_hi, s_lo = _split_hi_lo(s)
    w = w_ref[...]
    w_hi, w_lo = _split_hi_lo(w)
    mod_ref[...] = _dot(s_hi, w_hi) + _dot(s_hi, w_lo) + _dot(s_lo, w_hi) + b_ref[...]
    d1 = jnp.sum(lq1[...] * lk1[...], axis=-1, keepdims=True)
    d2 = jnp.sum(lq2[...] * lk2[...], axis=-1, keepdims=True)
    lam = jnp.exp(d1) - jnp.exp(d2) + lam_init
    lam_ref[...] = jnp.broadcast_to(lam, lam_ref.shape)


def _adaln(c, ada_w, ada_b, lq1, lk1, lq2, lk2, lam_init):
    B, D = c.shape
    n = ada_w.shape[1] // D
    vec = lambda: pl.BlockSpec((1, ATTN_DK), lambda j: (0, 0))
    return pl.pallas_call(
        functools.partial(_adaln_kernel, lam_init),
        grid=(n,),
        in_specs=[
            pl.BlockSpec((B, D), lambda j: (0, 0)),
            pl.BlockSpec((D, D), lambda j: (0, j)),
            pl.BlockSpec((1, D), lambda j: (0, j)),
            vec(), vec(), vec(), vec(),
        ],
        out_specs=[
            pl.BlockSpec((B, D), lambda j: (0, j)),
            pl.BlockSpec((1, LANES), lambda j: (0, 0)),
        ],
        out_shape=[
            jax.ShapeDtypeStruct((B, n * D), F32),
            jax.ShapeDtypeStruct((1, LANES), F32),
        ],
        compiler_params=_cparams(("arbitrary",)),
        name="adaln",
    )(c, ada_w, ada_b.reshape(1, -1), lq1.reshape(1, -1), lk1.reshape(1, -1),
      lq2.reshape(1, -1), lk2.reshape(1, -1))


def _rms_modulate(x, g, shift, scale):
    ms = jnp.mean(x * x, axis=-1, keepdims=True)
    y = x * lax.rsqrt(ms + EPS) * g
    return y * (1.0 + scale) + shift


def _group_rms_scale(q, group_ones):
    sq = (q * q).astype(BF16)
    w = group_ones.shape[0]
    ss = jnp.concatenate([_dot(sq[:, c:c + w], group_ones) for c in range(0, q.shape[1], w)], axis=1)
    return lax.rsqrt(ss * (1.0 / ATTN_DK) + EPS)


def _in_proj_kernel(qkw, aw, lw, x_ref, mod_ref, g_ref, w_ref, gq_ref, gk_ref, ones_ref,
                    q_ref, k_ref, v_ref, xr_ref, gr_ref):
    h = _rms_modulate(x_ref[...], g_ref[...], mod_ref[0:1, :], mod_ref[1:2, :])
    hb = h.astype(BF16)
    group_ones = ones_ref[...]
    o = 0
    q = _dot(hb, w_ref[:, o:o + qkw]); o += qkw
    q_ref[...] = (q * _group_rms_scale(q, group_ones) * gq_ref[...]).astype(BF16)
    k = _dot(hb, w_ref[:, o:o + qkw]); o += qkw
    k_ref[...] = (k * _group_rms_scale(k, group_ones) * gk_ref[...]).astype(BF16)
    v_ref[...] = _dot(hb, w_ref[:, o:o + aw]).astype(BF16); o += aw
    xr_ref[...] = _dot(hb, w_ref[:, o:o + lw]); o += lw
    gr_ref[...] = _dot(hb, w_ref[:, o:o + lw])


def _in_proj(x2, mod3, norm_g, w_in_b, gq, gk, S, tm, qkw, aw, lw):
    T, D = x2.shape
    tps = S // tm
    grp = jnp.arange(MXU_DEPTH, dtype=I32) // ATTN_DK
    group_ones = (grp[:, None] == grp[None, :]).astype(BF16)
    const = lambda shape: pl.BlockSpec(shape, lambda i: (0,) * len(shape))
    row = lambda w: pl.BlockSpec((tm, w), lambda i: (i, 0))
    return pl.pallas_call(
        functools.partial(_in_proj_kernel, qkw, aw, lw),
        grid=(T // tm,),
        in_specs=[
            row(D),
            pl.BlockSpec((None, 6, D), lambda i: (i // tps, 0, 0)),
            const((1, D)),
            const(w_in_b.shape),
            const((1, qkw)), const((1, qkw)),
            const((MXU_DEPTH, MXU_DEPTH)),
        ],
        out_specs=[row(qkw), row(qkw), row(aw), row(lw), row(lw)],
        out_shape=[
            jax.ShapeDtypeStruct((T, qkw), BF16),
            jax.ShapeDtypeStruct((T, qkw), BF16),
            jax.ShapeDtypeStruct((T, aw), BF16),
            jax.ShapeDtypeStruct((T, lw), F32),
            jax.ShapeDtypeStruct((T, lw), F32),
        ],
        compiler_params=_cparams(("arbitrary",)),
        name="in_proj",
    )(x2, mod3, norm_g, w_in_b, gq, gk, group_ones)


ATTN_LOOP_ROWS = 256
ATTN_DIAG_ROWS = 128


def _attn_kernel(bq, bk, out_scale, q_ref, k_ref, v_ref, lam_ref, g_ref, o_ref,
                 qq_ref, vp_ref, s_ref, sd_ref, m_ref, acc_ref):
    n_rows = 2 * bq

    col = lax.broadcasted_iota(I32, (vp_ref.shape[0], ATTN_DV), 1)
    vp_ref[:, :ATTN_DV] = v_ref[...]
    vp_ref[:, ATTN_DV:] = jnp.where(col == 0, 1.0, 0.0).astype(BF16)
    acc_ref[...] = jnp.zeros_like(acc_ref)

    def query_block(i, carry):
        q_rows = pl.ds(pl.multiple_of(i * bq, bq), bq)
        _attn_query_block(i, q_rows, bq, bk, n_rows, out_scale, q_ref, k_ref, lam_ref, g_ref, o_ref,
                          qq_ref, vp_ref, s_ref, sd_ref, m_ref, acc_ref)
        return carry

    lax.fori_loop(0, q_ref.shape[0] // bq, query_block, 0)


def _attn_query_block(i, q_rows, bq, bk, n_rows, out_scale, q_ref, k_ref, lam_ref, g_ref, o_ref,
                      qq_ref, vp_ref, s_ref, sd_ref, m_ref, acc_ref):
    q = q_ref[q_rows, :]
    lane = lax.broadcasted_iota(I32, q.shape, 1)
    zero = jnp.zeros_like(q)
    qq_ref[0:bq, :] = jnp.where(lane < ATTN_DK, q, zero)
    qq_ref[bq:, :] = jnp.where(lane >= ATTN_DK, q, zero)
    m_ref[...] = jnp.full(m_ref.shape, NEG_BIG, F32)

    def scores(rows, start, kw):
        return _dot_tb(qq_ref[rows, :], k_ref[pl.ds(start, kw), :])

    def softmax_pv(s, rows, start):
        kw = s.shape[1]
        m_old = m_ref[rows, :]
        m_new = jnp.maximum(m_old, jnp.max(s, axis=-1, keepdims=True))
        alpha = jnp.exp2(m_old - m_new)
        p = jnp.exp2(s - jnp.tile(m_new, (1, kw // LANES)))
        pv = _dot(p.astype(BF16), vp_ref[pl.ds(start, kw), :])
        acc_ref[rows, :] = acc_ref[rows, :] * jnp.tile(alpha, (1, 2)) + pv
        m_ref[rows, :] = m_new

    loop_chunks = [pl.ds(r, ATTN_LOOP_ROWS) for r in range(0, n_rows, ATTN_LOOP_ROWS)]
    for rows in loop_chunks:
        s_ref[rows, :] = scores(rows, 0, bk)

    def body(j, carry):
        start = pl.multiple_of(j * bk, bk)
        for rows in loop_chunks:
            s = s_ref[rows, :]
            s_ref[rows, :] = scores(rows, start + bk, bk)
            softmax_pv(s, rows, start)
        return carry

    n_full = i * (bq // bk)
    lax.fori_loop(0, n_full, body, 0)

    start = pl.multiple_of(n_full * bk, bk)
    rc = ATTN_DIAG_ROWS
    tri = (lax.broadcasted_iota(I32, (rc, rc), 1) <= lax.broadcasted_iota(I32, (rc, rc), 0))
    diag_chunks = [(pl.ds(r, rc), r % bq) for r in range(0, n_rows, rc)]
    for rows, q0 in diag_chunks:
        if q0 + rc > bk:
            sd_ref[rows, 0:q0 + rc - bk] = scores(rows, start + bk, q0 + rc - bk)
    for rows, q0 in diag_chunks:
        kw = q0 + rc
        parts = [s_ref[rows, 0:min(q0, bk)]] if q0 else []
        if q0 > bk:
            parts.append(sd_ref[rows, 0:q0 - bk])
        last = s_ref[rows, q0:kw] if kw <= bk else sd_ref[rows, q0 - bk:kw - bk]
        parts.append(jnp.where(tri, last, NEG_BIG))
        softmax_pv(parts[0] if len(parts) == 1 else jnp.concatenate(parts, axis=1), rows, start)

    acc = acc_ref[...]
    o = acc[:, :ATTN_DV] / acc[:, ATTN_DV:ATTN_DV + 1]
    a = o[:bq] - lam_ref[0:1, 0:1] * o[bq:]
    ms = jnp.mean(a * a, axis=-1, keepdims=True)
    o_ref[q_rows, :] = (a * lax.rsqrt(ms + EPS) * g_ref[...] * out_scale).astype(BF16)


def _attention(q, k, v, lam, subln_g, B, S, bq, bk, out_scale):
    T, qkw = q.shape
    H = qkw // ATTN_DV
    return pl.pallas_call(
        functools.partial(_attn_kernel, bq, bk, out_scale),
        grid=(B, H),
        in_specs=[
            pl.BlockSpec((S, ATTN_DV), lambda b, h: (b, h)),
            pl.BlockSpec((S, ATTN_DV), lambda b, h: (b, h)),
            pl.BlockSpec((S, ATTN_DV), lambda b, h: (b, h)),
            pl.BlockSpec((1, LANES), lambda b, h: (0, 0)),
            pl.BlockSpec((1, ATTN_DV), lambda b, h: (0, 0)),
        ],
        out_specs=pl.BlockSpec((S, ATTN_DV), lambda b, h: (b, h)),
        out_shape=jax.ShapeDtypeStruct((T, H * ATTN_DV), BF16),
        scratch_shapes=[
            pltpu.VMEM((2 * bq, ATTN_DV), BF16),
            pltpu.VMEM((S, 2 * ATTN_DV), BF16),
            pltpu.VMEM((2 * bq, bk), F32),
            pltpu.VMEM((2 * bq, max(bq - bk, LANES)), F32),
            pltpu.VMEM((2 * bq, LANES), F32),
            pltpu.VMEM((2 * bq, 2 * ATTN_DV), F32),
        ],
        compiler_params=_cparams(("arbitrary", "arbitrary")),
        name="attention",
    )(q, k, v, lam, subln_g)


def _lru_kernel(tc, cw, pitch, x_ref, gate_ref, cw_ref, cb_ref, wg_ref, bg_ref, lam_ref, og_ref,
                o_ref, ext_ref, a_ref, b_ref, hc_ref):
    c = pl.program_id(1)

    @pl.when(c == 0)
    def _():
        ext_ref[0:SUBLANES, :] = jnp.zeros((SUBLANES, cw), F32)
        hc_ref[...] = jnp.zeros_like(hc_ref)

    ext_ref[SUBLANES:SUBLANES + tc, :] = x_ref[...]
    xc = cb_ref[...] + cw_ref[CONV_W - 1:CONV_W, :] * x_ref[...]
    for w in range(CONV_W - 1):
        sh = CONV_W - 1 - w
        xc = xc + cw_ref[w:w + 1, :] * ext_ref[SUBLANES - sh:SUBLANES - sh + tc, :]
    tail = ext_ref[tc:tc + SUBLANES, :]
    ext_ref[0:SUBLANES, :] = tail

    g = _dot(xc.astype(BF16), wg_ref[...]) + bg_ref[...]
    t_r = jnp.tanh(g[:, :cw])
    t_i = jnp.tanh(g[:, cw:])
    nl = -lam_ref[...]
    softplus = jnp.maximum(nl, 0.0) + jnp.log1p(jnp.exp(-jnp.abs(nl)))
    c1 = (-0.5 * LRU_C) * softplus
    log_a = c1 * t_r + c1
    th = jnp.tanh(log_a)
    p = -2.0 * th
    root = jnp.where(p > 0.0, p * lax.rsqrt(p * (1.0 - th)), 0.0)
    a = jnp.exp(log_a)
    b = root * ((t_i + 1.0) * (0.5 * xc))

    steps = tc // SUBLANES
    n_lt = cw // LANES
    for q in range(SUBLANES):
        for l in range(n_lt):
            a_ref[l, q * pitch:q * pitch + steps, :] = a[q * steps:(q + 1) * steps, l * LANES:(l + 1) * LANES]
            b_ref[l, q * pitch:q * pitch + steps, :] = b[q * steps:(q + 1) * steps, l * LANES:(l + 1) * LANES]

    def scan_step(t, carry):
        out = []
        for l in range(n_lt):
            h, prod = carry[l]
            rows = pl.ds(t, SUBLANES, stride=pitch)
            a_t = a_ref[l, rows, :]
            h = a_t * h + b_ref[l, rows, :]
            prod = a_t * prod
            b_ref[l, rows, :] = h
            a_ref[l, rows, :] = prod
            out.append((h, prod))
        return tuple(out)

    init = tuple((jnp.zeros((SUBLANES, LANES), F32), jnp.ones((SUBLANES, LANES), F32)) for _ in range(n_lt))
    last = lax.fori_loop(0, steps, scan_step, init, unroll=4)

    h_cols = []
    for l in range(n_lt):
        h_last, p_last = last[l]
        entering = [hc_ref[0:1, l * LANES:(l + 1) * LANES]]
        for q in range(SUBLANES):
            entering.append(h_last[q:q + 1, :] + p_last[q:q + 1, :] * entering[q])
        hc_ref[:, l * LANES:(l + 1) * LANES] = jnp.broadcast_to(entering[SUBLANES], (SUBLANES, LANES))
        h_cols.append(jnp.concatenate(
            [b_ref[l, q * pitch:q * pitch + steps, :] + a_ref[l, q * pitch:q * pitch + steps, :] * entering[q]
             for q in range(SUBLANES)], axis=0))
    h = jnp.concatenate(h_cols, axis=1)

    gt = gate_ref[...]
    gelu = 0.5 * gt * (1.0 + jnp.tanh(math.sqrt(2.0 / math.pi) * (gt + 0.044715 * gt * gt * gt)))
    y = h * gelu
    ms = jnp.mean(y * y, axis=-1, keepdims=True)
    o_ref[...] = (y * lax.rsqrt(ms + EPS) * og_ref[...]).astype(BF16)


def _scan_pitch(steps):
    return steps if (steps // SUBLANES) % 2 else steps + SUBLANES


def _lru(xr, gr, conv_w, conv_b, w_gates, b_gates, lru_lambda, out_g, B, S, tc):
    T, cw = xr.shape
    nc = S // tc
    pitch = _scan_pitch(tc // SUBLANES)
    const = lambda shape: pl.BlockSpec(shape, lambda b, c: (0,) * len(shape))
    row = pl.BlockSpec((tc, cw), lambda b, c: (b * nc + c, 0))
    return pl.pallas_call(
        functools.partial(_lru_kernel, tc, cw, pitch),
        grid=(B, nc),
        in_specs=[row, row, const((CONV_W, cw)), const((1, cw)), const((cw, 2 * cw)),
                  const((1, 2 * cw)), const((1, cw)), const((1, cw))],
        out_specs=row,
        out_shape=jax.ShapeDtypeStruct((T, cw), BF16),
        scratch_shapes=[
            pltpu.VMEM((tc + SUBLANES, cw), F32),
            pltpu.VMEM((cw // LANES, SUBLANES * pitch, LANES), F32),
            pltpu.VMEM((cw // LANES, SUBLANES * pitch, LANES), F32),
            pltpu.VMEM((SUBLANES, cw), F32),
        ],
        compiler_params=_cparams(("arbitrary", "arbitrary")),
        name="lru",
    )(xr, gr, conv_w, conv_b, w_gates, b_gates, lru_lambda, out_g)


def _out_proj_kernel(tm, n_sub, aw, *refs):
    for u in range(n_sub):
        _out_proj_tile(u, tm, aw, *refs)


def _out_proj_tile(u, tm, aw, attn_ref, lru_ref, x_ref, mod_ref, w_ref, g_ref, rw_ref, rb_ref, tri_ref, low_ref,
                   x1_ref, h2_ref, meta_ref, cnt_ref):
    rows = pl.ds(u * tm, tm)
    mix = _dot(attn_ref[rows, :], w_ref[0:aw, :]) + _dot(lru_ref[rows, :], w_ref[aw:, :])
    x1 = x_ref[rows, :] + mod_ref[2:3, :] * mix
    x1_ref[rows, :] = x1
    h2 = _rms_modulate(x1, g_ref[...], mod_ref[3:4, :], mod_ref[4:5, :])
    for j in range(h2.shape[1] // LANES):
        h2_ref[pl.ds(u * tm * SUBLANES + j, tm, stride=SUBLANES), :] = h2[:, j * LANES:(j + 1) * LANES]

    h_hi, h_lo = _split_hi_lo(h2)
    w_hi, w_lo = _split_hi_lo(rw_ref[...])
    by_hi = _dot_tb(jnp.concatenate([w_hi, w_lo], axis=0), h_hi)
    lg = by_hi[:N_EXPERTS] + by_hi[N_EXPERTS:] + _dot_tb(w_hi, h_lo) + rb_ref[...]

    eidx = lax.broadcasted_iota(I32, lg.shape, 0)
    picked = jnp.zeros(lg.shape, F32)
    vals, idxs = [], []
    for _ in range(TOP_K):
        m = jnp.max(lg, axis=0, keepdims=True)
        idx = jnp.min(jnp.where(lg == m, eidx, N_EXPERTS), axis=0, keepdims=True)
        sel = eidx == idx
        lg = jnp.where(sel, -jnp.inf, lg)
        picked = picked + sel.astype(F32)
        vals.append(m)
        idxs.append(idx)
    es = [jnp.exp(v - vals[0]) for v in vals]
    den = es[0] + es[1] + es[2] + es[3]
    gates = jnp.concatenate([e / den for e in es], axis=0)

    cnt = jnp.sum(picked, axis=1, keepdims=True)
    cnt_b = jnp.broadcast_to(cnt, cnt_ref.shape[1:])
    cnt_hi = jnp.floor(cnt_b * (1.0 / BF16_EXACT_INT))
    cnt_lo = cnt_b - BF16_EXACT_INT * cnt_hi
    run_start = BF16_EXACT_INT * _dot(low_ref[...], cnt_hi.astype(BF16)) + _dot(low_ref[...], cnt_lo.astype(BF16))
    before = _dot(picked.astype(BF16), tri_ref[...]) + run_start[:, 0:1]
    pos = [jnp.sum(jnp.where(eidx == idx, before, 0.0), axis=0, keepdims=True) for idx in idxs]
    meta_ref[u, 0:TOP_K, :] = jnp.concatenate(pos, axis=0).astype(I32) * SUBLANES
    meta_ref[u, TOP_K:, :] = lax.bitcast_convert_type(gates, I32)
    cnt_ref[u] = cnt_b.astype(I32)


def _out_proj(attn, lru, x2, mod3, w_out_b, norm_g, router_wt, router_b, S, tm, n_sub):
    T, D = x2.shape
    aw = attn.shape[1]
    lw = lru.shape[1]
    ts = tm * n_sub
    tps = S // ts
    tri = (jnp.arange(tm, dtype=I32)[:, None] < jnp.arange(tm, dtype=I32)[None, :]).astype(BF16)
    experts = jnp.arange(N_EXPERTS, dtype=I32)
    low = (experts[None, :] < experts[:, None]).astype(BF16)
    const = lambda shape: pl.BlockSpec(shape, lambda i: (0,) * len(shape))
    row = lambda w: pl.BlockSpec((ts, w), lambda i: (i, 0))
    return pl.pallas_call(
        functools.partial(_out_proj_kernel, tm, n_sub, aw),
        grid=(T // ts,),
        in_specs=[
            row(aw), row(lw), row(D),
            pl.BlockSpec((None, 6, D), lambda i: (i // tps, 0, 0)),
            const((D, D)), const((1, D)), const((N_EXPERTS, D)), const((N_EXPERTS, 1)),
            const((tm, tm)), const((N_EXPERTS, N_EXPERTS)),
        ],
        out_specs=[
            row(D),
            pl.BlockSpec((ts * SUBLANES, LANES), lambda i: (i, 0)),
            pl.BlockSpec((n_sub, 2 * TOP_K, tm), lambda i: (i, 0, 0)),
            pl.BlockSpec((n_sub, N_EXPERTS, LANES), lambda i: (i, 0, 0)),
        ],
        out_shape=[
            jax.ShapeDtypeStruct((T, D), F32),
            jax.ShapeDtypeStruct((T * SUBLANES, LANES), F32),
            jax.ShapeDtypeStruct((T // tm, 2 * TOP_K, tm), I32),
            jax.ShapeDtypeStruct((T // tm, N_EXPERTS, LANES), I32),
        ],
        compiler_params=_cparams(("arbitrary",)),
        name="out_proj",
    )(attn, lru, x2, mod3, w_out_b, norm_g, router_wt, router_b, tri, low)


def _rows(ref, first_row, n_rows):
    return ref.at[pl.ds(pl.multiple_of(first_row * SUBLANES, SUBLANES), n_rows * SUBLANES), :]


def _row_at(ref, sublane_offset):
    return ref.at[pl.ds(pl.multiple_of(sublane_offset, SUBLANES), SUBLANES), :]


def _fetch_tile(meta_hbm, smems, sems, t):
    i = pl.program_id(0)
    n = pl.num_programs(0)
    slot = i % 2
    per_tile = 2 * TOP_K * t

    def copies(tile, s):
        return [pltpu.make_async_copy(meta_hbm.at[pl.ds(pl.multiple_of(tile * per_tile + j * t, t), t)],
                                      smem.at[pl.ds(pl.multiple_of(s * t, t), t)], sems.at[s])
                for j, smem in enumerate(smems)]

    @pl.when(i == 0)
    def _():
        for c in copies(0, 0):
            c.start()

    for c in copies(i, slot):
        c.wait()

    @pl.when(i + 1 < n)
    def _():
        for c in copies(i + 1, 1 - slot):
            c.start()

    return slot * t


def _run_copies(t, tile, cnt_ref, off_ref, dst_ref, make_copy):
    def expert(e, carry):
        n = cnt_ref[tile * N_EXPERTS + e]
        off = off_ref[tile * N_EXPERTS + e]
        dst = dst_ref[tile * N_EXPERTS + e]
        for b in reversed(range(t.bit_length())):
            size = SUBLANES << b
            done = n & ~(2 * size - 1)

            @pl.when((n & size) != 0)
            def _():
                make_copy(off + done, dst + done, size).start()
        return carry

    lax.fori_loop(0, N_EXPERTS, expert, 0)


def _span(ref, first_sublane, n_sublanes):
    return ref.at[pl.ds(pl.multiple_of(first_sublane, SUBLANES), n_sublanes), :]


def _dispatch_kernel(t, n_blocks, cnt_ref, off_ref, dst_ref, pad_ref, meta_hbm, h_ref, xs_hbm,
                     pos0, pos1, pos2, pos3, stage_ref, zero_ref, psem, sem, zsem):
    i = pl.program_id(0)
    n = pl.num_programs(0)
    slot = i % 2

    def zero_fill(go):
        def expert(e, carry):
            first = pad_ref[e]
            length = pad_ref[N_EXPERTS + e]
            for b in reversed(range((MOE_BLOCK - 1).bit_length())):
                size = SUBLANES << b
                done = length & ~(2 * size - 1)

                @pl.when((length & size) != 0)
                def _():
                    go(pltpu.make_async_copy(_span(zero_ref, 0, size), _span(xs_hbm, first + done, size), zsem))
            return carry

        lax.fori_loop(0, N_EXPERTS, expert, 0)

        def block(b, carry):
            go(pltpu.make_async_copy(zero_ref, _rows(xs_hbm, b * MOE_BLOCK, MOE_BLOCK), zsem))
            return carry

        lax.fori_loop(pad_ref[2 * N_EXPERTS], n_blocks, block, 0)

    @pl.when(i == 0)
    def _():
        zero_ref[...] = jnp.zeros_like(zero_ref)
        zero_fill(lambda c: c.start())

    pos = (pos0, pos1, pos2, pos3)
    base = _fetch_tile(meta_hbm, pos, psem, t)
    stage = stage_ref.at[slot]

    def place(r, carry):
        row = _rows(h_ref, r, 1)[...]
        for k in range(TOP_K):
            _row_at(stage, pos[k][base + r])[...] = row
        return carry

    lax.fori_loop(0, t, place, 0, unroll=8)

    def all_runs(s):
        return pltpu.make_async_copy(stage_ref.at[s], _rows(xs_hbm, 0, TOP_K * t), sem.at[s])

    _run_copies(t, i, cnt_ref, off_ref, dst_ref,
                lambda a, b, m: pltpu.make_async_copy(_span(stage, a, m), _span(xs_hbm, b, m), sem.at[slot]))

    @pl.when(i > 0)
    def _():
        all_runs(1 - slot).wait()

    @pl.when(i == n - 1)
    def _():
        all_runs(slot).wait()
        zero_fill(lambda c: c.wait())


def _dispatch(tile_cnt, tile_off, tile_dst, pad_info, meta, h2t, n_blocks, t):
    n_tiles = meta.shape[0] // (2 * TOP_K * t)
    n_slots = n_blocks * MOE_BLOCK
    grid_spec = pltpu.PrefetchScalarGridSpec(
        num_scalar_prefetch=4,
        grid=(n_tiles,),
        in_specs=[pl.BlockSpec(memory_space=pl.ANY),
                  pl.BlockSpec((t * SUBLANES, LANES), lambda i, *_: (i, 0))],
        out_specs=pl.BlockSpec(memory_space=pl.ANY),
        scratch_shapes=[pltpu.SMEM((2 * t,), I32)] * TOP_K + [
            pltpu.VMEM((2, TOP_K * t * SUBLANES, LANES), F32),
            pltpu.VMEM((MOE_BLOCK * SUBLANES, LANES), F32),
            pltpu.SemaphoreType.DMA((2,)),
            pltpu.SemaphoreType.DMA((2,)),
            pltpu.SemaphoreType.DMA,
        ],
    )
    return pl.pallas_call(
        functools.partial(_dispatch_kernel, t, n_blocks),
        grid_spec=grid_spec,
        out_shape=jax.ShapeDtypeStruct((n_slots * SUBLANES, LANES), F32),
        compiler_params=_cparams(("arbitrary",)),
        name="dispatch",
    )(tile_cnt, tile_off, tile_dst, pad_info, meta, h2t)


def _experts_kernel(ff, be_ref, nv_ref, nxt_ref, xs_ref, wgu_hbm, bgu_ref, wdn_hbm, bdn_ref, ys_ref,
                    wgu_f, wdn_f, wgu_b, wdn_b, sem):
    i = pl.program_id(0)
    nv = nv_ref[i]
    n_slabs = xs_ref.shape[0] // MOE_BLOCK

    def fetch(e):
        return (pltpu.make_async_copy(wgu_hbm.at[e], wgu_f, sem.at[0]),
                pltpu.make_async_copy(wdn_hbm.at[e], wdn_f, sem.at[1]))

    @pl.when(i == 0)
    def _():
        for c in fetch(be_ref[0]):
            c.start()

    @pl.when((i == 0) | (be_ref[i] != be_ref[jnp.maximum(i - 1, 0)]))
    def _():
        for c in fetch(be_ref[i]):
            c.wait()
        wgu_b[...] = wgu_f[...].astype(BF16)
        wdn_b[...] = wdn_f[...].astype(BF16)

        @pl.when(nxt_ref[i] >= 0)
        def _():
            for c in fetch(nxt_ref[i]):
                c.start()

    @pl.when(nv > 0)
    def _():
        x = jnp.concatenate(
            [xs_ref[pl.ds(j, MOE_BLOCK, stride=SUBLANES), :] for j in range(n_slabs)], axis=1)
        gu = _dot(x.astype(BF16), wgu_b[...]) + bgu_ref[...]
        gate = jnp.minimum(gu[:, :ff], SWIGLU_LIMIT)
        up = jnp.clip(gu[:, ff:], -SWIGLU_LIMIT, SWIGLU_LIMIT)
        act = (up + 1.0) * (gate * _sigmoid(SWIGLU_ALPHA * gate))
        y = _dot(act.astype(BF16), wdn_b[...]) + bdn_ref[...]
        for j in range(n_slabs):
            ys_ref[pl.ds(j, MOE_BLOCK, stride=SUBLANES), :] = y[:, j * LANES:(j + 1) * LANES]

    @pl.when(nv == 0)
    def _():
        ys_ref[...] = jnp.zeros_like(ys_ref)


def _experts(block_expert, block_valid, xs, w_gu, b_gu, w_dn, b_dn):
    n_blocks = block_expert.shape[0]
    E, D, ff2 = w_gu.shape
    ff = ff2 // 2
    later = block_expert[None, :] > block_expert[:, None]
    nxt = jnp.min(jnp.where(later, block_expert[None, :], E), axis=1)
    nxt = jnp.where(nxt < E, nxt, -1)
    rows = pl.BlockSpec((MOE_BLOCK * SUBLANES, LANES), lambda i, *_: (i, 0))
    grid_spec = pltpu.PrefetchScalarGridSpec(
        num_scalar_prefetch=3,
        grid=(n_blocks,),
        in_specs=[
            rows,
            pl.BlockSpec(memory_space=pl.ANY),
            pl.BlockSpec((None, 1, ff2), lambda i, be, *_: (be[i], 0, 0)),
            pl.BlockSpec(memory_space=pl.ANY),
            pl.BlockSpec((None, 1, D), lambda i, be, *_: (be[i], 0, 0)),
        ],
        out_specs=rows,
        scratch_shapes=[
            pltpu.VMEM((D, ff2), F32), pltpu.VMEM((ff, D), F32),
            pltpu.VMEM((D, ff2), BF16), pltpu.VMEM((ff, D), BF16),
            pltpu.SemaphoreType.DMA((2,)),
        ],
    )
    return pl.pallas_call(
        functools.partial(_experts_kernel, ff),
        grid_spec=grid_spec,
        out_shape=jax.ShapeDtypeStruct(xs.shape, F32),
        compiler_params=_cparams(("arbitrary",)),
        name="experts",
    )(block_expert, block_valid, nxt.astype(I32), xs, w_gu, b_gu.reshape(E, 1, ff2), w_dn,
      b_dn.reshape(E, 1, D))


def _combine_kernel(t, cnt_ref, off_ref, dst_ref, meta_hbm, ys_hbm, x1_ref, mod_ref, o_ref,
                    pos0, pos1, pos2, pos3, gate0, gate1, gate2, gate3, stage_ref, y_ref, psem, sem):
    i = pl.program_id(0)
    n = pl.num_programs(0)
    slot = i % 2
    pos = (pos0, pos1, pos2, pos3)
    gates = (gate0, gate1, gate2, gate3)
    base = _fetch_tile(meta_hbm, pos + gates, psem, t)

    def fetch_runs(tile, s):
        stage = stage_ref.at[s]
        _run_copies(t, tile, cnt_ref, off_ref, dst_ref,
                    lambda a, b, m: pltpu.make_async_copy(_span(ys_hbm, b, m), _span(stage, a, m), sem.at[s]))

    @pl.when(i == 0)
    def _():
        fetch_runs(0, 0)

    @pl.when(i + 1 < n)
    def _():
        fetch_runs(i + 1, 1 - slot)

    pltpu.make_async_copy(_rows(ys_hbm, 0, TOP_K * t), stage_ref.at[slot], sem.at[slot]).wait()
    stage = stage_ref.at[slot]

    def gather(r, carry):
        acc = None
        for k in range(TOP_K):
            gate = lax.bitcast_convert_type(gates[k][base + r], F32)
            term = gate * _row_at(stage, pos[k][base + r])[...]
            acc = term if acc is None else acc + term
        _rows(y_ref, r, 1)[...] = acc
        return carry

    lax.fori_loop(0, t, gather, 0, unroll=8)

    n_slabs = o_ref.shape[1] // LANES
    y = jnp.concatenate([y_ref[pl.ds(j, t, stride=SUBLANES), :] for j in range(n_slabs)], axis=1)
    o_ref[...] = x1_ref[...] + mod_ref[5:6, :] * y


def _combine(tile_cnt, tile_off, tile_dst, meta, ys, x1, mod3, S, t):
    T, D = x1.shape
    n_tiles = T // t
    tps = S // t
    grid_spec = pltpu.PrefetchScalarGridSpec(
        num_scalar_prefetch=3,
        grid=(n_tiles,),
        in_specs=[
            pl.BlockSpec(memory_space=pl.ANY),
            pl.BlockSpec(memory_space=pl.ANY),
            pl.BlockSpec((t, D), lambda i, *_: (i, 0)),
            pl.BlockSpec((None, 6, D), lambda i, *_: (i // tps, 0, 0)),
        ],
        out_specs=pl.BlockSpec((t, D), lambda i, *_: (i, 0)),
        scratch_shapes=[pltpu.SMEM((2 * t,), I32)] * (2 * TOP_K) + [
            pltpu.VMEM((2, TOP_K * t * SUBLANES, LANES), F32),
            pltpu.VMEM((t * SUBLANES, LANES), F32),
            pltpu.SemaphoreType.DMA((2,)),
            pltpu.SemaphoreType.DMA((2,)),
        ],
    )
    return pl.pallas_call(
        functools.partial(_combine_kernel, t),
        grid_spec=grid_spec,
        out_shape=jax.ShapeDtypeStruct((T, D), F32),
        compiler_params=_cparams(("arbitrary",)),
        name="combine",
    )(tile_cnt, tile_off, tile_dst, meta, ys, x1, mod3)


def _tile(n, target):
    t = min(n, target)
    while n % t:
        t //= 2
    return t


def _block_diag(w):
    n, bw, _ = w.shape
    eye = jnp.eye(n, dtype=w.dtype)
    return (eye[:, None, :, None] * w[:, :, None, :]).reshape(n * bw, n * bw)


def _layer(l, x2, B, S, c, ada_w, ada_b, norm1_g, w_in, q_norm_g, k_norm_g, lambda_q1, lambda_k1, lambda_q2,
           lambda_k2, attn_subln_g, conv_w, conv_b, lru_wa, lru_ba, lru_wx, lru_bx, lru_lambda, lru_out_g,
           w_out, norm2_g, router_w, router_b, w_gate_up, b_gate_up, w_down, b_down):
    T, D = x2.shape
    lam_init = 0.8 - 0.6 * math.exp(-0.3 * l)
    aw = D // 2
    lw = D - aw
    heads = aw // ATTN_DV
    qkw = heads * 2 * ATTN_DK

    mod, lam = _adaln(c, ada_w, ada_b, lambda_q1, lambda_k1, lambda_q2, lambda_k2, lam_init)
    mod3 = mod.reshape(c.shape[0], 6, D)

    reps = qkw // ATTN_DK
    gq = (jnp.tile(q_norm_g, reps) * (ATTN_DK ** -0.5 * math.log2(math.e))).reshape(1, qkw)
    gk = jnp.tile(k_norm_g, reps).reshape(1, qkw)
    tm = _tile(S, ROUTE_TILE_ROWS)
    q, k, v, xr, gr = _in_proj(x2, mod3, norm1_g.reshape(1, D), w_in.astype(BF16), gq, gk, S,
                               _tile(S, IN_PROJ_ROWS), qkw, aw, lw)

    attn = _attention(q, k, v, lam, attn_subln_g.reshape(1, ATTN_DV), B, S, _tile(S, ATTN_Q_ROWS),
                      _tile(S, ATTN_KV_ROWS), 1.0 - lam_init)

    w_gates = (0.5 * jnp.concatenate([_block_diag(lru_wa), _block_diag(lru_wx)], axis=1)).astype(BF16)
    b_gates = 0.5 * jnp.concatenate([lru_ba, lru_bx]).reshape(1, 2 * lw)
    lru = _lru(xr, gr, conv_w, conv_b.reshape(1, lw), w_gates, b_gates, lru_lambda.reshape(1, lw),
               lru_out_g.reshape(1, lw), B, S, _tile(S, LRU_CHUNK_ROWS))

    x1, h2t, meta, tile_cnt = _out_proj(
        attn, lru, x2, mod3, w_out.astype(BF16), norm2_g.reshape(1, D), router_w.T,
        router_b.reshape(N_EXPERTS, 1), S, tm, _tile(S // tm, OUT_PROJ_TILES))

    tile_cnt = tile_cnt[:, :, 0]
    counts = jnp.sum(tile_cnt, axis=0)
    padded = ((counts + MOE_BLOCK - 1) // MOE_BLOCK) * MOE_BLOCK
    pad_ends = jnp.cumsum(padded)
    pad_starts = pad_ends - padded
    n_blocks = (T * TOP_K) // MOE_BLOCK + N_EXPERTS
    blk_start = jnp.arange(n_blocks, dtype=I32) * MOE_BLOCK
    owner = blk_start[:, None] >= pad_ends[None, :]
    block_expert = jnp.minimum(jnp.sum(owner, axis=1), N_EXPERTS - 1).astype(I32)
    onehot = block_expert[:, None] == jnp.arange(N_EXPERTS, dtype=I32)[None, :]
    row_end = jnp.sum(jnp.where(onehot, (pad_starts + counts)[None, :], 0), axis=1)
    block_valid = jnp.clip(row_end - blk_start, 0, MOE_BLOCK).astype(I32)
    tile_off = jnp.cumsum(tile_cnt, axis=1) - tile_cnt
    tile_dst = pad_starts[None, :] + jnp.cumsum(tile_cnt, axis=0) - tile_cnt
    flat = lambda a: (a * SUBLANES).reshape(-1).astype(I32)

    meta = meta.reshape(-1)
    pad_info = jnp.concatenate([(pad_starts + counts) * SUBLANES, (padded - counts) * SUBLANES,
                                pad_ends[-1:] // MOE_BLOCK]).astype(I32)
    xs = _dispatch(flat(tile_cnt), flat(tile_off), flat(tile_dst), pad_info, meta, h2t, n_blocks, tm)
    ys = _experts(block_expert, block_valid, xs, w_gate_up, b_gate_up, w_down, b_down)
    return _combine(flat(tile_cnt), flat(tile_off), flat(tile_dst), meta, ys, x1, mod3, S, tm)


def kernel(x, c, ada_w, ada_b, norm1_g, w_in, q_norm_g, k_norm_g, lambda_q1, lambda_k1, lambda_q2, lambda_k2,
           attn_subln_g, conv_w, conv_b, lru_wa, lru_ba, lru_wx, lru_bx, lru_lambda, lru_out_g, w_out, norm2_g,
           router_w, router_b, w_gate_up, b_gate_up, w_down, b_down):
    B, S, D = x.shape
    params = (ada_w, ada_b, norm1_g, w_in, q_norm_g, k_norm_g, lambda_q1, lambda_k1, lambda_q2, lambda_k2,
              attn_subln_g, conv_w, conv_b, lru_wa, lru_ba, lru_wx, lru_bx, lru_lambda, lru_out_g, w_out,
              norm2_g, router_w, router_b, w_gate_up, b_gate_up, w_down, b_down)
    x2 = x.reshape(B * S, D)
    for l in range(ada_w.shape[0]):
        x2 = _layer(l, x2, B, S, c, *[p[l] for p in params])
    return x2.reshape(B, S, D)
```

```python
import functools
import math

import jax
import jax.numpy as jnp
from jax import lax
from jax.experimental import pallas as pl
from jax.experimental.pallas import tpu as pltpu

F32 = jnp.float32
BF16 = jnp.bfloat16
I32 = jnp.int32

ATTN_DK = 64
ATTN_DV = 2 * ATTN_DK
CONV_W = 4
LRU_C = 8.0
N_EXPERTS = 32
TOP_K = 4
SWIGLU_LIMIT = 7.0
SWIGLU_ALPHA = 1.702
MOE_BLOCK = 512
EPS = 1e-6
NEG_BIG = -1e30

IN_PROJ_ROWS = 1024
ATTN_Q_ROWS = 1024
ATTN_KV_ROWS = 512
LRU_CHUNK_ROWS = 1024
ROUTE_TILE_ROWS = 512
OUT_PROJ_TILES = 2
ROW_LOOP_UNROLL = 32
BF16_EXACT_INT = 256.0

LANES = 128
SUBLANES = 8
MXU_DEPTH = 256
VMEM_LIMIT = 40 * 1024 * 1024


def _cparams(sem):
    return pltpu.CompilerParams(dimension_semantics=sem, vmem_limit_bytes=VMEM_LIMIT)


def _split_hi_lo(x):
    hi = x.astype(BF16)
    lo = (x - hi.astype(F32)).astype(BF16)
    return hi, lo


def _sigmoid(x):
    return 0.5 * jnp.tanh(0.5 * x) + 0.5


def _dot(a, b):
    return jnp.dot(a, b, preferred_element_type=F32)


def _dot_tb(a, b):
    return lax.dot_general(a, b, (((1,), (1,)), ((), ())), preferred_element_type=F32)


def _adaln_kernel(lam_init, c_ref, w_ref, b_ref, lq1, lk1, lq2, lk2, mod_ref, lam_ref):
    c = c_ref[...]
    s = c * jax.nn.sigmoid(c)
    s_hi, s_lo = _split_hi_lo(s)
    w = w_ref[...]
    w_hi, w_lo = _split_hi_lo(w)
    mod_ref[...] = _dot(s_hi, w_hi) + _dot(s_hi, w_lo) + _dot(s_lo, w_hi) + b_ref[...]
    d1 = jnp.sum(lq1[...] * lk1[...], axis=-1, keepdims=True)
    d2 = jnp.sum(lq2[...] * lk2[...], axis=-1, keepdims=True)
    lam = jnp.exp(d1) - jnp.exp(d2) + lam_init
    lam_ref[...] = jnp.broadcast_to(lam, lam_ref.shape)


def _adaln(c, ada_w, ada_b, lq1, lk1, lq2, lk2, lam_init):
    B, D = c.shape
    n = ada_w.shape[1] // D
    vec = lambda: pl.BlockSpec((1, ATTN_DK), lambda j: (0, 0))
    return pl.pallas_call(
        functools.partial(_adaln_kernel, lam_init),
        grid=(n,),
        in_specs=[
            pl.BlockSpec((B, D), lambda j: (0, 0)),
            pl.BlockSpec((D, D), lambda j: (0, j)),
            pl.BlockSpec((1, D), lambda j: (0, j)),
            vec(), vec(), vec(), vec(),
        ],
        out_specs=[
            pl.BlockSpec((B, D), lambda j: (0, j)),
            pl.BlockSpec((1, LANES), lambda j: (0, 0)),
        ],
        out_shape=[
            jax.ShapeDtypeStruct((B, n * D), F32),
            jax.ShapeDtypeStruct((1, LANES), F32),
        ],
        compiler_params=_cparams(("arbitrary",)),
        name="adaln",
    )(c, ada_w, ada_b.reshape(1, -1), lq1.reshape(1, -1), lk1.reshape(1, -1),
      lq2.reshape(1, -1), lk2.reshape(1, -1))


def _rms_modulate(x, g, shift, scale):
    ms = jnp.mean(x * x, axis=-1, keepdims=True)
    y = x * lax.rsqrt(ms + EPS) * g
    return y * (1.0 + scale) + shift


def _group_rms_scale(q, group_ones):
    sq = (q * q).astype(BF16)
    w = group_ones.shape[0]
    ss = jnp.concatenate([_dot(sq[:, c:c + w], group_ones) for c in range(0, q.shape[1], w)], axis=1)
    return lax.rsqrt(ss * (1.0 / ATTN_DK) + EPS)


def _in_proj_kernel(qkw, aw, lw, x_ref, mod_ref, g_ref, w_ref, gq_ref, gk_ref, ones_ref,
                    q_ref, k_ref, v_ref, xr_ref, gr_ref):
    h = _rms_modulate(x_ref[...], g_ref[...], mod_ref[0:1, :], mod_ref[1:2, :])
    hb = h.astype(BF16)
    group_ones = ones_ref[...]
    o = 0
    q = _dot(hb, w_ref[:, o:o + qkw]); o += qkw
    q_ref[...] = (q * _group_rms_scale(q, group_ones) * gq_ref[...]).astype(BF16)
    k = _dot(hb, w_ref[:, o:o + qkw]); o += qkw
    k_ref[...] = (k * _group_rms_scale(k, group_ones) * gk_ref[...]).astype(BF16)
    v_ref[...] = _dot(hb, w_ref[:, o:o + aw]).astype(BF16); o += aw
    xr_ref[...] = _dot(hb, w_ref[:, o:o + lw]); o += lw
    gr_ref[...] = _dot(hb, w_ref[:, o:o + lw])


def _in_proj(x2, mod3, norm_g, w_in_b, gq, gk, S, tm, qkw, aw, lw):
    T, D = x2.shape
    tps = S // tm
    grp = jnp.arange(MXU_DEPTH, dtype=I32) // ATTN_DK
    group_ones = (grp[:, None] == grp[None, :]).astype(BF16)
    const = lambda shape: pl.BlockSpec(shape, lambda i: (0,) * len(shape))
    row = lambda w: pl.BlockSpec((tm, w), lambda i: (i, 0))
    return pl.pallas_call(
        functools.partial(_in_proj_kernel, qkw, aw, lw),
        grid=(T // tm,),
        in_specs=[
            row(D),
            pl.BlockSpec((None, 6, D), lambda i: (i // tps, 0, 0)),
            const((1, D)),
            const(w_in_b.shape),
            const((1, qkw)), const((1, qkw)),
            const((MXU_DEPTH, MXU_DEPTH)),
        ],
        out_specs=[row(qkw), row(qkw), row(aw), row(lw), row(lw)],
        out_shape=[
            jax.ShapeDtypeStruct((T, qkw), BF16),
            jax.ShapeDtypeStruct((T, qkw), BF16),
            jax.ShapeDtypeStruct((T, aw), BF16),
            jax.ShapeDtypeStruct((T, lw), F32),
            jax.ShapeDtypeStruct((T, lw), F32),
        ],
        compiler_params=_cparams(("arbitrary",)),
        name="in_proj",
    )(x2, mod3, norm_g, w_in_b, gq, gk, group_ones)


ATTN_LOOP_ROWS = 256
ATTN_DIAG_ROWS = 128


def _attn_kernel(bq, bk, out_scale, q_ref, k_ref, v_ref, lam_ref, g_ref, o_ref,
                 qq_ref, vp_ref, s_ref, sd_ref, m_ref, acc_ref):
    n_rows = 2 * bq

    col = lax.broadcasted_iota(I32, (vp_ref.shape[0], ATTN_DV), 1)
    vp_ref[:, :ATTN_DV] = v_ref[...]
    vp_ref[:, ATTN_DV:] = jnp.where(col == 0, 1.0, 0.0).astype(BF16)
    acc_ref[...] = jnp.zeros_like(acc_ref)

    def query_block(i, carry):
        q_rows = pl.ds(pl.multiple_of(i * bq, bq), bq)
        _attn_query_block(i, q_rows, bq, bk, n_rows, out_scale, q_ref, k_ref, lam_ref, g_ref, o_ref,
                          qq_ref, vp_ref, s_ref, sd_ref, m_ref, acc_ref)
        return carry

    lax.fori_loop(0, q_ref.shape[0] // bq, query_block, 0)


def _attn_query_block(i, q_rows, bq, bk, n_rows, out_scale, q_ref, k_ref, lam_ref, g_ref, o_ref,
                      qq_ref, vp_ref, s_ref, sd_ref, m_ref, acc_ref):
    q = q_ref[q_rows, :]
    lane = lax.broadcasted_iota(I32, q.shape, 1)
    zero = jnp.zeros_like(q)
    qq_ref[0:bq, :] = jnp.where(lane < ATTN_DK, q, zero)
    qq_ref[bq:, :] = jnp.where(lane >= ATTN_DK, q, zero)
    m_ref[...] = jnp.full(m_ref.shape, NEG_BIG, F32)

    def scores(rows, start, kw):
        return _dot_tb(qq_ref[rows, :], k_ref[pl.ds(start, kw), :])

    def softmax_pv(s, rows, start):
        kw = s.shape[1]
        m_old = m_ref[rows, :]
        m_new = jnp.maximum(m_old, jnp.max(s, axis=-1, keepdims=True))
        alpha = jnp.exp2(m_old - m_new)
        p = jnp.exp2(s - jnp.tile(m_new, (1, kw // LANES)))
        pv = _dot(p.astype(BF16), vp_ref[pl.ds(start, kw), :])
        acc_ref[rows, :] = acc_ref[rows, :] * jnp.tile(alpha, (1, 2)) + pv
        m_ref[rows, :] = m_new

    loop_chunks = [pl.ds(r, ATTN_LOOP_ROWS) for r in range(0, n_rows, ATTN_LOOP_ROWS)]
    for rows in loop_chunks:
        s_ref[rows, :] = scores(rows, 0, bk)

    def body(j, carry):
        start = pl.multiple_of(j * bk, bk)
        for rows in loop_chunks:
            s = s_ref[rows, :]
            s_ref[rows, :] = scores(rows, start + bk, bk)
            softmax_pv(s, rows, start)
        return carry

    n_full = i * (bq // bk)
    lax.fori_loop(0, n_full, body, 0)

    start = pl.multiple_of(n_full * bk, bk)
    rc = ATTN_DIAG_ROWS
    tri = (lax.broadcasted_iota(I32, (rc, rc), 1) <= lax.broadcasted_iota(I32, (rc, rc), 0))
    diag_chunks = [(pl.ds(r, rc), r % bq) for r in range(0, n_rows, rc)]
    for rows, q0 in diag_chunks:
        if q0 + rc > bk:
            sd_ref[rows, 0:q0 + rc - bk] = scores(rows, start + bk, q0 + rc - bk)
    for rows, q0 in diag_chunks:
        kw = q0 + rc
        parts = [s_ref[rows, 0:min(q0, bk)]] if q0 else []
        if q0 > bk:
            parts.append(sd_ref[rows, 0:q0 - bk])
        last = s_ref[rows, q0:kw] if kw <= bk else sd_ref[rows, q0 - bk:kw - bk]
        parts.append(jnp.where(tri, last, NEG_BIG))
        softmax_pv(parts[0] if len(parts) == 1 else jnp.concatenate(parts, axis=1), rows, start)

    acc = acc_ref[...]
    o = acc[:, :ATTN_DV] / acc[:, ATTN_DV:ATTN_DV + 1]
    a = o[:bq] - lam_ref[0:1, 0:1] * o[bq:]
    ms = jnp.mean(a * a, axis=-1, keepdims=True)
    o_ref[q_rows, :] = (a * lax.rsqrt(ms + EPS) * g_ref[...] * out_scale).astype(BF16)


def _attention(q, k, v, lam, subln_g, B, S, bq, bk, out_scale):
    T, qkw = q.shape
    H = qkw // ATTN_DV
    return pl.pallas_call(
        functools.partial(_attn_kernel, bq, bk, out_scale),
        grid=(B, H),
        in_specs=[
            pl.BlockSpec((S, ATTN_DV), lambda b, h: (b, h)),
            pl.BlockSpec((S, ATTN_DV), lambda b, h: (b, h)),
            pl.BlockSpec((S, ATTN_DV), lambda b, h: (b, h)),
            pl.BlockSpec((1, LANES), lambda b, h: (0, 0)),
            pl.BlockSpec((1, ATTN_DV), lambda b, h: (0, 0)),
        ],
        out_specs=pl.BlockSpec((S, ATTN_DV), lambda b, h: (b, h)),
        out_shape=jax.ShapeDtypeStruct((T, H * ATTN_DV), BF16),
        scratch_shapes=[
            pltpu.VMEM((2 * bq, ATTN_DV), BF16),
            pltpu.VMEM((S, 2 * ATTN_DV), BF16),
            pltpu.VMEM((2 * bq, bk), F32),
            pltpu.VMEM((2 * bq, max(bq - bk, LANES)), F32),
            pltpu.VMEM((2 * bq, LANES), F32),
            pltpu.VMEM((2 * bq, 2 * ATTN_DV), F32),
        ],
        compiler_params=_cparams(("arbitrary", "arbitrary")),
        name="attention",
    )(q, k, v, lam, subln_g)


def _lru_kernel(tc, cw, pitch, x_ref, gate_ref, cw_ref, cb_ref, wg_ref, bg_ref, lam_ref, og_ref,
                o_ref, ext_ref, a_ref, b_ref, hc_ref):
    c = pl.program_id(1)

    @pl.when(c == 0)
    def _():
        ext_ref[0:SUBLANES, :] = jnp.zeros((SUBLANES, cw), F32)
        hc_ref[...] = jnp.zeros_like(hc_ref)

    ext_ref[SUBLANES:SUBLANES + tc, :] = x_ref[...]
    xc = cb_ref[...] + cw_ref[CONV_W - 1:CONV_W, :] * x_ref[...]
    for w in range(CONV_W - 1):
        sh = CONV_W - 1 - w
        xc = xc + cw_ref[w:w + 1, :] * ext_ref[SUBLANES - sh:SUBLANES - sh + tc, :]
    tail = ext_ref[tc:tc + SUBLANES, :]
    ext_ref[0:SUBLANES, :] = tail

    g = _dot(xc.astype(BF16), wg_ref[...]) + bg_ref[...]
    t_r = jnp.tanh(g[:, :cw])
    t_i = jnp.tanh(g[:, cw:])
    nl = -lam_ref[...]
    softplus = jnp.maximum(nl, 0.0) + jnp.log1p(jnp.exp(-jnp.abs(nl)))
    c1 = (-0.5 * LRU_C) * softplus
    log_a = c1 * t_r + c1
    th = jnp.tanh(log_a)
    p = -2.0 * th
    root = jnp.where(p > 0.0, p * lax.rsqrt(p * (1.0 - th)), 0.0)
    a = jnp.exp(log_a)
    b = root * ((t_i + 1.0) * (0.5 * xc))

    steps = tc // SUBLANES
    n_lt = cw // LANES
    for q in range(SUBLANES):
        for l in range(n_lt):
            a_ref[l, q * pitch:q * pitch + steps, :] = a[q * steps:(q + 1) * steps, l * LANES:(l + 1) * LANES]
            b_ref[l, q * pitch:q * pitch + steps, :] = b[q * steps:(q + 1) * steps, l * LANES:(l + 1) * LANES]

    def scan_step(t, carry):
        out = []
        for l in range(n_lt):
            h, prod = carry[l]
            rows = pl.ds(t, SUBLANES, stride=pitch)
            a_t = a_ref[l, rows, :]
            h = a_t * h + b_ref[l, rows, :]
            prod = a_t * prod
            b_ref[l, rows, :] = h
            a_ref[l, rows, :] = prod
            out.append((h, prod))
        return tuple(out)

    init = tuple((jnp.zeros((SUBLANES, LANES), F32), jnp.ones((SUBLANES, LANES), F32)) for _ in range(n_lt))
    last = lax.fori_loop(0, steps, scan_step, init, unroll=4)

    h_cols = []
    for l in range(n_lt):
        h_last, p_last = last[l]
        entering = [hc_ref[0:1, l * LANES:(l + 1) * LANES]]
        for q in range(SUBLANES):
            entering.append(h_last[q:q + 1, :] + p_last[q:q + 1, :] * entering[q])
        hc_ref[:, l * LANES:(l + 1) * LANES] = jnp.broadcast_to(entering[SUBLANES], (SUBLANES, LANES))
        h_cols.append(jnp.concatenate(
            [b_ref[l, q * pitch:q * pitch + steps, :] + a_ref[l, q * pitch:q * pitch + steps, :] * entering[q]
             for q in range(SUBLANES)], axis=0))
    h = jnp.concatenate(h_cols, axis=1)

    gt = gate_ref[...]
    gelu = 0.5 * gt * (1.0 + jnp.tanh(math.sqrt(2.0 / math.pi) * (gt + 0.044715 * gt * gt * gt)))
    y = h * gelu
    ms = jnp.mean(y * y, axis=-1, keepdims=True)
    o_ref[...] = (y * lax.rsqrt(ms + EPS) * og_ref[...]).astype(BF16)


def _scan_pitch(steps):
    return steps if (steps // SUBLANES) % 2 else steps + SUBLANES


def _lru(xr, gr, conv_w, conv_b, w_gates, b_gates, lru_lambda, out_g, B, S, tc):
    T, cw = xr.shape
    nc = S // tc
    pitch = _scan_pitch(tc // SUBLANES)
    const = lambda shape: pl.BlockSpec(shape, lambda b, c: (0,) * len(shape))
    row = pl.BlockSpec((tc, cw), lambda b, c: (b * nc + c, 0))
    return pl.pallas_call(
        functools.partial(_lru_kernel, tc, cw, pitch),
        grid=(B, nc),
        in_specs=[row, row, const((CONV_W, cw)), const((1, cw)), const((cw, 2 * cw)),
                  const((1, 2 * cw)), const((1, cw)), const((1, cw))],
        out_specs=row,
        out_shape=jax.ShapeDtypeStruct((T, cw), BF16),
        scratch_shapes=[
            pltpu.VMEM((tc + SUBLANES, cw), F32),
            pltpu.VMEM((cw // LANES, SUBLANES * pitch, LANES), F32),
            pltpu.VMEM((cw // LANES, SUBLANES * pitch, LANES), F32),
            pltpu.VMEM((SUBLANES, cw), F32),
        ],
        compiler_params=_cparams(("arbitrary", "arbitrary")),
        name="lru",
    )(xr, gr, conv_w, conv_b, w_gates, b_gates, lru_lambda, out_g)


def _out_proj_kernel(tm, n_sub, aw, *refs):
    for u in range(n_sub):
        _out_proj_tile(u, tm, aw, *refs)


def _out_proj_tile(u, tm, aw, attn_ref, lru_ref, x_ref, mod_ref, w_ref, g_ref, rw_ref, rb_ref, tri_ref, low_ref,
                   x1_ref, h2_ref, meta_ref, cnt_ref):
    rows = pl.ds(u * tm, tm)
    mix = _dot(attn_ref[rows, :], w_ref[0:aw, :]) + _dot(lru_ref[rows, :], w_ref[aw:, :])
    x1 = x_ref[rows, :] + mod_ref[2:3, :] * mix
    x1_ref[rows, :] = x1
    h2 = _rms_modulate(x1, g_ref[...], mod_ref[3:4, :], mod_ref[4:5, :])
    for j in range(h2.shape[1] // LANES):
        h2_ref[pl.ds(u * tm * SUBLANES + j, tm, stride=SUBLANES), :] = h2[:, j * LANES:(j + 1) * LANES]

    h_hi, h_lo = _split_hi_lo(h2)
    w_hi, w_lo = _split_hi_lo(rw_ref[...])
    by_hi = _dot_tb(jnp.concatenate([w_hi, w_lo], axis=0), h_hi)
    lg = by_hi[:N_EXPERTS] + by_hi[N_EXPERTS:] + _dot_tb(w_hi, h_lo) + rb_ref[...]

    eidx = lax.broadcasted_iota(I32, lg.shape, 0)
    picked = jnp.zeros(lg.shape, F32)
    vals, idxs = [], []
    for _ in range(TOP_K):
        m = jnp.max(lg, axis=0, keepdims=True)
        idx = jnp.min(jnp.where(lg == m, eidx, N_EXPERTS), axis=0, keepdims=True)
        sel = eidx == idx
        lg = jnp.where(sel, -jnp.inf, lg)
        picked = picked + sel.astype(F32)
        vals.append(m)
        idxs.append(idx)
    es = [jnp.exp(v - vals[0]) for v in vals]
    den = es[0] + es[1] + es[2] + es[3]
    gates = jnp.concatenate([e / den for e in es], axis=0)

    cnt = jnp.sum(picked, axis=1, keepdims=True)
    cnt_b = jnp.broadcast_to(cnt, cnt_ref.shape[1:])
    cnt_hi = jnp.floor(cnt_b * (1.0 / BF16_EXACT_INT))
    cnt_lo = cnt_b - BF16_EXACT_INT * cnt_hi
    run_start = BF16_EXACT_INT * _dot(low_ref[...], cnt_hi.astype(BF16)) + _dot(low_ref[...], cnt_lo.astype(BF16))
    before = _dot(picked.astype(BF16), tri_ref[...]) + run_start[:, 0:1]
    pos = [jnp.sum(jnp.where(eidx == idx, before, 0.0), axis=0, keepdims=True) for idx in idxs]
    meta_ref[u, 0:TOP_K, :] = jnp.concatenate(pos, axis=0).astype(I32) * SUBLANES
    meta_ref[u, TOP_K:, :] = lax.bitcast_convert_type(gates, I32)
    cnt_ref[u] = cnt_b.astype(I32)


def _out_proj(attn, lru, x2, mod3, w_out_b, norm_g, router_wt, router_b, S, tm, n_sub):
    T, D = x2.shape
    aw = attn.shape[1]
    lw = lru.shape[1]
    ts = tm * n_sub
    tps = S // ts
    tri = (jnp.arange(tm, dtype=I32)[:, None] < jnp.arange(tm, dtype=I32)[None, :]).astype(BF16)
    experts = jnp.arange(N_EXPERTS, dtype=I32)
    low = (experts[None, :] < experts[:, None]).astype(BF16)
    const = lambda shape: pl.BlockSpec(shape, lambda i: (0,) * len(shape))
    row = lambda w: pl.BlockSpec((ts, w), lambda i: (i, 0))
    return pl.pallas_call(
        functools.partial(_out_proj_kernel, tm, n_sub, aw),
        grid=(T // ts,),
        in_specs=[
            row(aw), row(lw), row(D),
            pl.BlockSpec((None, 6, D), lambda i: (i // tps, 0, 0)),
            const((D, D)), const((1, D)), const((N_EXPERTS, D)), const((N_EXPERTS, 1)),
            const((tm, tm)), const((N_EXPERTS, N_EXPERTS)),
        ],
        out_specs=[
            row(D),
            pl.BlockSpec((ts * SUBLANES, LANES), lambda i: (i, 0)),
            pl.BlockSpec((n_sub, 2 * TOP_K, tm), lambda i: (i, 0, 0)),
            pl.BlockSpec((n_sub, N_EXPERTS, LANES), lambda i: (i, 0, 0)),
        ],
        out_shape=[
            jax.ShapeDtypeStruct((T, D), F32),
            jax.ShapeDtypeStruct((T * SUBLANES, LANES), F32),
            jax.ShapeDtypeStruct((T // tm, 2 * TOP_K, tm), I32),
            jax.ShapeDtypeStruct((T // tm, N_EXPERTS, LANES), I32),
        ],
        compiler_params=_cparams(("arbitrary",)),
        name="out_proj",
    )(attn, lru, x2, mod3, w_out_b, norm_g, router_wt, router_b, tri, low)


def _rows(ref, first_row, n_rows):
    return ref.at[pl.ds(pl.multiple_of(first_row * SUBLANES, SUBLANES), n_rows * SUBLANES), :]


def _row_at(ref, sublane_offset):
    return ref.at[pl.ds(pl.multiple_of(sublane_offset, SUBLANES), SUBLANES), :]


def _fetch_tile(meta_hbm, smems, sems, t):
    i = pl.program_id(0)
    n = pl.num_programs(0)
    slot = i % 2
    per_tile = 2 * TOP_K * t

    def copies(tile, s):
        return [pltpu.make_async_copy(meta_hbm.at[pl.ds(pl.multiple_of(tile * per_tile + j * t, t), t)],
                                      smem.at[pl.ds(pl.multiple_of(s * t, t), t)], sems.at[s])
                for j, smem in enumerate(smems)]

    @pl.when(i == 0)
    def _():
        for c in copies(0, 0):
            c.start()

    for c in copies(i, slot):
        c.wait()

    @pl.when(i + 1 < n)
    def _():
        for c in copies(i + 1, 1 - slot):
            c.start()

    return slot * t


def _run_copies(t, tile, cnt_ref, off_ref, dst_ref, make_copy):
    def expert(e, carry):
        n = cnt_ref[tile * N_EXPERTS + e]
        off = off_ref[tile * N_EXPERTS + e]
        dst = dst_ref[tile * N_EXPERTS + e]
        for b in reversed(range(t.bit_length())):
            size = SUBLANES << b
            done = n & ~(2 * size - 1)

            @pl.when((n & size) != 0)
            def _():
                make_copy(off + done, dst + done, size).start()
        return carry

    lax.fori_loop(0, N_EXPERTS, expert, 0)


def _span(ref, first_sublane, n_sublanes):
    return ref.at[pl.ds(pl.multiple_of(first_sublane, SUBLANES), n_sublanes), :]


def _dispatch_kernel(t, n_blocks, cnt_ref, off_ref, dst_ref, pad_ref, meta_hbm, h_ref, xs_hbm,
                     pos0, pos1, pos2, pos3, stage_ref, zero_ref, psem, sem, zsem):
    i = pl.program_id(0)
    n = pl.num_programs(0)
    slot = i % 2

    def zero_fill(go):
        def expert(e, carry):
            first = pad_ref[e]
            length = pad_ref[N_EXPERTS + e]
            for b in reversed(range((MOE_BLOCK - 1).bit_length())):
                size = SUBLANES << b
                done = length & ~(2 * size - 1)

                @pl.when((length & size) != 0)
                def _():
                    go(pltpu.make_async_copy(_span(zero_ref, 0, size), _span(xs_hbm, first + done, size), zsem))
            return carry

        lax.fori_loop(0, N_EXPERTS, expert, 0)

        def block(b, carry):
            go(pltpu.make_async_copy(zero_ref, _rows(xs_hbm, b * MOE_BLOCK, MOE_BLOCK), zsem))
            return carry

        lax.fori_loop(pad_ref[2 * N_EXPERTS], n_blocks, block, 0)

    @pl.when(i == 0)
    def _():
        zero_ref[...] = jnp.zeros_like(zero_ref)
        zero_fill(lambda c: c.start())

    pos = (pos0, pos1, pos2, pos3)
    base = _fetch_tile(meta_hbm, pos, psem, t)
    stage = stage_ref.at[slot]

    def place(r, carry):
        row = _rows(h_ref, r, 1)[...]
        for k in range(TOP_K):
            _row_at(stage, pos[k][base + r])[...] = row
        return carry

    lax.fori_loop(0, t, place, 0, unroll=ROW_LOOP_UNROLL)

    def all_runs(s):
        return pltpu.make_async_copy(stage_ref.at[s], _rows(xs_hbm, 0, TOP_K * t), sem.at[s])

    _run_copies(t, i, cnt_ref, off_ref, dst_ref,
                lambda a, b, m: pltpu.make_async_copy(_span(stage, a, m), _span(xs_hbm, b, m), sem.at[slot]))

    @pl.when(i > 0)
    def _():
        all_runs(1 - slot).wait()

    @pl.when(i == n - 1)
    def _():
        all_runs(slot).wait()
        zero_fill(lambda c: c.wait())


def _dispatch(tile_cnt, tile_off, tile_dst, pad_info, meta, h2t, n_blocks, t):
    n_tiles = meta.shape[0] // (2 * TOP_K * t)
    n_slots = n_blocks * MOE_BLOCK
    grid_spec = pltpu.PrefetchScalarGridSpec(
        num_scalar_prefetch=4,
        grid=(n_tiles,),
        in_specs=[pl.BlockSpec(memory_space=pl.ANY),
                  pl.BlockSpec((t * SUBLANES, LANES), lambda i, *_: (i, 0))],
        out_specs=pl.BlockSpec(memory_space=pl.ANY),
        scratch_shapes=[pltpu.SMEM((2 * t,), I32)] * TOP_K + [
            pltpu.VMEM((2, TOP_K * t * SUBLANES, LANES), F32),
            pltpu.VMEM((MOE_BLOCK * SUBLANES, LANES), F32),
            pltpu.SemaphoreType.DMA((2,)),
            pltpu.SemaphoreType.DMA((2,)),
            pltpu.SemaphoreType.DMA,
        ],
    )
    return pl.pallas_call(
        functools.partial(_dispatch_kernel, t, n_blocks),
        grid_spec=grid_spec,
        out_shape=jax.ShapeDtypeStruct((n_slots * SUBLANES, LANES), F32),
        compiler_params=_cparams(("arbitrary",)),
        name="dispatch",
    )(tile_cnt, tile_off, tile_dst, pad_info, meta, h2t)


def _experts_kernel(ff, be_ref, nv_ref, nxt_ref, xs_ref, wgu_hbm, bgu_ref, wdn_hbm, bdn_ref, ys_ref,
                    wgu_f, wdn_f, wgu_b, wdn_b, sem):
    i = pl.program_id(0)
    nv = nv_ref[i]
    n_slabs = xs_ref.shape[0] // MOE_BLOCK

    def fetch(e):
        return (pltpu.make_async_copy(wgu_hbm.at[e], wgu_f, sem.at[0]),
                pltpu.make_async_copy(wdn_hbm.at[e], wdn_f, sem.at[1]))

    @pl.when(i == 0)
    def _():
        for c in fetch(be_ref[0]):
            c.start()

    @pl.when((i == 0) | (be_ref[i] != be_ref[jnp.maximum(i - 1, 0)]))
    def _():
        for c in fetch(be_ref[i]):
            c.wait()
        wgu_b[...] = wgu_f[...].astype(BF16)
        wdn_b[...] = wdn_f[...].astype(BF16)

        @pl.when(nxt_ref[i] >= 0)
        def _():
            for c in fetch(nxt_ref[i]):
                c.start()

    @pl.when(nv > 0)
    def _():
        x = jnp.concatenate(
            [xs_ref[pl.ds(j, MOE_BLOCK, stride=SUBLANES), :] for j in range(n_slabs)], axis=1)
        gu = _dot(x.astype(BF16), wgu_b[...]) + bgu_ref[...]
        gate = jnp.minimum(gu[:, :ff], SWIGLU_LIMIT)
        up = jnp.clip(gu[:, ff:], -SWIGLU_LIMIT, SWIGLU_LIMIT)
        act = (up + 1.0) * (gate * _sigmoid(SWIGLU_ALPHA * gate))
        y = _dot(act.astype(BF16), wdn_b[...]) + bdn_ref[...]
        for j in range(n_slabs):
            ys_ref[pl.ds(j, MOE_BLOCK, stride=SUBLANES), :] = y[:, j * LANES:(j + 1) * LANES]

    @pl.when(nv == 0)
    def _():
        ys_ref[...] = jnp.zeros_like(ys_ref)


def _experts(block_expert, block_valid, xs, w_gu, b_gu, w_dn, b_dn):
    n_blocks = block_expert.shape[0]
    E, D, ff2 = w_gu.shape
    ff = ff2 // 2
    later = block_expert[None, :] > block_expert[:, None]
    nxt = jnp.min(jnp.where(later, block_expert[None, :], E), axis=1)
    nxt = jnp.where(nxt < E, nxt, -1)
    rows = pl.BlockSpec((MOE_BLOCK * SUBLANES, LANES), lambda i, *_: (i, 0))
    grid_spec = pltpu.PrefetchScalarGridSpec(
        num_scalar_prefetch=3,
        grid=(n_blocks,),
        in_specs=[
            rows,
            pl.BlockSpec(memory_space=pl.ANY),
            pl.BlockSpec((None, 1, ff2), lambda i, be, *_: (be[i], 0, 0)),
            pl.BlockSpec(memory_space=pl.ANY),
            pl.BlockSpec((None, 1, D), lambda i, be, *_: (be[i], 0, 0)),
        ],
        out_specs=rows,
        scratch_shapes=[
            pltpu.VMEM((D, ff2), F32), pltpu.VMEM((ff, D), F32),
            pltpu.VMEM((D, ff2), BF16), pltpu.VMEM((ff, D), BF16),
            pltpu.SemaphoreType.DMA((2,)),
        ],
    )
    return pl.pallas_call(
        functools.partial(_experts_kernel, ff),
        grid_spec=grid_spec,
        out_shape=jax.ShapeDtypeStruct(xs.shape, F32),
        compiler_params=_cparams(("arbitrary",)),
        name="experts",
    )(block_expert, block_valid, nxt.astype(I32), xs, w_gu, b_gu.reshape(E, 1, ff2), w_dn,
      b_dn.reshape(E, 1, D))


def _combine_kernel(t, cnt_ref, off_ref, dst_ref, meta_hbm, ys_hbm, x1_ref, mod_ref, o_ref,
                    pos0, pos1, pos2, pos3, gate0, gate1, gate2, gate3, stage_ref, y_ref, psem, sem):
    i = pl.program_id(0)
    n = pl.num_programs(0)
    slot = i % 2
    pos = (pos0, pos1, pos2, pos3)
    gates = (gate0, gate1, gate2, gate3)
    base = _fetch_tile(meta_hbm, pos + gates, psem, t)

    def fetch_runs(tile, s):
        stage = stage_ref.at[s]
        _run_copies(t, tile, cnt_ref, off_ref, dst_ref,
                    lambda a, b, m: pltpu.make_async_copy(_span(ys_hbm, b, m), _span(stage, a, m), sem.at[s]))

    @pl.when(i == 0)
    def _():
        fetch_runs(0, 0)

    @pl.when(i + 1 < n)
    def _():
        fetch_runs(i + 1, 1 - slot)

    pltpu.make_async_copy(_rows(ys_hbm, 0, TOP_K * t), stage_ref.at[slot], sem.at[slot]).wait()
    stage = stage_ref.at[slot]

    def gather(r, carry):
        acc = None
        for k in range(TOP_K):
            gate = lax.bitcast_convert_type(gates[k][base + r], F32)
            term = gate * _row_at(stage, pos[k][base + r])[...]
            acc = term if acc is None else acc + term
        _rows(y_ref, r, 1)[...] = acc
        return carry

    lax.fori_loop(0, t, gather, 0, unroll=ROW_LOOP_UNROLL)

    n_slabs = o_ref.shape[1] // LANES
    y = jnp.concatenate([y_ref[pl.ds(j, t, stride=SUBLANES), :] for j in range(n_slabs)], axis=1)
    o_ref[...] = x1_ref[...] + mod_ref[5:6, :] * y


def _combine(tile_cnt, tile_off, tile_dst, meta, ys, x1, mod3, S, t):
    T, D = x1.shape
    n_tiles = T // t
    tps = S // t
    grid_spec = pltpu.PrefetchScalarGridSpec(
        num_scalar_prefetch=3,
        grid=(n_tiles,),
        in_specs=[
            pl.BlockSpec(memory_space=pl.ANY),
            pl.BlockSpec(memory_space=pl.ANY),
            pl.BlockSpec((t, D), lambda i, *_: (i, 0)),
            pl.BlockSpec((None, 6, D), lambda i, *_: (i // tps, 0, 0)),
        ],
        out_specs=pl.BlockSpec((t, D), lambda i, *_: (i, 0)),
        scratch_shapes=[pltpu.SMEM((2 * t,), I32)] * (2 * TOP_K) + [
            pltpu.VMEM((2, TOP_K * t * SUBLANES, LANES), F32),
            pltpu.VMEM((t * SUBLANES, LANES), F32),
            pltpu.SemaphoreType.DMA((2,)),
            pltpu.SemaphoreType.DMA((2,)),
        ],
    )
    return pl.pallas_call(
        functools.partial(_combine_kernel, t),
        grid_spec=grid_spec,
        out_shape=jax.ShapeDtypeStruct((T, D), F32),
        compiler_params=_cparams(("arbitrary",)),
        name="combine",
    )(tile_cnt, tile_off, tile_dst, meta, ys, x1, mod3)


def _tile(n, target):
    t = min(n, target)
    while n % t:
        t //= 2
    return t


def _block_diag(w):
    n, bw, _ = w.shape
    eye = jnp.eye(n, dtype=w.dtype)
    return (eye[:, None, :, None] * w[:, :, None, :]).reshape(n * bw, n * bw)


def _layer(l, x2, B, S, c, ada_w, ada_b, norm1_g, w_in, q_norm_g, k_norm_g, lambda_q1, lambda_k1, lambda_q2,
           lambda_k2, attn_subln_g, conv_w, conv_b, lru_wa, lru_ba, lru_wx, lru_bx, lru_lambda, lru_out_g,
           w_out, norm2_g, router_w, router_b, w_gate_up, b_gate_up, w_down, b_down):
    T, D = x2.shape
    lam_init = 0.8 - 0.6 * math.exp(-0.3 * l)
    aw = D // 2
    lw = D - aw
    heads = aw // ATTN_DV
    qkw = heads * 2 * ATTN_DK

    mod, lam = _adaln(c, ada_w, ada_b, lambda_q1, lambda_k1, lambda_q2, lambda_k2, lam_init)
    mod3 = mod.reshape(c.shape[0], 6, D)

    reps = qkw // ATTN_DK
    gq = (jnp.tile(q_norm_g, reps) * (ATTN_DK ** -0.5 * math.log2(math.e))).reshape(1, qkw)
    gk = jnp.tile(k_norm_g, reps).reshape(1, qkw)
    tm = _tile(S, ROUTE_TILE_ROWS)
    q, k, v, xr, gr = _in_proj(x2, mod3, norm1_g.reshape(1, D), w_in.astype(BF16), gq, gk, S,
                               _tile(S, IN_PROJ_ROWS), qkw, aw, lw)

    attn = _attention(q, k, v, lam, attn_subln_g.reshape(1, ATTN_DV), B, S, _tile(S, ATTN_Q_ROWS),
                      _tile(S, ATTN_KV_ROWS), 1.0 - lam_init)

    w_gates = (0.5 * jnp.concatenate([_block_diag(lru_wa), _block_diag(lru_wx)], axis=1)).astype(BF16)
    b_gates = 0.5 * jnp.concatenate([lru_ba, lru_bx]).reshape(1, 2 * lw)
    lru = _lru(xr, gr, conv_w, conv_b.reshape(1, lw), w_gates, b_gates, lru_lambda.reshape(1, lw),
               lru_out_g.reshape(1, lw), B, S, _tile(S, LRU_CHUNK_ROWS))

    x1, h2t, meta, tile_cnt = _out_proj(
        attn, lru, x2, mod3, w_out.astype(BF16), norm2_g.reshape(1, D), router_w.T,
        router_b.reshape(N_EXPERTS, 1), S, tm, _tile(S // tm, OUT_PROJ_TILES))

    tile_cnt = tile_cnt[:, :, 0]
    counts = jnp.sum(tile_cnt, axis=0)
    padded = ((counts + MOE_BLOCK - 1) // MOE_BLOCK) * MOE_BLOCK
    pad_ends = jnp.cumsum(padded)
    pad_starts = pad_ends - padded
    n_blocks = (T * TOP_K) // MOE_BLOCK + N_EXPERTS
    blk_start = jnp.arange(n_blocks, dtype=I32) * MOE_BLOCK
    owner = blk_start[:, None] >= pad_ends[None, :]
    block_expert = jnp.minimum(jnp.sum(owner, axis=1), N_EXPERTS - 1).astype(I32)
    onehot = block_expert[:, None] == jnp.arange(N_EXPERTS, dtype=I32)[None, :]
    row_end = jnp.sum(jnp.where(onehot, (pad_starts + counts)[None, :], 0), axis=1)
    block_valid = jnp.clip(row_end - blk_start, 0, MOE_BLOCK).astype(I32)
    tile_off = jnp.cumsum(tile_cnt, axis=1) - tile_cnt
    tile_dst = pad_starts[None, :] + jnp.cumsum(tile_cnt, axis=0) - tile_cnt
    flat = lambda a: (a * SUBLANES).reshape(-1).astype(I32)

    meta = meta.reshape(-1)
    pad_info = jnp.concatenate([(pad_starts + counts) * SUBLANES, (padded - counts) * SUBLANES,
                                pad_ends[-1:] // MOE_BLOCK]).astype(I32)
    xs = _dispatch(flat(tile_cnt), flat(tile_off), flat(tile_dst), pad_info, meta, h2t, n_blocks, tm)
    ys = _experts(block_expert, block_valid, xs, w_gate_up, b_gate_up, w_down, b_down)
    return _combine(flat(tile_cnt), flat(tile_off), flat(tile_dst), meta, ys, x1, mod3, S, tm)


def kernel(x, c, ada_w, ada_b, norm1_g, w_in, q_norm_g, k_norm_g, lambda_q1, lambda_k1, lambda_q2, lambda_k2,
           attn_subln_g, conv_w, conv_b, lru_wa, lru_ba, lru_wx, lru_bx, lru_lambda, lru_out_g, w_out, norm2_g,
           router_w, router_b, w_gate_up, b_gate_up, w_down, b_down):
    B, S, D = x.shape
    params = (ada_w, ada_b, norm1_g, w_in, q_norm_g, k_norm_g, lambda_q1, lambda_k1, lambda_q2, lambda_k2,
              attn_subln_g, conv_w, conv_b, lru_wa, lru_ba, lru_wx, lru_bx, lru_lambda, lru_out_g, w_out,
              norm2_g, router_w, router_b, w_gate_up, b_gate_up, w_down, b_down)
    x2 = x.reshape(B * S, D)
    for l in range(ada_w.shape[0]):
        x2 = _layer(l, x2, B, S, c, *[p[l] for p in params])
    return x2.reshape(B, S, D)
```

```python
import functools
import math

import jax
import jax.numpy as jnp
from jax import lax
from jax.experimental import pallas as pl
from jax.experimental.pallas import tpu as pltpu

F32 = jnp.float32
BF16 = jnp.bfloat16
I32 = jnp.int32

ATTN_DK = 64
ATTN_DV = 2 * ATTN_DK
CONV_W = 4
LRU_C = 8.0
N_EXPERTS = 32
TOP_K = 4
SWIGLU_LIMIT = 7.0
SWIGLU_ALPHA = 1.702
MOE_BLOCK = 512
EPS = 1e-6
NEG_BIG = -1e30

IN_PROJ_ROWS = 1024
ATTN_Q_ROWS = 1024
ATTN_KV_ROWS = 512
LRU_CHUNK_ROWS = 1024
ROUTE_TILE_ROWS = 512
OUT_PROJ_TILES = 2
ROW_LOOP_UNROLL = 32
BF16_EXACT_INT = 256.0

LANES = 128
SUBLANES = 8
MXU_DEPTH = 256
VMEM_LIMIT = 40 * 1024 * 1024


def _cparams(sem):
    return pltpu.CompilerParams(dimension_semantics=sem, vmem_limit_bytes=VMEM_LIMIT)


def _split_hi_lo(x):
    hi = x.astype(BF16)
    lo = (x - hi.astype(F32)).astype(BF16)
    return hi, lo


def _sigmoid(x):
    return 0.5 * jnp.tanh(0.5 * x) + 0.5


def _dot(a, b):
    return jnp.dot(a, b, preferred_element_type=F32)


def _dot_tb(a, b):
    return lax.dot_general(a, b, (((1,), (1,)), ((), ())), preferred_element_type=F32)


def _adaln_kernel(lam_init, c_ref, w_ref, b_ref, lq1, lk1, lq2, lk2, mod_ref, lam_ref):
    c = c_ref[...]
    s = c * jax.nn.sigmoid(c)
    s_hi, s_lo = _split_hi_lo(s)
    w = w_ref[...]
    w_hi, w_lo = _split_hi_lo(w)
    mod_ref[...] = _dot(s_hi, w_hi) + _dot(s_hi, w_lo) + _dot(s_lo, w_hi) + b_ref[...]
    d1 = jnp.sum(lq1[...] * lk1[...], axis=-1, keepdims=True)
    d2 = jnp.sum(lq2[...] * lk2[...], axis=-1, keepdims=True)
    lam = jnp.exp(d1) - jnp.exp(d2) + lam_init
    lam_ref[...] = jnp.broadcast_to(lam, lam_ref.shape)


def _adaln(c, ada_w, ada_b, lq1, lk1, lq2, lk2, lam_init):
    B, D = c.shape
    n = ada_w.shape[1] // D
    vec = lambda: pl.BlockSpec((1, ATTN_DK), lambda j: (0, 0))
    return pl.pallas_call(
        functools.partial(_adaln_kernel, lam_init),
        grid=(n,),
        in_specs=[
            pl.BlockSpec((B, D), lambda j: (0, 0)),
            pl.BlockSpec((D, D), lambda j: (0, j)),
            pl.BlockSpec((1, D), lambda j: (0, j)),
            vec(), vec(), vec(), vec(),
        ],
        out_specs=[
            pl.BlockSpec((B, D), lambda j: (0, j)),
            pl.BlockSpec((1, LANES), lambda j: (0, 0)),
        ],
        out_shape=[
            jax.ShapeDtypeStruct((B, n * D), F32),
            jax.ShapeDtypeStruct((1, LANES), F32),
        ],
        compiler_params=_cparams(("arbitrary",)),
        name="adaln",
    )(c, ada_w, ada_b.reshape(1, -1), lq1.reshape(1, -1), lk1.reshape(1, -1),
      lq2.reshape(1, -1), lk2.reshape(1, -1))


def _rms_modulate(x, g, shift, scale):
    ms = jnp.mean(x * x, axis=-1, keepdims=True)
    y = x * lax.rsqrt(ms + EPS) * g
    return y * (1.0 + scale) + shift


def _group_rms_scale(q, group_ones):
    sq = (q * q).astype(BF16)
    w = group_ones.shape[0]
    ss = jnp.concatenate([_dot(sq[:, c:c + w], group_ones) for c in range(0, q.shape[1], w)], axis=1)
    return lax.rsqrt(ss * (1.0 / ATTN_DK) + EPS)


def _in_proj_kernel(qkw, aw, lw, x_ref, mod_ref, g_ref, w_ref, gq_ref, gk_ref, ones_ref,
                    q_ref, k_ref, v_ref, xr_ref, gr_ref):
    h = _rms_modulate(x_ref[...], g_ref[...], mod_ref[0:1, :], mod_ref[1:2, :])
    hb = h.astype(BF16)
    group_ones = ones_ref[...]
    o = 0
    q = _dot(hb, w_ref[:, o:o + qkw]); o += qkw
    q_ref[...] = (q * _group_rms_scale(q, group_ones) * gq_ref[...]).astype(BF16)
    k = _dot(hb, w_ref[:, o:o + qkw]); o += qkw
    k_ref[...] = (k * _group_rms_scale(k, group_ones) * gk_ref[...]).astype(BF16)
    v_ref[...] = _dot(hb, w_ref[:, o:o + aw]).astype(BF16); o += aw
    xr_ref[...] = _dot(hb, w_ref[:, o:o + lw]); o += lw
    gr_ref[...] = _dot(hb, w_ref[:, o:o + lw])


def _in_proj(x2, mod3, norm_g, w_in_b, gq, gk, S, tm, qkw, aw, lw):
    T, D = x2.shape
    tps = S // tm
    grp = jnp.arange(MXU_DEPTH, dtype=I32) // ATTN_DK
    group_ones = (grp[:, None] == grp[None, :]).astype(BF16)
    const = lambda shape: pl.BlockSpec(shape, lambda i: (0,) * len(shape))
    row = lambda w: pl.BlockSpec((tm, w), lambda i: (i, 0))
    return pl.pallas_call(
        functools.partial(_in_proj_kernel, qkw, aw, lw),
        grid=(T // tm,),
        in_specs=[
            row(D),
            pl.BlockSpec((None, 6, D), lambda i: (i // tps, 0, 0)),
            const((1, D)),
            const(w_in_b.shape),
            const((1, qkw)), const((1, qkw)),
            const((MXU_DEPTH, MXU_DEPTH)),
        ],
        out_specs=[row(qkw), row(qkw), row(aw), row(lw), row(lw)],
        out_shape=[
            jax.ShapeDtypeStruct((T, qkw), BF16),
            jax.ShapeDtypeStruct((T, qkw), BF16),
            jax.ShapeDtypeStruct((T, aw), BF16),
            jax.ShapeDtypeStruct((T, lw), F32),
            jax.ShapeDtypeStruct((T, lw), F32),
        ],
        compiler_params=_cparams(("arbitrary",)),
        name="in_proj",
    )(x2, mod3, norm_g, w_in_b, gq, gk, group_ones)


ATTN_LOOP_ROWS = 256
ATTN_DIAG_ROWS = 128


def _attn_kernel(bq, bk, out_scale, q_ref, k_ref, v_ref, lam_ref, g_ref, o_ref,
                 qq_ref, vp_ref, s_ref, sd_ref, m_ref, acc_ref):
    n_rows = 2 * bq

    col = lax.broadcasted_iota(I32, (vp_ref.shape[0], ATTN_DV), 1)
    vp_ref[:, :ATTN_DV] = v_ref[...]
    vp_ref[:, ATTN_DV:] = jnp.where(col == 0, 1.0, 0.0).astype(BF16)
    acc_ref[...] = jnp.zeros_like(acc_ref)

    def query_block(i, carry):
        q_rows = pl.ds(pl.multiple_of(i * bq, bq), bq)
        _attn_query_block(i, q_rows, bq, bk, n_rows, out_scale, q_ref, k_ref, lam_ref, g_ref, o_ref,
                          qq_ref, vp_ref, s_ref, sd_ref, m_ref, acc_ref)
        return carry

    lax.fori_loop(0, q_ref.shape[0] // bq, query_block, 0)


def _attn_query_block(i, q_rows, bq, bk, n_rows, out_scale, q_ref, k_ref, lam_ref, g_ref, o_ref,
                      qq_ref, vp_ref, s_ref, sd_ref, m_ref, acc_ref):
    q = q_ref[q_rows, :]
    lane = lax.broadcasted_iota(I32, q.shape, 1)
    zero = jnp.zeros_like(q)
    qq_ref[0:bq, :] = jnp.where(lane < ATTN_DK, q, zero)
    qq_ref[bq:, :] = jnp.where(lane >= ATTN_DK, q, zero)
    m_ref[...] = jnp.full(m_ref.shape, NEG_BIG, F32)

    def scores(rows, start, kw):
        return _dot_tb(qq_ref[rows, :], k_ref[pl.ds(start, kw), :])

    def softmax_pv(s, rows, start):
        kw = s.shape[1]
        m_old = m_ref[rows, :]
        m_new = jnp.maximum(m_old, jnp.max(s, axis=-1, keepdims=True))
        alpha = jnp.exp2(m_old - m_new)
        p = jnp.exp2(s - jnp.tile(m_new, (1, kw // LANES)))
        pv = _dot(p.astype(BF16), vp_ref[pl.ds(start, kw), :])
        acc_ref[rows, :] = acc_ref[rows, :] * jnp.tile(alpha, (1, 2)) + pv
        m_ref[rows, :] = m_new

    loop_chunks = [pl.ds(r, ATTN_LOOP_ROWS) for r in range(0, n_rows, ATTN_LOOP_ROWS)]
    for rows in loop_chunks:
        s_ref[rows, :] = scores(rows, 0, bk)

    def body(j, carry):
        start = pl.multiple_of(j * bk, bk)
        for rows in loop_chunks:
            s = s_ref[rows, :]
            s_ref[rows, :] = scores(rows, start + bk, bk)
            softmax_pv(s, rows, start)
        return carry

    n_full = i * (bq // bk)
    lax.fori_loop(0, n_full, body, 0)

    start = pl.multiple_of(n_full * bk, bk)
    rc = ATTN_DIAG_ROWS
    tri = (lax.broadcasted_iota(I32, (rc, rc), 1) <= lax.broadcasted_iota(I32, (rc, rc), 0))
    diag_chunks = [(pl.ds(r, rc), r % bq) for r in range(0, n_rows, rc)]
    for rows, q0 in diag_chunks:
        if q0 + rc > bk:
            sd_ref[rows, 0:q0 + rc - bk] = scores(rows, start + bk, q0 + rc - bk)
    for rows, q0 in diag_chunks:
        kw = q0 + rc
        parts = [s_ref[rows, 0:min(q0, bk)]] if q0 else []
        if q0 > bk:
            parts.append(sd_ref[rows, 0:q0 - bk])
        last = s_ref[rows, q0:kw] if kw <= bk else sd_ref[rows, q0 - bk:kw - bk]
        parts.append(jnp.where(tri, last, NEG_BIG))
        softmax_pv(parts[0] if len(parts) == 1 else jnp.concatenate(parts, axis=1), rows, start)

    acc = acc_ref[...]
    o = acc[:, :ATTN_DV] / acc[:, ATTN_DV:ATTN_DV + 1]
    a = o[:bq] - lam_ref[0:1, 0:1] * o[bq:]
    ms = jnp.mean(a * a, axis=-1, keepdims=True)
    o_ref[q_rows, :] = (a * lax.rsqrt(ms + EPS) * g_ref[...] * out_scale).astype(BF16)


def _attention(q, k, v, lam, subln_g, B, S, bq, bk, out_scale):
    T, qkw = q.shape
    H = qkw // ATTN_DV
    return pl.pallas_call(
        functools.partial(_attn_kernel, bq, bk, out_scale),
        grid=(B, H),
        in_specs=[
            pl.BlockSpec((S, ATTN_DV), lambda b, h: (b, h)),
            pl.BlockSpec((S, ATTN_DV), lambda b, h: (b, h)),
            pl.BlockSpec((S, ATTN_DV), lambda b, h: (b, h)),
            pl.BlockSpec((1, LANES), lambda b, h: (0, 0)),
            pl.BlockSpec((1, ATTN_DV), lambda b, h: (0, 0)),
        ],
        out_specs=pl.BlockSpec((S, ATTN_DV), lambda b, h: (b, h)),
        out_shape=jax.ShapeDtypeStruct((T, H * ATTN_DV), BF16),
        scratch_shapes=[
            pltpu.VMEM((2 * bq, ATTN_DV), BF16),
            pltpu.VMEM((S, 2 * ATTN_DV), BF16),
            pltpu.VMEM((2 * bq, bk), F32),
            pltpu.VMEM((2 * bq, max(bq - bk, LANES)), F32),
            pltpu.VMEM((2 * bq, LANES), F32),
            pltpu.VMEM((2 * bq, 2 * ATTN_DV), F32),
        ],
        compiler_params=_cparams(("arbitrary", "arbitrary")),
        name="attention",
    )(q, k, v, lam, subln_g)


def _lru_kernel(tc, cw, pitch, x_ref, gate_ref, cw_ref, cb_ref, wg_ref, bg_ref, lam_ref, og_ref,
                o_ref, ext_ref, a_ref, b_ref, hc_ref):
    c = pl.program_id(1)

    @pl.when(c == 0)
    def _():
        ext_ref[0:SUBLANES, :] = jnp.zeros((SUBLANES, cw), F32)
        hc_ref[...] = jnp.zeros_like(hc_ref)

    ext_ref[SUBLANES:SUBLANES + tc, :] = x_ref[...]
    xc = cb_ref[...] + cw_ref[CONV_W - 1:CONV_W, :] * x_ref[...]
    for w in range(CONV_W - 1):
        sh = CONV_W - 1 - w
        xc = xc + cw_ref[w:w + 1, :] * ext_ref[SUBLANES - sh:SUBLANES - sh + tc, :]
    tail = ext_ref[tc:tc + SUBLANES, :]
    ext_ref[0:SUBLANES, :] = tail

    g = _dot(xc.astype(BF16), wg_ref[...]) + bg_ref[...]
    t_r = jnp.tanh(g[:, :cw])
    t_i = jnp.tanh(g[:, cw:])
    nl = -lam_ref[...]
    softplus = jnp.maximum(nl, 0.0) + jnp.log1p(jnp.exp(-jnp.abs(nl)))
    c1 = (-0.5 * LRU_C) * softplus
    log_a = c1 * t_r + c1
    th = jnp.tanh(log_a)
    p = -2.0 * th
    root = jnp.where(p > 0.0, p * lax.rsqrt(p * (1.0 - th)), 0.0)
    a = jnp.exp(log_a)
    b = root * ((t_i + 1.0) * (0.5 * xc))

    steps = tc // SUBLANES
    n_lt = cw // LANES
    for q in range(SUBLANES):
        for l in range(n_lt):
            a_ref[l, q * pitch:q * pitch + steps, :] = a[q * steps:(q + 1) * steps, l * LANES:(l + 1) * LANES]
            b_ref[l, q * pitch:q * pitch + steps, :] = b[q * steps:(q + 1) * steps, l * LANES:(l + 1) * LANES]

    def scan_step(t, carry):
        out = []
        for l in range(n_lt):
            h, prod = carry[l]
            rows = pl.ds(t, SUBLANES, stride=pitch)
            a_t = a_ref[l, rows, :]
            h = a_t * h + b_ref[l, rows, :]
            prod = a_t * prod
            b_ref[l, rows, :] = h
            a_ref[l, rows, :] = prod
            out.append((h, prod))
        return tuple(out)

    init = tuple((jnp.zeros((SUBLANES, LANES), F32), jnp.ones((SUBLANES, LANES), F32)) for _ in range(n_lt))
    last = lax.fori_loop(0, steps, scan_step, init, unroll=4)

    h_cols = []
    for l in range(n_lt):
        h_last, p_last = last[l]
        entering = [hc_ref[0:1, l * LANES:(l + 1) * LANES]]
        for q in range(SUBLANES):
            entering.append(h_last[q:q + 1, :] + p_last[q:q + 1, :] * entering[q])
        hc_ref[:, l * LANES:(l + 1) * LANES] = jnp.broadcast_to(entering[SUBLANES], (SUBLANES, LANES))
        h_cols.append(jnp.concatenate(
            [b_ref[l, q * pitch:q * pitch + steps, :] + a_ref[l, q * pitch:q * pitch + steps, :] * entering[q]
             for q in range(SUBLANES)], axis=0))
    h = jnp.concatenate(h_cols, axis=1)

    gt = gate_ref[...]
    gelu = 0.5 * gt * (1.0 + jnp.tanh(math.sqrt(2.0 / math.pi) * (gt + 0.044715 * gt * gt * gt)))
    y = h * gelu
    ms = jnp.mean(y * y, axis=-1, keepdims=True)
    o_ref[...] = (y * lax.rsqrt(ms + EPS) * og_ref[...]).astype(BF16)


def _scan_pitch(steps):
    return steps if (steps // SUBLANES) % 2 else steps + SUBLANES


def _lru(xr, gr, conv_w, conv_b, w_gates, b_gates, lru_lambda, out_g, B, S, tc):
    T, cw = xr.shape
    nc = S // tc
    pitch = _scan_pitch(tc // SUBLANES)
    const = lambda shape: pl.BlockSpec(shape, lambda b, c: (0,) * len(shape))
    row = pl.BlockSpec((tc, cw), lambda b, c: (b * nc + c, 0))
    return pl.pallas_call(
        functools.partial(_lru_kernel, tc, cw, pitch),
        grid=(B, nc),
        in_specs=[row, row, const((CONV_W, cw)), const((1, cw)), const((cw, 2 * cw)),
                  const((1, 2 * cw)), const((1, cw)), const((1, cw))],
        out_specs=row,
        out_shape=jax.ShapeDtypeStruct((T, cw), BF16),
        scratch_shapes=[
            pltpu.VMEM((tc + SUBLANES, cw), F32),
            pltpu.VMEM((cw // LANES, SUBLANES * pitch, LANES), F32),
            pltpu.VMEM((cw // LANES, SUBLANES * pitch, LANES), F32),
            pltpu.VMEM((SUBLANES, cw), F32),
        ],
        compiler_params=_cparams(("arbitrary", "arbitrary")),
        name="lru",
    )(xr, gr, conv_w, conv_b, w_gates, b_gates, lru_lambda, out_g)


def _out_proj_kernel(tm, n_sub, aw, *refs):
    for u in range(n_sub):
        _out_proj_tile(u, tm, aw, *refs)


def _out_proj_tile(u, tm, aw, attn_ref, lru_ref, x_ref, mod_ref, w_ref, g_ref, rw_ref, rb_ref, tri_ref, low_ref,
                   x1_ref, h2_ref, meta_ref, cnt_ref):
    rows = pl.ds(u * tm, tm)
    mix = _dot(attn_ref[rows, :], w_ref[0:aw, :]) + _dot(lru_ref[rows, :], w_ref[aw:, :])
    x1 = x_ref[rows, :] + mod_ref[2:3, :] * mix
    x1_ref[rows, :] = x1
    h2 = _rms_modulate(x1, g_ref[...], mod_ref[3:4, :], mod_ref[4:5, :])
    for j in range(h2.shape[1] // LANES):
        h2_ref[pl.ds(u * tm * SUBLANES + j, tm, stride=SUBLANES), :] = h2[:, j * LANES:(j + 1) * LANES]

    h_hi, h_lo = _split_hi_lo(h2)
    w_hi, w_lo = _split_hi_lo(rw_ref[...])
    by_hi = _dot_tb(jnp.concatenate([w_hi, w_lo], axis=0), h_hi)
    lg = by_hi[:N_EXPERTS] + by_hi[N_EXPERTS:] + _dot_tb(w_hi, h_lo) + rb_ref[...]

    eidx = lax.broadcasted_iota(I32, lg.shape, 0)
    picked = jnp.zeros(lg.shape, F32)
    vals, idxs = [], []
    for _ in range(TOP_K):
        m = jnp.max(lg, axis=0, keepdims=True)
        idx = jnp.min(jnp.where(lg == m, eidx, N_EXPERTS), axis=0, keepdims=True)
        sel = eidx == idx
        lg = jnp.where(sel, -jnp.inf, lg)
        picked = picked + sel.astype(F32)
        vals.append(m)
        idxs.append(idx)
    es = [jnp.exp(v - vals[0]) for v in vals]
    den = es[0] + es[1] + es[2] + es[3]
    gates = jnp.concatenate([e / den for e in es], axis=0)

    cnt = jnp.sum(picked, axis=1, keepdims=True)
    cnt_b = jnp.broadcast_to(cnt, cnt_ref.shape[1:])
    cnt_hi = jnp.floor(cnt_b * (1.0 / BF16_EXACT_INT))
    cnt_lo = cnt_b - BF16_EXACT_INT * cnt_hi
    run_start = BF16_EXACT_INT * _dot(low_ref[...], cnt_hi.astype(BF16)) + _dot(low_ref[...], cnt_lo.astype(BF16))
    before = _dot(picked.astype(BF16), tri_ref[...]) + run_start[:, 0:1]
    pos = [jnp.sum(jnp.where(eidx == idx, before, 0.0), axis=0, keepdims=True) for idx in idxs]
    meta_ref[u, 0:TOP_K, :] = jnp.concatenate(pos, axis=0).astype(I32) * SUBLANES
    meta_ref[u, TOP_K:, :] = lax.bitcast_convert_type(gates, I32)
    cnt_ref[u] = cnt_b.astype(I32)


def _out_proj(attn, lru, x2, mod3, w_out_b, norm_g, router_wt, router_b, S, tm, n_sub):
    T, D = x2.shape
    aw = attn.shape[1]
    lw = lru.shape[1]
    ts = tm * n_sub
    tps = S // ts
    tri = (jnp.arange(tm, dtype=I32)[:, None] < jnp.arange(tm, dtype=I32)[None, :]).astype(BF16)
    experts = jnp.arange(N_EXPERTS, dtype=I32)
    low = (experts[None, :] < experts[:, None]).astype(BF16)
    const = lambda shape: pl.BlockSpec(shape, lambda i: (0,) * len(shape))
    row = lambda w: pl.BlockSpec((ts, w), lambda i: (i, 0))
    return pl.pallas_call(
        functools.partial(_out_proj_kernel, tm, n_sub, aw),
        grid=(T // ts,),
        in_specs=[
            row(aw), row(lw), row(D),
            pl.BlockSpec((None, 6, D), lambda i: (i // tps, 0, 0)),
            const((D, D)), const((1, D)), const((N_EXPERTS, D)), const((N_EXPERTS, 1)),
            const((tm, tm)), const((N_EXPERTS, N_EXPERTS)),
        ],
        out_specs=[
            row(D),
            pl.BlockSpec((ts * SUBLANES, LANES), lambda i: (i, 0)),
            pl.BlockSpec((n_sub, 2 * TOP_K, tm), lambda i: (i, 0, 0)),
            pl.BlockSpec((n_sub, N_EXPERTS, LANES), lambda i: (i, 0, 0)),
        ],
        out_shape=[
            jax.ShapeDtypeStruct((T, D), F32),
            jax.ShapeDtypeStruct((T * SUBLANES, LANES), F32),
            jax.ShapeDtypeStruct((T // tm, 2 * TOP_K, tm), I32),
            jax.ShapeDtypeStruct((T // tm, N_EXPERTS, LANES), I32),
        ],
        compiler_params=_cparams(("arbitrary",)),
        name="out_proj",
    )(attn, lru, x2, mod3, w_out_b, norm_g, router_wt, router_b, tri, low)


def _rows(ref, first_row, n_rows):
    return ref.at[pl.ds(pl.multiple_of(first_row * SUBLANES, SUBLANES), n_rows * SUBLANES), :]


def _row_at(ref, sublane_offset):
    return ref.at[pl.ds(pl.multiple_of(sublane_offset, SUBLANES), SUBLANES), :]


def _fetch_tile(meta_hbm, smems, sems, t):
    i = pl.program_id(0)
    n = pl.num_programs(0)
    slot = i % 2
    per_tile = 2 * TOP_K * t

    def copies(tile, s):
        return [pltpu.make_async_copy(meta_hbm.at[pl.ds(pl.multiple_of(tile * per_tile + j * t, t), t)],
                                      smem.at[pl.ds(pl.multiple_of(s * t, t), t)], sems.at[s])
                for j, smem in enumerate(smems)]

    @pl.when(i == 0)
    def _():
        for c in copies(0, 0):
            c.start()

    for c in copies(i, slot):
        c.wait()

    @pl.when(i + 1 < n)
    def _():
        for c in copies(i + 1, 1 - slot):
            c.start()

    return slot * t


def _run_copies(t, tile, cnt_ref, off_ref, dst_ref, make_copy):
    def expert(e, carry):
        n = cnt_ref[tile * N_EXPERTS + e]
        off = off_ref[tile * N_EXPERTS + e]
        dst = dst_ref[tile * N_EXPERTS + e]
        for b in reversed(range(t.bit_length())):
            size = SUBLANES << b
            done = n & ~(2 * size - 1)

            @pl.when((n & size) != 0)
            def _():
                make_copy(off + done, dst + done, size).start(priority=b % 2)
        return carry

    lax.fori_loop(0, N_EXPERTS, expert, 0)


def _span(ref, first_sublane, n_sublanes):
    return ref.at[pl.ds(pl.multiple_of(first_sublane, SUBLANES), n_sublanes), :]


def _dispatch_kernel(t, n_blocks, cnt_ref, off_ref, dst_ref, pad_ref, meta_hbm, h_ref, xs_hbm,
                     pos0, pos1, pos2, pos3, stage_ref, zero_ref, psem, sem, zsem):
    i = pl.program_id(0)
    n = pl.num_programs(0)
    slot = i % 2

    def zero_fill(go):
        def expert(e, carry):
            first = pad_ref[e]
            length = pad_ref[N_EXPERTS + e]
            for b in reversed(range((MOE_BLOCK - 1).bit_length())):
                size = SUBLANES << b
                done = length & ~(2 * size - 1)

                @pl.when((length & size) != 0)
                def _():
                    go(pltpu.make_async_copy(_span(zero_ref, 0, size), _span(xs_hbm, first + done, size), zsem))
            return carry

        lax.fori_loop(0, N_EXPERTS, expert, 0)

        def block(b, carry):
            go(pltpu.make_async_copy(zero_ref, _rows(xs_hbm, b * MOE_BLOCK, MOE_BLOCK), zsem))
            return carry

        lax.fori_loop(pad_ref[2 * N_EXPERTS], n_blocks, block, 0)

    @pl.when(i == 0)
    def _():
        zero_ref[...] = jnp.zeros_like(zero_ref)
        zero_fill(lambda c: c.start())

    pos = (pos0, pos1, pos2, pos3)
    base = _fetch_tile(meta_hbm, pos, psem, t)
    stage = stage_ref.at[slot]

    def place(r, carry):
        row = _rows(h_ref, r, 1)[...]
        for k in range(TOP_K):
            _row_at(stage, pos[k][base + r])[...] = row
        return carry

    lax.fori_loop(0, t, place, 0, unroll=ROW_LOOP_UNROLL)

    def all_runs(s):
        return pltpu.make_async_copy(stage_ref.at[s], _rows(xs_hbm, 0, TOP_K * t), sem.at[s])

    _run_copies(t, i, cnt_ref, off_ref, dst_ref,
                lambda a, b, m: pltpu.make_async_copy(_span(stage, a, m), _span(xs_hbm, b, m), sem.at[slot]))

    @pl.when(i > 0)
    def _():
        all_runs(1 - slot).wait()

    @pl.when(i == n - 1)
    def _():
        all_runs(slot).wait()
        zero_fill(lambda c: c.wait())


def _dispatch(tile_cnt, tile_off, tile_dst, pad_info, meta, h2t, n_blocks, t):
    n_tiles = meta.shape[0] // (2 * TOP_K * t)
    n_slots = n_blocks * MOE_BLOCK
    grid_spec = pltpu.PrefetchScalarGridSpec(
        num_scalar_prefetch=4,
        grid=(n_tiles,),
        in_specs=[pl.BlockSpec(memory_space=pl.ANY),
                  pl.BlockSpec((t * SUBLANES, LANES), lambda i, *_: (i, 0))],
        out_specs=pl.BlockSpec(memory_space=pl.ANY),
        scratch_shapes=[pltpu.SMEM((2 * t,), I32)] * TOP_K + [
            pltpu.VMEM((2, TOP_K * t * SUBLANES, LANES), F32),
            pltpu.VMEM((MOE_BLOCK * SUBLANES, LANES), F32),
            pltpu.SemaphoreType.DMA((2,)),
            pltpu.SemaphoreType.DMA((2,)),
            pltpu.SemaphoreType.DMA,
        ],
    )
    return pl.pallas_call(
        functools.partial(_dispatch_kernel, t, n_blocks),
        grid_spec=grid_spec,
        out_shape=jax.ShapeDtypeStruct((n_slots * SUBLANES, LANES), F32),
        compiler_params=_cparams(("arbitrary",)),
        name="dispatch",
    )(tile_cnt, tile_off, tile_dst, pad_info, meta, h2t)


def _experts_kernel(ff, be_ref, nv_ref, nxt_ref, xs_ref, wgu_hbm, bgu_ref, wdn_hbm, bdn_ref, ys_ref,
                    wgu_f, wdn_f, wgu_b, wdn_b, sem):
    i = pl.program_id(0)
    nv = nv_ref[i]
    n_slabs = xs_ref.shape[0] // MOE_BLOCK

    def fetch(e):
        return (pltpu.make_async_copy(wgu_hbm.at[e], wgu_f, sem.at[0]),
                pltpu.make_async_copy(wdn_hbm.at[e], wdn_f, sem.at[1]))

    @pl.when(i == 0)
    def _():
        for c in fetch(be_ref[0]):
            c.start()

    @pl.when((i == 0) | (be_ref[i] != be_ref[jnp.maximum(i - 1, 0)]))
    def _():
        for c in fetch(be_ref[i]):
            c.wait()
        wgu_b[...] = wgu_f[...].astype(BF16)
        wdn_b[...] = wdn_f[...].astype(BF16)

        @pl.when(nxt_ref[i] >= 0)
        def _():
            for c in fetch(nxt_ref[i]):
                c.start()

    @pl.when(nv > 0)
    def _():
        x = jnp.concatenate(
            [xs_ref[pl.ds(j, MOE_BLOCK, stride=SUBLANES), :] for j in range(n_slabs)], axis=1)
        gu = _dot(x.astype(BF16), wgu_b[...]) + bgu_ref[...]
        gate = jnp.minimum(gu[:, :ff], SWIGLU_LIMIT)
        up = jnp.clip(gu[:, ff:], -SWIGLU_LIMIT, SWIGLU_LIMIT)
        act = (up + 1.0) * (gate * _sigmoid(SWIGLU_ALPHA * gate))
        y = _dot(act.astype(BF16), wdn_b[...]) + bdn_ref[...]
        for j in range(n_slabs):
            ys_ref[pl.ds(j, MOE_BLOCK, stride=SUBLANES), :] = y[:, j * LANES:(j + 1) * LANES]

    @pl.when(nv == 0)
    def _():
        ys_ref[...] = jnp.zeros_like(ys_ref)


def _experts(block_expert, block_valid, xs, w_gu, b_gu, w_dn, b_dn):
    n_blocks = block_expert.shape[0]
    E, D, ff2 = w_gu.shape
    ff = ff2 // 2
    later = block_expert[None, :] > block_expert[:, None]
    nxt = jnp.min(jnp.where(later, block_expert[None, :], E), axis=1)
    nxt = jnp.where(nxt < E, nxt, -1)
    rows = pl.BlockSpec((MOE_BLOCK * SUBLANES, LANES), lambda i, *_: (i, 0))
    grid_spec = pltpu.PrefetchScalarGridSpec(
        num_scalar_prefetch=3,
        grid=(n_blocks,),
        in_specs=[
            rows,
            pl.BlockSpec(memory_space=pl.ANY),
            pl.BlockSpec((None, 1, ff2), lambda i, be, *_: (be[i], 0, 0)),
            pl.BlockSpec(memory_space=pl.ANY),
            pl.BlockSpec((None, 1, D), lambda i, be, *_: (be[i], 0, 0)),
        ],
        out_specs=rows,
        scratch_shapes=[
            pltpu.VMEM((D, ff2), F32), pltpu.VMEM((ff, D), F32),
            pltpu.VMEM((D, ff2), BF16), pltpu.VMEM((ff, D), BF16),
            pltpu.SemaphoreType.DMA((2,)),
        ],
    )
    return pl.pallas_call(
        functools.partial(_experts_kernel, ff),
        grid_spec=grid_spec,
        out_shape=jax.ShapeDtypeStruct(xs.shape, F32),
        compiler_params=_cparams(("arbitrary",)),
        name="experts",
    )(block_expert, block_valid, nxt.astype(I32), xs, w_gu, b_gu.reshape(E, 1, ff2), w_dn,
      b_dn.reshape(E, 1, D))


def _combine_kernel(t, cnt_ref, off_ref, dst_ref, meta_hbm, ys_hbm, x1_ref, mod_ref, o_ref,
                    pos0, pos1, pos2, pos3, gate0, gate1, gate2, gate3, stage_ref, y_ref, psem, sem):
    i = pl.program_id(0)
    n = pl.num_programs(0)
    slot = i % 2
    pos = (pos0, pos1, pos2, pos3)
    gates = (gate0, gate1, gate2, gate3)
    base = _fetch_tile(meta_hbm, pos + gates, psem, t)

    def fetch_runs(tile, s):
        stage = stage_ref.at[s]
        _run_copies(t, tile, cnt_ref, off_ref, dst_ref,
                    lambda a, b, m: pltpu.make_async_copy(_span(ys_hbm, b, m), _span(stage, a, m), sem.at[s]))

    @pl.when(i == 0)
    def _():
        fetch_runs(0, 0)

    @pl.when(i + 1 < n)
    def _():
        fetch_runs(i + 1, 1 - slot)

    pltpu.make_async_copy(_rows(ys_hbm, 0, TOP_K * t), stage_ref.at[slot], sem.at[slot]).wait()
    stage = stage_ref.at[slot]

    def gather(r, carry):
        acc = None
        for k in range(TOP_K):
            gate = lax.bitcast_convert_type(gates[k][base + r], F32)
            term = gate * _row_at(stage, pos[k][base + r])[...]
            acc = term if acc is None else acc + term
        _rows(y_ref, r, 1)[...] = acc
        return carry

    lax.fori_loop(0, t, gather, 0, unroll=ROW_LOOP_UNROLL)

    n_slabs = o_ref.shape[1] // LANES
    y = jnp.concatenate([y_ref[pl.ds(j, t, stride=SUBLANES), :] for j in range(n_slabs)], axis=1)
    o_ref[...] = x1_ref[...] + mod_ref[5:6, :] * y


def _combine(tile_cnt, tile_off, tile_dst, meta, ys, x1, mod3, S, t):
    T, D = x1.shape
    n_tiles = T // t
    tps = S // t
    grid_spec = pltpu.PrefetchScalarGridSpec(
        num_scalar_prefetch=3,
        grid=(n_tiles,),
        in_specs=[
            pl.BlockSpec(memory_space=pl.ANY),
            pl.BlockSpec(memory_space=pl.ANY),
            pl.BlockSpec((t, D), lambda i, *_: (i, 0)),
            pl.BlockSpec((None, 6, D), lambda i, *_: (i // tps, 0, 0)),
        ],
        out_specs=pl.BlockSpec((t, D), lambda i, *_: (i, 0)),
        scratch_shapes=[pltpu.SMEM((2 * t,), I32)] * (2 * TOP_K) + [
            pltpu.VMEM((2, TOP_K * t * SUBLANES, LANES), F32),
            pltpu.VMEM((t * SUBLANES, LANES), F32),
            pltpu.SemaphoreType.DMA((2,)),
            pltpu.SemaphoreType.DMA((2,)),
        ],
    )
    return pl.pallas_call(
        functools.partial(_combine_kernel, t),
        grid_spec=grid_spec,
        out_shape=jax.ShapeDtypeStruct((T, D), F32),
        compiler_params=_cparams(("arbitrary",)),
        name="combine",
    )(tile_cnt, tile_off, tile_dst, meta, ys, x1, mod3)


def _tile(n, target):
    t = min(n, target)
    while n % t:
        t //= 2
    return t


def _block_diag(w):
    n, bw, _ = w.shape
    eye = jnp.eye(n, dtype=w.dtype)
    return (eye[:, None, :, None] * w[:, :, None, :]).reshape(n * bw, n * bw)


def _layer(l, x2, B, S, c, ada_w, ada_b, norm1_g, w_in, q_norm_g, k_norm_g, lambda_q1, lambda_k1, lambda_q2,
           lambda_k2, attn_subln_g, conv_w, conv_b, lru_wa, lru_ba, lru_wx, lru_bx, lru_lambda, lru_out_g,
           w_out, norm2_g, router_w, router_b, w_gate_up, b_gate_up, w_down, b_down):
    T, D = x2.shape
    lam_init = 0.8 - 0.6 * math.exp(-0.3 * l)
    aw = D // 2
    lw = D - aw
    heads = aw // ATTN_DV
    qkw = heads * 2 * ATTN_DK

    mod, lam = _adaln(c, ada_w, ada_b, lambda_q1, lambda_k1, lambda_q2, lambda_k2, lam_init)
    mod3 = mod.reshape(c.shape[0], 6, D)

    reps = qkw // ATTN_DK
    gq = (jnp.tile(q_norm_g, reps) * (ATTN_DK ** -0.5 * math.log2(math.e))).reshape(1, qkw)
    gk = jnp.tile(k_norm_g, reps).reshape(1, qkw)
    tm = _tile(S, ROUTE_TILE_ROWS)
    q, k, v, xr, gr = _in_proj(x2, mod3, norm1_g.reshape(1, D), w_in.astype(BF16), gq, gk, S,
                               _tile(S, IN_PROJ_ROWS), qkw, aw, lw)

    attn = _attention(q, k, v, lam, attn_subln_g.reshape(1, ATTN_DV), B, S, _tile(S, ATTN_Q_ROWS),
                      _tile(S, ATTN_KV_ROWS), 1.0 - lam_init)

    w_gates = (0.5 * jnp.concatenate([_block_diag(lru_wa), _block_diag(lru_wx)], axis=1)).astype(BF16)
    b_gates = 0.5 * jnp.concatenate([lru_ba, lru_bx]).reshape(1, 2 * lw)
    lru = _lru(xr, gr, conv_w, conv_b.reshape(1, lw), w_gates, b_gates, lru_lambda.reshape(1, lw),
               lru_out_g.reshape(1, lw), B, S, _tile(S, LRU_CHUNK_ROWS))

    x1, h2t, meta, tile_cnt = _out_proj(
        attn, lru, x2, mod3, w_out.astype(BF16), norm2_g.reshape(1, D), router_w.T,
        router_b.reshape(N_EXPERTS, 1), S, tm, _tile(S // tm, OUT_PROJ_TILES))

    tile_cnt = tile_cnt[:, :, 0]
    counts = jnp.sum(tile_cnt, axis=0)
    padded = ((counts + MOE_BLOCK - 1) // MOE_BLOCK) * MOE_BLOCK
    pad_ends = jnp.cumsum(padded)
    pad_starts = pad_ends - padded
    n_blocks = (T * TOP_K) // MOE_BLOCK + N_EXPERTS
    blk_start = jnp.arange(n_blocks, dtype=I32) * MOE_BLOCK
    owner = blk_start[:, None] >= pad_ends[None, :]
    block_expert = jnp.minimum(jnp.sum(owner, axis=1), N_EXPERTS - 1).astype(I32)
    onehot = block_expert[:, None] == jnp.arange(N_EXPERTS, dtype=I32)[None, :]
    row_end = jnp.sum(jnp.where(onehot, (pad_starts + counts)[None, :], 0), axis=1)
    block_valid = jnp.clip(row_end - blk_start, 0, MOE_BLOCK).astype(I32)
    tile_off = jnp.cumsum(tile_cnt, axis=1) - tile_cnt
    tile_dst = pad_starts[None, :] + jnp.cumsum(tile_cnt, axis=0) - tile_cnt
    flat = lambda a: (a * SUBLANES).reshape(-1).astype(I32)

    meta = meta.reshape(-1)
    pad_info = jnp.concatenate([(pad_starts + counts) * SUBLANES, (padded - counts) * SUBLANES,
                                pad_ends[-1:] // MOE_BLOCK]).astype(I32)
    xs = _dispatch(flat(tile_cnt), flat(tile_off), flat(tile_dst), pad_info, meta, h2t, n_blocks, tm)
    ys = _experts(block_expert, block_valid, xs, w_gate_up, b_gate_up, w_down, b_down)
    return _combine(flat(tile_cnt), flat(tile_off), flat(tile_dst), meta, ys, x1, mod3, S, tm)


def kernel(x, c, ada_w, ada_b, norm1_g, w_in, q_norm_g, k_norm_g, lambda_q1, lambda_k1, lambda_q2, lambda_k2,
           attn_subln_g, conv_w, conv_b, lru_wa, lru_ba, lru_wx, lru_bx, lru_lambda, lru_out_g, w_out, norm2_g,
           router_w, router_b, w_gate_up, b_gate_up, w_down, b_down):
    B, S, D = x.shape
    params = (ada_w, ada_b, norm1_g, w_in, q_norm_g, k_norm_g, lambda_q1, lambda_k1, lambda_q2, lambda_k2,
              attn_subln_g, conv_w, conv_b, lru_wa, lru_ba, lru_wx, lru_bx, lru_lambda, lru_out_g, w_out,
              norm2_g, router_w, router_b, w_gate_up, b_gate_up, w_down, b_down)
    x2 = x.reshape(B * S, D)
    for l in range(ada_w.shape[0]):
        x2 = _layer(l, x2, B, S, c, *[p[l] for p in params])
    return x2.reshape(B, S, D)
```
